```python
import math
import jax, jax.numpy as jnp
from jax import lax
import numpy as np

D_MODEL = 1024
BATCH = 2
SEQ = 16384
DEPTH = 2

GRID_W = 64
CTX_LEN = 256
HEAD_DIM = 64
EPS = 1e-6
ROPE_BASE = 10000.0
ROPE_PAIRS = HEAD_DIM // 4
Q_BLOCK = 128
DA_HEADS = 4
DA_QK = 2 * HEAD_DIM
DA_V = 2 * HEAD_DIM
GLA_HEADS = 4
GLA_DK = 64
GLA_DV = 128
GLA_RANK = 16
GLA_TAU = 16.0
GLA_CHUNK = 64
NA_HEADS = 8
NA_DIM = 64
NA_KR = 8
NA_KC = 16
BR_W = 512
D_FF = 4 * D_MODEL
IN_SPLITS = (DA_HEADS * DA_QK, DA_HEADS * DA_QK, DA_HEADS * DA_V,
             GLA_HEADS * GLA_DK, GLA_HEADS * GLA_DK, GLA_HEADS * GLA_DV, GLA_HEADS * GLA_DV, 2 * GLA_RANK,
             NA_HEADS * NA_DIM, NA_HEADS * NA_DIM, NA_HEADS * NA_DIM,
             D_MODEL, D_MODEL, D_MODEL)
N_IN = sum(IN_SPLITS)

kernel_name = "hybrid_diffattn_gla_natten_prefix_block"


def rms_norm(x, g):
    xf = x.astype(jnp.float32)
    y = xf * lax.rsqrt(jnp.mean(xf * xf, axis=-1, keepdims=True) + EPS)
    return (y * g.astype(jnp.float32)).astype(x.dtype)


def to_heads(t, n_heads):
    b, n, w = t.shape
    return t.reshape(b, n, n_heads, w // n_heads).transpose(0, 2, 1, 3)


def from_heads(t):
    b, h, n, d = t.shape
    return t.transpose(0, 2, 1, 3).reshape(b, n, h * d)


def split_cols(z):
    idx = []
    acc = 0
    for s in IN_SPLITS[:-1]:
        acc += s
        idx.append(acc)
    return jnp.split(z, idx, axis=-1)


def rotate_pairs(x, ang):
    half = x.shape[-1] // 2
    x1, x2 = x[..., :half], x[..., half:]
    cos = jnp.cos(ang).astype(x.dtype)
    sin = jnp.sin(ang).astype(x.dtype)
    return jnp.concatenate([x1 * cos - x2 * sin, x1 * sin + x2 * cos], axis=-1)


def axial_rope(x, ang_row, ang_col):
    half = x.shape[-1] // 2
    return jnp.concatenate([rotate_pairs(x[..., :half], ang_row), rotate_pairs(x[..., half:], ang_col)], axis=-1)


def softmax_attend(q, k, v):
    s = jnp.einsum('bhqd,bhkd->bhqk', q, k).astype(jnp.float32)
    p = jax.nn.softmax(s, axis=-1)
    return jnp.einsum('bhqk,bhkd->bhqd', p.astype(v.dtype), v)


def da_heads(dq, dk, dv, qn_g, kn_g):
    scale = HEAD_DIM ** -0.5
    q = to_heads(dq, DA_HEADS)
    k = to_heads(dk, DA_HEADS)
    q1 = rms_norm(q[..., :HEAD_DIM], qn_g) * scale
    q2 = rms_norm(q[..., HEAD_DIM:], qn_g) * scale
    k1 = rms_norm(k[..., :HEAD_DIM], kn_g)
    k2 = rms_norm(k[..., HEAD_DIM:], kn_g)
    return q1, q2, k1, k2, to_heads(dv, DA_HEADS)


def diff_attention(q1, q2, k1, k2, v, lam, q_block):
    b, h, tq, d = q1.shape
    nb = tq // q_block

    def blocks(t):
        return t.reshape(b, h, nb, q_block, d).transpose(2, 0, 1, 3, 4)

    def one(args):
        a1, a2 = args
        p1 = jax.nn.softmax(jnp.einsum('bhqd,bhkd->bhqk', a1, k1).astype(jnp.float32), axis=-1)
        p2 = jax.nn.softmax(jnp.einsum('bhqd,bhkd->bhqk', a2, k2).astype(jnp.float32), axis=-1)
        return jnp.einsum('bhqk,bhkv->bhqv', (p1 - lam * p2).astype(v.dtype), v)

    o = lax.map(one, (blocks(q1), blocks(q2)))
    return o.transpose(1, 2, 0, 3, 4).reshape(b, h, tq, v.shape[-1])


def da_output(o, subln_g, lam_init):
    return from_heads(rms_norm(o, subln_g) * (1.0 - lam_init))


def gla_inputs(gq, gk, gv, ga, a2, ab):
    q = to_heads(gq, GLA_HEADS) * (GLA_DK ** -0.5)
    k = to_heads(gk, GLA_HEADS)
    v = to_heads(gv, GLA_HEADS)
    log_a = []
    for i in range(2):
        z = ga[..., i * GLA_RANK:(i + 1) * GLA_RANK] @ a2[i] + ab[i]
        log_a.append(to_heads(jax.nn.log_sigmoid(z.astype(jnp.float32)) / GLA_TAU, GLA_HEADS))
    return q, k, v, log_a[0], log_a[1]


def gla_chunked(q, k, v, log_a, s0):
    b, h, t, dk = q.shape
    dv = v.shape[-1]
    c = GLA_CHUNK
    n = t // c
    f32 = jnp.float32
    q = q.astype(f32).reshape(b, h, n, c, dk)
    k = k.astype(f32).reshape(b, h, n, c, dk)
    v = v.astype(f32).reshape(b, h, n, c, dv)
    cum = jnp.cumsum(log_a.reshape(b, h, n, c, dk), axis=3)
    cum_last = cum[:, :, :, -1:, :]
    qe = q * jnp.exp(cum)
    ke = k * jnp.exp(-cum)
    kd = k * jnp.exp(cum_last - cum)
    mask = jnp.tril(jnp.ones((c, c), dtype=bool))
    a_intra = jnp.where(mask, jnp.einsum('bhncd,bhnsd->bhncs', qe, ke), 0.0)
    o_intra = jnp.einsum('bhncs,bhnsv->bhncv', a_intra, v)
    d_state = jnp.einsum('bhncd,bhncv->nbhdv', kd, v)
    decay = jnp.exp(cum_last[:, :, :, 0, :]).transpose(2, 0, 1, 3)

    def step(s, inp):
        dec, ds = inp
        return dec[..., None] * s + ds, s

    s_final, s_prev = lax.scan(step, s0, (decay, d_state))
    o_inter = jnp.einsum('bhncd,nbhdv->bhncv', qe, s_prev)
    return (o_intra + o_inter).reshape(b, h, t, dv), s_final


def gla_bidirectional(lat, ctx):
    ql, kl, vl, lfl, lbl = lat
    qc, kc, vc, lfc, lbc = ctx
    b, h, _, dk = ql.shape
    s0 = jnp.zeros((b, h, dk, GLA_DV), jnp.float32)
    flip = lambda t: jnp.flip(t, axis=2)
    oc_f, sc_f = gla_chunked(qc, kc, vc, lfc, s0)
    oc_b, sc_b = gla_chunked(flip(qc), flip(kc), flip(vc), flip(lbc), s0)
    ol_f, _ = gla_chunked(ql, kl, vl, lfl, sc_f)
    ol_b, _ = gla_chunked(flip(ql), flip(kl), flip(vl), flip(lbl), sc_b)
    return ol_f + flip(ol_b), oc_f + flip(oc_b)


def gla_output(o, gg, gn_g):
    return from_heads(rms_norm(o, gn_g)).astype(gg.dtype) * jax.nn.silu(gg)


def na_heads(nq, nk, nv, qn_g, kn_g):
    q = rms_norm(to_heads(nq, NA_HEADS), qn_g) * (NA_DIM ** -0.5)
    k = rms_norm(to_heads(nk, NA_HEADS), kn_g)
    return q, k, to_heads(nv, NA_HEADS)


def neighborhood_attention(q, k, v, kc, vc, rpb):
    b, h, t, d = q.shape
    rows = t // GRID_W
    kr = min(NA_KR, rows)
    qg = q.reshape(b, h, rows, GRID_W, d)
    kg = k.reshape(b, h, rows, GRID_W, d)
    vg = v.reshape(b, h, rows, GRID_W, d)
    r_idx = jnp.arange(rows)
    row_start = jnp.clip(r_idx - kr // 2, 0, rows - kr)
    c_idx = jnp.arange(GRID_W)
    col_start = jnp.clip(c_idx - NA_KC // 2, 0, GRID_W - NA_KC)
    col_win = col_start[:, None] + jnp.arange(NA_KC)
    rel_col = col_win - c_idx[:, None] + (NA_KC - 1)
    rpb = rpb.astype(jnp.float32)
    n_loc = kr * NA_KC

    def row_fn(args):
        q_row, r, rs = args
        k_band = lax.dynamic_slice_in_dim(kg, rs, kr, axis=2)
        v_band = lax.dynamic_slice_in_dim(vg, rs, kr, axis=2)
        k_win = k_band[:, :, :, col_win, :]
        v_win = v_band[:, :, :, col_win, :]
        s_loc = jnp.einsum('bhwd,bhjwmd->bhwjm', q_row, k_win).astype(jnp.float32)
        rel_row = rs + jnp.arange(kr) - r + (NA_KR - 1)
        bias = rpb[:, rel_row[:, None, None], rel_col[None, :, :]]
        s_loc = s_loc + bias.transpose(0, 2, 1, 3)[None]
        s_ctx = jnp.einsum('bhwd,bhkd->bhwk', q_row, kc).astype(jnp.float32)
        s = jnp.concatenate([s_loc.reshape(b, h, GRID_W, n_loc), s_ctx], axis=-1)
        p = jax.nn.softmax(s, axis=-1).astype(v.dtype)
        p_loc = p[..., :n_loc].reshape(b, h, GRID_W, kr, NA_KC)
        return (jnp.einsum('bhwjm,bhjwmd->bhwd', p_loc, v_win)
                + jnp.einsum('bhwk,bhkd->bhwd', p[..., n_loc:], vc))

    o = lax.map(row_fn, (qg.transpose(2, 0, 1, 3, 4), r_idx, row_start))
    return o.transpose(1, 2, 0, 3, 4).reshape(b, h, t, d)


def branch_merge(y_da, y_gla, y_na, g_da, g_gla, g_na, w_br_da, w_br_gla, w_br_na, w_out):
    m = (jax.nn.sigmoid(g_da) * (y_da @ w_br_da)
         + jax.nn.sigmoid(g_gla) * (y_gla @ w_br_gla)
         + jax.nn.sigmoid(g_na) * (y_na @ w_br_na))
    return m @ w_out


def hybrid_mixer(h_lat, h_ctx, ang_row, ang_col, lam_init, need_ctx, w_in, da_qn_g, da_kn_g, da_lambda,
                 da_subln_g, gla_a2, gla_a_b, gla_gn_g, na_qn_g, na_kn_g, na_rpb, w_br_da, w_br_gla, w_br_na, w_out):
    zl = split_cols(h_lat @ w_in)
    zc = split_cols(h_ctx @ w_in)
    q1l, q2l, k1l, k2l, vl = da_heads(zl[0], zl[1], zl[2], da_qn_g, da_kn_g)
    q1l, q2l, k1l, k2l = [axial_rope(t, ang_row, ang_col) for t in (q1l, q2l, k1l, k2l)]
    q1c, q2c, k1c, k2c, vc = da_heads(zc[0], zc[1], zc[2], da_qn_g, da_kn_g)
    lp = da_lambda.astype(jnp.float32)
    lam = jnp.exp(jnp.sum(lp[0] * lp[1])) - jnp.exp(jnp.sum(lp[2] * lp[3])) + lam_init
    k1_all = jnp.concatenate([k1l, k1c], axis=2)
    k2_all = jnp.concatenate([k2l, k2c], axis=2)
    v_all = jnp.concatenate([vl, vc], axis=2)
    y_da_l = da_output(diff_attention(q1l, q2l, k1_all, k2_all, v_all, lam, Q_BLOCK), da_subln_g, lam_init)
    gl = gla_inputs(zl[3], zl[4], zl[5], zl[7], gla_a2, gla_a_b)
    gc = gla_inputs(zc[3], zc[4], zc[5], zc[7], gla_a2, gla_a_b)
    o_gla_l, o_gla_c = gla_bidirectional(gl, gc)
    y_gla_l = gla_output(o_gla_l, zl[6], gla_gn_g)
    nql, nkl, nvl = na_heads(zl[8], zl[9], zl[10], na_qn_g, na_kn_g)
    nqc, nkc, nvc = na_heads(zc[8], zc[9], zc[10], na_qn_g, na_kn_g)
    y_na_l = from_heads(neighborhood_attention(nql, nkl, nvl, nkc, nvc, na_rpb))
    out_l = branch_merge(y_da_l, y_gla_l, y_na_l, zl[11], zl[12], zl[13], w_br_da, w_br_gla, w_br_na, w_out)
    if not need_ctx:
        return out_l, None
    y_da_c = da_output(diff_attention(q1c, q2c, k1c, k2c, vc, lam, q1c.shape[2]), da_subln_g, lam_init)
    y_gla_c = gla_output(o_gla_c, zc[6], gla_gn_g)
    y_na_c = from_heads(softmax_attend(nqc, nkc, nvc))
    out_c = branch_merge(y_da_c, y_gla_c, y_na_c, zc[11], zc[12], zc[13], w_br_da, w_br_gla, w_br_na, w_out)
    return out_l, out_c


def sqrelu_mlp(h, w1, w2):
    a = jax.nn.relu(h @ w1)
    return (a * a) @ w2


def setup_inputs(seed: int = 0) -> dict:
    key = jax.random.key(seed)
    ks = jax.random.split(key, 32)
    f32 = jnp.float32
    nrm = lambda k, shape, s: jax.random.normal(k, shape, f32) * s
    d = D_MODEL
    return {
        'x': nrm(ks[0], (BATCH, SEQ, d), 1.0),
        'c': nrm(ks[1], (BATCH, d), 1.0),
        'ctx': nrm(ks[2], (BATCH, CTX_LEN, d), 1.0),
        'c_ctx': nrm(ks[3], (d,), 1.0),
        'w_mod': nrm(ks[4], (DEPTH, d, 6 * d), 0.5 * d ** -0.5),
        'b_mod': nrm(ks[5], (DEPTH, 6 * d), 0.02),
        'norm1_g': 1.0 + nrm(ks[6], (DEPTH, d), 0.02),
        'norm2_g': 1.0 + nrm(ks[7], (DEPTH, d), 0.02),
        'w_in': nrm(ks[8], (DEPTH, d, N_IN), d ** -0.5),
        'da_qn_g': 1.0 + nrm(ks[9], (DEPTH, HEAD_DIM), 0.02),
        'da_kn_g': 1.0 + nrm(ks[10], (DEPTH, HEAD_DIM), 0.02),
        'da_lambda': nrm(ks[11], (DEPTH, 4, HEAD_DIM), 0.1),
        'da_subln_g': 1.0 + nrm(ks[12], (DEPTH, DA_V), 0.02),
        'gla_a2': nrm(ks[13], (DEPTH, 2, GLA_RANK, GLA_HEADS * GLA_DK), GLA_RANK ** -0.5),
        'gla_a_b': nrm(ks[14], (DEPTH, 2, GLA_HEADS * GLA_DK), 0.1),
        'gla_gn_g': 1.0 + nrm(ks[15], (DEPTH, GLA_DV), 0.02),
        'na_qn_g': 1.0 + nrm(ks[16], (DEPTH, NA_DIM), 0.02),
        'na_kn_g': 1.0 + nrm(ks[17], (DEPTH, NA_DIM), 0.02),
        'na_rpb': nrm(ks[18], (DEPTH, NA_HEADS, 2 * NA_KR - 1, 2 * NA_KC - 1), 0.02),
        'w_br_da': nrm(ks[19], (DEPTH, BR_W, d), BR_W ** -0.5),
        'w_br_gla': nrm(ks[20], (DEPTH, BR_W, d), BR_W ** -0.5),
        'w_br_na': nrm(ks[21], (DEPTH, BR_W, d), BR_W ** -0.5),
        'w_out': nrm(ks[22], (DEPTH, d, d), d ** -0.5),
        'w_ff1': nrm(ks[23], (DEPTH, d, D_FF), d ** -0.5),
        'w_ff2': nrm(ks[24], (DEPTH, D_FF, d), D_FF ** -0.5),
    }


def reference(x, c, ctx, c_ctx, w_mod, b_mod, norm1_g, norm2_g, w_in, da_qn_g, da_kn_g, da_lambda, da_subln_g,
              gla_a2, gla_a_b, gla_gn_g, na_qn_g, na_kn_g, na_rpb, w_br_da, w_br_gla, w_br_na, w_out, w_ff1, w_ff2):
    t_len = x.shape[1]
    pos = jnp.arange(t_len)
    row = (pos // GRID_W).astype(jnp.float32)
    col = (pos % GRID_W).astype(jnp.float32)
    freqs = ROPE_BASE ** (-jnp.arange(ROPE_PAIRS, dtype=jnp.float32) / ROPE_PAIRS)
    ang_row = row[:, None] * freqs
    ang_col = col[:, None] * freqs
    for l in range(DEPTH):
        need_ctx = l < DEPTH - 1
        lam_init = 0.8 - 0.6 * math.exp(-0.3 * l)
        mod = jax.nn.silu(c) @ w_mod[l] + b_mod[l]
        mod_c = jax.nn.silu(c_ctx) @ w_mod[l] + b_mod[l]
        sh1, sc1, g1, sh2, sc2, g2 = [m[:, None, :] for m in jnp.split(mod, 6, axis=-1)]
        sh1c, sc1c, g1c, sh2c, sc2c, g2c = jnp.split(mod_c, 6, axis=-1)
        h_lat = rms_norm(x, norm1_g[l]) * (1.0 + sc1) + sh1
        h_ctx = rms_norm(ctx, norm1_g[l]) * (1.0 + sc1c) + sh1c
        m_lat, m_ctx = hybrid_mixer(h_lat, h_ctx, ang_row, ang_col, lam_init, need_ctx, w_in[l], da_qn_g[l],
                                    da_kn_g[l], da_lambda[l], da_subln_g[l], gla_a2[l], gla_a_b[l], gla_gn_g[l],
                                    na_qn_g[l], na_kn_g[l], na_rpb[l], w_br_da[l], w_br_gla[l], w_br_na[l], w_out[l])
        x = x + g1 * m_lat
        x = x + g2 * sqrelu_mlp(rms_norm(x, norm2_g[l]) * (1.0 + sc2) + sh2, w_ff1[l], w_ff2[l])
        if need_ctx:
            ctx = ctx + g1c * m_ctx
            ctx = ctx + g2c * sqrelu_mlp(rms_norm(ctx, norm2_g[l]) * (1.0 + sc2c) + sh2c, w_ff1[l], w_ff2[l])
    return x
```

```python
import functools
import math

import numpy as np
import jax
import jax.numpy as jnp
from jax import lax
from jax.experimental import pallas as pl
from jax.experimental.pallas import tpu as pltpu

F32 = jnp.float32
BF16 = jnp.bfloat16

D_MODEL = 1024
GRID_W = 64
HEAD_DIM = 64
EPS = 1e-6
ROPE_BASE = 10000.0
ROPE_PAIRS = HEAD_DIM // 4
DA_HEADS = 4
GLA_HEADS = 4
GLA_DK = 64
GLA_DV = 128
GLA_RANK = 16
GLA_TAU = 16.0
GLA_CHUNK = 64
NA_HEADS = 8
NA_KR = 8
NA_KC = 16
BR_W = 512
D_FF = 4 * D_MODEL

LANES = 128
NA_QROWS = 4
NA_BAND = 16
NEG_BIG = -1e30
VMEM_LIMIT = 48 * 1024 * 1024


def _cparams(sem):
    return pltpu.CompilerParams(dimension_semantics=sem, vmem_limit_bytes=VMEM_LIMIT)


def _dot(a, b):
    return jnp.dot(a, b, preferred_element_type=F32)


def _dot_nt(a, b):
    return lax.dot_general(a, b, (((1,), (1,)), ((), ())), preferred_element_type=F32)


def _dot_tn(a, b):
    return lax.dot_general(a, b, (((0,), (0,)), ((), ())), preferred_element_type=F32)


def _split_dot(x, w_exact):
    hi = x.astype(BF16)
    lo = (x - hi.astype(F32)).astype(BF16)
    return _dot(hi, w_exact) + _dot(lo, w_exact)


def _split_dot_left(w_exact, x):
    hi = x.astype(BF16)
    lo = (x - hi.astype(F32)).astype(BF16)
    return _dot(w_exact, hi) + _dot(w_exact, lo)


def _sigmoid(x):
    return 1.0 / (1.0 + jnp.exp(-x))


def _row_tile(tall):
    for tm in (640, 256):
        if tall % tm == 0:
            return tm
    raise ValueError(f"unsupported token count {tall}")


def _mod_body(c_ref, w_ref, b_ref, o_ref):
    cv = c_ref[...]
    s = cv * _sigmoid(cv)
    o_ref[0] = _dot(s.astype(BF16), w_ref[0].astype(BF16)) + b_ref[0]


def _mod_call(cvec, w_mod, b_mod):
    depth, d, n = w_mod.shape
    tn = 1536
    return pl.pallas_call(
        _mod_body,
        grid=(depth, n // tn),
        in_specs=[pl.BlockSpec((8, d), lambda l, j: (0, 0)),
                  pl.BlockSpec((1, d, tn), lambda l, j: (l, 0, j)),
                  pl.BlockSpec((1, 1, tn), lambda l, j: (l, 0, j))],
        out_specs=pl.BlockSpec((1, 8, tn), lambda l, j: (l, 0, j)),
        out_shape=jax.ShapeDtypeStruct((depth, 8, n), F32),
        compiler_params=_cparams(("parallel", "parallel")),
        name="mod",
    )(cvec, w_mod, b_mod.reshape(depth, 1, n))


def _rope_body(freq_ref, cos_ref, sin_ref, *, t_lat, tm):
    i = pl.program_id(0)
    t = i * tm + lax.broadcasted_iota(jnp.int32, (tm, LANES), 0)
    lane = lax.broadcasted_iota(jnp.int32, (tm, LANES), 1)
    shift = int(math.log2(GRID_W))
    row = lax.shift_right_logical(t, shift).astype(F32)
    col = jnp.bitwise_and(t, GRID_W - 1).astype(F32)
    use_row = jnp.bitwise_and(lane, 2 * ROPE_PAIRS) == 0
    first_half = jnp.bitwise_and(lane, ROPE_PAIRS) == 0
    ang = jnp.where(use_row, row, col) * freq_ref[...]
    is_lat = t < t_lat
    cos_ref[...] = jnp.where(is_lat, jnp.cos(ang), 1.0)
    sn = jnp.sin(ang)
    sin_ref[...] = jnp.where(is_lat, jnp.where(first_half, -sn, sn), 0.0)


def _rope_tables(t_lat, tall):
    tm = 256
    freqs = ROPE_BASE ** (-jnp.arange(ROPE_PAIRS, dtype=F32) / ROPE_PAIRS)
    freq_lane = jnp.tile(freqs, LANES // ROPE_PAIRS).reshape(1, LANES)
    return pl.pallas_call(
        functools.partial(_rope_body, t_lat=t_lat, tm=tm),
        grid=(tall // tm,),
        in_specs=[pl.BlockSpec((1, LANES), lambda i: (0, 0))],
        out_specs=[pl.BlockSpec((tm, LANES), lambda i: (i, 0))] * 2,
        out_shape=[jax.ShapeDtypeStruct((tall, LANES), F32)] * 2,
        compiler_params=_cparams(("parallel",)),
        name="rope_tables",
    )(freq_lane)


def _is_ctx_rows(r, tm, t_lat):
    t = r * tm + lax.broadcasted_iota(jnp.int32, (tm, 1), 0)
    return t >= t_lat


def _norm_mod(x, g, sc, sh):
    ms = jnp.mean(x * x, axis=-1, keepdims=True)
    return (x * lax.rsqrt(ms + EPS) * g) * (1.0 + sc) + sh


def _segnorm64(z, bd, gain):
    ss = _split_dot(z * z, bd)
    return z * lax.rsqrt(ss * (1.0 / HEAD_DIM) + EPS) * gain


def _hnorm_body(x_ref, g_ref, mb_ref, mc_ref, h_ref, *, t_lat, tm, d):
    r = pl.program_id(1)
    ctx = _is_ctx_rows(r, tm, t_lat)
    sh = jnp.where(ctx, mc_ref[0, :, 0:d], mb_ref[0, :, 0:d])
    sc = jnp.where(ctx, mc_ref[0, :, d:2 * d], mb_ref[0, :, d:2 * d])
    h_ref[0] = _norm_mod(x_ref[0], g_ref[...], sc, sh).astype(BF16)


def _hnorm_call(xall, g, mod3, t_lat):
    b, tall, d = xall.shape
    tm = _row_tile(tall)
    nb = b
    return pl.pallas_call(
        functools.partial(_hnorm_body, t_lat=t_lat, tm=tm, d=d),
        grid=(b, tall // tm),
        in_specs=[pl.BlockSpec((1, tm, d), lambda bi, r: (bi, r, 0)),
                  pl.BlockSpec((1, d), lambda bi, r: (0, 0)),
                  pl.BlockSpec((1, 1, 6 * d), lambda bi, r: (bi, 0, 0)),
                  pl.BlockSpec((1, 1, 6 * d), lambda bi, r: (nb, 0, 0))],
        out_specs=pl.BlockSpec((1, tm, d), lambda bi, r: (bi, r, 0)),
        out_shape=jax.ShapeDtypeStruct((b, tall, d), BF16),
        compiler_params=_cparams(("parallel", "parallel")),
        name="hnorm",
    )(xall, g.reshape(1, d), mod3, mod3)


def _rope128(x, cs, sn, first_half):
    partner = jnp.where(first_half, pltpu.roll(x, LANES - ROPE_PAIRS, 1), pltpu.roll(x, ROPE_PAIRS, 1))
    return x * cs + partner * sn


def _proj_da_body(h_ref, wq_ref, wk_ref, wv_ref, bd_ref, gq_ref, gk_ref, cos_ref, sin_ref,
                  qt_ref, k_ref, vt_ref):
    h = h_ref[0]
    bd = bd_ref[...]
    cs = cos_ref[...]
    sn = sin_ref[...]
    lane = lax.broadcasted_iota(jnp.int32, cs.shape, 1)
    first_half = jnp.bitwise_and(lane, ROPE_PAIRS) == 0
    zq = _dot(h, wq_ref[...])
    zk = _dot(h, wk_ref[...])
    scale = HEAD_DIM ** -0.5
    for hd in range(DA_HEADS):
        sl = slice(hd * LANES, (hd + 1) * LANES)
        q = _segnorm64(zq[:, sl], bd, gq_ref[...]) * scale
        q = _rope128(q, cs, sn, first_half)
        qt_ref[0, sl, :] = q.T.astype(BF16)
        k = _segnorm64(zk[:, sl], bd, gk_ref[...])
        k_ref[0, :, sl] = _rope128(k, cs, sn, first_half).astype(BF16)
    zv = _dot(h, wv_ref[...])
    vt_ref[0] = zv.T.astype(BF16)


def _proj_da_call(h, wq, wk, wv, bd, gq, gk, cos_t, sin_t):
    b, tall, d = h.shape
    tm = _row_tile(tall)
    w = DA_HEADS * LANES
    const = lambda bi, r: (0, 0)
    return pl.pallas_call(
        _proj_da_body,
        grid=(b, tall // tm),
        in_specs=[pl.BlockSpec((1, tm, d), lambda bi, r: (bi, r, 0)),
                  pl.BlockSpec((d, w), const), pl.BlockSpec((d, w), const), pl.BlockSpec((d, w), const),
                  pl.BlockSpec((LANES, LANES), const),
                  pl.BlockSpec((1, LANES), const), pl.BlockSpec((1, LANES), const),
                  pl.BlockSpec((tm, LANES), lambda bi, r: (r, 0)),
                  pl.BlockSpec((tm, LANES), lambda bi, r: (r, 0))],
        out_specs=[pl.BlockSpec((1, w, tm), lambda bi, r: (bi, 0, r)),
                   pl.BlockSpec((1, tm, w), lambda bi, r: (bi, r, 0)),
                   pl.BlockSpec((1, w, tm), lambda bi, r: (bi, 0, r))],
        out_shape=[jax.ShapeDtypeStruct((b, w, tall), BF16),
                   jax.ShapeDtypeStruct((b, tall, w), BF16),
                   jax.ShapeDtypeStruct((b, w, tall), BF16)],
        compiler_params=_cparams(("parallel", "parallel")),
        name="proj_da",
    )(h, wq, wk, wv, bd, gq, gk, cos_t, sin_t)


def _proj_gla_body(h_ref, wq_ref, wk_ref, wv_ref, wg_ref, wa_ref, a2_ref, ab_ref,
                   q_ref, k_ref, v_ref, sg_ref, la_ref):
    h = h_ref[0]
    q_ref[0] = _dot(h, wq_ref[...]) * (GLA_DK ** -0.5)
    k_ref[0] = _dot(h, wk_ref[...])
    v_ref[0] = _dot(h, wv_ref[...]).astype(BF16)
    g = _dot(h, wg_ref[...])
    sg_ref[0] = (g * _sigmoid(g)).astype(BF16)
    ga = _dot(h, wa_ref[...])
    z = _dot(ga.astype(BF16), a2_ref[...]) + ab_ref[...]
    la_ref[0] = (jnp.minimum(z, 0.0) - jnp.log1p(jnp.exp(-jnp.abs(z)))) * (1.0 / GLA_TAU)


def _proj_gla_call(h, wq, wk, wv, wg, wa, a2p, ab):
    b, tall, d = h.shape
    tm = _row_tile(tall)
    wqk = GLA_HEADS * GLA_DK
    wv_ = GLA_HEADS * GLA_DV
    const = lambda bi, r: (0, 0)
    row = lambda n: pl.BlockSpec((1, tm, n), lambda bi, r: (bi, r, 0))
    return pl.pallas_call(
        _proj_gla_body,
        grid=(b, tall // tm),
        in_specs=[row(d),
                  pl.BlockSpec((d, wqk), const), pl.BlockSpec((d, wqk), const),
                  pl.BlockSpec((d, wv_), const), pl.BlockSpec((d, wv_), const),
                  pl.BlockSpec((d, LANES), const),
                  pl.BlockSpec((LANES, 2 * wqk), const), pl.BlockSpec((1, 2 * wqk), const)],
        out_specs=[row(wqk), row(wqk), row(wv_), row(wv_), row(2 * wqk)],
        out_shape=[jax.ShapeDtypeStruct((b, tall, wqk), F32),
                   jax.ShapeDtypeStruct((b, tall, wqk), F32),
                   jax.ShapeDtypeStruct((b, tall, wv_), BF16),
                   jax.ShapeDtypeStruct((b, tall, wv_), BF16),
                   jax.ShapeDtypeStruct((b, tall, 2 * wqk), F32)],
        compiler_params=_cparams(("parallel", "parallel")),
        name="proj_gla",
    )(h, wq, wk, wv, wg, wa, a2p, ab)


def _proj_na_body(h_ref, wq_ref, wk_ref, wv_ref, bd_ref, gq_ref, gk_ref, q_ref, k_ref, v_ref):
    h = h_ref[0]
    bd = bd_ref[...]
    zq = _dot(h, wq_ref[...])
    zk = _dot(h, wk_ref[...])
    scale = HEAD_DIM ** -0.5
    for j in range(NA_HEADS * HEAD_DIM // LANES):
        sl = slice(j * LANES, (j + 1) * LANES)
        q_ref[0, :, sl] = (_segnorm64(zq[:, sl], bd, gq_ref[...]) * scale).astype(BF16)
        k_ref[0, :, sl] = _segnorm64(zk[:, sl], bd, gk_ref[...]).astype(BF16)
    v_ref[0] = _dot(h, wv_ref[...]).astype(BF16)


def _proj_na_call(h, wq, wk, wv, bd, gq, gk):
    b, tall, d = h.shape
    tm = _row_tile(tall)
    w = NA_HEADS * HEAD_DIM
    const = lambda bi, r: (0, 0)
    row = lambda n: pl.BlockSpec((1, tm, n), lambda bi, r: (bi, r, 0))
    return pl.pallas_call(
        _proj_na_body,
        grid=(b, tall // tm),
        in_specs=[row(d), pl.BlockSpec((d, w), const), pl.BlockSpec((d, w), const), pl.BlockSpec((d, w), const),
                  pl.BlockSpec((LANES, LANES), const),
                  pl.BlockSpec((1, LANES), const), pl.BlockSpec((1, LANES), const)],
        out_specs=[row(w), row(w), row(w)],
        out_shape=[jax.ShapeDtypeStruct((b, tall, w), BF16)] * 3,
        compiler_params=_cparams(("parallel", "parallel")),
        name="proj_na",
    )(h, wq, wk, wv, bd, gq, gk)


def _proj_gate_body(h_ref, w_ref, o_ref):
    h = h_ref[0]
    n = w_ref.shape[1]
    step = 512
    for j in range(n // step):
        sl = slice(j * step, (j + 1) * step)
        o_ref[0, :, sl] = _sigmoid(_dot(h, w_ref[:, sl])).astype(BF16)


def _proj_gate_call(h, w):
    b, tall, d = h.shape
    tm = _row_tile(tall)
    n = w.shape[1]
    return pl.pallas_call(
        _proj_gate_body,
        grid=(b, tall // tm),
        in_specs=[pl.BlockSpec((1, tm, d), lambda bi, r: (bi, r, 0)),
                  pl.BlockSpec((d, n), lambda bi, r: (0, 0))],
        out_specs=pl.BlockSpec((1, tm, n), lambda bi, r: (bi, r, 0)),
        out_shape=jax.ShapeDtypeStruct((b, tall, n), BF16),
        compiler_params=_cparams(("parallel", "parallel")),
        name="proj_gate",
    )(h, w)


def _da_body(qt_ref, k_ref, vt_ref, lp_ref, sg_ref, o_ref, m_ref, l_ref, acc_ref, *, lam_init, nk):
    j = pl.program_id(3)

    @pl.when(j == 0)
    def _():
        m_ref[...] = jnp.full(m_ref.shape, -jnp.inf, F32)
        l_ref[...] = jnp.zeros(l_ref.shape, F32)
        acc_ref[...] = jnp.zeros(acc_ref.shape, F32)

    qt = qt_ref[0]
    kk = k_ref[0]
    vt = vt_ref[0]
    rowid = lax.broadcasted_iota(jnp.int32, qt.shape, 0)
    zero = jnp.zeros_like(qt)
    for a in range(2):
        w = jnp.where((rowid < HEAD_DIM) if a == 0 else (rowid >= HEAD_DIM), qt, zero)
        s = _dot(kk, w)
        m_old = m_ref[a:a + 1, :]
        m_new = jnp.maximum(m_old, jnp.max(s, axis=0, keepdims=True))
        alpha = jnp.exp(m_old - m_new)
        p = jnp.exp(s - m_new)
        l_ref[a:a + 1, :] = alpha * l_ref[a:a + 1, :] + jnp.sum(p, axis=0, keepdims=True)
        acc_ref[a] = alpha * acc_ref[a] + _dot(vt, p.astype(BF16))
        m_ref[a:a + 1, :] = m_new

    @pl.when(j == nk - 1)
    def _():
        lp = lp_ref[...]
        e1 = jnp.exp(jnp.sum(lp[0:1] * lp[1:2], axis=-1, keepdims=True))
        e2 = jnp.exp(jnp.sum(lp[2:3] * lp[3:4], axis=-1, keepdims=True))
        lam = e1 - e2 + lam_init
        o = acc_ref[0] / l_ref[0:1, :] - lam * (acc_ref[1] / l_ref[1:2, :])
        ms = jnp.mean(o * o, axis=0, keepdims=True)
        y = (o * lax.rsqrt(ms + EPS) * sg_ref[...]) * (1.0 - lam_init)
        o_ref[0] = y.T.astype(BF16)


def _da_call(qt, kk, vt, lp, subg, lam_init, *, q_off, nq, k_off, nk, tq, tk):
    b = qt.shape[0]
    return pl.pallas_call(
        functools.partial(_da_body, lam_init=lam_init, nk=nk),
        grid=(b, DA_HEADS, nq, nk),
        in_specs=[pl.BlockSpec((1, LANES, tq), lambda bi, h, i, j: (bi, h, i + q_off)),
                  pl.BlockSpec((1, tk, LANES), lambda bi, h, i, j: (bi, j + k_off, h)),
                  pl.BlockSpec((1, LANES, tk), lambda bi, h, i, j: (bi, h, j + k_off)),
                  pl.BlockSpec((4, HEAD_DIM), lambda bi, h, i, j: (0, 0)),
                  pl.BlockSpec((LANES, 1), lambda bi, h, i, j: (0, 0))],
        out_specs=pl.BlockSpec((1, tq, LANES), lambda bi, h, i, j: (bi, i, h)),
        out_shape=jax.ShapeDtypeStruct((b, nq * tq, DA_HEADS * LANES), BF16),
        scratch_shapes=[pltpu.VMEM((8, tq), F32), pltpu.VMEM((8, tq), F32), pltpu.VMEM((2, LANES, tq), F32)],
        compiler_params=_cparams(("parallel", "parallel", "parallel", "arbitrary")),
        name="diff_attn",
    )(qt, kk, vt, lp, subg)


def _gla_body(q_ref, k_ref, v_ref, la_ref, tri_ref, o_ref, s_ref, *, reverse, tb):
    i = pl.program_id(1)

    @pl.when(i == 0)
    def _():
        s_ref[...] = jnp.zeros(s_ref.shape, F32)

    c = GLA_CHUNK
    w = GLA_HEADS * GLA_DK
    tri = tri_ref[...]
    ri = lax.broadcasted_iota(jnp.int32, (c, c), 0)
    ci = lax.broadcasted_iota(jnp.int32, (c, c), 1)
    keep = (ci >= ri) if reverse else (ci <= ri)
    lane = lax.broadcasted_iota(jnp.int32, (1, w), 1)
    chunks = range(tb // c)
    for ch in (reversed(chunks) if reverse else chunks):
        rows = slice(ch * c, (ch + 1) * c)
        la = la_ref[0, rows, :]
        cum = _split_dot_left(tri, la)
        cl = cum[0:1, :] if reverse else cum[c - 1:c, :]
        q = q_ref[0, rows, :]
        k = k_ref[0, rows, :]
        qe = q * jnp.exp(cum)
        ke = (k * jnp.exp(-cum)).astype(BF16)
        kd = (k * jnp.exp(cl - cum)).astype(BF16)
        dec = jnp.exp(cl)
        for hd in range(GLA_HEADS):
            hm = (lane >= hd * GLA_DK) & (lane < (hd + 1) * GLA_DK)
            qh = jnp.where(hm, qe, 0.0).astype(BF16)
            a = jnp.where(keep, _dot_nt(qh, ke), 0.0)
            vh = v_ref[0, rows, hd * GLA_DV:(hd + 1) * GLA_DV]
            st = s_ref[hd]
            o = _dot(a.astype(BF16), vh) + _dot_nt(qh, st.astype(BF16))
            o_ref[0, rows, hd * GLA_DV:(hd + 1) * GLA_DV] = o
            s_ref[hd] = st * dec + _dot_tn(vh, kd)


def _gla_call(gq, gk, gv, la, tri, *, t_lat, reverse):
    b, tall, w = gq.shape
    tb = 256
    n_lat = t_lat // tb
    nblk = tall // tb
    if reverse:
        blk = lambda i: jnp.where(i == 0, n_lat, n_lat - i)
    else:
        blk = lambda i: jnp.where(i == 0, n_lat, i - 1)
    wv_ = GLA_HEADS * GLA_DV
    return pl.pallas_call(
        functools.partial(_gla_body, reverse=reverse, tb=tb),
        grid=(b, nblk),
        in_specs=[pl.BlockSpec((1, tb, w), lambda bi, i: (bi, blk(i), 0)),
                  pl.BlockSpec((1, tb, w), lambda bi, i: (bi, blk(i), 0)),
                  pl.BlockSpec((1, tb, wv_), lambda bi, i: (bi, blk(i), 0)),
                  pl.BlockSpec((1, tb, w), lambda bi, i: (bi, blk(i), 1 if reverse else 0)),
                  pl.BlockSpec((GLA_CHUNK, GLA_CHUNK), lambda bi, i: (0, 0))],
        out_specs=pl.BlockSpec((1, tb, wv_), lambda bi, i: (bi, blk(i), 0)),
        out_shape=jax.ShapeDtypeStruct((b, tall, wv_), F32),
        scratch_shapes=[pltpu.VMEM((GLA_HEADS, GLA_DV, w), F32)],
        compiler_params=_cparams(("parallel", "arbitrary")),
        name="gla_bwd" if reverse else "gla_fwd",
    )(gq, gk, gv, la, tri)


def _na_body(q_ref, k_ref, v_ref, kc_ref, vc_ref, bias_ref, o_ref, *, rows):
    i = pl.program_id(2)
    kb0 = jnp.clip(i * NA_QROWS - NA_KR // 2, 0, rows - NA_BAND)
    start = pl.multiple_of(kb0 * GRID_W, GRID_W)
    nband = NA_BAND * GRID_W
    q = q_ref[0]
    kb = k_ref[0, pl.ds(start, nband), :]
    vb = v_ref[0, pl.ds(start, nband), :]
    kc = kc_ref[0]
    vc = vc_ref[0]
    lane = lax.broadcasted_iota(jnp.int32, q.shape, 1)
    outs = []
    for hh in range(2):
        qh = jnp.where((lane < HEAD_DIM) if hh == 0 else (lane >= HEAD_DIM), q, jnp.zeros_like(q))
        s_loc = _dot_nt(qh, kb) + bias_ref[0, hh]
        s_ctx = _dot_nt(qh, kc)
        m = jnp.maximum(jnp.max(s_loc, axis=-1, keepdims=True), jnp.max(s_ctx, axis=-1, keepdims=True))
        p_loc = jnp.exp(s_loc - m)
        p_ctx = jnp.exp(s_ctx - m)
        l = jnp.sum(p_loc, axis=-1, keepdims=True) + jnp.sum(p_ctx, axis=-1, keepdims=True)
        outs.append((_dot(p_loc.astype(BF16), vb) + _dot(p_ctx.astype(BF16), vc)) / l)
    o_ref[0] = jnp.where(lane < HEAD_DIM, outs[0], outs[1]).astype(BF16)


def _na_call(nq, nk, nv, bias, *, t_lat):
    b, tall, w = nq.shape
    tc = tall - t_lat
    rows = t_lat // GRID_W
    tq = NA_QROWS * GRID_W
    nsteps = rows // NA_QROWS
    npair = w // LANES
    ctx_blk = t_lat // tc

    def btype(i):
        return jnp.where(i == 0, 0, jnp.where(i == nsteps - 2, 2, jnp.where(i == nsteps - 1, 3, 1)))

    return pl.pallas_call(
        functools.partial(_na_body, rows=rows),
        grid=(b, npair, nsteps),
        in_specs=[pl.BlockSpec((1, tq, LANES), lambda bi, hp, i: (bi, i, hp)),
                  pl.BlockSpec((1, t_lat, LANES), lambda bi, hp, i: (bi, 0, hp)),
                  pl.BlockSpec((1, t_lat, LANES), lambda bi, hp, i: (bi, 0, hp)),
                  pl.BlockSpec((1, tc, LANES), lambda bi, hp, i: (bi, ctx_blk, hp)),
                  pl.BlockSpec((1, tc, LANES), lambda bi, hp, i: (bi, ctx_blk, hp)),
                  pl.BlockSpec((1, 2, tq, NA_BAND * GRID_W), lambda bi, hp, i: (btype(i), hp, 0, 0))],
        out_specs=pl.BlockSpec((1, tq, LANES), lambda bi, hp, i: (bi, i, hp)),
        out_shape=jax.ShapeDtypeStruct((b, t_lat, w), BF16),
        compiler_params=_cparams(("parallel", "parallel", "arbitrary")),
        name="nbr_attn",
    )(nq, nk, nv, nk, nv, bias)


def _na_bias_tiles(rpb, rows):
    assert rows >= NA_BAND and rows % NA_QROWS == 0
    tiles = []
    for r0 in (0, NA_QROWS, rows - 2 * NA_QROWS, rows - NA_QROWS):
        kb0 = min(max(r0 - NA_KR // 2, 0), rows - NA_BAND)
        qi = np.arange(NA_QROWS * GRID_W)
        qr, qc = r0 + qi // GRID_W, qi % GRID_W
        kj = np.arange(NA_BAND * GRID_W)
        kr, kc = kb0 + kj // GRID_W, kj % GRID_W
        rs = np.clip(qr - NA_KR // 2, 0, rows - NA_KR)
        cs = np.clip(qc - NA_KC // 2, 0, GRID_W - NA_KC)
        valid = ((kr[None, :] >= rs[:, None]) & (kr[None, :] < rs[:, None] + NA_KR)
                 & (kc[None, :] >= cs[:, None]) & (kc[None, :] < cs[:, None] + NA_KC))
        rel_r = np.clip(kr[None, :] - qr[:, None] + NA_KR - 1, 0, 2 * NA_KR - 2)
        rel_c = np.clip(kc[None, :] - qc[:, None] + NA_KC - 1, 0, 2 * NA_KC - 2)
        tiles.append(jnp.where(valid[None], rpb[:, rel_r, rel_c], NEG_BIG))
    return jnp.stack(tiles)


def _ctx_attn_body(q_ref, k_ref, v_ref, o_ref):
    q = q_ref[0]
    k = k_ref[0]
    v = v_ref[0]
    lane = lax.broadcasted_iota(jnp.int32, q.shape, 1)
    outs = []
    for hh in range(2):
        qh = jnp.where((lane < HEAD_DIM) if hh == 0 else (lane >= HEAD_DIM), q, jnp.zeros_like(q))
        s = _dot_nt(qh, k)
        m = jnp.max(s, axis=-1, keepdims=True)
        p = jnp.exp(s - m)
        outs.append(_dot(p.astype(BF16), v) / jnp.sum(p, axis=-1, keepdims=True))
    o_ref[0] = jnp.where(lane < HEAD_DIM, outs[0], outs[1]).astype(BF16)


def _ctx_attn_call(nq, nk, nv, *, t_lat):
    b, tall, w = nq.shape
    tc = tall - t_lat
    ctx_blk = t_lat // tc
    spec = pl.BlockSpec((1, tc, LANES), lambda bi, hp: (bi, ctx_blk, hp))
    return pl.pallas_call(
        _ctx_attn_body,
        grid=(b, w // LANES),
        in_specs=[spec, spec, spec],
        out_specs=pl.BlockSpec((1, tc, LANES), lambda bi, hp: (bi, 0, hp)),
        out_shape=jax.ShapeDtypeStruct((b, tc, w), BF16),
        compiler_params=_cparams(("parallel", "parallel")),
        name="ctx_attn",
    )(nq, nk, nv)


def _merge_body(x_ref, yd_ref, of_ref, ob_ref, sg_ref, yn_ref, gt_ref, wd_ref, wg_ref, wn_ref, wo_ref,
                gn_ref, mb_ref, mc_ref, o_ref, *, t_lat, tm, d):
    r = pl.program_id(1)
    ctx = _is_ctx_rows(r, tm, t_lat)
    g1 = jnp.where(ctx, mc_ref[0, :, 2 * d:3 * d], mb_ref[0, :, 2 * d:3 * d])
    og = of_ref[0] + ob_ref[0]
    parts = []
    for hd in range(GLA_HEADS):
        oh = og[:, hd * GLA_DV:(hd + 1) * GLA_DV]
        ms = jnp.mean(oh * oh, axis=-1, keepdims=True)
        parts.append(oh * lax.rsqrt(ms + EPS) * gn_ref[...])
    yg = (jnp.concatenate(parts, axis=-1) * sg_ref[0].astype(F32)).astype(BF16)
    m = (gt_ref[0, :, 0:d].astype(F32) * _dot(yd_ref[0], wd_ref[...])
         + gt_ref[0, :, d:2 * d].astype(F32) * _dot(yg, wg_ref[...])
         + gt_ref[0, :, 2 * d:3 * d].astype(F32) * _dot(yn_ref[0], wn_ref[...]))
    o_ref[0] = x_ref[0] + g1 * _dot(m.astype(BF16), wo_ref[...])


def _merge_call(xall, yd, of, ob, sg, yn, gates, wd, wg, wn, wo, gn, mod3, t_lat):
    b, tall, d = xall.shape
    tm = _row_tile(tall)
    nb = b
    const = lambda bi, r: (0, 0)
    row = lambda n: pl.BlockSpec((1, tm, n), lambda bi, r: (bi, r, 0))
    return pl.pallas_call(
        functools.partial(_merge_body, t_lat=t_lat, tm=tm, d=d),
        grid=(b, tall // tm),
        in_specs=[row(d), row(BR_W), row(BR_W), row(BR_W), row(BR_W), row(BR_W), row(3 * d),
                  pl.BlockSpec((BR_W, d), const), pl.BlockSpec((BR_W, d), const), pl.BlockSpec((BR_W, d), const),
                  pl.BlockSpec((d, d), const), pl.BlockSpec((1, GLA_DV), const),
                  pl.BlockSpec((1, 1, 6 * d), lambda bi, r: (bi, 0, 0)),
                  pl.BlockSpec((1, 1, 6 * d), lambda bi, r: (nb, 0, 0))],
        out_specs=row(d),
        out_shape=jax.ShapeDtypeStruct((b, tall, d), F32),
        compiler_params=_cparams(("parallel", "parallel")),
        name="merge",
    )(xall, yd, of, ob, sg, yn, gates, wd, wg, wn, wo, gn, mod3, mod3)


def _mlp_body(x_ref, g_ref, w1_ref, w2_ref, mb_ref, mc_ref, o_ref, *, t_lat, tm, d):
    r = pl.program_id(1)
    ctx = _is_ctx_rows(r, tm, t_lat)
    sh = jnp.where(ctx, mc_ref[0, :, 3 * d:4 * d], mb_ref[0, :, 3 * d:4 * d])
    sc = jnp.where(ctx, mc_ref[0, :, 4 * d:5 * d], mb_ref[0, :, 4 * d:5 * d])
    g2 = jnp.where(ctx, mc_ref[0, :, 5 * d:6 * d], mb_ref[0, :, 5 * d:6 * d])
    x = x_ref[0]
    h = _norm_mod(x, g_ref[...], sc, sh).astype(BF16)
    acc = jnp.zeros((tm, d), F32)
    step = 1024
    for j in range(w1_ref.shape[1] // step):
        a = jnp.maximum(_dot(h, w1_ref[:, j * step:(j + 1) * step]), 0.0)
        acc = acc + _dot((a * a).astype(BF16), w2_ref[j * step:(j + 1) * step, :])
    o_ref[0] = x + g2 * acc


def _mlp_call(xall, g, w1, w2, mod3, t_lat):
    b, tall, d = xall.shape
    tm = _row_tile(tall)
    nb = b
    dff = w1.shape[1]
    const = lambda bi, r: (0, 0)
    return pl.pallas_call(
        functools.partial(_mlp_body, t_lat=t_lat, tm=tm, d=d),
        grid=(b, tall // tm),
        in_specs=[pl.BlockSpec((1, tm, d), lambda bi, r: (bi, r, 0)),
                  pl.BlockSpec((1, d), const),
                  pl.BlockSpec((d, dff), const, pipeline_mode=pl.Buffered(1)),
                  pl.BlockSpec((dff, d), const, pipeline_mode=pl.Buffered(1)),
                  pl.BlockSpec((1, 1, 6 * d), lambda bi, r: (bi, 0, 0)),
                  pl.BlockSpec((1, 1, 6 * d), lambda bi, r: (nb, 0, 0))],
        out_specs=pl.BlockSpec((1, tm, d), lambda bi, r: (bi, r, 0)),
        out_shape=jax.ShapeDtypeStruct((b, tall, d), F32),
        compiler_params=_cparams(("parallel", "parallel")),
        name="mlp",
    )(xall, g.reshape(1, d), w1, w2, mod3, mod3)


def kernel(x, c, ctx, c_ctx, w_mod, b_mod, norm1_g, norm2_g, w_in, da_qn_g, da_kn_g, da_lambda, da_subln_g,
           gla_a2, gla_a_b, gla_gn_g, na_qn_g, na_kn_g, na_rpb, w_br_da, w_br_gla, w_br_na, w_out, w_ff1, w_ff2):
    b, t_lat, d = x.shape
    tc = ctx.shape[1]
    tall = t_lat + tc
    depth = w_mod.shape[0]
    rows = t_lat // GRID_W
    assert d == D_MODEL and t_lat % (2 * tc) == 0 and tc == 256 and b < 8

    xall = jnp.concatenate([x, ctx], axis=1)
    cvec = jnp.zeros((8, d), F32).at[0:b].set(c).at[b].set(c_ctx)
    mod = _mod_call(cvec, w_mod, b_mod)
    cos_t, sin_t = _rope_tables(t_lat, tall)

    seg = np.arange(LANES) // HEAD_DIM
    bd = jnp.asarray(seg[:, None] == seg[None, :], BF16)
    ci = np.arange(GLA_CHUNK)
    tri_f = jnp.asarray(ci[None, :] <= ci[:, None], BF16)
    tri_b = jnp.asarray(ci[None, :] >= ci[:, None], BF16)

    o_dq, o_dk, o_dv = 0, 512, 1024
    o_gq, o_gk, o_gv, o_gg, o_ga = 1536, 1792, 2048, 2560, 3072
    o_nq, o_nk, o_nv = 3104, 3616, 4128
    o_gate = 4640
    tq_da = 512 if t_lat % 512 == 0 else 256
    tk_da = 1280 if tall % 1280 == 0 else 256

    for l in range(depth):
        need_ctx = l < depth - 1
        lam_init = 0.8 - 0.6 * math.exp(-0.3 * l)
        mod3 = mod[l].reshape(8, 1, 6 * d)
        wl = w_in[l].astype(BF16)
        cut = lambda a, n: wl[:, a:a + n]
        wa = jnp.concatenate([cut(o_ga, 2 * GLA_RANK), jnp.zeros((d, LANES - 2 * GLA_RANK), BF16)], axis=1)
        a2p = jnp.zeros((LANES, 2 * GLA_HEADS * GLA_DK), F32)
        a2p = a2p.at[0:GLA_RANK, 0:256].set(gla_a2[l, 0]).at[GLA_RANK:2 * GLA_RANK, 256:512].set(gla_a2[l, 1])
        ab = gla_a_b[l].reshape(1, 2 * GLA_HEADS * GLA_DK)
        tile2 = lambda g: jnp.tile(g, LANES // HEAD_DIM).reshape(1, LANES)

        h = _hnorm_call(xall, norm1_g[l], mod3, t_lat)
        qt, kk, vt = _proj_da_call(h, cut(o_dq, 512), cut(o_dk, 512), cut(o_dv, 512), bd,
                                   tile2(da_qn_g[l]), tile2(da_kn_g[l]), cos_t, sin_t)
        gq, gk, gv, sgg, la = _proj_gla_call(h, cut(o_gq, 256), cut(o_gk, 256), cut(o_gv, 512), cut(o_gg, 512),
                                             wa, a2p.astype(BF16), ab)
        nq, nk, nv = _proj_na_call(h, cut(o_nq, 512), cut(o_nk, 512), cut(o_nv, 512), bd,
                                   tile2(na_qn_g[l]), tile2(na_kn_g[l]))
        gates = _proj_gate_call(h, cut(o_gate, 3 * d))

        subg = da_subln_g[l].reshape(LANES, 1)
        y_da = _da_call(qt, kk, vt, da_lambda[l], subg, lam_init,
                        q_off=0, nq=t_lat // tq_da, k_off=0, nk=tall // tk_da, tq=tq_da, tk=tk_da)
        o_f = _gla_call(gq, gk, gv, la, tri_f, t_lat=t_lat, reverse=False)
        o_b = _gla_call(gq, gk, gv, la, tri_b, t_lat=t_lat, reverse=True)
        bias = _na_bias_tiles(na_rpb[l], rows)
        y_na = _na_call(nq, nk, nv, bias, t_lat=t_lat)
        if need_ctx:
            y_da_c = _da_call(qt, kk, vt, da_lambda[l], subg, lam_init,
                              q_off=t_lat // tc, nq=1, k_off=t_lat // tc, nk=1, tq=tc, tk=tc)
            y_na_c = _ctx_attn_call(nq, nk, nv, t_lat=t_lat)
        else:
            y_da_c = jnp.zeros((b, tc, BR_W), BF16)
            y_na_c = jnp.zeros((b, tc, BR_W), BF16)
        y_da = jnp.concatenate([y_da, y_da_c], axis=1)
        y_na = jnp.concatenate([y_na, y_na_c], axis=1)

        xall = _merge_call(xall, y_da, o_f, o_b, sgg, y_na, gates,
                           w_br_da[l].astype(BF16), w_br_gla[l].astype(BF16), w_br_na[l].astype(BF16),
                           w_out[l].astype(BF16), gla_gn_g[l].reshape(1, GLA_DV), mod3, t_lat)
        xall = _mlp_call(xall, norm2_g[l], w_ff1[l].astype(BF16), w_ff2[l].astype(BF16), mod3, t_lat)
    return xall[:, :t_lat]
```

```python
import functools
import math

import numpy as np
import jax
import jax.numpy as jnp
from jax import lax
from jax.experimental import pallas as pl
from jax.experimental.pallas import tpu as pltpu

F32 = jnp.float32
BF16 = jnp.bfloat16

D_MODEL = 1024
GRID_W = 64
HEAD_DIM = 64
EPS = 1e-6
ROPE_BASE = 10000.0
ROPE_PAIRS = HEAD_DIM // 4
DA_HEADS = 4
GLA_HEADS = 4
GLA_DK = 64
GLA_DV = 128
GLA_RANK = 16
GLA_TAU = 16.0
GLA_CHUNK = 64
NA_HEADS = 8
NA_KR = 8
NA_KC = 16
BR_W = 512
D_FF = 4 * D_MODEL

DA_V = 2 * HEAD_DIM
DA_VPAD = 16
DA_VROWS = DA_V + DA_VPAD
LOG2E = math.log2(math.e)
DA_SAFE_LOG2 = 45.0

LANES = 128
NA_QROWS = 4
NA_BAND = 16
NEG_BIG = -1e30
VMEM_LIMIT = 48 * 1024 * 1024


def _cparams(sem):
    return pltpu.CompilerParams(dimension_semantics=sem, vmem_limit_bytes=VMEM_LIMIT)


def _dot(a, b):
    return jnp.dot(a, b, preferred_element_type=F32)


def _dot_nt(a, b):
    return lax.dot_general(a, b, (((1,), (1,)), ((), ())), preferred_element_type=F32)


def _dot_tn(a, b):
    return lax.dot_general(a, b, (((0,), (0,)), ((), ())), preferred_element_type=F32)


def _split_dot(x, w_exact):
    hi = x.astype(BF16)
    lo = (x - hi.astype(F32)).astype(BF16)
    return _dot(hi, w_exact) + _dot(lo, w_exact)


def _split_dot_left(w_exact, x):
    hi = x.astype(BF16)
    lo = (x - hi.astype(F32)).astype(BF16)
    return _dot(w_exact, hi) + _dot(w_exact, lo)


def _sigmoid(x):
    return 1.0 / (1.0 + jnp.exp(-x))


def _row_tile(tall):
    for tm in (640, 256):
        if tall % tm == 0:
            return tm
    raise ValueError(f"unsupported token count {tall}")


def _mod_body(c_ref, w_ref, b_ref, o_ref):
    cv = c_ref[...]
    s = cv * _sigmoid(cv)
    o_ref[0] = _dot(s.astype(BF16), w_ref[0].astype(BF16)) + b_ref[0]


def _mod_call(cvec, w_mod, b_mod):
    depth, d, n = w_mod.shape
    tn = 1536
    return pl.pallas_call(
        _mod_body,
        grid=(depth, n // tn),
        in_specs=[pl.BlockSpec((8, d), lambda l, j: (0, 0)),
                  pl.BlockSpec((1, d, tn), lambda l, j: (l, 0, j)),
                  pl.BlockSpec((1, 1, tn), lambda l, j: (l, 0, j))],
        out_specs=pl.BlockSpec((1, 8, tn), lambda l, j: (l, 0, j)),
        out_shape=jax.ShapeDtypeStruct((depth, 8, n), F32),
        compiler_params=_cparams(("parallel", "parallel")),
        name="mod",
    )(cvec, w_mod, b_mod.reshape(depth, 1, n))


def _rope_body(freq_ref, cos_ref, sin_ref, *, t_lat, tm):
    i = pl.program_id(0)
    t = i * tm + lax.broadcasted_iota(jnp.int32, (tm, LANES), 0)
    lane = lax.broadcasted_iota(jnp.int32, (tm, LANES), 1)
    shift = int(math.log2(GRID_W))
    row = lax.shift_right_logical(t, shift).astype(F32)
    col = jnp.bitwise_and(t, GRID_W - 1).astype(F32)
    use_row = jnp.bitwise_and(lane, 2 * ROPE_PAIRS) == 0
    first_half = jnp.bitwise_and(lane, ROPE_PAIRS) == 0
    ang = jnp.where(use_row, row, col) * freq_ref[...]
    is_lat = t < t_lat
    cos_ref[...] = jnp.where(is_lat, jnp.cos(ang), 1.0)
    sn = jnp.sin(ang)
    sin_ref[...] = jnp.where(is_lat, jnp.where(first_half, -sn, sn), 0.0)


def _rope_tables(t_lat, tall):
    tm = 256
    freqs = ROPE_BASE ** (-jnp.arange(ROPE_PAIRS, dtype=F32) / ROPE_PAIRS)
    freq_lane = jnp.tile(freqs, LANES // ROPE_PAIRS).reshape(1, LANES)
    return pl.pallas_call(
        functools.partial(_rope_body, t_lat=t_lat, tm=tm),
        grid=(tall // tm,),
        in_specs=[pl.BlockSpec((1, LANES), lambda i: (0, 0))],
        out_specs=[pl.BlockSpec((tm, LANES), lambda i: (i, 0))] * 2,
        out_shape=[jax.ShapeDtypeStruct((tall, LANES), F32)] * 2,
        compiler_params=_cparams(("parallel",)),
        name="rope_tables",
    )(freq_lane)


def _is_ctx_rows(r, tm, t_lat):
    t = r * tm + lax.broadcasted_iota(jnp.int32, (tm, 1), 0)
    return t >= t_lat


def _norm_mod(x, g, sc, sh):
    ms = jnp.mean(x * x, axis=-1, keepdims=True)
    return (x * lax.rsqrt(ms + EPS) * g) * (1.0 + sc) + sh


def _segnorm64(z, bd, gain):
    ss = _split_dot(z * z, bd)
    return z * lax.rsqrt(ss * (1.0 / HEAD_DIM) + EPS) * gain


def _hnorm_body(x_ref, g_ref, mb_ref, mc_ref, h_ref, *, t_lat, tm, d):
    r = pl.program_id(1)
    ctx = _is_ctx_rows(r, tm, t_lat)
    sh = jnp.where(ctx, mc_ref[0, :, 0:d], mb_ref[0, :, 0:d])
    sc = jnp.where(ctx, mc_ref[0, :, d:2 * d], mb_ref[0, :, d:2 * d])
    h_ref[0] = _norm_mod(x_ref[0], g_ref[...], sc, sh).astype(BF16)


def _hnorm_call(xall, g, mod3, t_lat):
    b, tall, d = xall.shape
    tm = _row_tile(tall)
    nb = b
    return pl.pallas_call(
        functools.partial(_hnorm_body, t_lat=t_lat, tm=tm, d=d),
        grid=(b, tall // tm),
        in_specs=[pl.BlockSpec((1, tm, d), lambda bi, r: (bi, r, 0)),
                  pl.BlockSpec((1, d), lambda bi, r: (0, 0)),
                  pl.BlockSpec((1, 1, 6 * d), lambda bi, r: (bi, 0, 0)),
                  pl.BlockSpec((1, 1, 6 * d), lambda bi, r: (nb, 0, 0))],
        out_specs=pl.BlockSpec((1, tm, d), lambda bi, r: (bi, r, 0)),
        out_shape=jax.ShapeDtypeStruct((b, tall, d), BF16),
        compiler_params=_cparams(("parallel", "parallel")),
        name="hnorm",
    )(xall, g.reshape(1, d), mod3, mod3)


def _rope128(x, cs, sn, first_half):
    partner = jnp.where(first_half, pltpu.roll(x, LANES - ROPE_PAIRS, 1), pltpu.roll(x, ROPE_PAIRS, 1))
    return x * cs + partner * sn


def _proj_da_body(h_ref, wq_ref, wk_ref, wv_ref, bd_ref, gq_ref, gk_ref, cos_ref, sin_ref,
                  qt_ref, k_ref, vt_ref, nrm_ref):
    h = h_ref[0]
    tm = h.shape[0]
    bd = bd_ref[...]
    cs = cos_ref[...]
    sn = sin_ref[...]
    lane = lax.broadcasted_iota(jnp.int32, cs.shape, 1)
    first_half = jnp.bitwise_and(lane, ROPE_PAIRS) == 0
    zq = _dot(h, wq_ref[...])
    zk = _dot(h, wk_ref[...])
    zv = _dot(h, wv_ref[...])
    scale = HEAD_DIM ** -0.5 * LOG2E
    ones_row = (lax.broadcasted_iota(jnp.int32, (DA_VPAD, tm), 0) == 0).astype(BF16)
    qn2 = jnp.zeros((1, LANES), F32)
    kn2 = jnp.zeros((1, LANES), F32)
    for hd in range(DA_HEADS):
        sl = slice(hd * LANES, (hd + 1) * LANES)
        q = _segnorm64(zq[:, sl], bd, gq_ref[...]) * scale
        q = _rope128(q, cs, sn, first_half)
        qt_ref[0, sl, :] = q.T.astype(BF16)
        k = _segnorm64(zk[:, sl], bd, gk_ref[...])
        k = _rope128(k, cs, sn, first_half)
        k_ref[0, :, sl] = k.astype(BF16)
        qn2 = jnp.maximum(qn2, jnp.max(_split_dot(q * q, bd), axis=0, keepdims=True))
        kn2 = jnp.maximum(kn2, jnp.max(_split_dot(k * k, bd), axis=0, keepdims=True))
        v0 = hd * DA_VROWS
        vt_ref[0, v0:v0 + DA_V, :] = zv[:, sl].T.astype(BF16)
        vt_ref[0, v0 + DA_V:v0 + DA_VROWS, :] = ones_row
    nrm_ref[0, 0:1, :] = qn2
    nrm_ref[0, 1:2, :] = kn2


def _proj_da_call(h, wq, wk, wv, bd, gq, gk, cos_t, sin_t):
    b, tall, d = h.shape
    tm = _row_tile(tall)
    nt = tall // tm
    w = DA_HEADS * LANES
    const = lambda bi, r: (0, 0)
    return pl.pallas_call(
        _proj_da_body,
        grid=(b, tall // tm),
        in_specs=[pl.BlockSpec((1, tm, d), lambda bi, r: (bi, r, 0)),
                  pl.BlockSpec((d, w), const), pl.BlockSpec((d, w), const), pl.BlockSpec((d, w), const),
                  pl.BlockSpec((LANES, LANES), const),
                  pl.BlockSpec((1, LANES), const), pl.BlockSpec((1, LANES), const),
                  pl.BlockSpec((tm, LANES), lambda bi, r: (r, 0)),
                  pl.BlockSpec((tm, LANES), lambda bi, r: (r, 0))],
        out_specs=[pl.BlockSpec((1, w, tm), lambda bi, r: (bi, 0, r)),
                   pl.BlockSpec((1, tm, w), lambda bi, r: (bi, r, 0)),
                   pl.BlockSpec((1, DA_HEADS * DA_VROWS, tm), lambda bi, r: (bi, 0, r)),
                   pl.BlockSpec((1, 2, LANES), lambda bi, r: (bi * nt + r, 0, 0))],
        out_shape=[jax.ShapeDtypeStruct((b, w, tall), BF16),
                   jax.ShapeDtypeStruct((b, tall, w), BF16),
                   jax.ShapeDtypeStruct((b, DA_HEADS * DA_VROWS, tall), BF16),
                   jax.ShapeDtypeStruct((b * nt, 2, LANES), F32)],
        compiler_params=_cparams(("parallel", "parallel")),
        name="proj_da",
    )(h, wq, wk, wv, bd, gq, gk, cos_t, sin_t)


def _proj_gla_body(h_ref, wq_ref, wk_ref, wv_ref, wg_ref, wa_ref, a2_ref, ab_ref,
                   q_ref, k_ref, v_ref, sg_ref, la_ref):
    h = h_ref[0]
    q_ref[0] = _dot(h, wq_ref[...]) * (GLA_DK ** -0.5)
    k_ref[0] = _dot(h, wk_ref[...])
    v_ref[0] = _dot(h, wv_ref[...]).astype(BF16)
    g = _dot(h, wg_ref[...])
    sg_ref[0] = (g * _sigmoid(g)).astype(BF16)
    ga = _dot(h, wa_ref[...])
    z = _dot(ga.astype(BF16), a2_ref[...]) + ab_ref[...]
    la_ref[0] = (jnp.minimum(z, 0.0) - jnp.log1p(jnp.exp(-jnp.abs(z)))) * (1.0 / GLA_TAU)


def _proj_gla_call(h, wq, wk, wv, wg, wa, a2p, ab):
    b, tall, d = h.shape
    tm = _row_tile(tall)
    wqk = GLA_HEADS * GLA_DK
    wv_ = GLA_HEADS * GLA_DV
    const = lambda bi, r: (0, 0)
    row = lambda n: pl.BlockSpec((1, tm, n), lambda bi, r: (bi, r, 0))
    return pl.pallas_call(
        _proj_gla_body,
        grid=(b, tall // tm),
        in_specs=[row(d),
                  pl.BlockSpec((d, wqk), const), pl.BlockSpec((d, wqk), const),
                  pl.BlockSpec((d, wv_), const), pl.BlockSpec((d, wv_), const),
                  pl.BlockSpec((d, LANES), const),
                  pl.BlockSpec((LANES, 2 * wqk), const), pl.BlockSpec((1, 2 * wqk), const)],
        out_specs=[row(wqk), row(wqk), row(wv_), row(wv_), row(2 * wqk)],
        out_shape=[jax.ShapeDtypeStruct((b, tall, wqk), F32),
                   jax.ShapeDtypeStruct((b, tall, wqk), F32),
                   jax.ShapeDtypeStruct((b, tall, wv_), BF16),
                   jax.ShapeDtypeStruct((b, tall, wv_), BF16),
                   jax.ShapeDtypeStruct((b, tall, 2 * wqk), F32)],
        compiler_params=_cparams(("parallel", "parallel")),
        name="proj_gla",
    )(h, wq, wk, wv, wg, wa, a2p, ab)


def _proj_na_body(h_ref, wq_ref, wk_ref, wv_ref, bd_ref, gq_ref, gk_ref, q_ref, k_ref, v_ref):
    h = h_ref[0]
    bd = bd_ref[...]
    zq = _dot(h, wq_ref[...])
    zk = _dot(h, wk_ref[...])
    scale = HEAD_DIM ** -0.5
    for j in range(NA_HEADS * HEAD_DIM // LANES):
        sl = slice(j * LANES, (j + 1) * LANES)
        q_ref[0, :, sl] = (_segnorm64(zq[:, sl], bd, gq_ref[...]) * scale).astype(BF16)
        k_ref[0, :, sl] = _segnorm64(zk[:, sl], bd, gk_ref[...]).astype(BF16)
    v_ref[0] = _dot(h, wv_ref[...]).astype(BF16)


def _proj_na_call(h, wq, wk, wv, bd, gq, gk):
    b, tall, d = h.shape
    tm = _row_tile(tall)
    w = NA_HEADS * HEAD_DIM
    const = lambda bi, r: (0, 0)
    row = lambda n: pl.BlockSpec((1, tm, n), lambda bi, r: (bi, r, 0))
    return pl.pallas_call(
        _proj_na_body,
        grid=(b, tall // tm),
        in_specs=[row(d), pl.BlockSpec((d, w), const), pl.BlockSpec((d, w), const), pl.BlockSpec((d, w), const),
                  pl.BlockSpec((LANES, LANES), const),
                  pl.BlockSpec((1, LANES), const), pl.BlockSpec((1, LANES), const)],
        out_specs=[row(w), row(w), row(w)],
        out_shape=[jax.ShapeDtypeStruct((b, tall, w), BF16)] * 3,
        compiler_params=_cparams(("parallel", "parallel")),
        name="proj_na",
    )(h, wq, wk, wv, bd, gq, gk)


def _proj_gate_body(h_ref, w_ref, o_ref):
    h = h_ref[0]
    n = w_ref.shape[1]
    step = 512
    for j in range(n // step):
        sl = slice(j * step, (j + 1) * step)
        o_ref[0, :, sl] = _sigmoid(_dot(h, w_ref[:, sl])).astype(BF16)


def _proj_gate_call(h, w):
    b, tall, d = h.shape
    tm = _row_tile(tall)
    n = w.shape[1]
    return pl.pallas_call(
        _proj_gate_body,
        grid=(b, tall // tm),
        in_specs=[pl.BlockSpec((1, tm, d), lambda bi, r: (bi, r, 0)),
                  pl.BlockSpec((d, n), lambda bi, r: (0, 0))],
        out_specs=pl.BlockSpec((1, tm, n), lambda bi, r: (bi, r, 0)),
        out_shape=jax.ShapeDtypeStruct((b, tall, n), BF16),
        compiler_params=_cparams(("parallel", "parallel")),
        name="proj_gate",
    )(h, w)


def _da_body(safe_ref, qt_ref, k_ref, vt_ref, lp_ref, sg_ref, o_ref, m_ref, a_ref, acc_ref, *, lam_init, nk):
    j = pl.program_id(3)
    use_lag = safe_ref[0] > 0.0

    @pl.when(j == 0)
    def _():
        m_ref[...] = jnp.full(m_ref.shape, jnp.where(use_lag, -safe_ref[0], -jnp.inf), F32)
        a_ref[...] = jnp.ones(a_ref.shape, F32)
        acc_ref[...] = jnp.zeros(acc_ref.shape, F32)

    qt = qt_ref[0]
    kk = k_ref[0]
    vt = vt_ref[0]
    rowid = lax.broadcasted_iota(jnp.int32, qt.shape, 0)
    zero = jnp.zeros_like(qt)

    def step(lagged):
        for a in range(2):
            w = jnp.where((rowid < HEAD_DIM) if a == 0 else (rowid >= HEAD_DIM), qt, zero)
            s = _dot(kk, w)
            m_old = m_ref[a:a + 1, :]
            m_new = jnp.maximum(m_old, jnp.max(s, axis=0, keepdims=True))
            owed = a_ref[a:a + 1, :]
            if lagged:
                p = jnp.exp2(s - m_old)
                acc_ref[a] = owed * acc_ref[a] + _dot(vt, p.astype(BF16))
                a_ref[a:a + 1, :] = jnp.exp2(m_old - m_new)
            else:
                p = jnp.exp2(s - m_new)
                acc_ref[a] = (owed * jnp.exp2(m_old - m_new)) * acc_ref[a] + _dot(vt, p.astype(BF16))
                a_ref[a:a + 1, :] = jnp.ones_like(owed)
            m_ref[a:a + 1, :] = m_new

    pl.when(use_lag)(functools.partial(step, True))
    pl.when(jnp.logical_not(use_lag))(functools.partial(step, False))

    @pl.when(j == nk - 1)
    def _():
        lp = lp_ref[...]
        e1 = jnp.exp(jnp.sum(lp[0:1] * lp[1:2], axis=-1, keepdims=True))
        e2 = jnp.exp(jnp.sum(lp[2:3] * lp[3:4], axis=-1, keepdims=True))
        lam = e1 - e2 + lam_init
        o1 = acc_ref[0, 0:DA_V, :] / acc_ref[0, DA_V:DA_V + 1, :]
        o2 = acc_ref[1, 0:DA_V, :] / acc_ref[1, DA_V:DA_V + 1, :]
        o = o1 - lam * o2
        ms = jnp.mean(o * o, axis=0, keepdims=True)
        y = (o * lax.rsqrt(ms + EPS) * sg_ref[...]) * (1.0 - lam_init)
        o_ref[0] = y.T.astype(BF16)


def _da_call(safe, qt, kk, vt, lp, subg, lam_init, *, q_off, nq, k_off, nk, tq, tk):
    b = qt.shape[0]
    return pl.pallas_call(
        functools.partial(_da_body, lam_init=lam_init, nk=nk),
        grid=(b, DA_HEADS, nq, nk),
        in_specs=[pl.BlockSpec(memory_space=pltpu.SMEM),
                  pl.BlockSpec((1, LANES, tq), lambda bi, h, i, j: (bi, h, i + q_off)),
                  pl.BlockSpec((1, tk, LANES), lambda bi, h, i, j: (bi, j + k_off, h)),
                  pl.BlockSpec((1, DA_VROWS, tk), lambda bi, h, i, j: (bi, h, j + k_off)),
                  pl.BlockSpec((4, HEAD_DIM), lambda bi, h, i, j: (0, 0)),
                  pl.BlockSpec((LANES, 1), lambda bi, h, i, j: (0, 0))],
        out_specs=pl.BlockSpec((1, tq, LANES), lambda bi, h, i, j: (bi, i, h)),
        out_shape=jax.ShapeDtypeStruct((b, nq * tq, DA_HEADS * LANES), BF16),
        scratch_shapes=[pltpu.VMEM((8, tq), F32), pltpu.VMEM((8, tq), F32),
                        pltpu.VMEM((2, DA_VROWS, tq), F32)],
        compiler_params=_cparams(("parallel", "parallel", "parallel", "arbitrary")),
        name="diff_attn",
    )(safe, qt, kk, vt, lp, subg)


def _gla_body(q_ref, k_ref, v_ref, la_ref, tri_ref, o_ref, s_ref, *, reverse, tb):
    i = pl.program_id(1)

    @pl.when(i == 0)
    def _():
        s_ref[...] = jnp.zeros(s_ref.shape, F32)

    c = GLA_CHUNK
    w = GLA_HEADS * GLA_DK
    tri = tri_ref[...]
    ri = lax.broadcasted_iota(jnp.int32, (c, c), 0)
    ci = lax.broadcasted_iota(jnp.int32, (c, c), 1)
    keep = (ci >= ri) if reverse else (ci <= ri)
    lane = lax.broadcasted_iota(jnp.int32, (1, w), 1)
    chunks = range(tb // c)
    for ch in (reversed(chunks) if reverse else chunks):
        rows = slice(ch * c, (ch + 1) * c)
        la = la_ref[0, rows, :]
        cum = _split_dot_left(tri, la)
        cl = cum[0:1, :] if reverse else cum[c - 1:c, :]
        q = q_ref[0, rows, :]
        k = k_ref[0, rows, :]
        qe = q * jnp.exp(cum)
        ke = (k * jnp.exp(-cum)).astype(BF16)
        kd = (k * jnp.exp(cl - cum)).astype(BF16)
        dec = jnp.exp(cl)
        for hd in range(GLA_HEADS):
            hm = (lane >= hd * GLA_DK) & (lane < (hd + 1) * GLA_DK)
            qh = jnp.where(hm, qe, 0.0).astype(BF16)
            a = jnp.where(keep, _dot_nt(qh, ke), 0.0)
            vh = v_ref[0, rows, hd * GLA_DV:(hd + 1) * GLA_DV]
            st = s_ref[hd]
            o = _dot(a.astype(BF16), vh) + _dot_nt(qh, st.astype(BF16))
            o_ref[0, rows, hd * GLA_DV:(hd + 1) * GLA_DV] = o
            s_ref[hd] = st * dec + _dot_tn(vh, kd)


def _gla_call(gq, gk, gv, la, tri, *, t_lat, reverse):
    b, tall, w = gq.shape
    tb = 256
    n_lat = t_lat // tb
    nblk = tall // tb
    if reverse:
        blk = lambda i: jnp.where(i == 0, n_lat, n_lat - i)
    else:
        blk = lambda i: jnp.where(i == 0, n_lat, i - 1)
    wv_ = GLA_HEADS * GLA_DV
    return pl.pallas_call(
        functools.partial(_gla_body, reverse=reverse, tb=tb),
        grid=(b, nblk),
        in_specs=[pl.BlockSpec((1, tb, w), lambda bi, i: (bi, blk(i), 0)),
                  pl.BlockSpec((1, tb, w), lambda bi, i: (bi, blk(i), 0)),
                  pl.BlockSpec((1, tb, wv_), lambda bi, i: (bi, blk(i), 0)),
                  pl.BlockSpec((1, tb, w), lambda bi, i: (bi, blk(i), 1 if reverse else 0)),
                  pl.BlockSpec((GLA_CHUNK, GLA_CHUNK), lambda bi, i: (0, 0))],
        out_specs=pl.BlockSpec((1, tb, wv_), lambda bi, i: (bi, blk(i), 0)),
        out_shape=jax.ShapeDtypeStruct((b, tall, wv_), F32),
        scratch_shapes=[pltpu.VMEM((GLA_HEADS, GLA_DV, w), F32)],
        compiler_params=_cparams(("parallel", "arbitrary")),
        name="gla_bwd" if reverse else "gla_fwd",
    )(gq, gk, gv, la, tri)


def _na_body(q_ref, k_ref, v_ref, kc_ref, vc_ref, bias_ref, o_ref, *, rows):
    i = pl.program_id(2)
    kb0 = jnp.clip(i * NA_QROWS - NA_KR // 2, 0, rows - NA_BAND)
    start = pl.multiple_of(kb0 * GRID_W, GRID_W)
    nband = NA_BAND * GRID_W
    q = q_ref[0]
    kb = k_ref[0, pl.ds(start, nband), :]
    vb = v_ref[0, pl.ds(start, nband), :]
    kc = kc_ref[0]
    vc = vc_ref[0]
    lane = lax.broadcasted_iota(jnp.int32, q.shape, 1)
    outs = []
    for hh in range(2):
        qh = jnp.where((lane < HEAD_DIM) if hh == 0 else (lane >= HEAD_DIM), q, jnp.zeros_like(q))
        s_loc = _dot_nt(qh, kb) + bias_ref[0, hh]
        s_ctx = _dot_nt(qh, kc)
        m = jnp.maximum(jnp.max(s_loc, axis=-1, keepdims=True), jnp.max(s_ctx, axis=-1, keepdims=True))
        p_loc = jnp.exp(s_loc - m)
        p_ctx = jnp.exp(s_ctx - m)
        l = jnp.sum(p_loc, axis=-1, keepdims=True) + jnp.sum(p_ctx, axis=-1, keepdims=True)
        outs.append((_dot(p_loc.astype(BF16), vb) + _dot(p_ctx.astype(BF16), vc)) / l)
    o_ref[0] = jnp.where(lane < HEAD_DIM, outs[0], outs[1]).astype(BF16)


def _na_call(nq, nk, nv, bias, *, t_lat):
    b, tall, w = nq.shape
    tc = tall - t_lat
    rows = t_lat // GRID_W
    tq = NA_QROWS * GRID_W
    nsteps = rows // NA_QROWS
    npair = w // LANES
    ctx_blk = t_lat // tc

    def btype(i):
        return jnp.where(i == 0, 0, jnp.where(i == nsteps - 2, 2, jnp.where(i == nsteps - 1, 3, 1)))

    return pl.pallas_call(
        functools.partial(_na_body, rows=rows),
        grid=(b, npair, nsteps),
        in_specs=[pl.BlockSpec((1, tq, LANES), lambda bi, hp, i: (bi, i, hp)),
                  pl.BlockSpec((1, t_lat, LANES), lambda bi, hp, i: (bi, 0, hp)),
                  pl.BlockSpec((1, t_lat, LANES), lambda bi, hp, i: (bi, 0, hp)),
                  pl.BlockSpec((1, tc, LANES), lambda bi, hp, i: (bi, ctx_blk, hp)),
                  pl.BlockSpec((1, tc, LANES), lambda bi, hp, i: (bi, ctx_blk, hp)),
                  pl.BlockSpec((1, 2, tq, NA_BAND * GRID_W), lambda bi, hp, i: (btype(i), hp, 0, 0))],
        out_specs=pl.BlockSpec((1, tq, LANES), lambda bi, hp, i: (bi, i, hp)),
        out_shape=jax.ShapeDtypeStruct((b, t_lat, w), BF16),
        compiler_params=_cparams(("parallel", "parallel", "arbitrary")),
        name="nbr_attn",
    )(nq, nk, nv, nk, nv, bias)


def _na_bias_tiles(rpb, rows):
    assert rows >= NA_BAND and rows % NA_QROWS == 0
    nh, na, nb = rpb.shape
    cidx = np.arange(GRID_W)
    rel_c = cidx[None, :] - cidx[:, None] + NA_KC - 1
    sel = jnp.asarray(rel_c[None] == np.arange(nb)[:, None, None], F32)
    toep = jnp.einsum('hab,bqk->haqk', rpb.astype(F32), sel, precision=lax.Precision.HIGHEST)
    toep = jnp.pad(toep, ((0, 0), (NA_BAND, NA_BAND), (0, 0), (0, 0)))
    tiles = []
    for r0 in (0, NA_QROWS, rows - 2 * NA_QROWS, rows - NA_QROWS):
        kb0 = min(max(r0 - NA_KR // 2, 0), rows - NA_BAND)
        qi = np.arange(NA_QROWS * GRID_W)
        qr, qc = r0 + qi // GRID_W, qi % GRID_W
        kj = np.arange(NA_BAND * GRID_W)
        kr, kc = kb0 + kj // GRID_W, kj % GRID_W
        rs = np.clip(qr - NA_KR // 2, 0, rows - NA_KR)
        cs = np.clip(qc - NA_KC // 2, 0, GRID_W - NA_KC)
        valid = ((kr[None, :] >= rs[:, None]) & (kr[None, :] < rs[:, None] + NA_KR)
                 & (kc[None, :] >= cs[:, None]) & (kc[None, :] < cs[:, None] + NA_KC))
        parts = []
        for q_row in range(NA_QROWS):
            a0 = kb0 - (r0 + q_row) + NA_KR - 1 + NA_BAND
            blk = toep[:, a0:a0 + NA_BAND]
            parts.append(jnp.transpose(blk, (0, 2, 1, 3)).reshape(nh, GRID_W, NA_BAND * GRID_W))
        tile = jnp.concatenate(parts, axis=1)
        tiles.append(jnp.where(jnp.asarray(valid)[None], tile, NEG_BIG))
    return jnp.stack(tiles)


def _ctx_attn_body(q_ref, k_ref, v_ref, o_ref):
    q = q_ref[0]
    k = k_ref[0]
    v = v_ref[0]
    lane = lax.broadcasted_iota(jnp.int32, q.shape, 1)
    outs = []
    for hh in range(2):
        qh = jnp.where((lane < HEAD_DIM) if hh == 0 else (lane >= HEAD_DIM), q, jnp.zeros_like(q))
        s = _dot_nt(qh, k)
        m = jnp.max(s, axis=-1, keepdims=True)
        p = jnp.exp(s - m)
        outs.append(_dot(p.astype(BF16), v) / jnp.sum(p, axis=-1, keepdims=True))
    o_ref[0] = jnp.where(lane < HEAD_DIM, outs[0], outs[1]).astype(BF16)


def _ctx_attn_call(nq, nk, nv, *, t_lat):
    b, tall, w = nq.shape
    tc = tall - t_lat
    ctx_blk = t_lat // tc
    spec = pl.BlockSpec((1, tc, LANES), lambda bi, hp: (bi, ctx_blk, hp))
    return pl.pallas_call(
        _ctx_attn_body,
        grid=(b, w // LANES),
        in_specs=[spec, spec, spec],
        out_specs=pl.BlockSpec((1, tc, LANES), lambda bi, hp: (bi, 0, hp)),
        out_shape=jax.ShapeDtypeStruct((b, tc, w), BF16),
        compiler_params=_cparams(("parallel", "parallel")),
        name="ctx_attn",
    )(nq, nk, nv)


def _merge_body(x_ref, yd_ref, of_ref, ob_ref, sg_ref, yn_ref, gt_ref, wd_ref, wg_ref, wn_ref, wo_ref,
                gn_ref, mb_ref, mc_ref, o_ref, *, t_lat, tm, d):
    r = pl.program_id(1)
    ctx = _is_ctx_rows(r, tm, t_lat)
    g1 = jnp.where(ctx, mc_ref[0, :, 2 * d:3 * d], mb_ref[0, :, 2 * d:3 * d])
    og = of_ref[0] + ob_ref[0]
    parts = []
    for hd in range(GLA_HEADS):
        oh = og[:, hd * GLA_DV:(hd + 1) * GLA_DV]
        ms = jnp.mean(oh * oh, axis=-1, keepdims=True)
        parts.append(oh * lax.rsqrt(ms + EPS) * gn_ref[...])
    yg = (jnp.concatenate(parts, axis=-1) * sg_ref[0].astype(F32)).astype(BF16)
    m = (gt_ref[0, :, 0:d].astype(F32) * _dot(yd_ref[0], wd_ref[...])
         + gt_ref[0, :, d:2 * d].astype(F32) * _dot(yg, wg_ref[...])
         + gt_ref[0, :, 2 * d:3 * d].astype(F32) * _dot(yn_ref[0], wn_ref[...]))
    o_ref[0] = x_ref[0] + g1 * _dot(m.astype(BF16), wo_ref[...])


def _merge_call(xall, yd, of, ob, sg, yn, gates, wd, wg, wn, wo, gn, mod3, t_lat):
    b, tall, d = xall.shape
    tm = _row_tile(tall)
    nb = b
    const = lambda bi, r: (0, 0)
    row = lambda n: pl.BlockSpec((1, tm, n), lambda bi, r: (bi, r, 0))
    return pl.pallas_call(
        functools.partial(_merge_body, t_lat=t_lat, tm=tm, d=d),
        grid=(b, tall // tm),
        in_specs=[row(d), row(BR_W), row(BR_W), row(BR_W), row(BR_W), row(BR_W), row(3 * d),
                  pl.BlockSpec((BR_W, d), const), pl.BlockSpec((BR_W, d), const), pl.BlockSpec((BR_W, d), const),
                  pl.BlockSpec((d, d), const), pl.BlockSpec((1, GLA_DV), const),
                  pl.BlockSpec((1, 1, 6 * d), lambda bi, r: (bi, 0, 0)),
                  pl.BlockSpec((1, 1, 6 * d), lambda bi, r: (nb, 0, 0))],
        out_specs=row(d),
        out_shape=jax.ShapeDtypeStruct((b, tall, d), F32),
        compiler_params=_cparams(("parallel", "parallel")),
        name="merge",
    )(xall, yd, of, ob, sg, yn, gates, wd, wg, wn, wo, gn, mod3, mod3)


def _mlp_body(x_ref, g_ref, w1_ref, w2_ref, mb_ref, mc_ref, o_ref, *, t_lat, tm, d):
    r = pl.program_id(1)
    ctx = _is_ctx_rows(r, tm, t_lat)
    sh = jnp.where(ctx, mc_ref[0, :, 3 * d:4 * d], mb_ref[0, :, 3 * d:4 * d])
    sc = jnp.where(ctx, mc_ref[0, :, 4 * d:5 * d], mb_ref[0, :, 4 * d:5 * d])
    g2 = jnp.where(ctx, mc_ref[0, :, 5 * d:6 * d], mb_ref[0, :, 5 * d:6 * d])
    x = x_ref[0]
    h = _norm_mod(x, g_ref[...], sc, sh).astype(BF16)
    acc = jnp.zeros((tm, d), F32)
    step = 1024
    for j in range(w1_ref.shape[1] // step):
        a = jnp.maximum(_dot(h, w1_ref[:, j * step:(j + 1) * step]), 0.0)
        acc = acc + _dot((a * a).astype(BF16), w2_ref[j * step:(j + 1) * step, :])
    o_ref[0] = x + g2 * acc


def _mlp_call(xall, g, w1, w2, mod3, t_lat):
    b, tall, d = xall.shape
    tm = _row_tile(tall)
    nb = b
    dff = w1.shape[1]
    const = lambda bi, r: (0, 0)
    return pl.pallas_call(
        functools.partial(_mlp_body, t_lat=t_lat, tm=tm, d=d),
        grid=(b, tall // tm),
        in_specs=[pl.BlockSpec((1, tm, d), lambda bi, r: (bi, r, 0)),
                  pl.BlockSpec((1, d), const),
                  pl.BlockSpec((d, dff), const, pipeline_mode=pl.Buffered(1)),
                  pl.BlockSpec((dff, d), const, pipeline_mode=pl.Buffered(1)),
                  pl.BlockSpec((1, 1, 6 * d), lambda bi, r: (bi, 0, 0)),
                  pl.BlockSpec((1, 1, 6 * d), lambda bi, r: (nb, 0, 0))],
        out_specs=pl.BlockSpec((1, tm, d), lambda bi, r: (bi, r, 0)),
        out_shape=jax.ShapeDtypeStruct((b, tall, d), F32),
        compiler_params=_cparams(("parallel", "parallel")),
        name="mlp",
    )(xall, g.reshape(1, d), w1, w2, mod3, mod3)


def kernel(x, c, ctx, c_ctx, w_mod, b_mod, norm1_g, norm2_g, w_in, da_qn_g, da_kn_g, da_lambda, da_subln_g,
           gla_a2, gla_a_b, gla_gn_g, na_qn_g, na_kn_g, na_rpb, w_br_da, w_br_gla, w_br_na, w_out, w_ff1, w_ff2):
    b, t_lat, d = x.shape
    tc = ctx.shape[1]
    tall = t_lat + tc
    depth = w_mod.shape[0]
    rows = t_lat // GRID_W
    assert d == D_MODEL and t_lat % (2 * tc) == 0 and tc == 256 and b < 8

    xall = jnp.concatenate([x, ctx], axis=1)
    cvec = jnp.zeros((8, d), F32).at[0:b].set(c).at[b].set(c_ctx)
    mod = _mod_call(cvec, w_mod, b_mod)
    cos_t, sin_t = _rope_tables(t_lat, tall)

    seg = np.arange(LANES) // HEAD_DIM
    bd = jnp.asarray(seg[:, None] == seg[None, :], BF16)
    ci = np.arange(GLA_CHUNK)
    tri_f = jnp.asarray(ci[None, :] <= ci[:, None], BF16)
    tri_b = jnp.asarray(ci[None, :] >= ci[:, None], BF16)

    o_dq, o_dk, o_dv = 0, 512, 1024
    o_gq, o_gk, o_gv, o_gg, o_ga = 1536, 1792, 2048, 2560, 3072
    o_nq, o_nk, o_nv = 3104, 3616, 4128
    o_gate = 4640
    tq_da = 512 if t_lat % 512 == 0 else 256
    tk_da = 1280 if tall % 1280 == 0 else 256

    for l in range(depth):
        need_ctx = l < depth - 1
        lam_init = 0.8 - 0.6 * math.exp(-0.3 * l)
        mod3 = mod[l].reshape(8, 1, 6 * d)
        wl = w_in[l].astype(BF16)
        cut = lambda a, n: wl[:, a:a + n]
        wa = jnp.concatenate([cut(o_ga, 2 * GLA_RANK), jnp.zeros((d, LANES - 2 * GLA_RANK), BF16)], axis=1)
        a2p = jnp.zeros((LANES, 2 * GLA_HEADS * GLA_DK), F32)
        a2p = a2p.at[0:GLA_RANK, 0:256].set(gla_a2[l, 0]).at[GLA_RANK:2 * GLA_RANK, 256:512].set(gla_a2[l, 1])
        ab = gla_a_b[l].reshape(1, 2 * GLA_HEADS * GLA_DK)
        tile2 = lambda g: jnp.tile(g, LANES // HEAD_DIM).reshape(1, LANES)

        h = _hnorm_call(xall, norm1_g[l], mod3, t_lat)
        qt, kk, vt, nrm = _proj_da_call(h, cut(o_dq, 512), cut(o_dk, 512), cut(o_dv, 512), bd,
                                        tile2(da_qn_g[l]), tile2(da_kn_g[l]), cos_t, sin_t)
        bound = jnp.sqrt(jnp.max(nrm[:, 0, :]) * jnp.max(nrm[:, 1, :])) * 1.01 + 1e-3
        safe = jnp.where(bound <= DA_SAFE_LOG2, bound, -1.0).astype(F32).reshape(1)
        gq, gk, gv, sgg, la = _proj_gla_call(h, cut(o_gq, 256), cut(o_gk, 256), cut(o_gv, 512), cut(o_gg, 512),
                                             wa, a2p.astype(BF16), ab)
        nq, nk, nv = _proj_na_call(h, cut(o_nq, 512), cut(o_nk, 512), cut(o_nv, 512), bd,
                                   tile2(na_qn_g[l]), tile2(na_kn_g[l]))
        gates = _proj_gate_call(h, cut(o_gate, 3 * d))

        subg = da_subln_g[l].reshape(LANES, 1)
        y_da = _da_call(safe, qt, kk, vt, da_lambda[l], subg, lam_init,
                        q_off=0, nq=t_lat // tq_da, k_off=0, nk=tall // tk_da, tq=tq_da, tk=tk_da)
        o_f = _gla_call(gq, gk, gv, la, tri_f, t_lat=t_lat, reverse=False)
        o_b = _gla_call(gq, gk, gv, la, tri_b, t_lat=t_lat, reverse=True)
        bias = _na_bias_tiles(na_rpb[l], rows)
        y_na = _na_call(nq, nk, nv, bias, t_lat=t_lat)
        if need_ctx:
            y_da_c = _da_call(safe, qt, kk, vt, da_lambda[l], subg, lam_init,
                              q_off=t_lat // tc, nq=1, k_off=t_lat // tc, nk=1, tq=tc, tk=tc)
            y_na_c = _ctx_attn_call(nq, nk, nv, t_lat=t_lat)
        else:
            y_da_c = jnp.zeros((b, tc, BR_W), BF16)
            y_na_c = jnp.zeros((b, tc, BR_W), BF16)
        y_da = jnp.concatenate([y_da, y_da_c], axis=1)
        y_na = jnp.concatenate([y_na, y_na_c], axis=1)

        xall = _merge_call(xall, y_da, o_f, o_b, sgg, y_na, gates,
                           w_br_da[l].astype(BF16), w_br_gla[l].astype(BF16), w_br_na[l].astype(BF16),
                           w_out[l].astype(BF16), gla_gn_g[l].reshape(1, GLA_DV), mod3, t_lat)
        xall = _mlp_call(xall, norm2_g[l], w_ff1[l].astype(BF16), w_ff2[l].astype(BF16), mod3, t_lat)
    return xall[:, :t_lat]
```

```python
import functools
import math

import numpy as np
import jax
import jax.numpy as jnp
from jax import lax
from jax.experimental import pallas as pl
from jax.experimental.pallas import tpu as pltpu

F32 = jnp.float32
BF16 = jnp.bfloat16

D_MODEL = 1024
GRID_W = 64
HEAD_DIM = 64
EPS = 1e-6
ROPE_BASE = 10000.0
ROPE_PAIRS = HEAD_DIM // 4
DA_HEADS = 4
GLA_HEADS = 4
GLA_DK = 64
GLA_DV = 128
GLA_RANK = 16
GLA_TAU = 16.0
GLA_CHUNK = 64
NA_HEADS = 8
NA_KR = 8
NA_KC = 16
BR_W = 512
D_FF = 4 * D_MODEL

DA_V = 2 * HEAD_DIM
DA_VPAD = 16
DA_VROWS = DA_V + DA_VPAD
LOG2E = math.log2(math.e)
DA_SAFE_LOG2 = 45.0

LANES = 128
NA_QROWS = 4
NA_BAND = 16
NEG_BIG = -1e30
VMEM_LIMIT = 48 * 1024 * 1024


def _cparams(sem):
    return pltpu.CompilerParams(dimension_semantics=sem, vmem_limit_bytes=VMEM_LIMIT)


def _dot(a, b):
    return jnp.dot(a, b, preferred_element_type=F32)


def _dot_nt(a, b):
    return lax.dot_general(a, b, (((1,), (1,)), ((), ())), preferred_element_type=F32)


def _dot_tn(a, b):
    return lax.dot_general(a, b, (((0,), (0,)), ((), ())), preferred_element_type=F32)


def _split_dot(x, w_exact):
    hi = x.astype(BF16)
    lo = (x - hi.astype(F32)).astype(BF16)
    return _dot(hi, w_exact) + _dot(lo, w_exact)


def _split_dot_left(w_exact, x):
    hi = x.astype(BF16)
    lo = (x - hi.astype(F32)).astype(BF16)
    return _dot(w_exact, hi) + _dot(w_exact, lo)


def _sigmoid(x):
    return 1.0 / (1.0 + jnp.exp(-x))


def _row_tile(tall):
    for tm in (640, 256):
        if tall % tm == 0:
            return tm
    raise ValueError(f"unsupported token count {tall}")


def _mod_body(c_ref, w_ref, b_ref, o_ref):
    cv = c_ref[...]
    s = cv * _sigmoid(cv)
    o_ref[0] = _dot(s.astype(BF16), w_ref[0].astype(BF16)) + b_ref[0]


def _mod_call(cvec, w_mod, b_mod):
    depth, d, n = w_mod.shape
    tn = 1536
    return pl.pallas_call(
        _mod_body,
        grid=(depth, n // tn),
        in_specs=[pl.BlockSpec((8, d), lambda l, j: (0, 0)),
                  pl.BlockSpec((1, d, tn), lambda l, j: (l, 0, j)),
                  pl.BlockSpec((1, 1, tn), lambda l, j: (l, 0, j))],
        out_specs=pl.BlockSpec((1, 8, tn), lambda l, j: (l, 0, j)),
        out_shape=jax.ShapeDtypeStruct((depth, 8, n), F32),
        compiler_params=_cparams(("parallel", "parallel")),
        name="mod",
    )(cvec, w_mod, b_mod.reshape(depth, 1, n))


def _rope_body(freq_ref, cos_ref, sin_ref, *, t_lat, tm):
    i = pl.program_id(0)
    t = i * tm + lax.broadcasted_iota(jnp.int32, (tm, LANES), 0)
    lane = lax.broadcasted_iota(jnp.int32, (tm, LANES), 1)
    shift = int(math.log2(GRID_W))
    row = lax.shift_right_logical(t, shift).astype(F32)
    col = jnp.bitwise_and(t, GRID_W - 1).astype(F32)
    use_row = jnp.bitwise_and(lane, 2 * ROPE_PAIRS) == 0
    first_half = jnp.bitwise_and(lane, ROPE_PAIRS) == 0
    ang = jnp.where(use_row, row, col) * freq_ref[...]
    is_lat = t < t_lat
    cos_ref[...] = jnp.where(is_lat, jnp.cos(ang), 1.0)
    sn = jnp.sin(ang)
    sin_ref[...] = jnp.where(is_lat, jnp.where(first_half, -sn, sn), 0.0)


def _rope_tables(t_lat, tall):
    tm = 256
    freqs = ROPE_BASE ** (-jnp.arange(ROPE_PAIRS, dtype=F32) / ROPE_PAIRS)
    freq_lane = jnp.tile(freqs, LANES // ROPE_PAIRS).reshape(1, LANES)
    return pl.pallas_call(
        functools.partial(_rope_body, t_lat=t_lat, tm=tm),
        grid=(tall // tm,),
        in_specs=[pl.BlockSpec((1, LANES), lambda i: (0, 0))],
        out_specs=[pl.BlockSpec((tm, LANES), lambda i: (i, 0))] * 2,
        out_shape=[jax.ShapeDtypeStruct((tall, LANES), F32)] * 2,
        compiler_params=_cparams(("parallel",)),
        name="rope_tables",
    )(freq_lane)


def _is_ctx_rows(r, tm, t_lat):
    t = r * tm + lax.broadcasted_iota(jnp.int32, (tm, 1), 0)
    return t >= t_lat


def _norm_mod(x, g, sc, sh):
    ms = jnp.mean(x * x, axis=-1, keepdims=True)
    return (x * lax.rsqrt(ms + EPS) * g) * (1.0 + sc) + sh


def _segnorm64(z, bd, gain):
    ss = _split_dot(z * z, bd)
    return z * lax.rsqrt(ss * (1.0 / HEAD_DIM) + EPS) * gain


def _hnorm_body(x_ref, g_ref, mb_ref, mc_ref, h_ref, *, t_lat, tm, d):
    r = pl.program_id(1)
    ctx = _is_ctx_rows(r, tm, t_lat)
    sh = jnp.where(ctx, mc_ref[0, :, 0:d], mb_ref[0, :, 0:d])
    sc = jnp.where(ctx, mc_ref[0, :, d:2 * d], mb_ref[0, :, d:2 * d])
    h_ref[0] = _norm_mod(x_ref[0], g_ref[...], sc, sh).astype(BF16)


def _hnorm_call(xall, g, mod3, t_lat):
    b, tall, d = xall.shape
    tm = _row_tile(tall)
    nb = b
    return pl.pallas_call(
        functools.partial(_hnorm_body, t_lat=t_lat, tm=tm, d=d),
        grid=(b, tall // tm),
        in_specs=[pl.BlockSpec((1, tm, d), lambda bi, r: (bi, r, 0)),
                  pl.BlockSpec((1, d), lambda bi, r: (0, 0)),
                  pl.BlockSpec((1, 1, 6 * d), lambda bi, r: (bi, 0, 0)),
                  pl.BlockSpec((1, 1, 6 * d), lambda bi, r: (nb, 0, 0))],
        out_specs=pl.BlockSpec((1, tm, d), lambda bi, r: (bi, r, 0)),
        out_shape=jax.ShapeDtypeStruct((b, tall, d), BF16),
        compiler_params=_cparams(("parallel", "parallel")),
        name="hnorm",
    )(xall, g.reshape(1, d), mod3, mod3)


def _rope128(x, cs, sn, first_half):
    partner = jnp.where(first_half, pltpu.roll(x, LANES - ROPE_PAIRS, 1), pltpu.roll(x, ROPE_PAIRS, 1))
    return x * cs + partner * sn


def _proj_da_body(h_ref, wq_ref, wk_ref, wv_ref, bd_ref, gq_ref, gk_ref, cos_ref, sin_ref,
                  qt_ref, k_ref, vt_ref, nrm_ref):
    h = h_ref[0]
    tm = h.shape[0]
    bd = bd_ref[...]
    cs = cos_ref[...]
    sn = sin_ref[...]
    lane = lax.broadcasted_iota(jnp.int32, cs.shape, 1)
    first_half = jnp.bitwise_and(lane, ROPE_PAIRS) == 0
    zq = _dot(h, wq_ref[...])
    zk = _dot(h, wk_ref[...])
    zv = _dot(h, wv_ref[...])
    scale = HEAD_DIM ** -0.5 * LOG2E
    ones_row = (lax.broadcasted_iota(jnp.int32, (DA_VPAD, tm), 0) == 0).astype(BF16)
    qn2 = jnp.zeros((1, LANES), F32)
    kn2 = jnp.zeros((1, LANES), F32)
    for hd in range(DA_HEADS):
        sl = slice(hd * LANES, (hd + 1) * LANES)
        q = _segnorm64(zq[:, sl], bd, gq_ref[...]) * scale
        q = _rope128(q, cs, sn, first_half)
        qt_ref[0, sl, :] = q.T.astype(BF16)
        k = _segnorm64(zk[:, sl], bd, gk_ref[...])
        k = _rope128(k, cs, sn, first_half)
        k_ref[0, :, sl] = k.astype(BF16)
        qn2 = jnp.maximum(qn2, jnp.max(_split_dot(q * q, bd), axis=0, keepdims=True))
        kn2 = jnp.maximum(kn2, jnp.max(_split_dot(k * k, bd), axis=0, keepdims=True))
        v0 = hd * DA_VROWS
        vt_ref[0, v0:v0 + DA_V, :] = zv[:, sl].T.astype(BF16)
        vt_ref[0, v0 + DA_V:v0 + DA_VROWS, :] = ones_row
    nrm_ref[0, 0:1, :] = qn2
    nrm_ref[0, 1:2, :] = kn2


def _proj_da_call(h, wq, wk, wv, bd, gq, gk, cos_t, sin_t):
    b, tall, d = h.shape
    tm = _row_tile(tall)
    nt = tall // tm
    w = DA_HEADS * LANES
    const = lambda bi, r: (0, 0)
    return pl.pallas_call(
        _proj_da_body,
        grid=(b, tall // tm),
        in_specs=[pl.BlockSpec((1, tm, d), lambda bi, r: (bi, r, 0)),
                  pl.BlockSpec((d, w), const), pl.BlockSpec((d, w), const), pl.BlockSpec((d, w), const),
                  pl.BlockSpec((LANES, LANES), const),
                  pl.BlockSpec((1, LANES), const), pl.BlockSpec((1, LANES), const),
                  pl.BlockSpec((tm, LANES), lambda bi, r: (r, 0)),
                  pl.BlockSpec((tm, LANES), lambda bi, r: (r, 0))],
        out_specs=[pl.BlockSpec((1, w, tm), lambda bi, r: (bi, 0, r)),
                   pl.BlockSpec((1, tm, w), lambda bi, r: (bi, r, 0)),
                   pl.BlockSpec((1, DA_HEADS * DA_VROWS, tm), lambda bi, r: (bi, 0, r)),
                   pl.BlockSpec((1, 2, LANES), lambda bi, r: (bi * nt + r, 0, 0))],
        out_shape=[jax.ShapeDtypeStruct((b, w, tall), BF16),
                   jax.ShapeDtypeStruct((b, tall, w), BF16),
                   jax.ShapeDtypeStruct((b, DA_HEADS * DA_VROWS, tall), BF16),
                   jax.ShapeDtypeStruct((b * nt, 2, LANES), F32)],
        compiler_params=_cparams(("parallel", "parallel")),
        name="proj_da",
    )(h, wq, wk, wv, bd, gq, gk, cos_t, sin_t)


def _proj_gla_body(h_ref, wq_ref, wk_ref, wv_ref, wg_ref, wa_ref, a2_ref, ab_ref,
                   q_ref, k_ref, v_ref, sg_ref, la_ref):
    h = h_ref[0]
    q_ref[0] = _dot(h, wq_ref[...]) * (GLA_DK ** -0.5)
    k_ref[0] = _dot(h, wk_ref[...])
    v_ref[0] = _dot(h, wv_ref[...]).astype(BF16)
    g = _dot(h, wg_ref[...])
    sg_ref[0] = (g * _sigmoid(g)).astype(BF16)
    ga = _dot(h, wa_ref[...])
    z = _dot(ga.astype(BF16), a2_ref[...]) + ab_ref[...]
    la_ref[0] = (jnp.minimum(z, 0.0) - jnp.log1p(jnp.exp(-jnp.abs(z)))) * (1.0 / GLA_TAU)


def _proj_gla_call(h, wq, wk, wv, wg, wa, a2p, ab):
    b, tall, d = h.shape
    tm = _row_tile(tall)
    wqk = GLA_HEADS * GLA_DK
    wv_ = GLA_HEADS * GLA_DV
    const = lambda bi, r: (0, 0)
    row = lambda n: pl.BlockSpec((1, tm, n), lambda bi, r: (bi, r, 0))
    return pl.pallas_call(
        _proj_gla_body,
        grid=(b, tall // tm),
        in_specs=[row(d),
                  pl.BlockSpec((d, wqk), const), pl.BlockSpec((d, wqk), const),
                  pl.BlockSpec((d, wv_), const), pl.BlockSpec((d, wv_), const),
                  pl.BlockSpec((d, LANES), const),
                  pl.BlockSpec((LANES, 2 * wqk), const), pl.BlockSpec((1, 2 * wqk), const)],
        out_specs=[row(wqk), row(wqk), row(wv_), row(wv_), row(2 * wqk)],
        out_shape=[jax.ShapeDtypeStruct((b, tall, wqk), F32),
                   jax.ShapeDtypeStruct((b, tall, wqk), F32),
                   jax.ShapeDtypeStruct((b, tall, wv_), BF16),
                   jax.ShapeDtypeStruct((b, tall, wv_), BF16),
                   jax.ShapeDtypeStruct((b, tall, 2 * wqk), F32)],
        compiler_params=_cparams(("parallel", "parallel")),
        name="proj_gla",
    )(h, wq, wk, wv, wg, wa, a2p, ab)


def _proj_na_body(h_ref, wq_ref, wk_ref, wv_ref, bd_ref, gq_ref, gk_ref, q_ref, k_ref, v_ref):
    h = h_ref[0]
    bd = bd_ref[...]
    zq = _dot(h, wq_ref[...])
    zk = _dot(h, wk_ref[...])
    scale = HEAD_DIM ** -0.5
    for j in range(NA_HEADS * HEAD_DIM // LANES):
        sl = slice(j * LANES, (j + 1) * LANES)
        q_ref[0, :, sl] = (_segnorm64(zq[:, sl], bd, gq_ref[...]) * scale).astype(BF16)
        k_ref[0, :, sl] = _segnorm64(zk[:, sl], bd, gk_ref[...]).astype(BF16)
    v_ref[0] = _dot(h, wv_ref[...]).astype(BF16)


def _proj_na_call(h, wq, wk, wv, bd, gq, gk):
    b, tall, d = h.shape
    tm = _row_tile(tall)
    w = NA_HEADS * HEAD_DIM
    const = lambda bi, r: (0, 0)
    row = lambda n: pl.BlockSpec((1, tm, n), lambda bi, r: (bi, r, 0))
    return pl.pallas_call(
        _proj_na_body,
        grid=(b, tall // tm),
        in_specs=[row(d), pl.BlockSpec((d, w), const), pl.BlockSpec((d, w), const), pl.BlockSpec((d, w), const),
                  pl.BlockSpec((LANES, LANES), const),
                  pl.BlockSpec((1, LANES), const), pl.BlockSpec((1, LANES), const)],
        out_specs=[row(w), row(w), row(w)],
        out_shape=[jax.ShapeDtypeStruct((b, tall, w), BF16)] * 3,
        compiler_params=_cparams(("parallel", "parallel")),
        name="proj_na",
    )(h, wq, wk, wv, bd, gq, gk)


def _proj_gate_body(h_ref, w_ref, o_ref):
    h = h_ref[0]
    n = w_ref.shape[1]
    step = 512
    for j in range(n // step):
        sl = slice(j * step, (j + 1) * step)
        o_ref[0, :, sl] = _sigmoid(_dot(h, w_ref[:, sl])).astype(BF16)


def _proj_gate_call(h, w):
    b, tall, d = h.shape
    tm = _row_tile(tall)
    n = w.shape[1]
    return pl.pallas_call(
        _proj_gate_body,
        grid=(b, tall // tm),
        in_specs=[pl.BlockSpec((1, tm, d), lambda bi, r: (bi, r, 0)),
                  pl.BlockSpec((d, n), lambda bi, r: (0, 0))],
        out_specs=pl.BlockSpec((1, tm, n), lambda bi, r: (bi, r, 0)),
        out_shape=jax.ShapeDtypeStruct((b, tall, n), BF16),
        compiler_params=_cparams(("parallel", "parallel")),
        name="proj_gate",
    )(h, w)


def _da_body(safe_ref, qt_ref, k_ref, vt_ref, lp_ref, sg_ref, o_ref, m_ref, a_ref, acc_ref, *, lam_init, nk):
    j = pl.program_id(3)
    use_lag = safe_ref[0] > 0.0

    @pl.when(j == 0)
    def _():
        m_ref[...] = jnp.full(m_ref.shape, jnp.where(use_lag, -safe_ref[0], -jnp.inf), F32)
        a_ref[...] = jnp.ones(a_ref.shape, F32)
        acc_ref[...] = jnp.zeros(acc_ref.shape, F32)

    def step(lagged):
        qt = qt_ref[0]
        kk = k_ref[0]
        vt = vt_ref[0]
        rowid = lax.broadcasted_iota(jnp.int32, qt.shape, 0)
        zero = jnp.zeros_like(qt)
        for a in range(2):
            w = jnp.where((rowid < HEAD_DIM) if a == 0 else (rowid >= HEAD_DIM), qt, zero)
            s = _dot(kk, w)
            m_old = m_ref[a:a + 1, :]
            m_new = jnp.maximum(m_old, jnp.max(s, axis=0, keepdims=True))
            owed = a_ref[a:a + 1, :]
            if lagged:
                p = jnp.exp2(s - m_old)
                acc_ref[a] = owed * acc_ref[a] + _dot(vt, p.astype(BF16))
                a_ref[a:a + 1, :] = jnp.exp2(m_old - m_new)
            else:
                p = jnp.exp2(s - m_new)
                acc_ref[a] = (owed * jnp.exp2(m_old - m_new)) * acc_ref[a] + _dot(vt, p.astype(BF16))
                a_ref[a:a + 1, :] = jnp.ones_like(owed)
            m_ref[a:a + 1, :] = m_new

    pl.when(use_lag)(functools.partial(step, True))
    pl.when(jnp.logical_not(use_lag))(functools.partial(step, False))

    @pl.when(j == nk - 1)
    def _():
        lp = lp_ref[...]
        e1 = jnp.exp(jnp.sum(lp[0:1] * lp[1:2], axis=-1, keepdims=True))
        e2 = jnp.exp(jnp.sum(lp[2:3] * lp[3:4], axis=-1, keepdims=True))
        lam = e1 - e2 + lam_init
        o1 = acc_ref[0, 0:DA_V, :] / acc_ref[0, DA_V:DA_V + 1, :]
        o2 = acc_ref[1, 0:DA_V, :] / acc_ref[1, DA_V:DA_V + 1, :]
        o = o1 - lam * o2
        ms = jnp.mean(o * o, axis=0, keepdims=True)
        y = (o * lax.rsqrt(ms + EPS) * sg_ref[...]) * (1.0 - lam_init)
        o_ref[0] = y.T.astype(BF16)


def _da_call(safe, qt, kk, vt, lp, subg, lam_init, *, q_off, nq, k_off, nk, tq, tk):
    b = qt.shape[0]
    return pl.pallas_call(
        functools.partial(_da_body, lam_init=lam_init, nk=nk),
        grid=(b, DA_HEADS, nq, nk),
        in_specs=[pl.BlockSpec(memory_space=pltpu.SMEM),
                  pl.BlockSpec((1, LANES, tq), lambda bi, h, i, j: (bi, h, i + q_off)),
                  pl.BlockSpec((1, tk, LANES), lambda bi, h, i, j: (bi, j + k_off, h)),
                  pl.BlockSpec((1, DA_VROWS, tk), lambda bi, h, i, j: (bi, h, j + k_off)),
                  pl.BlockSpec((4, HEAD_DIM), lambda bi, h, i, j: (0, 0)),
                  pl.BlockSpec((LANES, 1), lambda bi, h, i, j: (0, 0))],
        out_specs=pl.BlockSpec((1, tq, LANES), lambda bi, h, i, j: (bi, i, h)),
        out_shape=jax.ShapeDtypeStruct((b, nq * tq, DA_HEADS * LANES), BF16),
        scratch_shapes=[pltpu.VMEM((8, tq), F32), pltpu.VMEM((8, tq), F32),
                        pltpu.VMEM((2, DA_VROWS, tq), F32)],
        compiler_params=_cparams(("parallel", "parallel", "parallel", "arbitrary")),
        name="diff_attn",
    )(safe, qt, kk, vt, lp, subg)


def _gla_body(q_ref, k_ref, v_ref, la_ref, tri_ref, o_ref, s_ref, *, reverse, tb):
    i = pl.program_id(1)

    @pl.when(i == 0)
    def _():
        s_ref[...] = jnp.zeros(s_ref.shape, F32)

    c = GLA_CHUNK
    w = GLA_HEADS * GLA_DK
    tri = tri_ref[...]
    ri = lax.broadcasted_iota(jnp.int32, (c, c), 0)
    ci = lax.broadcasted_iota(jnp.int32, (c, c), 1)
    keep = (ci >= ri) if reverse else (ci <= ri)
    lane = lax.broadcasted_iota(jnp.int32, (1, w), 1)
    chunks = range(tb // c)
    for ch in (reversed(chunks) if reverse else chunks):
        rows = slice(ch * c, (ch + 1) * c)
        la = la_ref[0, rows, :]
        cum = _split_dot_left(tri, la)
        cl = cum[0:1, :] if reverse else cum[c - 1:c, :]
        q = q_ref[0, rows, :]
        k = k_ref[0, rows, :]
        qe = q * jnp.exp(cum)
        ke = (k * jnp.exp(-cum)).astype(BF16)
        kd = (k * jnp.exp(cl - cum)).astype(BF16)
        dec = jnp.exp(cl)
        for hd in range(GLA_HEADS):
            hm = (lane >= hd * GLA_DK) & (lane < (hd + 1) * GLA_DK)
            qh = jnp.where(hm, qe, 0.0).astype(BF16)
            a = jnp.where(keep, _dot_nt(qh, ke), 0.0)
            vh = v_ref[0, rows, hd * GLA_DV:(hd + 1) * GLA_DV]
            st = s_ref[hd]
            o = _dot(a.astype(BF16), vh) + _dot_nt(qh, st.astype(BF16))
            o_ref[0, rows, hd * GLA_DV:(hd + 1) * GLA_DV] = o
            s_ref[hd] = st * dec + _dot_tn(vh, kd)


def _gla_call(gq, gk, gv, la, tri, *, t_lat, reverse):
    b, tall, w = gq.shape
    tb = 256
    n_lat = t_lat // tb
    nblk = tall // tb
    if reverse:
        blk = lambda i: jnp.where(i == 0, n_lat, n_lat - i)
    else:
        blk = lambda i: jnp.where(i == 0, n_lat, i - 1)
    wv_ = GLA_HEADS * GLA_DV
    return pl.pallas_call(
        functools.partial(_gla_body, reverse=reverse, tb=tb),
        grid=(b, nblk),
        in_specs=[pl.BlockSpec((1, tb, w), lambda bi, i: (bi, blk(i), 0)),
                  pl.BlockSpec((1, tb, w), lambda bi, i: (bi, blk(i), 0)),
                  pl.BlockSpec((1, tb, wv_), lambda bi, i: (bi, blk(i), 0)),
                  pl.BlockSpec((1, tb, w), lambda bi, i: (bi, blk(i), 1 if reverse else 0)),
                  pl.BlockSpec((GLA_CHUNK, GLA_CHUNK), lambda bi, i: (0, 0))],
        out_specs=pl.BlockSpec((1, tb, wv_), lambda bi, i: (bi, blk(i), 0)),
        out_shape=jax.ShapeDtypeStruct((b, tall, wv_), F32),
        scratch_shapes=[pltpu.VMEM((GLA_HEADS, GLA_DV, w), F32)],
        compiler_params=_cparams(("parallel", "arbitrary")),
        name="gla_bwd" if reverse else "gla_fwd",
    )(gq, gk, gv, la, tri)


def _na_body(q_ref, k_ref, v_ref, kc_ref, vc_ref, bias_ref, o_ref, *, rows):
    i = pl.program_id(2)
    kb0 = jnp.clip(i * NA_QROWS - NA_KR // 2, 0, rows - NA_BAND)
    start = pl.multiple_of(kb0 * GRID_W, GRID_W)
    nband = NA_BAND * GRID_W
    q = q_ref[0]
    kb = k_ref[0, pl.ds(start, nband), :]
    vb = v_ref[0, pl.ds(start, nband), :]
    kc = kc_ref[0]
    vc = vc_ref[0]
    lane = lax.broadcasted_iota(jnp.int32, q.shape, 1)
    outs = []
    for hh in range(2):
        qh = jnp.where((lane < HEAD_DIM) if hh == 0 else (lane >= HEAD_DIM), q, jnp.zeros_like(q))
        s_loc = _dot_nt(qh, kb) + bias_ref[0, hh]
        s_ctx = _dot_nt(qh, kc)
        m = jnp.maximum(jnp.max(s_loc, axis=-1, keepdims=True), jnp.max(s_ctx, axis=-1, keepdims=True))
        p_loc = jnp.exp(s_loc - m)
        p_ctx = jnp.exp(s_ctx - m)
        l = jnp.sum(p_loc, axis=-1, keepdims=True) + jnp.sum(p_ctx, axis=-1, keepdims=True)
        outs.append((_dot(p_loc.astype(BF16), vb) + _dot(p_ctx.astype(BF16), vc)) / l)
    o_ref[0] = jnp.where(lane < HEAD_DIM, outs[0], outs[1]).astype(BF16)


def _na_call(nq, nk, nv, bias, *, t_lat):
    b, tall, w = nq.shape
    tc = tall - t_lat
    rows = t_lat // GRID_W
    tq = NA_QROWS * GRID_W
    nsteps = rows // NA_QROWS
    npair = w // LANES
    ctx_blk = t_lat // tc

    def btype(i):
        return jnp.where(i == 0, 0, jnp.where(i == nsteps - 2, 2, jnp.where(i == nsteps - 1, 3, 1)))

    return pl.pallas_call(
        functools.partial(_na_body, rows=rows),
        grid=(b, npair, nsteps),
        in_specs=[pl.BlockSpec((1, tq, LANES), lambda bi, hp, i: (bi, i, hp)),
                  pl.BlockSpec((1, t_lat, LANES), lambda bi, hp, i: (bi, 0, hp)),
                  pl.BlockSpec((1, t_lat, LANES), lambda bi, hp, i: (bi, 0, hp)),
                  pl.BlockSpec((1, tc, LANES), lambda bi, hp, i: (bi, ctx_blk, hp)),
                  pl.BlockSpec((1, tc, LANES), lambda bi, hp, i: (bi, ctx_blk, hp)),
                  pl.BlockSpec((1, 2, tq, NA_BAND * GRID_W), lambda bi, hp, i: (btype(i), hp, 0, 0))],
        out_specs=pl.BlockSpec((1, tq, LANES), lambda bi, hp, i: (bi, i, hp)),
        out_shape=jax.ShapeDtypeStruct((b, t_lat, w), BF16),
        compiler_params=_cparams(("parallel", "parallel", "arbitrary")),
        name="nbr_attn",
    )(nq, nk, nv, nk, nv, bias)


def _na_bias_tiles(rpb, rows):
    assert rows >= NA_BAND and rows % NA_QROWS == 0
    nh, na, nb = rpb.shape
    cidx = np.arange(GRID_W)
    rel_c = cidx[None, :] - cidx[:, None] + NA_KC - 1
    sel = jnp.asarray(rel_c[None] == np.arange(nb)[:, None, None], F32)
    toep = jnp.einsum('hab,bqk->haqk', rpb.astype(F32), sel, precision=lax.Precision.HIGHEST)
    toep = jnp.pad(toep, ((0, 0), (NA_BAND, NA_BAND), (0, 0), (0, 0)))
    tiles = []
    for r0 in (0, NA_QROWS, rows - 2 * NA_QROWS, rows - NA_QROWS):
        kb0 = min(max(r0 - NA_KR // 2, 0), rows - NA_BAND)
        qi = np.arange(NA_QROWS * GRID_W)
        qr, qc = r0 + qi // GRID_W, qi % GRID_W
        kj = np.arange(NA_BAND * GRID_W)
        kr, kc = kb0 + kj // GRID_W, kj % GRID_W
        rs = np.clip(qr - NA_KR // 2, 0, rows - NA_KR)
        cs = np.clip(qc - NA_KC // 2, 0, GRID_W - NA_KC)
        valid = ((kr[None, :] >= rs[:, None]) & (kr[None, :] < rs[:, None] + NA_KR)
                 & (kc[None, :] >= cs[:, None]) & (kc[None, :] < cs[:, None] + NA_KC))
        parts = []
        for q_row in range(NA_QROWS):
            a0 = kb0 - (r0 + q_row) + NA_KR - 1 + NA_BAND
            blk = toep[:, a0:a0 + NA_BAND]
            parts.append(jnp.transpose(blk, (0, 2, 1, 3)).reshape(nh, GRID_W, NA_BAND * GRID_W))
        tile = jnp.concatenate(parts, axis=1)
        tiles.append(jnp.where(jnp.asarray(valid)[None], tile, NEG_BIG))
    return jnp.stack(tiles)


def _ctx_attn_body(q_ref, k_ref, v_ref, o_ref):
    q = q_ref[0]
    k = k_ref[0]
    v = v_ref[0]
    lane = lax.broadcasted_iota(jnp.int32, q.shape, 1)
    outs = []
    for hh in range(2):
        qh = jnp.where((lane < HEAD_DIM) if hh == 0 else (lane >= HEAD_DIM), q, jnp.zeros_like(q))
        s = _dot_nt(qh, k)
        m = jnp.max(s, axis=-1, keepdims=True)
        p = jnp.exp(s - m)
        outs.append(_dot(p.astype(BF16), v) / jnp.sum(p, axis=-1, keepdims=True))
    o_ref[0] = jnp.where(lane < HEAD_DIM, outs[0], outs[1]).astype(BF16)


def _ctx_attn_call(nq, nk, nv, *, t_lat):
    b, tall, w = nq.shape
    tc = tall - t_lat
    ctx_blk = t_lat // tc
    spec = pl.BlockSpec((1, tc, LANES), lambda bi, hp: (bi, ctx_blk, hp))
    return pl.pallas_call(
        _ctx_attn_body,
        grid=(b, w // LANES),
        in_specs=[spec, spec, spec],
        out_specs=pl.BlockSpec((1, tc, LANES), lambda bi, hp: (bi, 0, hp)),
        out_shape=jax.ShapeDtypeStruct((b, tc, w), BF16),
        compiler_params=_cparams(("parallel", "parallel")),
        name="ctx_attn",
    )(nq, nk, nv)


def _merge_body(x_ref, yd_ref, of_ref, ob_ref, sg_ref, yn_ref, gt_ref, wd_ref, wg_ref, wn_ref, wo_ref,
                gn_ref, mb_ref, mc_ref, o_ref, *, t_lat, tm, d):
    r = pl.program_id(1)
    ctx = _is_ctx_rows(r, tm, t_lat)
    g1 = jnp.where(ctx, mc_ref[0, :, 2 * d:3 * d], mb_ref[0, :, 2 * d:3 * d])
    og = of_ref[0] + ob_ref[0]
    parts = []
    for hd in range(GLA_HEADS):
        oh = og[:, hd * GLA_DV:(hd + 1) * GLA_DV]
        ms = jnp.mean(oh * oh, axis=-1, keepdims=True)
        parts.append(oh * lax.rsqrt(ms + EPS) * gn_ref[...])
    yg = (jnp.concatenate(parts, axis=-1) * sg_ref[0].astype(F32)).astype(BF16)
    m = (gt_ref[0, :, 0:d].astype(F32) * _dot(yd_ref[0], wd_ref[...])
         + gt_ref[0, :, d:2 * d].astype(F32) * _dot(yg, wg_ref[...])
         + gt_ref[0, :, 2 * d:3 * d].astype(F32) * _dot(yn_ref[0], wn_ref[...]))
    o_ref[0] = x_ref[0] + g1 * _dot(m.astype(BF16), wo_ref[...])


def _merge_call(xall, yd, of, ob, sg, yn, gates, wd, wg, wn, wo, gn, mod3, t_lat):
    b, tall, d = xall.shape
    tm = _row_tile(tall)
    nb = b
    const = lambda bi, r: (0, 0)
    row = lambda n: pl.BlockSpec((1, tm, n), lambda bi, r: (bi, r, 0))
    return pl.pallas_call(
        functools.partial(_merge_body, t_lat=t_lat, tm=tm, d=d),
        grid=(b, tall // tm),
        in_specs=[row(d), row(BR_W), row(BR_W), row(BR_W), row(BR_W), row(BR_W), row(3 * d),
                  pl.BlockSpec((BR_W, d), const), pl.BlockSpec((BR_W, d), const), pl.BlockSpec((BR_W, d), const),
                  pl.BlockSpec((d, d), const), pl.BlockSpec((1, GLA_DV), const),
                  pl.BlockSpec((1, 1, 6 * d), lambda bi, r: (bi, 0, 0)),
                  pl.BlockSpec((1, 1, 6 * d), lambda bi, r: (nb, 0, 0))],
        out_specs=row(d),
        out_shape=jax.ShapeDtypeStruct((b, tall, d), F32),
        compiler_params=_cparams(("parallel", "parallel")),
        name="merge",
    )(xall, yd, of, ob, sg, yn, gates, wd, wg, wn, wo, gn, mod3, mod3)


def _mlp_body(x_ref, g_ref, w1_ref, w2_ref, mb_ref, mc_ref, o_ref, *, t_lat, tm, d):
    r = pl.program_id(1)
    ctx = _is_ctx_rows(r, tm, t_lat)
    sh = jnp.where(ctx, mc_ref[0, :, 3 * d:4 * d], mb_ref[0, :, 3 * d:4 * d])
    sc = jnp.where(ctx, mc_ref[0, :, 4 * d:5 * d], mb_ref[0, :, 4 * d:5 * d])
    g2 = jnp.where(ctx, mc_ref[0, :, 5 * d:6 * d], mb_ref[0, :, 5 * d:6 * d])
    x = x_ref[0]
    h = _norm_mod(x, g_ref[...], sc, sh).astype(BF16)
    acc = jnp.zeros((tm, d), F32)
    step = 1024
    for j in range(w1_ref.shape[1] // step):
        a = jnp.maximum(_dot(h, w1_ref[:, j * step:(j + 1) * step]), 0.0)
        acc = acc + _dot((a * a).astype(BF16), w2_ref[j * step:(j + 1) * step, :])
    o_ref[0] = x + g2 * acc


def _mlp_call(xall, g, w1, w2, mod3, t_lat):
    b, tall, d = xall.shape
    tm = _row_tile(tall)
    nb = b
    dff = w1.shape[1]
    const = lambda bi, r: (0, 0)
    return pl.pallas_call(
        functools.partial(_mlp_body, t_lat=t_lat, tm=tm, d=d),
        grid=(b, tall // tm),
        in_specs=[pl.BlockSpec((1, tm, d), lambda bi, r: (bi, r, 0)),
                  pl.BlockSpec((1, d), const),
                  pl.BlockSpec((d, dff), const, pipeline_mode=pl.Buffered(1)),
                  pl.BlockSpec((dff, d), const, pipeline_mode=pl.Buffered(1)),
                  pl.BlockSpec((1, 1, 6 * d), lambda bi, r: (bi, 0, 0)),
                  pl.BlockSpec((1, 1, 6 * d), lambda bi, r: (nb, 0, 0))],
        out_specs=pl.BlockSpec((1, tm, d), lambda bi, r: (bi, r, 0)),
        out_shape=jax.ShapeDtypeStruct((b, tall, d), F32),
        compiler_params=_cparams(("parallel", "parallel")),
        name="mlp",
    )(xall, g.reshape(1, d), w1, w2, mod3, mod3)


def kernel(x, c, ctx, c_ctx, w_mod, b_mod, norm1_g, norm2_g, w_in, da_qn_g, da_kn_g, da_lambda, da_subln_g,
           gla_a2, gla_a_b, gla_gn_g, na_qn_g, na_kn_g, na_rpb, w_br_da, w_br_gla, w_br_na, w_out, w_ff1, w_ff2):
    b, t_lat, d = x.shape
    tc = ctx.shape[1]
    tall = t_lat + tc
    depth = w_mod.shape[0]
    rows = t_lat // GRID_W
    assert d == D_MODEL and t_lat % (2 * tc) == 0 and tc == 256 and b < 8

    xall = jnp.concatenate([x, ctx], axis=1)
    cvec = jnp.zeros((8, d), F32).at[0:b].set(c).at[b].set(c_ctx)
    mod = _mod_call(cvec, w_mod, b_mod)
    cos_t, sin_t = _rope_tables(t_lat, tall)

    seg = np.arange(LANES) // HEAD_DIM
    bd = jnp.asarray(seg[:, None] == seg[None, :], BF16)
    ci = np.arange(GLA_CHUNK)
    tri_f = jnp.asarray(ci[None, :] <= ci[:, None], BF16)
    tri_b = jnp.asarray(ci[None, :] >= ci[:, None], BF16)

    o_dq, o_dk, o_dv = 0, 512, 1024
    o_gq, o_gk, o_gv, o_gg, o_ga = 1536, 1792, 2048, 2560, 3072
    o_nq, o_nk, o_nv = 3104, 3616, 4128
    o_gate = 4640
    tq_da = 512 if t_lat % 512 == 0 else 256
    tk_da = next(t for t in (3328, 1280, 256) if tall % t == 0)

    for l in range(depth):
        need_ctx = l < depth - 1
        lam_init = 0.8 - 0.6 * math.exp(-0.3 * l)
        mod3 = mod[l].reshape(8, 1, 6 * d)
        wl = w_in[l].astype(BF16)
        cut = lambda a, n: wl[:, a:a + n]
        wa = jnp.concatenate([cut(o_ga, 2 * GLA_RANK), jnp.zeros((d, LANES - 2 * GLA_RANK), BF16)], axis=1)
        a2p = jnp.zeros((LANES, 2 * GLA_HEADS * GLA_DK), F32)
        a2p = a2p.at[0:GLA_RANK, 0:256].set(gla_a2[l, 0]).at[GLA_RANK:2 * GLA_RANK, 256:512].set(gla_a2[l, 1])
        ab = gla_a_b[l].reshape(1, 2 * GLA_HEADS * GLA_DK)
        tile2 = lambda g: jnp.tile(g, LANES // HEAD_DIM).reshape(1, LANES)

        h = _hnorm_call(xall, norm1_g[l], mod3, t_lat)
        qt, kk, vt, nrm = _proj_da_call(h, cut(o_dq, 512), cut(o_dk, 512), cut(o_dv, 512), bd,
                                        tile2(da_qn_g[l]), tile2(da_kn_g[l]), cos_t, sin_t)
        bound = jnp.sqrt(jnp.max(nrm[:, 0, :]) * jnp.max(nrm[:, 1, :])) * 1.01 + 1e-3
        safe = jnp.where(bound <= DA_SAFE_LOG2, bound, -1.0).astype(F32).reshape(1)
        gq, gk, gv, sgg, la = _proj_gla_call(h, cut(o_gq, 256), cut(o_gk, 256), cut(o_gv, 512), cut(o_gg, 512),
                                             wa, a2p.astype(BF16), ab)
        nq, nk, nv = _proj_na_call(h, cut(o_nq, 512), cut(o_nk, 512), cut(o_nv, 512), bd,
                                   tile2(na_qn_g[l]), tile2(na_kn_g[l]))
        gates = _proj_gate_call(h, cut(o_gate, 3 * d))

        subg = da_subln_g[l].reshape(LANES, 1)
        y_da = _da_call(safe, qt, kk, vt, da_lambda[l], subg, lam_init,
                        q_off=0, nq=t_lat // tq_da, k_off=0, nk=tall // tk_da, tq=tq_da, tk=tk_da)
        o_f = _gla_call(gq, gk, gv, la, tri_f, t_lat=t_lat, reverse=False)
        o_b = _gla_call(gq, gk, gv, la, tri_b, t_lat=t_lat, reverse=True)
        bias = _na_bias_tiles(na_rpb[l], rows)
        y_na = _na_call(nq, nk, nv, bias, t_lat=t_lat)
        if need_ctx:
            y_da_c = _da_call(safe, qt, kk, vt, da_lambda[l], subg, lam_init,
                              q_off=t_lat // tc, nq=1, k_off=t_lat // tc, nk=1, tq=tc, tk=tc)
            y_na_c = _ctx_attn_call(nq, nk, nv, t_lat=t_lat)
        else:
            y_da_c = jnp.zeros((b, tc, BR_W), BF16)
            y_na_c = jnp.zeros((b, tc, BR_W), BF16)
        y_da = jnp.concatenate([y_da, y_da_c], axis=1)
        y_na = jnp.concatenate([y_na, y_na_c], axis=1)

        xall = _merge_call(xall, y_da, o_f, o_b, sgg, y_na, gates,
                           w_br_da[l].astype(BF16), w_br_gla[l].astype(BF16), w_br_na[l].astype(BF16),
                           w_out[l].astype(BF16), gla_gn_g[l].reshape(1, GLA_DV), mod3, t_lat)
        xall = _mlp_call(xall, norm2_g[l], w_ff1[l].astype(BF16), w_ff2[l].astype(BF16), mod3, t_lat)
    return xall[:, :t_lat]
```

```python
import functools
import math

import numpy as np
import jax
import jax.numpy as jnp
from jax import lax
from jax.experimental import pallas as pl
from jax.experimental.pallas import tpu as pltpu

F32 = jnp.float32
BF16 = jnp.bfloat16
F8 = jnp.float8_e4m3fn

D_MODEL = 1024
GRID_W = 64
HEAD_DIM = 64
EPS = 1e-6
ROPE_BASE = 10000.0
ROPE_PAIRS = HEAD_DIM // 4
DA_HEADS = 4
GLA_HEADS = 4
GLA_DK = 64
GLA_DV = 128
GLA_RANK = 16
GLA_TAU = 16.0
GLA_CHUNK = 64
NA_HEADS = 8
NA_KR = 8
NA_KC = 16
BR_W = 512
D_FF = 4 * D_MODEL

DA_V = 2 * HEAD_DIM
DA_VPAD = 16
DA_VROWS = DA_V + DA_VPAD
LOG2E = math.log2(math.e)
DA_SAFE_LOG2 = 45.0

LANES = 128
NA_QROWS = 4
NA_BAND = 16
NEG_BIG = -1e30
VMEM_LIMIT = 48 * 1024 * 1024


def _cparams(sem):
    return pltpu.CompilerParams(dimension_semantics=sem, vmem_limit_bytes=VMEM_LIMIT)


def _dot(a, b):
    return jnp.dot(a, b, preferred_element_type=F32)


def _dot_nt(a, b):
    return lax.dot_general(a, b, (((1,), (1,)), ((), ())), preferred_element_type=F32)


def _dot_tn(a, b):
    return lax.dot_general(a, b, (((0,), (0,)), ((), ())), preferred_element_type=F32)


def _split_dot(x, w_exact):
    hi = x.astype(BF16)
    lo = (x - hi.astype(F32)).astype(BF16)
    return _dot(hi, w_exact) + _dot(lo, w_exact)


def _split_dot_left(w_exact, x):
    hi = x.astype(BF16)
    lo = (x - hi.astype(F32)).astype(BF16)
    return _dot(w_exact, hi) + _dot(w_exact, lo)


def _sigmoid(x):
    return 1.0 / (1.0 + jnp.exp(-x))


def _row_tile(tall):
    for tm in (640, 256):
        if tall % tm == 0:
            return tm
    raise ValueError(f"unsupported token count {tall}")


def _mod_body(c_ref, w_ref, b_ref, o_ref):
    cv = c_ref[...]
    s = cv * _sigmoid(cv)
    o_ref[0] = _dot(s.astype(BF16), w_ref[0].astype(BF16)) + b_ref[0]


def _mod_call(cvec, w_mod, b_mod):
    depth, d, n = w_mod.shape
    tn = 1536
    return pl.pallas_call(
        _mod_body,
        grid=(depth, n // tn),
        in_specs=[pl.BlockSpec((8, d), lambda l, j: (0, 0)),
                  pl.BlockSpec((1, d, tn), lambda l, j: (l, 0, j)),
                  pl.BlockSpec((1, 1, tn), lambda l, j: (l, 0, j))],
        out_specs=pl.BlockSpec((1, 8, tn), lambda l, j: (l, 0, j)),
        out_shape=jax.ShapeDtypeStruct((depth, 8, n), F32),
        compiler_params=_cparams(("parallel", "parallel")),
        name="mod",
    )(cvec, w_mod, b_mod.reshape(depth, 1, n))


def _rope_body(freq_ref, cos_ref, sin_ref, *, t_lat, tm):
    i = pl.program_id(0)
    t = i * tm + lax.broadcasted_iota(jnp.int32, (tm, LANES), 0)
    lane = lax.broadcasted_iota(jnp.int32, (tm, LANES), 1)
    shift = int(math.log2(GRID_W))
    row = lax.shift_right_logical(t, shift).astype(F32)
    col = jnp.bitwise_and(t, GRID_W - 1).astype(F32)
    use_row = jnp.bitwise_and(lane, 2 * ROPE_PAIRS) == 0
    first_half = jnp.bitwise_and(lane, ROPE_PAIRS) == 0
    ang = jnp.where(use_row, row, col) * freq_ref[...]
    is_lat = t < t_lat
    cos_ref[...] = jnp.where(is_lat, jnp.cos(ang), 1.0)
    sn = jnp.sin(ang)
    sin_ref[...] = jnp.where(is_lat, jnp.where(first_half, -sn, sn), 0.0)


def _rope_tables(t_lat, tall):
    tm = 256
    freqs = ROPE_BASE ** (-jnp.arange(ROPE_PAIRS, dtype=F32) / ROPE_PAIRS)
    freq_lane = jnp.tile(freqs, LANES // ROPE_PAIRS).reshape(1, LANES)
    return pl.pallas_call(
        functools.partial(_rope_body, t_lat=t_lat, tm=tm),
        grid=(tall // tm,),
        in_specs=[pl.BlockSpec((1, LANES), lambda i: (0, 0))],
        out_specs=[pl.BlockSpec((tm, LANES), lambda i: (i, 0))] * 2,
        out_shape=[jax.ShapeDtypeStruct((tall, LANES), F32)] * 2,
        compiler_params=_cparams(("parallel",)),
        name="rope_tables",
    )(freq_lane)


def _is_ctx_rows(r, tm, t_lat):
    t = r * tm + lax.broadcasted_iota(jnp.int32, (tm, 1), 0)
    return t >= t_lat


def _norm_mod(x, g, sc, sh):
    ms = jnp.mean(x * x, axis=-1, keepdims=True)
    return (x * lax.rsqrt(ms + EPS) * g) * (1.0 + sc) + sh


def _segnorm64(z, bd, gain):
    ss = _split_dot(z * z, bd)
    return z * lax.rsqrt(ss * (1.0 / HEAD_DIM) + EPS) * gain


def _hnorm_body(x_ref, g_ref, mb_ref, mc_ref, h_ref, *, t_lat, tm, d):
    r = pl.program_id(1)
    ctx = _is_ctx_rows(r, tm, t_lat)
    sh = jnp.where(ctx, mc_ref[0, :, 0:d], mb_ref[0, :, 0:d])
    sc = jnp.where(ctx, mc_ref[0, :, d:2 * d], mb_ref[0, :, d:2 * d])
    h_ref[0] = _norm_mod(x_ref[0], g_ref[...], sc, sh).astype(BF16)


def _hnorm_call(xall, g, mod3, t_lat):
    b, tall, d = xall.shape
    tm = _row_tile(tall)
    nb = b
    return pl.pallas_call(
        functools.partial(_hnorm_body, t_lat=t_lat, tm=tm, d=d),
        grid=(b, tall // tm),
        in_specs=[pl.BlockSpec((1, tm, d), lambda bi, r: (bi, r, 0)),
                  pl.BlockSpec((1, d), lambda bi, r: (0, 0)),
                  pl.BlockSpec((1, 1, 6 * d), lambda bi, r: (bi, 0, 0)),
                  pl.BlockSpec((1, 1, 6 * d), lambda bi, r: (nb, 0, 0))],
        out_specs=pl.BlockSpec((1, tm, d), lambda bi, r: (bi, r, 0)),
        out_shape=jax.ShapeDtypeStruct((b, tall, d), BF16),
        compiler_params=_cparams(("parallel", "parallel")),
        name="hnorm",
    )(xall, g.reshape(1, d), mod3, mod3)


def _rope128(x, cs, sn, first_half):
    partner = jnp.where(first_half, pltpu.roll(x, LANES - ROPE_PAIRS, 1), pltpu.roll(x, ROPE_PAIRS, 1))
    return x * cs + partner * sn


def _proj_da_body(h_ref, wq_ref, wk_ref, wv_ref, bd_ref, gq_ref, gk_ref, cos_ref, sin_ref,
                  qt_ref, k_ref, vt_ref, nrm_ref):
    h = h_ref[0]
    tm = h.shape[0]
    bd = bd_ref[...]
    cs = cos_ref[...]
    sn = sin_ref[...]
    lane = lax.broadcasted_iota(jnp.int32, cs.shape, 1)
    first_half = jnp.bitwise_and(lane, ROPE_PAIRS) == 0
    zq = _dot(h, wq_ref[...])
    zk = _dot(h, wk_ref[...])
    zv = _dot(h, wv_ref[...])
    scale = math.sqrt(HEAD_DIM ** -0.5 * LOG2E)
    ones_row = (lax.broadcasted_iota(jnp.int32, (DA_VPAD, tm), 0) == 0).astype(BF16)
    low_lanes = lane < HEAD_DIM
    qn2 = jnp.zeros((1, LANES), F32)
    kn2 = jnp.zeros((1, LANES), F32)
    for hd in range(DA_HEADS):
        sl = slice(hd * LANES, (hd + 1) * LANES)
        q = _segnorm64(zq[:, sl], bd, gq_ref[...]) * scale
        q = _rope128(q, cs, sn, first_half)
        q_hi = q.astype(F8).astype(F32)
        qh_t = q_hi.T
        ql_t = (q - q_hi).T
        r0 = hd * 2 * LANES
        for a in range(2):
            qt_ref[0, r0 + a * LANES:r0 + a * LANES + HEAD_DIM, :] = qh_t[a * HEAD_DIM:(a + 1) * HEAD_DIM].astype(F8)
            qt_ref[0, r0 + a * LANES + HEAD_DIM:r0 + (a + 1) * LANES, :] = ql_t[a * HEAD_DIM:(a + 1) * HEAD_DIM].astype(F8)
        k = _segnorm64(zk[:, sl], bd, gk_ref[...]) * scale
        k = _rope128(k, cs, sn, first_half)
        k_hi = k.astype(F8).astype(F32)
        k_lo = k - k_hi
        k_ref[0, :, r0:r0 + LANES] = jnp.where(low_lanes, k_hi, pltpu.roll(k_lo, HEAD_DIM, 1)).astype(F8)
        k_ref[0, :, r0 + LANES:r0 + 2 * LANES] = jnp.where(low_lanes, pltpu.roll(k_hi, HEAD_DIM, 1), k_lo).astype(F8)
        qn2 = jnp.maximum(qn2, jnp.max(_split_dot(q * q, bd), axis=0, keepdims=True))
        kn2 = jnp.maximum(kn2, jnp.max(_split_dot(k * k, bd), axis=0, keepdims=True))
        v0 = hd * DA_VROWS
        vt_ref[0, v0:v0 + DA_V, :] = zv[:, sl].T.astype(BF16)
        vt_ref[0, v0 + DA_V:v0 + DA_VROWS, :] = ones_row
    nrm_ref[0, 0:1, :] = qn2
    nrm_ref[0, 1:2, :] = kn2


def _proj_da_call(h, wq, wk, wv, bd, gq, gk, cos_t, sin_t):
    b, tall, d = h.shape
    tm = _row_tile(tall)
    nt = tall // tm
    w = DA_HEADS * LANES
    const = lambda bi, r: (0, 0)
    return pl.pallas_call(
        _proj_da_body,
        grid=(b, tall // tm),
        in_specs=[pl.BlockSpec((1, tm, d), lambda bi, r: (bi, r, 0)),
                  pl.BlockSpec((d, w), const), pl.BlockSpec((d, w), const), pl.BlockSpec((d, w), const),
                  pl.BlockSpec((LANES, LANES), const),
                  pl.BlockSpec((1, LANES), const), pl.BlockSpec((1, LANES), const),
                  pl.BlockSpec((tm, LANES), lambda bi, r: (r, 0)),
                  pl.BlockSpec((tm, LANES), lambda bi, r: (r, 0))],
        out_specs=[pl.BlockSpec((1, 2 * w, tm), lambda bi, r: (bi, 0, r)),
                   pl.BlockSpec((1, tm, 2 * w), lambda bi, r: (bi, r, 0)),
                   pl.BlockSpec((1, DA_HEADS * DA_VROWS, tm), lambda bi, r: (bi, 0, r)),
                   pl.BlockSpec((1, 2, LANES), lambda bi, r: (bi * nt + r, 0, 0))],
        out_shape=[jax.ShapeDtypeStruct((b, 2 * w, tall), F8),
                   jax.ShapeDtypeStruct((b, tall, 2 * w), F8),
                   jax.ShapeDtypeStruct((b, DA_HEADS * DA_VROWS, tall), BF16),
                   jax.ShapeDtypeStruct((b * nt, 2, LANES), F32)],
        compiler_params=_cparams(("parallel", "parallel")),
        name="proj_da",
    )(h, wq, wk, wv, bd, gq, gk, cos_t, sin_t)


def _proj_gla_body(h_ref, wq_ref, wk_ref, wv_ref, wg_ref, wa_ref, a2_ref, ab_ref,
                   q_ref, k_ref, v_ref, sg_ref, la_ref):
    h = h_ref[0]
    q_ref[0] = _dot(h, wq_ref[...]) * (GLA_DK ** -0.5)
    k_ref[0] = _dot(h, wk_ref[...])
    v_ref[0] = _dot(h, wv_ref[...]).astype(BF16)
    g = _dot(h, wg_ref[...])
    sg_ref[0] = (g * _sigmoid(g)).astype(BF16)
    ga = _dot(h, wa_ref[...])
    z = _dot(ga.astype(BF16), a2_ref[...]) + ab_ref[...]
    la_ref[0] = (jnp.minimum(z, 0.0) - jnp.log1p(jnp.exp(-jnp.abs(z)))) * (1.0 / GLA_TAU)


def _proj_gla_call(h, wq, wk, wv, wg, wa, a2p, ab):
    b, tall, d = h.shape
    tm = _row_tile(tall)
    wqk = GLA_HEADS * GLA_DK
    wv_ = GLA_HEADS * GLA_DV
    const = lambda bi, r: (0, 0)
    row = lambda n: pl.BlockSpec((1, tm, n), lambda bi, r: (bi, r, 0))
    return pl.pallas_call(
        _proj_gla_body,
        grid=(b, tall // tm),
        in_specs=[row(d),
                  pl.BlockSpec((d, wqk), const), pl.BlockSpec((d, wqk), const),
                  pl.BlockSpec((d, wv_), const), pl.BlockSpec((d, wv_), const),
                  pl.BlockSpec((d, LANES), const),
                  pl.BlockSpec((LANES, 2 * wqk), const), pl.BlockSpec((1, 2 * wqk), const)],
        out_specs=[row(wqk), row(wqk), row(wv_), row(wv_), row(2 * wqk)],
        out_shape=[jax.ShapeDtypeStruct((b, tall, wqk), F32),
                   jax.ShapeDtypeStruct((b, tall, wqk), F32),
                   jax.ShapeDtypeStruct((b, tall, wv_), BF16),
                   jax.ShapeDtypeStruct((b, tall, wv_), BF16),
                   jax.ShapeDtypeStruct((b, tall, 2 * wqk), F32)],
        compiler_params=_cparams(("parallel", "parallel")),
        name="proj_gla",
    )(h, wq, wk, wv, wg, wa, a2p, ab)


def _proj_na_body(h_ref, wq_ref, wk_ref, wv_ref, bd_ref, gq_ref, gk_ref, q_ref, k_ref, v_ref):
    h = h_ref[0]
    bd = bd_ref[...]
    zq = _dot(h, wq_ref[...])
    zk = _dot(h, wk_ref[...])
    scale = HEAD_DIM ** -0.5
    for j in range(NA_HEADS * HEAD_DIM // LANES):
        sl = slice(j * LANES, (j + 1) * LANES)
        q_ref[0, :, sl] = (_segnorm64(zq[:, sl], bd, gq_ref[...]) * scale).astype(BF16)
        k_ref[0, :, sl] = _segnorm64(zk[:, sl], bd, gk_ref[...]).astype(BF16)
    v_ref[0] = _dot(h, wv_ref[...]).astype(BF16)


def _proj_na_call(h, wq, wk, wv, bd, gq, gk):
    b, tall, d = h.shape
    tm = _row_tile(tall)
    w = NA_HEADS * HEAD_DIM
    const = lambda bi, r: (0, 0)
    row = lambda n: pl.BlockSpec((1, tm, n), lambda bi, r: (bi, r, 0))
    return pl.pallas_call(
        _proj_na_body,
        grid=(b, tall // tm),
        in_specs=[row(d), pl.BlockSpec((d, w), const), pl.BlockSpec((d, w), const), pl.BlockSpec((d, w), const),
                  pl.BlockSpec((LANES, LANES), const),
                  pl.BlockSpec((1, LANES), const), pl.BlockSpec((1, LANES), const)],
        out_specs=[row(w), row(w), row(w)],
        out_shape=[jax.ShapeDtypeStruct((b, tall, w), BF16)] * 3,
        compiler_params=_cparams(("parallel", "parallel")),
        name="proj_na",
    )(h, wq, wk, wv, bd, gq, gk)


def _proj_gate_body(h_ref, w_ref, o_ref):
    h = h_ref[0]
    n = w_ref.shape[1]
    step = 512
    for j in range(n // step):
        sl = slice(j * step, (j + 1) * step)
        o_ref[0, :, sl] = _sigmoid(_dot(h, w_ref[:, sl])).astype(BF16)


def _proj_gate_call(h, w):
    b, tall, d = h.shape
    tm = _row_tile(tall)
    n = w.shape[1]
    return pl.pallas_call(
        _proj_gate_body,
        grid=(b, tall // tm),
        in_specs=[pl.BlockSpec((1, tm, d), lambda bi, r: (bi, r, 0)),
                  pl.BlockSpec((d, n), lambda bi, r: (0, 0))],
        out_specs=pl.BlockSpec((1, tm, n), lambda bi, r: (bi, r, 0)),
        out_shape=jax.ShapeDtypeStruct((b, tall, n), BF16),
        compiler_params=_cparams(("parallel", "parallel")),
        name="proj_gate",
    )(h, w)


def _da_body(safe_ref, qt_ref, k_ref, vt_ref, lp_ref, sg_ref, o_ref, m_ref, a_ref, acc_ref, *, lam_init, nk):
    j = pl.program_id(3)
    use_lag = safe_ref[0] > 0.0

    @pl.when(j == 0)
    def _():
        m_ref[...] = jnp.full(m_ref.shape, jnp.where(use_lag, -safe_ref[0], -jnp.inf), F32)
        a_ref[...] = jnp.ones(a_ref.shape, F32)
        acc_ref[...] = jnp.zeros(acc_ref.shape, F32)

    def step(lagged):
        vt = vt_ref[0]
        for a in range(2):
            kb = k_ref[0, :, a * LANES:(a + 1) * LANES]
            q_hi = qt_ref[0, a * LANES:a * LANES + HEAD_DIM, :]
            q_lo = qt_ref[0, a * LANES + HEAD_DIM:(a + 1) * LANES, :]
            w = jnp.concatenate([q_hi, q_hi, q_lo, q_lo], axis=0)
            s = _dot(jnp.concatenate([kb, kb], axis=1), w)
            m_old = m_ref[a:a + 1, :]
            m_new = jnp.maximum(m_old, jnp.max(s, axis=0, keepdims=True))
            owed = a_ref[a:a + 1, :]
            if lagged:
                p = jnp.exp2(s - m_old)
                acc_ref[a] = owed * acc_ref[a] + _dot(vt, p.astype(BF16))
                a_ref[a:a + 1, :] = jnp.exp2(m_old - m_new)
            else:
                p = jnp.exp2(s - m_new)
                acc_ref[a] = (owed * jnp.exp2(m_old - m_new)) * acc_ref[a] + _dot(vt, p.astype(BF16))
                a_ref[a:a + 1, :] = jnp.ones_like(owed)
            m_ref[a:a + 1, :] = m_new

    pl.when(use_lag)(functools.partial(step, True))
    pl.when(jnp.logical_not(use_lag))(functools.partial(step, False))

    @pl.when(j == nk - 1)
    def _():
        lp = lp_ref[...]
        e1 = jnp.exp(jnp.sum(lp[0:1] * lp[1:2], axis=-1, keepdims=True))
        e2 = jnp.exp(jnp.sum(lp[2:3] * lp[3:4], axis=-1, keepdims=True))
        lam = e1 - e2 + lam_init
        o1 = acc_ref[0, 0:DA_V, :] / acc_ref[0, DA_V:DA_V + 1, :]
        o2 = acc_ref[1, 0:DA_V, :] / acc_ref[1, DA_V:DA_V + 1, :]
        o = o1 - lam * o2
        ms = jnp.mean(o * o, axis=0, keepdims=True)
        y = (o * lax.rsqrt(ms + EPS) * sg_ref[...]) * (1.0 - lam_init)
        o_ref[0] = y.T.astype(BF16)


def _da_call(safe, qt, kk, vt, lp, subg, lam_init, *, q_off, nq, k_off, nk, tq, tk):
    b = qt.shape[0]
    return pl.pallas_call(
        functools.partial(_da_body, lam_init=lam_init, nk=nk),
        grid=(b, DA_HEADS, nq, nk),
        in_specs=[pl.BlockSpec(memory_space=pltpu.SMEM),
                  pl.BlockSpec((1, 2 * LANES, tq), lambda bi, h, i, j: (bi, h, i + q_off)),
                  pl.BlockSpec((1, tk, 2 * LANES), lambda bi, h, i, j: (bi, j + k_off, h)),
                  pl.BlockSpec((1, DA_VROWS, tk), lambda bi, h, i, j: (bi, h, j + k_off)),
                  pl.BlockSpec((4, HEAD_DIM), lambda bi, h, i, j: (0, 0)),
                  pl.BlockSpec((LANES, 1), lambda bi, h, i, j: (0, 0))],
        out_specs=pl.BlockSpec((1, tq, LANES), lambda bi, h, i, j: (bi, i, h)),
        out_shape=jax.ShapeDtypeStruct((b, nq * tq, DA_HEADS * LANES), BF16),
        scratch_shapes=[pltpu.VMEM((8, tq), F32), pltpu.VMEM((8, tq), F32),
                        pltpu.VMEM((2, DA_VROWS, tq), F32)],
        compiler_params=_cparams(("parallel", "parallel", "parallel", "arbitrary")),
        name="diff_attn",
    )(safe, qt, kk, vt, lp, subg)


def _gla_body(q_ref, k_ref, v_ref, la_ref, tri_ref, o_ref, s_ref, *, reverse, tb):
    i = pl.program_id(1)

    @pl.when(i == 0)
    def _():
        s_ref[...] = jnp.zeros(s_ref.shape, F32)

    c = GLA_CHUNK
    w = GLA_HEADS * GLA_DK
    tri = tri_ref[...]
    ri = lax.broadcasted_iota(jnp.int32, (c, c), 0)
    ci = lax.broadcasted_iota(jnp.int32, (c, c), 1)
    keep = (ci >= ri) if reverse else (ci <= ri)
    lane = lax.broadcasted_iota(jnp.int32, (1, w), 1)
    chunks = range(tb // c)
    for ch in (reversed(chunks) if reverse else chunks):
        rows = slice(ch * c, (ch + 1) * c)
        la = la_ref[0, rows, :]
        cum = _split_dot_left(tri, la)
        cl = cum[0:1, :] if reverse else cum[c - 1:c, :]
        q = q_ref[0, rows, :]
        k = k_ref[0, rows, :]
        qe = q * jnp.exp(cum)
        ke = (k * jnp.exp(-cum)).astype(BF16)
        kd = (k * jnp.exp(cl - cum)).astype(BF16)
        dec = jnp.exp(cl)
        for hd in range(GLA_HEADS):
            hm = (lane >= hd * GLA_DK) & (lane < (hd + 1) * GLA_DK)
            qh = jnp.where(hm, qe, 0.0).astype(BF16)
            a = jnp.where(keep, _dot_nt(qh, ke), 0.0)
            vh = v_ref[0, rows, hd * GLA_DV:(hd + 1) * GLA_DV]
            st = s_ref[hd]
            o = _dot(a.astype(BF16), vh) + _dot_nt(qh, st.astype(BF16))
            o_ref[0, rows, hd * GLA_DV:(hd + 1) * GLA_DV] = o
            s_ref[hd] = st * dec + _dot_tn(vh, kd)


def _gla_call(gq, gk, gv, la, tri, *, t_lat, reverse):
    b, tall, w = gq.shape
    tb = 256
    n_lat = t_lat // tb
    nblk = tall // tb
    if reverse:
        blk = lambda i: jnp.where(i == 0, n_lat, n_lat - i)
    else:
        blk = lambda i: jnp.where(i == 0, n_lat, i - 1)
    wv_ = GLA_HEADS * GLA_DV
    return pl.pallas_call(
        functools.partial(_gla_body, reverse=reverse, tb=tb),
        grid=(b, nblk),
        in_specs=[pl.BlockSpec((1, tb, w), lambda bi, i: (bi, blk(i), 0)),
                  pl.BlockSpec((1, tb, w), lambda bi, i: (bi, blk(i), 0)),
                  pl.BlockSpec((1, tb, wv_), lambda bi, i: (bi, blk(i), 0)),
                  pl.BlockSpec((1, tb, w), lambda bi, i: (bi, blk(i), 1 if reverse else 0)),
                  pl.BlockSpec((GLA_CHUNK, GLA_CHUNK), lambda bi, i: (0, 0))],
        out_specs=pl.BlockSpec((1, tb, wv_), lambda bi, i: (bi, blk(i), 0)),
        out_shape=jax.ShapeDtypeStruct((b, tall, wv_), F32),
        scratch_shapes=[pltpu.VMEM((GLA_HEADS, GLA_DV, w), F32)],
        compiler_params=_cparams(("parallel", "arbitrary")),
        name="gla_bwd" if reverse else "gla_fwd",
    )(gq, gk, gv, la, tri)


def _na_body(q_ref, k_ref, v_ref, kc_ref, vc_ref, bias_ref, o_ref, *, rows):
    i = pl.program_id(2)
    kb0 = jnp.clip(i * NA_QROWS - NA_KR // 2, 0, rows - NA_BAND)
    start = pl.multiple_of(kb0 * GRID_W, GRID_W)
    nband = NA_BAND * GRID_W
    q = q_ref[0]
    kb = k_ref[0, pl.ds(start, nband), :]
    vb = v_ref[0, pl.ds(start, nband), :]
    kc = kc_ref[0]
    vc = vc_ref[0]
    lane = lax.broadcasted_iota(jnp.int32, q.shape, 1)
    outs = []
    for hh in range(2):
        qh = jnp.where((lane < HEAD_DIM) if hh == 0 else (lane >= HEAD_DIM), q, jnp.zeros_like(q))
        s_loc = _dot_nt(qh, kb) + bias_ref[0, hh]
        s_ctx = _dot_nt(qh, kc)
        m = jnp.maximum(jnp.max(s_loc, axis=-1, keepdims=True), jnp.max(s_ctx, axis=-1, keepdims=True))
        p_loc = jnp.exp(s_loc - m)
        p_ctx = jnp.exp(s_ctx - m)
        l = jnp.sum(p_loc, axis=-1, keepdims=True) + jnp.sum(p_ctx, axis=-1, keepdims=True)
        outs.append((_dot(p_loc.astype(BF16), vb) + _dot(p_ctx.astype(BF16), vc)) / l)
    o_ref[0] = jnp.where(lane < HEAD_DIM, outs[0], outs[1]).astype(BF16)


def _na_call(nq, nk, nv, bias, *, t_lat):
    b, tall, w = nq.shape
    tc = tall - t_lat
    rows = t_lat // GRID_W
    tq = NA_QROWS * GRID_W
    nsteps = rows // NA_QROWS
    npair = w // LANES
    ctx_blk = t_lat // tc

    def btype(i):
        return jnp.where(i == 0, 0, jnp.where(i == nsteps - 2, 2, jnp.where(i == nsteps - 1, 3, 1)))

    return pl.pallas_call(
        functools.partial(_na_body, rows=rows),
        grid=(b, npair, nsteps),
        in_specs=[pl.BlockSpec((1, tq, LANES), lambda bi, hp, i: (bi, i, hp)),
                  pl.BlockSpec((1, t_lat, LANES), lambda bi, hp, i: (bi, 0, hp)),
                  pl.BlockSpec((1, t_lat, LANES), lambda bi, hp, i: (bi, 0, hp)),
                  pl.BlockSpec((1, tc, LANES), lambda bi, hp, i: (bi, ctx_blk, hp)),
                  pl.BlockSpec((1, tc, LANES), lambda bi, hp, i: (bi, ctx_blk, hp)),
                  pl.BlockSpec((1, 2, tq, NA_BAND * GRID_W), lambda bi, hp, i: (btype(i), hp, 0, 0))],
        out_specs=pl.BlockSpec((1, tq, LANES), lambda bi, hp, i: (bi, i, hp)),
        out_shape=jax.ShapeDtypeStruct((b, t_lat, w), BF16),
        compiler_params=_cparams(("parallel", "parallel", "arbitrary")),
        name="nbr_attn",
    )(nq, nk, nv, nk, nv, bias)


def _na_bias_tiles(rpb, rows):
    assert rows >= NA_BAND and rows % NA_QROWS == 0
    nh, na, nb = rpb.shape
    cidx = np.arange(GRID_W)
    rel_c = cidx[None, :] - cidx[:, None] + NA_KC - 1
    sel = jnp.asarray(rel_c[None] == np.arange(nb)[:, None, None], F32)
    toep = jnp.einsum('hab,bqk->haqk', rpb.astype(F32), sel, precision=lax.Precision.HIGHEST)
    toep = jnp.pad(toep, ((0, 0), (NA_BAND, NA_BAND), (0, 0), (0, 0)))
    tiles = []
    for r0 in (0, NA_QROWS, rows - 2 * NA_QROWS, rows - NA_QROWS):
        kb0 = min(max(r0 - NA_KR // 2, 0), rows - NA_BAND)
        qi = np.arange(NA_QROWS * GRID_W)
        qr, qc = r0 + qi // GRID_W, qi % GRID_W
        kj = np.arange(NA_BAND * GRID_W)
        kr, kc = kb0 + kj // GRID_W, kj % GRID_W
        rs = np.clip(qr - NA_KR // 2, 0, rows - NA_KR)
        cs = np.clip(qc - NA_KC // 2, 0, GRID_W - NA_KC)
        valid = ((kr[None, :] >= rs[:, None]) & (kr[None, :] < rs[:, None] + NA_KR)
                 & (kc[None, :] >= cs[:, None]) & (kc[None, :] < cs[:, None] + NA_KC))
        parts = []
        for q_row in range(NA_QROWS):
            a0 = kb0 - (r0 + q_row) + NA_KR - 1 + NA_BAND
            blk = toep[:, a0:a0 + NA_BAND]
            parts.append(jnp.transpose(blk, (0, 2, 1, 3)).reshape(nh, GRID_W, NA_BAND * GRID_W))
        tile = jnp.concatenate(parts, axis=1)
        tiles.append(jnp.where(jnp.asarray(valid)[None], tile, NEG_BIG))
    return jnp.stack(tiles)


def _ctx_attn_body(q_ref, k_ref, v_ref, o_ref):
    q = q_ref[0]
    k = k_ref[0]
    v = v_ref[0]
    lane = lax.broadcasted_iota(jnp.int32, q.shape, 1)
    outs = []
    for hh in range(2):
        qh = jnp.where((lane < HEAD_DIM) if hh == 0 else (lane >= HEAD_DIM), q, jnp.zeros_like(q))
        s = _dot_nt(qh, k)
        m = jnp.max(s, axis=-1, keepdims=True)
        p = jnp.exp(s - m)
        outs.append(_dot(p.astype(BF16), v) / jnp.sum(p, axis=-1, keepdims=True))
    o_ref[0] = jnp.where(lane < HEAD_DIM, outs[0], outs[1]).astype(BF16)


def _ctx_attn_call(nq, nk, nv, *, t_lat):
    b, tall, w = nq.shape
    tc = tall - t_lat
    ctx_blk = t_lat // tc
    spec = pl.BlockSpec((1, tc, LANES), lambda bi, hp: (bi, ctx_blk, hp))
    return pl.pallas_call(
        _ctx_attn_body,
        grid=(b, w // LANES),
        in_specs=[spec, spec, spec],
        out_specs=pl.BlockSpec((1, tc, LANES), lambda bi, hp: (bi, 0, hp)),
        out_shape=jax.ShapeDtypeStruct((b, tc, w), BF16),
        compiler_params=_cparams(("parallel", "parallel")),
        name="ctx_attn",
    )(nq, nk, nv)


def _merge_body(x_ref, yd_ref, of_ref, ob_ref, sg_ref, yn_ref, gt_ref, wd_ref, wg_ref, wn_ref, wo_ref,
                gn_ref, mb_ref, mc_ref, o_ref, *, t_lat, tm, d):
    r = pl.program_id(1)
    ctx = _is_ctx_rows(r, tm, t_lat)
    g1 = jnp.where(ctx, mc_ref[0, :, 2 * d:3 * d], mb_ref[0, :, 2 * d:3 * d])
    og = of_ref[0] + ob_ref[0]
    parts = []
    for hd in range(GLA_HEADS):
        oh = og[:, hd * GLA_DV:(hd + 1) * GLA_DV]
        ms = jnp.mean(oh * oh, axis=-1, keepdims=True)
        parts.append(oh * lax.rsqrt(ms + EPS) * gn_ref[...])
    yg = (jnp.concatenate(parts, axis=-1) * sg_ref[0].astype(F32)).astype(BF16)
    m = (gt_ref[0, :, 0:d].astype(F32) * _dot(yd_ref[0], wd_ref[...])
         + gt_ref[0, :, d:2 * d].astype(F32) * _dot(yg, wg_ref[...])
         + gt_ref[0, :, 2 * d:3 * d].astype(F32) * _dot(yn_ref[0], wn_ref[...]))
    o_ref[0] = x_ref[0] + g1 * _dot(m.astype(BF16), wo_ref[...])


def _merge_call(xall, yd, of, ob, sg, yn, gates, wd, wg, wn, wo, gn, mod3, t_lat):
    b, tall, d = xall.shape
    tm = _row_tile(tall)
    nb = b
    const = lambda bi, r: (0, 0)
    row = lambda n: pl.BlockSpec((1, tm, n), lambda bi, r: (bi, r, 0))
    return pl.pallas_call(
        functools.partial(_merge_body, t_lat=t_lat, tm=tm, d=d),
        grid=(b, tall // tm),
        in_specs=[row(d), row(BR_W), row(BR_W), row(BR_W), row(BR_W), row(BR_W), row(3 * d),
                  pl.BlockSpec((BR_W, d), const), pl.BlockSpec((BR_W, d), const), pl.BlockSpec((BR_W, d), const),
                  pl.BlockSpec((d, d), const), pl.BlockSpec((1, GLA_DV), const),
                  pl.BlockSpec((1, 1, 6 * d), lambda bi, r: (bi, 0, 0)),
                  pl.BlockSpec((1, 1, 6 * d), lambda bi, r: (nb, 0, 0))],
        out_specs=row(d),
        out_shape=jax.ShapeDtypeStruct((b, tall, d), F32),
        compiler_params=_cparams(("parallel", "parallel")),
        name="merge",
    )(xall, yd, of, ob, sg, yn, gates, wd, wg, wn, wo, gn, mod3, mod3)


def _mlp_body(x_ref, g_ref, w1_ref, w2_ref, mb_ref, mc_ref, o_ref, *, t_lat, tm, d):
    r = pl.program_id(1)
    ctx = _is_ctx_rows(r, tm, t_lat)
    sh = jnp.where(ctx, mc_ref[0, :, 3 * d:4 * d], mb_ref[0, :, 3 * d:4 * d])
    sc = jnp.where(ctx, mc_ref[0, :, 4 * d:5 * d], mb_ref[0, :, 4 * d:5 * d])
    g2 = jnp.where(ctx, mc_ref[0, :, 5 * d:6 * d], mb_ref[0, :, 5 * d:6 * d])
    x = x_ref[0]
    h = _norm_mod(x, g_ref[...], sc, sh).astype(BF16)
    acc = jnp.zeros((tm, d), F32)
    step = 1024
    for j in range(w1_ref.shape[1] // step):
        a = jnp.maximum(_dot(h, w1_ref[:, j * step:(j + 1) * step]), 0.0)
        acc = acc + _dot((a * a).astype(BF16), w2_ref[j * step:(j + 1) * step, :])
    o_ref[0] = x + g2 * acc


def _mlp_call(xall, g, w1, w2, mod3, t_lat):
    b, tall, d = xall.shape
    tm = _row_tile(tall)
    nb = b
    dff = w1.shape[1]
    const = lambda bi, r: (0, 0)
    return pl.pallas_call(
        functools.partial(_mlp_body, t_lat=t_lat, tm=tm, d=d),
        grid=(b, tall // tm),
        in_specs=[pl.BlockSpec((1, tm, d), lambda bi, r: (bi, r, 0)),
                  pl.BlockSpec((1, d), const),
                  pl.BlockSpec((d, dff), const, pipeline_mode=pl.Buffered(1)),
                  pl.BlockSpec((dff, d), const, pipeline_mode=pl.Buffered(1)),
                  pl.BlockSpec((1, 1, 6 * d), lambda bi, r: (bi, 0, 0)),
                  pl.BlockSpec((1, 1, 6 * d), lambda bi, r: (nb, 0, 0))],
        out_specs=pl.BlockSpec((1, tm, d), lambda bi, r: (bi, r, 0)),
        out_shape=jax.ShapeDtypeStruct((b, tall, d), F32),
        compiler_params=_cparams(("parallel", "parallel")),
        name="mlp",
    )(xall, g.reshape(1, d), w1, w2, mod3, mod3)


def kernel(x, c, ctx, c_ctx, w_mod, b_mod, norm1_g, norm2_g, w_in, da_qn_g, da_kn_g, da_lambda, da_subln_g,
           gla_a2, gla_a_b, gla_gn_g, na_qn_g, na_kn_g, na_rpb, w_br_da, w_br_gla, w_br_na, w_out, w_ff1, w_ff2):
    b, t_lat, d = x.shape
    tc = ctx.shape[1]
    tall = t_lat + tc
    depth = w_mod.shape[0]
    rows = t_lat // GRID_W
    assert d == D_MODEL and t_lat % (2 * tc) == 0 and tc == 256 and b < 8

    xall = jnp.concatenate([x, ctx], axis=1)
    cvec = jnp.zeros((8, d), F32).at[0:b].set(c).at[b].set(c_ctx)
    mod = _mod_call(cvec, w_mod, b_mod)
    cos_t, sin_t = _rope_tables(t_lat, tall)

    seg = np.arange(LANES) // HEAD_DIM
    bd = jnp.asarray(seg[:, None] == seg[None, :], BF16)
    ci = np.arange(GLA_CHUNK)
    tri_f = jnp.asarray(ci[None, :] <= ci[:, None], BF16)
    tri_b = jnp.asarray(ci[None, :] >= ci[:, None], BF16)

    o_dq, o_dk, o_dv = 0, 512, 1024
    o_gq, o_gk, o_gv, o_gg, o_ga = 1536, 1792, 2048, 2560, 3072
    o_nq, o_nk, o_nv = 3104, 3616, 4128
    o_gate = 4640
    tq_da = 512 if t_lat % 512 == 0 else 256
    tk_da = next(t for t in (3328, 1280, 256) if tall % t == 0)

    for l in range(depth):
        need_ctx = l < depth - 1
        lam_init = 0.8 - 0.6 * math.exp(-0.3 * l)
        mod3 = mod[l].reshape(8, 1, 6 * d)
        wl = w_in[l].astype(BF16)
        cut = lambda a, n: wl[:, a:a + n]
        wa = jnp.concatenate([cut(o_ga, 2 * GLA_RANK), jnp.zeros((d, LANES - 2 * GLA_RANK), BF16)], axis=1)
        a2p = jnp.zeros((LANES, 2 * GLA_HEADS * GLA_DK), F32)
        a2p = a2p.at[0:GLA_RANK, 0:256].set(gla_a2[l, 0]).at[GLA_RANK:2 * GLA_RANK, 256:512].set(gla_a2[l, 1])
        ab = gla_a_b[l].reshape(1, 2 * GLA_HEADS * GLA_DK)
        tile2 = lambda g: jnp.tile(g, LANES // HEAD_DIM).reshape(1, LANES)

        h = _hnorm_call(xall, norm1_g[l], mod3, t_lat)
        qt, kk, vt, nrm = _proj_da_call(h, cut(o_dq, 512), cut(o_dk, 512), cut(o_dv, 512), bd,
                                        tile2(da_qn_g[l]), tile2(da_kn_g[l]), cos_t, sin_t)
        bound = jnp.sqrt(jnp.max(nrm[:, 0, :]) * jnp.max(nrm[:, 1, :])) * 1.01 + 1e-3
        safe = jnp.where(bound <= DA_SAFE_LOG2, bound, -1.0).astype(F32).reshape(1)
        gq, gk, gv, sgg, la = _proj_gla_call(h, cut(o_gq, 256), cut(o_gk, 256), cut(o_gv, 512), cut(o_gg, 512),
                                             wa, a2p.astype(BF16), ab)
        nq, nk, nv = _proj_na_call(h, cut(o_nq, 512), cut(o_nk, 512), cut(o_nv, 512), bd,
                                   tile2(na_qn_g[l]), tile2(na_kn_g[l]))
        gates = _proj_gate_call(h, cut(o_gate, 3 * d))

        subg = da_subln_g[l].reshape(LANES, 1)
        y_da = _da_call(safe, qt, kk, vt, da_lambda[l], subg, lam_init,
                        q_off=0, nq=t_lat // tq_da, k_off=0, nk=tall // tk_da, tq=tq_da, tk=tk_da)
        o_f = _gla_call(gq, gk, gv, la, tri_f, t_lat=t_lat, reverse=False)
        o_b = _gla_call(gq, gk, gv, la, tri_b, t_lat=t_lat, reverse=True)
        bias = _na_bias_tiles(na_rpb[l], rows)
        y_na = _na_call(nq, nk, nv, bias, t_lat=t_lat)
        if need_ctx:
            y_da_c = _da_call(safe, qt, kk, vt, da_lambda[l], subg, lam_init,
                              q_off=t_lat // tc, nq=1, k_off=t_lat // tc, nk=1, tq=tc, tk=tc)
            y_na_c = _ctx_attn_call(nq, nk, nv, t_lat=t_lat)
        else:
            y_da_c = jnp.zeros((b, tc, BR_W), BF16)
            y_na_c = jnp.zeros((b, tc, BR_W), BF16)
        y_da = jnp.concatenate([y_da, y_da_c], axis=1)
        y_na = jnp.concatenate([y_na, y_na_c], axis=1)

        xall = _merge_call(xall, y_da, o_f, o_b, sgg, y_na, gates,
                           w_br_da[l].astype(BF16), w_br_gla[l].astype(BF16), w_br_na[l].astype(BF16),
                           w_out[l].astype(BF16), gla_gn_g[l].reshape(1, GLA_DV), mod3, t_lat)
        xall = _mlp_call(xall, norm2_g[l], w_ff1[l].astype(BF16), w_ff2[l].astype(BF16), mod3, t_lat)
    return xall[:, :t_lat]
```

```python
import functools
import math

import numpy as np
import jax
import jax.numpy as jnp
from jax import lax
from jax.experimental import pallas as pl
from jax.experimental.pallas import tpu as pltpu

F32 = jnp.float32
BF16 = jnp.bfloat16
F8 = jnp.float8_e4m3fn

D_MODEL = 1024
GRID_W = 64
HEAD_DIM = 64
EPS = 1e-6
ROPE_BASE = 10000.0
ROPE_PAIRS = HEAD_DIM // 4
DA_HEADS = 4
GLA_HEADS = 4
GLA_DK = 64
GLA_DV = 128
GLA_RANK = 16
GLA_TAU = 16.0
GLA_CHUNK = 64
NA_HEADS = 8
NA_KR = 8
NA_KC = 16
BR_W = 512
D_FF = 4 * D_MODEL

DA_V = 2 * HEAD_DIM
DA_VPAD = 16
DA_VROWS = DA_V + DA_VPAD
LOG2E = math.log2(math.e)
DA_SAFE_LOG2 = 45.0

DA_SUB_TILES = 3
MXU_DIM = 256
LANES = 128
NA_QROWS = 4
NA_BAND = 16
NEG_BIG = -1e30
VMEM_LIMIT = 48 * 1024 * 1024


def _cparams(sem):
    return pltpu.CompilerParams(dimension_semantics=sem, vmem_limit_bytes=VMEM_LIMIT)


def _dot(a, b):
    return jnp.dot(a, b, preferred_element_type=F32)


def _dot_nt(a, b):
    return lax.dot_general(a, b, (((1,), (1,)), ((), ())), preferred_element_type=F32)


def _dot_tn(a, b):
    return lax.dot_general(a, b, (((0,), (0,)), ((), ())), preferred_element_type=F32)


def _split_dot(x, w_exact):
    hi = x.astype(BF16)
    lo = (x - hi.astype(F32)).astype(BF16)
    return _dot(hi, w_exact) + _dot(lo, w_exact)


def _split_dot_left(w_exact, x):
    hi = x.astype(BF16)
    lo = (x - hi.astype(F32)).astype(BF16)
    return _dot(w_exact, hi) + _dot(w_exact, lo)


def _sigmoid(x):
    return 1.0 / (1.0 + jnp.exp(-x))


def _row_tile(tall):
    for tm in (640, 256):
        if tall % tm == 0:
            return tm
    raise ValueError(f"unsupported token count {tall}")


def _da_subtiles(tk):
    rem = tk // MXU_DIM
    sizes = []
    for ramp in range(1, DA_SUB_TILES):
        if rem > 0:
            sizes.append(min(ramp, rem))
            rem -= sizes[-1]
    while rem > DA_SUB_TILES + 1:
        sizes.append(DA_SUB_TILES)
        rem -= DA_SUB_TILES
    if rem >= 2:
        sizes += [rem - 1, 1]
    elif rem == 1:
        sizes.append(1)
    starts = np.cumsum([0] + sizes[:-1])
    return [(int(s) * MXU_DIM, int(n) * MXU_DIM) for s, n in zip(starts, sizes)]


def _mod_body(c_ref, w_ref, b_ref, o_ref):
    cv = c_ref[...]
    s = cv * _sigmoid(cv)
    o_ref[0] = _dot(s.astype(BF16), w_ref[0].astype(BF16)) + b_ref[0]


def _mod_call(cvec, w_mod, b_mod):
    depth, d, n = w_mod.shape
    tn = 1536
    return pl.pallas_call(
        _mod_body,
        grid=(depth, n // tn),
        in_specs=[pl.BlockSpec((8, d), lambda l, j: (0, 0)),
                  pl.BlockSpec((1, d, tn), lambda l, j: (l, 0, j)),
                  pl.BlockSpec((1, 1, tn), lambda l, j: (l, 0, j))],
        out_specs=pl.BlockSpec((1, 8, tn), lambda l, j: (l, 0, j)),
        out_shape=jax.ShapeDtypeStruct((depth, 8, n), F32),
        compiler_params=_cparams(("parallel", "parallel")),
        name="mod",
    )(cvec, w_mod, b_mod.reshape(depth, 1, n))


def _rope_body(freq_ref, cos_ref, sin_ref, *, t_lat, tm):
    i = pl.program_id(0)
    t = i * tm + lax.broadcasted_iota(jnp.int32, (tm, LANES), 0)
    lane = lax.broadcasted_iota(jnp.int32, (tm, LANES), 1)
    shift = int(math.log2(GRID_W))
    row = lax.shift_right_logical(t, shift).astype(F32)
    col = jnp.bitwise_and(t, GRID_W - 1).astype(F32)
    use_row = jnp.bitwise_and(lane, 2 * ROPE_PAIRS) == 0
    first_half = jnp.bitwise_and(lane, ROPE_PAIRS) == 0
    ang = jnp.where(use_row, row, col) * freq_ref[...]
    is_lat = t < t_lat
    cos_ref[...] = jnp.where(is_lat, jnp.cos(ang), 1.0)
    sn = jnp.sin(ang)
    sin_ref[...] = jnp.where(is_lat, jnp.where(first_half, -sn, sn), 0.0)


def _rope_tables(t_lat, tall):
    tm = 256
    freqs = ROPE_BASE ** (-jnp.arange(ROPE_PAIRS, dtype=F32) / ROPE_PAIRS)
    freq_lane = jnp.tile(freqs, LANES // ROPE_PAIRS).reshape(1, LANES)
    return pl.pallas_call(
        functools.partial(_rope_body, t_lat=t_lat, tm=tm),
        grid=(tall // tm,),
        in_specs=[pl.BlockSpec((1, LANES), lambda i: (0, 0))],
        out_specs=[pl.BlockSpec((tm, LANES), lambda i: (i, 0))] * 2,
        out_shape=[jax.ShapeDtypeStruct((tall, LANES), F32)] * 2,
        compiler_params=_cparams(("parallel",)),
        name="rope_tables",
    )(freq_lane)


def _is_ctx_rows(r, tm, t_lat):
    t = r * tm + lax.broadcasted_iota(jnp.int32, (tm, 1), 0)
    return t >= t_lat


def _norm_mod(x, g, sc, sh):
    ms = jnp.mean(x * x, axis=-1, keepdims=True)
    return (x * lax.rsqrt(ms + EPS) * g) * (1.0 + sc) + sh


def _segnorm64(z, bd, gain):
    ss = _split_dot(z * z, bd)
    return z * lax.rsqrt(ss * (1.0 / HEAD_DIM) + EPS) * gain


def _hnorm_body(x_ref, g_ref, mb_ref, mc_ref, h_ref, *, t_lat, tm, d):
    r = pl.program_id(1)
    ctx = _is_ctx_rows(r, tm, t_lat)
    sh = jnp.where(ctx, mc_ref[0, :, 0:d], mb_ref[0, :, 0:d])
    sc = jnp.where(ctx, mc_ref[0, :, d:2 * d], mb_ref[0, :, d:2 * d])
    h_ref[0] = _norm_mod(x_ref[0], g_ref[...], sc, sh).astype(BF16)


def _hnorm_call(xall, g, mod3, t_lat):
    b, tall, d = xall.shape
    tm = _row_tile(tall)
    nb = b
    return pl.pallas_call(
        functools.partial(_hnorm_body, t_lat=t_lat, tm=tm, d=d),
        grid=(b, tall // tm),
        in_specs=[pl.BlockSpec((1, tm, d), lambda bi, r: (bi, r, 0)),
                  pl.BlockSpec((1, d), lambda bi, r: (0, 0)),
                  pl.BlockSpec((1, 1, 6 * d), lambda bi, r: (bi, 0, 0)),
                  pl.BlockSpec((1, 1, 6 * d), lambda bi, r: (nb, 0, 0))],
        out_specs=pl.BlockSpec((1, tm, d), lambda bi, r: (bi, r, 0)),
        out_shape=jax.ShapeDtypeStruct((b, tall, d), BF16),
        compiler_params=_cparams(("parallel", "parallel")),
        name="hnorm",
    )(xall, g.reshape(1, d), mod3, mod3)


def _rope128(x, cs, sn, first_half):
    partner = jnp.where(first_half, pltpu.roll(x, LANES - ROPE_PAIRS, 1), pltpu.roll(x, ROPE_PAIRS, 1))
    return x * cs + partner * sn


def _proj_da_body(h_ref, wq_ref, wk_ref, wv_ref, bd_ref, gq_ref, gk_ref, cos_ref, sin_ref,
                  qt_ref, k_ref, vt_ref, nrm_ref):
    h = h_ref[0]
    tm = h.shape[0]
    bd = bd_ref[...]
    cs = cos_ref[...]
    sn = sin_ref[...]
    lane = lax.broadcasted_iota(jnp.int32, cs.shape, 1)
    first_half = jnp.bitwise_and(lane, ROPE_PAIRS) == 0
    zq = _dot(h, wq_ref[...])
    zk = _dot(h, wk_ref[...])
    zv = _dot(h, wv_ref[...])
    scale = math.sqrt(HEAD_DIM ** -0.5 * LOG2E)
    ones_row = (lax.broadcasted_iota(jnp.int32, (DA_VPAD, tm), 0) == 0).astype(BF16)
    low_lanes = lane < HEAD_DIM
    qn2 = jnp.zeros((1, LANES), F32)
    kn2 = jnp.zeros((1, LANES), F32)
    for hd in range(DA_HEADS):
        sl = slice(hd * LANES, (hd + 1) * LANES)
        q = _segnorm64(zq[:, sl], bd, gq_ref[...]) * scale
        q = _rope128(q, cs, sn, first_half)
        q_hi = q.astype(F8).astype(F32)
        qh_t = q_hi.T
        ql_t = (q - q_hi).T
        r0 = hd * 2 * LANES
        for a in range(2):
            qt_ref[0, r0 + a * LANES:r0 + a * LANES + HEAD_DIM, :] = qh_t[a * HEAD_DIM:(a + 1) * HEAD_DIM].astype(F8)
            qt_ref[0, r0 + a * LANES + HEAD_DIM:r0 + (a + 1) * LANES, :] = ql_t[a * HEAD_DIM:(a + 1) * HEAD_DIM].astype(F8)
        k = _segnorm64(zk[:, sl], bd, gk_ref[...]) * scale
        k = _rope128(k, cs, sn, first_half)
        k_hi = k.astype(F8).astype(F32)
        k_lo = k - k_hi
        k_ref[0, :, r0:r0 + LANES] = jnp.where(low_lanes, k_hi, pltpu.roll(k_lo, HEAD_DIM, 1)).astype(F8)
        k_ref[0, :, r0 + LANES:r0 + 2 * LANES] = jnp.where(low_lanes, pltpu.roll(k_hi, HEAD_DIM, 1), k_lo).astype(F8)
        qn2 = jnp.maximum(qn2, jnp.max(_split_dot(q * q, bd), axis=0, keepdims=True))
        kn2 = jnp.maximum(kn2, jnp.max(_split_dot(k * k, bd), axis=0, keepdims=True))
        v0 = hd * DA_VROWS
        vt_ref[0, v0:v0 + DA_V, :] = zv[:, sl].T.astype(BF16)
        vt_ref[0, v0 + DA_V:v0 + DA_VROWS, :] = ones_row
    nrm_ref[0, 0:1, :] = qn2
    nrm_ref[0, 1:2, :] = kn2


def _proj_da_call(h, wq, wk, wv, bd, gq, gk, cos_t, sin_t):
    b, tall, d = h.shape
    tm = _row_tile(tall)
    nt = tall // tm
    w = DA_HEADS * LANES
    const = lambda bi, r: (0, 0)
    return pl.pallas_call(
        _proj_da_body,
        grid=(b, tall // tm),
        in_specs=[pl.BlockSpec((1, tm, d), lambda bi, r: (bi, r, 0)),
                  pl.BlockSpec((d, w), const), pl.BlockSpec((d, w), const), pl.BlockSpec((d, w), const),
                  pl.BlockSpec((LANES, LANES), const),
                  pl.BlockSpec((1, LANES), const), pl.BlockSpec((1, LANES), const),
                  pl.BlockSpec((tm, LANES), lambda bi, r: (r, 0)),
                  pl.BlockSpec((tm, LANES), lambda bi, r: (r, 0))],
        out_specs=[pl.BlockSpec((1, 2 * w, tm), lambda bi, r: (bi, 0, r)),
                   pl.BlockSpec((1, tm, 2 * w), lambda bi, r: (bi, r, 0)),
                   pl.BlockSpec((1, DA_HEADS * DA_VROWS, tm), lambda bi, r: (bi, 0, r)),
                   pl.BlockSpec((1, 2, LANES), lambda bi, r: (bi * nt + r, 0, 0))],
        out_shape=[jax.ShapeDtypeStruct((b, 2 * w, tall), F8),
                   jax.ShapeDtypeStruct((b, tall, 2 * w), F8),
                   jax.ShapeDtypeStruct((b, DA_HEADS * DA_VROWS, tall), BF16),
                   jax.ShapeDtypeStruct((b * nt, 2, LANES), F32)],
        compiler_params=_cparams(("parallel", "parallel")),
        name="proj_da",
    )(h, wq, wk, wv, bd, gq, gk, cos_t, sin_t)


def _proj_gla_body(h_ref, wq_ref, wk_ref, wv_ref, wg_ref, wa_ref, a2_ref, ab_ref,
                   q_ref, k_ref, v_ref, sg_ref, la_ref):
    h = h_ref[0]
    q_ref[0] = _dot(h, wq_ref[...]) * (GLA_DK ** -0.5)
    k_ref[0] = _dot(h, wk_ref[...])
    v_ref[0] = _dot(h, wv_ref[...]).astype(BF16)
    g = _dot(h, wg_ref[...])
    sg_ref[0] = (g * _sigmoid(g)).astype(BF16)
    ga = _dot(h, wa_ref[...])
    z = _dot(ga.astype(BF16), a2_ref[...]) + ab_ref[...]
    la_ref[0] = (jnp.minimum(z, 0.0) - jnp.log1p(jnp.exp(-jnp.abs(z)))) * (1.0 / GLA_TAU)


def _proj_gla_call(h, wq, wk, wv, wg, wa, a2p, ab):
    b, tall, d = h.shape
    tm = _row_tile(tall)
    wqk = GLA_HEADS * GLA_DK
    wv_ = GLA_HEADS * GLA_DV
    const = lambda bi, r: (0, 0)
    row = lambda n: pl.BlockSpec((1, tm, n), lambda bi, r: (bi, r, 0))
    return pl.pallas_call(
        _proj_gla_body,
        grid=(b, tall // tm),
        in_specs=[row(d),
                  pl.BlockSpec((d, wqk), const), pl.BlockSpec((d, wqk), const),
                  pl.BlockSpec((d, wv_), const), pl.BlockSpec((d, wv_), const),
                  pl.BlockSpec((d, LANES), const),
                  pl.BlockSpec((LANES, 2 * wqk), const), pl.BlockSpec((1, 2 * wqk), const)],
        out_specs=[row(wqk), row(wqk), row(wv_), row(wv_), row(2 * wqk)],
        out_shape=[jax.ShapeDtypeStruct((b, tall, wqk), F32),
                   jax.ShapeDtypeStruct((b, tall, wqk), F32),
                   jax.ShapeDtypeStruct((b, tall, wv_), BF16),
                   jax.ShapeDtypeStruct((b, tall, wv_), BF16),
                   jax.ShapeDtypeStruct((b, tall, 2 * wqk), F32)],
        compiler_params=_cparams(("parallel", "parallel")),
        name="proj_gla",
    )(h, wq, wk, wv, wg, wa, a2p, ab)


def _proj_na_body(h_ref, wq_ref, wk_ref, wv_ref, bd_ref, gq_ref, gk_ref, q_ref, k_ref, v_ref):
    h = h_ref[0]
    bd = bd_ref[...]
    zq = _dot(h, wq_ref[...])
    zk = _dot(h, wk_ref[...])
    scale = HEAD_DIM ** -0.5
    for j in range(NA_HEADS * HEAD_DIM // LANES):
        sl = slice(j * LANES, (j + 1) * LANES)
        q_ref[0, :, sl] = (_segnorm64(zq[:, sl], bd, gq_ref[...]) * scale).astype(BF16)
        k_ref[0, :, sl] = _segnorm64(zk[:, sl], bd, gk_ref[...]).astype(BF16)
    v_ref[0] = _dot(h, wv_ref[...]).astype(BF16)


def _proj_na_call(h, wq, wk, wv, bd, gq, gk):
    b, tall, d = h.shape
    tm = _row_tile(tall)
    w = NA_HEADS * HEAD_DIM
    const = lambda bi, r: (0, 0)
    row = lambda n: pl.BlockSpec((1, tm, n), lambda bi, r: (bi, r, 0))
    return pl.pallas_call(
        _proj_na_body,
        grid=(b, tall // tm),
        in_specs=[row(d), pl.BlockSpec((d, w), const), pl.BlockSpec((d, w), const), pl.BlockSpec((d, w), const),
                  pl.BlockSpec((LANES, LANES), const),
                  pl.BlockSpec((1, LANES), const), pl.BlockSpec((1, LANES), const)],
        out_specs=[row(w), row(w), row(w)],
        out_shape=[jax.ShapeDtypeStruct((b, tall, w), BF16)] * 3,
        compiler_params=_cparams(("parallel", "parallel")),
        name="proj_na",
    )(h, wq, wk, wv, bd, gq, gk)


def _proj_gate_body(h_ref, w_ref, o_ref):
    h = h_ref[0]
    n = w_ref.shape[1]
    step = 512
    for j in range(n // step):
        sl = slice(j * step, (j + 1) * step)
        o_ref[0, :, sl] = _sigmoid(_dot(h, w_ref[:, sl])).astype(BF16)


def _proj_gate_call(h, w):
    b, tall, d = h.shape
    tm = _row_tile(tall)
    n = w.shape[1]
    return pl.pallas_call(
        _proj_gate_body,
        grid=(b, tall // tm),
        in_specs=[pl.BlockSpec((1, tm, d), lambda bi, r: (bi, r, 0)),
                  pl.BlockSpec((d, n), lambda bi, r: (0, 0))],
        out_specs=pl.BlockSpec((1, tm, n), lambda bi, r: (bi, r, 0)),
        out_shape=jax.ShapeDtypeStruct((b, tall, n), BF16),
        compiler_params=_cparams(("parallel", "parallel")),
        name="proj_gate",
    )(h, w)


def _da_body(safe_ref, qt_ref, k_ref, vt_ref, lp_ref, sg_ref, o_ref, m_ref, a_ref, acc_ref, *, lam_init, nk):
    j = pl.program_id(3)
    use_lag = safe_ref[0] > 0.0

    @pl.when(j == 0)
    def _():
        m_ref[...] = jnp.full(m_ref.shape, jnp.where(use_lag, -safe_ref[0], -jnp.inf), F32)
        a_ref[...] = jnp.ones(a_ref.shape, F32)
        acc_ref[...] = jnp.zeros(acc_ref.shape, F32)

    def scores(a, rows):
        kb = k_ref[0, rows, a * LANES:(a + 1) * LANES]
        q_hi = qt_ref[0, a * LANES:a * LANES + HEAD_DIM, :]
        q_lo = qt_ref[0, a * LANES + HEAD_DIM:(a + 1) * LANES, :]
        w = jnp.concatenate([q_hi, q_hi, q_lo, q_lo], axis=0)
        return _dot(jnp.concatenate([kb, kb], axis=1), w)

    def lagged_step():
        tk = k_ref.shape[1]
        m_old = [m_ref[a:a + 1, :] for a in range(2)]
        pv = [None, None]
        xmax = [None, None]
        units = [(a, slice(c0, c0 + cn)) for c0, cn in _da_subtiles(tk) for a in range(2)]
        s_next = scores(*units[0])
        for u, (a, rows) in enumerate(units):
            s = s_next
            if u + 1 < len(units):
                s_next = scores(*units[u + 1])
            x = (s - m_old[a]).astype(BF16)
            d = _dot(vt_ref[0, :, rows], jnp.exp2(x))
            xm = jnp.max(x, axis=0, keepdims=True)
            pv[a] = d if pv[a] is None else pv[a] + d
            xmax[a] = xm if xmax[a] is None else jnp.maximum(xmax[a], xm)
        for a in range(2):
            acc_ref[a] = a_ref[a:a + 1, :] * acc_ref[a] + pv[a]
            shift = jnp.maximum(xmax[a].astype(F32), 0.0)
            a_ref[a:a + 1, :] = jnp.exp2(-shift)
            m_ref[a:a + 1, :] = m_old[a] + shift

    def plain_step():
        vt = vt_ref[0]
        for a in range(2):
            s = scores(a, slice(None))
            m_old = m_ref[a:a + 1, :]
            m_new = jnp.maximum(m_old, jnp.max(s, axis=0, keepdims=True))
            p = jnp.exp2(s - m_new)
            acc_ref[a] = (a_ref[a:a + 1, :] * jnp.exp2(m_old - m_new)) * acc_ref[a] + _dot(vt, p.astype(BF16))
            a_ref[a:a + 1, :] = jnp.ones_like(m_old)
            m_ref[a:a + 1, :] = m_new

    pl.when(use_lag)(lagged_step)
    pl.when(jnp.logical_not(use_lag))(plain_step)

    @pl.when(j == nk - 1)
    def _():
        lp = lp_ref[...]
        e1 = jnp.exp(jnp.sum(lp[0:1] * lp[1:2], axis=-1, keepdims=True))
        e2 = jnp.exp(jnp.sum(lp[2:3] * lp[3:4], axis=-1, keepdims=True))
        lam = e1 - e2 + lam_init
        o1 = acc_ref[0, 0:DA_V, :] / acc_ref[0, DA_V:DA_V + 1, :]
        o2 = acc_ref[1, 0:DA_V, :] / acc_ref[1, DA_V:DA_V + 1, :]
        o = o1 - lam * o2
        ms = jnp.mean(o * o, axis=0, keepdims=True)
        y = (o * lax.rsqrt(ms + EPS) * sg_ref[...]) * (1.0 - lam_init)
        o_ref[0] = y.T.astype(BF16)


def _da_call(safe, qt, kk, vt, lp, subg, lam_init, *, q_off, nq, k_off, nk, tq, tk):
    b = qt.shape[0]
    return pl.pallas_call(
        functools.partial(_da_body, lam_init=lam_init, nk=nk),
        grid=(b, DA_HEADS, nq, nk),
        in_specs=[pl.BlockSpec(memory_space=pltpu.SMEM),
                  pl.BlockSpec((1, 2 * LANES, tq), lambda bi, h, i, j: (bi, h, i + q_off)),
                  pl.BlockSpec((1, tk, 2 * LANES), lambda bi, h, i, j: (bi, j + k_off, h)),
                  pl.BlockSpec((1, DA_VROWS, tk), lambda bi, h, i, j: (bi, h, j + k_off)),
                  pl.BlockSpec((4, HEAD_DIM), lambda bi, h, i, j: (0, 0)),
                  pl.BlockSpec((LANES, 1), lambda bi, h, i, j: (0, 0))],
        out_specs=pl.BlockSpec((1, tq, LANES), lambda bi, h, i, j: (bi, i, h)),
        out_shape=jax.ShapeDtypeStruct((b, nq * tq, DA_HEADS * LANES), BF16),
        scratch_shapes=[pltpu.VMEM((8, tq), F32), pltpu.VMEM((8, tq), F32),
                        pltpu.VMEM((2, DA_VROWS, tq), F32)],
        compiler_params=_cparams(("parallel", "parallel", "parallel", "arbitrary")),
        name="diff_attn",
    )(safe, qt, kk, vt, lp, subg)


def _gla_body(q_ref, k_ref, v_ref, la_ref, tri_ref, o_ref, s_ref, *, reverse, tb):
    i = pl.program_id(1)

    @pl.when(i == 0)
    def _():
        s_ref[...] = jnp.zeros(s_ref.shape, F32)

    c = GLA_CHUNK
    w = GLA_HEADS * GLA_DK
    tri = tri_ref[...]
    ri = lax.broadcasted_iota(jnp.int32, (c, c), 0)
    ci = lax.broadcasted_iota(jnp.int32, (c, c), 1)
    keep = (ci >= ri) if reverse else (ci <= ri)
    lane = lax.broadcasted_iota(jnp.int32, (1, w), 1)
    chunks = range(tb // c)
    for ch in (reversed(chunks) if reverse else chunks):
        rows = slice(ch * c, (ch + 1) * c)
        la = la_ref[0, rows, :]
        cum = _split_dot_left(tri, la)
        cl = cum[0:1, :] if reverse else cum[c - 1:c, :]
        q = q_ref[0, rows, :]
        k = k_ref[0, rows, :]
        qe = q * jnp.exp(cum)
        ke = (k * jnp.exp(-cum)).astype(BF16)
        kd = (k * jnp.exp(cl - cum)).astype(BF16)
        dec = jnp.exp(cl)
        for hd in range(GLA_HEADS):
            hm = (lane >= hd * GLA_DK) & (lane < (hd + 1) * GLA_DK)
            qh = jnp.where(hm, qe, 0.0).astype(BF16)
            a = jnp.where(keep, _dot_nt(qh, ke), 0.0)
            vh = v_ref[0, rows, hd * GLA_DV:(hd + 1) * GLA_DV]
            st = s_ref[hd]
            o = _dot(a.astype(BF16), vh) + _dot_nt(qh, st.astype(BF16))
            o_ref[0, rows, hd * GLA_DV:(hd + 1) * GLA_DV] = o
            s_ref[hd] = st * dec + _dot_tn(vh, kd)


def _gla_call(gq, gk, gv, la, tri, *, t_lat, reverse):
    b, tall, w = gq.shape
    tb = 256
    n_lat = t_lat // tb
    nblk = tall // tb
    if reverse:
        blk = lambda i: jnp.where(i == 0, n_lat, n_lat - i)
    else:
        blk = lambda i: jnp.where(i == 0, n_lat, i - 1)
    wv_ = GLA_HEADS * GLA_DV
    return pl.pallas_call(
        functools.partial(_gla_body, reverse=reverse, tb=tb),
        grid=(b, nblk),
        in_specs=[pl.BlockSpec((1, tb, w), lambda bi, i: (bi, blk(i), 0)),
                  pl.BlockSpec((1, tb, w), lambda bi, i: (bi, blk(i), 0)),
                  pl.BlockSpec((1, tb, wv_), lambda bi, i: (bi, blk(i), 0)),
                  pl.BlockSpec((1, tb, w), lambda bi, i: (bi, blk(i), 1 if reverse else 0)),
                  pl.BlockSpec((GLA_CHUNK, GLA_CHUNK), lambda bi, i: (0, 0))],
        out_specs=pl.BlockSpec((1, tb, wv_), lambda bi, i: (bi, blk(i), 0)),
        out_shape=jax.ShapeDtypeStruct((b, tall, wv_), F32),
        scratch_shapes=[pltpu.VMEM((GLA_HEADS, GLA_DV, w), F32)],
        compiler_params=_cparams(("parallel", "arbitrary")),
        name="gla_bwd" if reverse else "gla_fwd",
    )(gq, gk, gv, la, tri)


def _na_body(q_ref, k_ref, v_ref, kc_ref, vc_ref, bias_ref, o_ref, *, rows):
    i = pl.program_id(2)
    kb0 = jnp.clip(i * NA_QROWS - NA_KR // 2, 0, rows - NA_BAND)
    start = pl.multiple_of(kb0 * GRID_W, GRID_W)
    nband = NA_BAND * GRID_W
    q = q_ref[0]
    kb = k_ref[0, pl.ds(start, nband), :]
    vb = v_ref[0, pl.ds(start, nband), :]
    kc = kc_ref[0]
    vc = vc_ref[0]
    lane = lax.broadcasted_iota(jnp.int32, q.shape, 1)
    outs = []
    for hh in range(2):
        qh = jnp.where((lane < HEAD_DIM) if hh == 0 else (lane >= HEAD_DIM), q, jnp.zeros_like(q))
        s_loc = _dot_nt(qh, kb) + bias_ref[0, hh]
        s_ctx = _dot_nt(qh, kc)
        m = jnp.maximum(jnp.max(s_loc, axis=-1, keepdims=True), jnp.max(s_ctx, axis=-1, keepdims=True))
        p_loc = jnp.exp(s_loc - m)
        p_ctx = jnp.exp(s_ctx - m)
        l = jnp.sum(p_loc, axis=-1, keepdims=True) + jnp.sum(p_ctx, axis=-1, keepdims=True)
        outs.append((_dot(p_loc.astype(BF16), vb) + _dot(p_ctx.astype(BF16), vc)) / l)
    o_ref[0] = jnp.where(lane < HEAD_DIM, outs[0], outs[1]).astype(BF16)


def _na_call(nq, nk, nv, bias, *, t_lat):
    b, tall, w = nq.shape
    tc = tall - t_lat
    rows = t_lat // GRID_W
    tq = NA_QROWS * GRID_W
    nsteps = rows // NA_QROWS
    npair = w // LANES
    ctx_blk = t_lat // tc

    def btype(i):
        return jnp.where(i == 0, 0, jnp.where(i == nsteps - 2, 2, jnp.where(i == nsteps - 1, 3, 1)))

    return pl.pallas_call(
        functools.partial(_na_body, rows=rows),
        grid=(b, npair, nsteps),
        in_specs=[pl.BlockSpec((1, tq, LANES), lambda bi, hp, i: (bi, i, hp)),
                  pl.BlockSpec((1, t_lat, LANES), lambda bi, hp, i: (bi, 0, hp)),
                  pl.BlockSpec((1, t_lat, LANES), lambda bi, hp, i: (bi, 0, hp)),
                  pl.BlockSpec((1, tc, LANES), lambda bi, hp, i: (bi, ctx_blk, hp)),
                  pl.BlockSpec((1, tc, LANES), lambda bi, hp, i: (bi, ctx_blk, hp)),
                  pl.BlockSpec((1, 2, tq, NA_BAND * GRID_W), lambda bi, hp, i: (btype(i), hp, 0, 0))],
        out_specs=pl.BlockSpec((1, tq, LANES), lambda bi, hp, i: (bi, i, hp)),
        out_shape=jax.ShapeDtypeStruct((b, t_lat, w), BF16),
        compiler_params=_cparams(("parallel", "parallel", "arbitrary")),
        name="nbr_attn",
    )(nq, nk, nv, nk, nv, bias)


def _na_bias_tiles(rpb, rows):
    assert rows >= NA_BAND and rows % NA_QROWS == 0
    nh, na, nb = rpb.shape
    cidx = np.arange(GRID_W)
    rel_c = cidx[None, :] - cidx[:, None] + NA_KC - 1
    sel = jnp.asarray(rel_c[None] == np.arange(nb)[:, None, None], F32)
    toep = jnp.einsum('hab,bqk->haqk', rpb.astype(F32), sel, precision=lax.Precision.HIGHEST)
    toep = jnp.pad(toep, ((0, 0), (NA_BAND, NA_BAND), (0, 0), (0, 0)))
    tiles = []
    for r0 in (0, NA_QROWS, rows - 2 * NA_QROWS, rows - NA_QROWS):
        kb0 = min(max(r0 - NA_KR // 2, 0), rows - NA_BAND)
        qi = np.arange(NA_QROWS * GRID_W)
        qr, qc = r0 + qi // GRID_W, qi % GRID_W
        kj = np.arange(NA_BAND * GRID_W)
        kr, kc = kb0 + kj // GRID_W, kj % GRID_W
        rs = np.clip(qr - NA_KR // 2, 0, rows - NA_KR)
        cs = np.clip(qc - NA_KC // 2, 0, GRID_W - NA_KC)
        valid = ((kr[None, :] >= rs[:, None]) & (kr[None, :] < rs[:, None] + NA_KR)
                 & (kc[None, :] >= cs[:, None]) & (kc[None, :] < cs[:, None] + NA_KC))
        parts = []
        for q_row in range(NA_QROWS):
            a0 = kb0 - (r0 + q_row) + NA_KR - 1 + NA_BAND
            blk = toep[:, a0:a0 + NA_BAND]
            parts.append(jnp.transpose(blk, (0, 2, 1, 3)).reshape(nh, GRID_W, NA_BAND * GRID_W))
        tile = jnp.concatenate(parts, axis=1)
        tiles.append(jnp.where(jnp.asarray(valid)[None], tile, NEG_BIG))
    return jnp.stack(tiles)


def _ctx_attn_body(q_ref, k_ref, v_ref, o_ref):
    q = q_ref[0]
    k = k_ref[0]
    v = v_ref[0]
    lane = lax.broadcasted_iota(jnp.int32, q.shape, 1)
    outs = []
    for hh in range(2):
        qh = jnp.where((lane < HEAD_DIM) if hh == 0 else (lane >= HEAD_DIM), q, jnp.zeros_like(q))
        s = _dot_nt(qh, k)
        m = jnp.max(s, axis=-1, keepdims=True)
        p = jnp.exp(s - m)
        outs.append(_dot(p.astype(BF16), v) / jnp.sum(p, axis=-1, keepdims=True))
    o_ref[0] = jnp.where(lane < HEAD_DIM, outs[0], outs[1]).astype(BF16)


def _ctx_attn_call(nq, nk, nv, *, t_lat):
    b, tall, w = nq.shape
    tc = tall - t_lat
    ctx_blk = t_lat // tc
    spec = pl.BlockSpec((1, tc, LANES), lambda bi, hp: (bi, ctx_blk, hp))
    return pl.pallas_call(
        _ctx_attn_body,
        grid=(b, w // LANES),
        in_specs=[spec, spec, spec],
        out_specs=pl.BlockSpec((1, tc, LANES), lambda bi, hp: (bi, 0, hp)),
        out_shape=jax.ShapeDtypeStruct((b, tc, w), BF16),
        compiler_params=_cparams(("parallel", "parallel")),
        name="ctx_attn",
    )(nq, nk, nv)


def _merge_body(x_ref, yd_ref, of_ref, ob_ref, sg_ref, yn_ref, gt_ref, wd_ref, wg_ref, wn_ref, wo_ref,
                gn_ref, mb_ref, mc_ref, o_ref, *, t_lat, tm, d):
    r = pl.program_id(1)
    ctx = _is_ctx_rows(r, tm, t_lat)
    g1 = jnp.where(ctx, mc_ref[0, :, 2 * d:3 * d], mb_ref[0, :, 2 * d:3 * d])
    og = of_ref[0] + ob_ref[0]
    parts = []
    for hd in range(GLA_HEADS):
        oh = og[:, hd * GLA_DV:(hd + 1) * GLA_DV]
        ms = jnp.mean(oh * oh, axis=-1, keepdims=True)
        parts.append(oh * lax.rsqrt(ms + EPS) * gn_ref[...])
    yg = (jnp.concatenate(parts, axis=-1) * sg_ref[0].astype(F32)).astype(BF16)
    m = (gt_ref[0, :, 0:d].astype(F32) * _dot(yd_ref[0], wd_ref[...])
         + gt_ref[0, :, d:2 * d].astype(F32) * _dot(yg, wg_ref[...])
         + gt_ref[0, :, 2 * d:3 * d].astype(F32) * _dot(yn_ref[0], wn_ref[...]))
    o_ref[0] = x_ref[0] + g1 * _dot(m.astype(BF16), wo_ref[...])


def _merge_call(xall, yd, of, ob, sg, yn, gates, wd, wg, wn, wo, gn, mod3, t_lat):
    b, tall, d = xall.shape
    tm = _row_tile(tall)
    nb = b
    const = lambda bi, r: (0, 0)
    row = lambda n: pl.BlockSpec((1, tm, n), lambda bi, r: (bi, r, 0))
    return pl.pallas_call(
        functools.partial(_merge_body, t_lat=t_lat, tm=tm, d=d),
        grid=(b, tall // tm),
        in_specs=[row(d), row(BR_W), row(BR_W), row(BR_W), row(BR_W), row(BR_W), row(3 * d),
                  pl.BlockSpec((BR_W, d), const), pl.BlockSpec((BR_W, d), const), pl.BlockSpec((BR_W, d), const),
                  pl.BlockSpec((d, d), const), pl.BlockSpec((1, GLA_DV), const),
                  pl.BlockSpec((1, 1, 6 * d), lambda bi, r: (bi, 0, 0)),
                  pl.BlockSpec((1, 1, 6 * d), lambda bi, r: (nb, 0, 0))],
        out_specs=row(d),
        out_shape=jax.ShapeDtypeStruct((b, tall, d), F32),
        compiler_params=_cparams(("parallel", "parallel")),
        name="merge",
    )(xall, yd, of, ob, sg, yn, gates, wd, wg, wn, wo, gn, mod3, mod3)


def _mlp_body(x_ref, g_ref, w1_ref, w2_ref, mb_ref, mc_ref, o_ref, *, t_lat, tm, d):
    r = pl.program_id(1)
    ctx = _is_ctx_rows(r, tm, t_lat)
    sh = jnp.where(ctx, mc_ref[0, :, 3 * d:4 * d], mb_ref[0, :, 3 * d:4 * d])
    sc = jnp.where(ctx, mc_ref[0, :, 4 * d:5 * d], mb_ref[0, :, 4 * d:5 * d])
    g2 = jnp.where(ctx, mc_ref[0, :, 5 * d:6 * d], mb_ref[0, :, 5 * d:6 * d])
    x = x_ref[0]
    h = _norm_mod(x, g_ref[...], sc, sh).astype(BF16)
    acc = jnp.zeros((tm, d), F32)
    step = 1024
    for j in range(w1_ref.shape[1] // step):
        a = jnp.maximum(_dot(h, w1_ref[:, j * step:(j + 1) * step]), 0.0)
        acc = acc + _dot((a * a).astype(BF16), w2_ref[j * step:(j + 1) * step, :])
    o_ref[0] = x + g2 * acc


def _mlp_call(xall, g, w1, w2, mod3, t_lat):
    b, tall, d = xall.shape
    tm = _row_tile(tall)
    nb = b
    dff = w1.shape[1]
    const = lambda bi, r: (0, 0)
    return pl.pallas_call(
        functools.partial(_mlp_body, t_lat=t_lat, tm=tm, d=d),
        grid=(b, tall // tm),
        in_specs=[pl.BlockSpec((1, tm, d), lambda bi, r: (bi, r, 0)),
                  pl.BlockSpec((1, d), const),
                  pl.BlockSpec((d, dff), const, pipeline_mode=pl.Buffered(1)),
                  pl.BlockSpec((dff, d), const, pipeline_mode=pl.Buffered(1)),
                  pl.BlockSpec((1, 1, 6 * d), lambda bi, r: (bi, 0, 0)),
                  pl.BlockSpec((1, 1, 6 * d), lambda bi, r: (nb, 0, 0))],
        out_specs=pl.BlockSpec((1, tm, d), lambda bi, r: (bi, r, 0)),
        out_shape=jax.ShapeDtypeStruct((b, tall, d), F32),
        compiler_params=_cparams(("parallel", "parallel")),
        name="mlp",
    )(xall, g.reshape(1, d), w1, w2, mod3, mod3)


def kernel(x, c, ctx, c_ctx, w_mod, b_mod, norm1_g, norm2_g, w_in, da_qn_g, da_kn_g, da_lambda, da_subln_g,
           gla_a2, gla_a_b, gla_gn_g, na_qn_g, na_kn_g, na_rpb, w_br_da, w_br_gla, w_br_na, w_out, w_ff1, w_ff2):
    b, t_lat, d = x.shape
    tc = ctx.shape[1]
    tall = t_lat + tc
    depth = w_mod.shape[0]
    rows = t_lat // GRID_W
    assert d == D_MODEL and t_lat % (2 * tc) == 0 and tc == 256 and b < 8

    xall = jnp.concatenate([x, ctx], axis=1)
    cvec = jnp.zeros((8, d), F32).at[0:b].set(c).at[b].set(c_ctx)
    mod = _mod_call(cvec, w_mod, b_mod)
    cos_t, sin_t = _rope_tables(t_lat, tall)

    seg = np.arange(LANES) // HEAD_DIM
    bd = jnp.asarray(seg[:, None] == seg[None, :], BF16)
    ci = np.arange(GLA_CHUNK)
    tri_f = jnp.asarray(ci[None, :] <= ci[:, None], BF16)
    tri_b = jnp.asarray(ci[None, :] >= ci[:, None], BF16)

    o_dq, o_dk, o_dv = 0, 512, 1024
    o_gq, o_gk, o_gv, o_gg, o_ga = 1536, 1792, 2048, 2560, 3072
    o_nq, o_nk, o_nv = 3104, 3616, 4128
    o_gate = 4640
    tq_da = 512 if t_lat % 512 == 0 else 256
    tk_da = next(t for t in (3328, 1280, 256) if tall % t == 0)

    for l in range(depth):
        need_ctx = l < depth - 1
        lam_init = 0.8 - 0.6 * math.exp(-0.3 * l)
        mod3 = mod[l].reshape(8, 1, 6 * d)
        wl = w_in[l].astype(BF16)
        cut = lambda a, n: wl[:, a:a + n]
        wa = jnp.concatenate([cut(o_ga, 2 * GLA_RANK), jnp.zeros((d, LANES - 2 * GLA_RANK), BF16)], axis=1)
        a2p = jnp.zeros((LANES, 2 * GLA_HEADS * GLA_DK), F32)
        a2p = a2p.at[0:GLA_RANK, 0:256].set(gla_a2[l, 0]).at[GLA_RANK:2 * GLA_RANK, 256:512].set(gla_a2[l, 1])
        ab = gla_a_b[l].reshape(1, 2 * GLA_HEADS * GLA_DK)
        tile2 = lambda g: jnp.tile(g, LANES // HEAD_DIM).reshape(1, LANES)

        h = _hnorm_call(xall, norm1_g[l], mod3, t_lat)
        qt, kk, vt, nrm = _proj_da_call(h, cut(o_dq, 512), cut(o_dk, 512), cut(o_dv, 512), bd,
                                        tile2(da_qn_g[l]), tile2(da_kn_g[l]), cos_t, sin_t)
        bound = jnp.sqrt(jnp.max(nrm[:, 0, :]) * jnp.max(nrm[:, 1, :])) * 1.01 + 1e-3
        safe = jnp.where(bound <= DA_SAFE_LOG2, bound, -1.0).astype(F32).reshape(1)
        gq, gk, gv, sgg, la = _proj_gla_call(h, cut(o_gq, 256), cut(o_gk, 256), cut(o_gv, 512), cut(o_gg, 512),
                                             wa, a2p.astype(BF16), ab)
        nq, nk, nv = _proj_na_call(h, cut(o_nq, 512), cut(o_nk, 512), cut(o_nv, 512), bd,
                                   tile2(na_qn_g[l]), tile2(na_kn_g[l]))
        gates = _proj_gate_call(h, cut(o_gate, 3 * d))

        subg = da_subln_g[l].reshape(LANES, 1)
        y_da = _da_call(safe, qt, kk, vt, da_lambda[l], subg, lam_init,
                        q_off=0, nq=t_lat // tq_da, k_off=0, nk=tall // tk_da, tq=tq_da, tk=tk_da)
        o_f = _gla_call(gq, gk, gv, la, tri_f, t_lat=t_lat, reverse=False)
        o_b = _gla_call(gq, gk, gv, la, tri_b, t_lat=t_lat, reverse=True)
        bias = _na_bias_tiles(na_rpb[l], rows)
        y_na = _na_call(nq, nk, nv, bias, t_lat=t_lat)
        if need_ctx:
            y_da_c = _da_call(safe, qt, kk, vt, da_lambda[l], subg, lam_init,
                              q_off=t_lat // tc, nq=1, k_off=t_lat // tc, nk=1, tq=tc, tk=tc)
            y_na_c = _ctx_attn_call(nq, nk, nv, t_lat=t_lat)
        else:
            y_da_c = jnp.zeros((b, tc, BR_W), BF16)
            y_na_c = jnp.zeros((b, tc, BR_W), BF16)
        y_da = jnp.concatenate([y_da, y_da_c], axis=1)
        y_na = jnp.concatenate([y_na, y_na_c], axis=1)

        xall = _merge_call(xall, y_da, o_f, o_b, sgg, y_na, gates,
                           w_br_da[l].astype(BF16), w_br_gla[l].astype(BF16), w_br_na[l].astype(BF16),
                           w_out[l].astype(BF16), gla_gn_g[l].reshape(1, GLA_DV), mod3, t_lat)
        xall = _mlp_call(xall, norm2_g[l], w_ff1[l].astype(BF16), w_ff2[l].astype(BF16), mod3, t_lat)
    return xall[:, :t_lat]
```

```python
import functools
import math

import numpy as np
import jax
import jax.numpy as jnp
from jax import lax
from jax.experimental import pallas as pl
from jax.experimental.pallas import tpu as pltpu

F32 = jnp.float32
BF16 = jnp.bfloat16
F8 = jnp.float8_e4m3fn

D_MODEL = 1024
GRID_W = 64
HEAD_DIM = 64
EPS = 1e-6
ROPE_BASE = 10000.0
ROPE_PAIRS = HEAD_DIM // 4
DA_HEADS = 4
GLA_HEADS = 4
GLA_DK = 64
GLA_DV = 128
GLA_RANK = 16
GLA_TAU = 16.0
GLA_CHUNK = 64
NA_HEADS = 8
NA_KR = 8
NA_KC = 16
BR_W = 512
D_FF = 4 * D_MODEL

DA_V = 2 * HEAD_DIM
DA_VPAD = 16
DA_VROWS = DA_V + DA_VPAD
LOG2E = math.log2(math.e)
DA_SAFE_LOG2 = 45.0

LANES = 128
NA_QROWS = 4
NA_BAND = 16
NEG_BIG = -1e30
VMEM_LIMIT = 48 * 1024 * 1024


def _cparams(sem):
    return pltpu.CompilerParams(dimension_semantics=sem, vmem_limit_bytes=VMEM_LIMIT)


def _dot(a, b):
    return jnp.dot(a, b, preferred_element_type=F32)


def _dot_nt(a, b):
    return lax.dot_general(a, b, (((1,), (1,)), ((), ())), preferred_element_type=F32)


def _dot_tn(a, b):
    return lax.dot_general(a, b, (((0,), (0,)), ((), ())), preferred_element_type=F32)


def _split_dot(x, w_exact):
    hi = x.astype(BF16)
    lo = (x - hi.astype(F32)).astype(BF16)
    return _dot(hi, w_exact) + _dot(lo, w_exact)


def _split_dot_left(w_exact, x):
    hi = x.astype(BF16)
    lo = (x - hi.astype(F32)).astype(BF16)
    return _dot(w_exact, hi) + _dot(w_exact, lo)


def _sigmoid(x):
    return 1.0 / (1.0 + jnp.exp(-x))


def _row_tile(tall):
    for tm in (640, 256):
        if tall % tm == 0:
            return tm
    raise ValueError(f"unsupported token count {tall}")


def _mod_body(c_ref, w_ref, b_ref, o_ref):
    cv = c_ref[...]
    s = cv * _sigmoid(cv)
    o_ref[0] = _dot(s.astype(BF16), w_ref[0].astype(BF16)) + b_ref[0]


def _mod_call(cvec, w_mod, b_mod):
    depth, d, n = w_mod.shape
    tn = 1536
    return pl.pallas_call(
        _mod_body,
        grid=(depth, n // tn),
        in_specs=[pl.BlockSpec((8, d), lambda l, j: (0, 0)),
                  pl.BlockSpec((1, d, tn), lambda l, j: (l, 0, j)),
                  pl.BlockSpec((1, 1, tn), lambda l, j: (l, 0, j))],
        out_specs=pl.BlockSpec((1, 8, tn), lambda l, j: (l, 0, j)),
        out_shape=jax.ShapeDtypeStruct((depth, 8, n), F32),
        compiler_params=_cparams(("parallel", "parallel")),
        name="mod",
    )(cvec, w_mod, b_mod.reshape(depth, 1, n))


def _rope_body(freq_ref, cos_ref, sin_ref, *, t_lat, tm):
    i = pl.program_id(0)
    t = i * tm + lax.broadcasted_iota(jnp.int32, (tm, LANES), 0)
    lane = lax.broadcasted_iota(jnp.int32, (tm, LANES), 1)
    shift = int(math.log2(GRID_W))
    row = lax.shift_right_logical(t, shift).astype(F32)
    col = jnp.bitwise_and(t, GRID_W - 1).astype(F32)
    use_row = jnp.bitwise_and(lane, 2 * ROPE_PAIRS) == 0
    first_half = jnp.bitwise_and(lane, ROPE_PAIRS) == 0
    ang = jnp.where(use_row, row, col) * freq_ref[...]
    is_lat = t < t_lat
    cos_ref[...] = jnp.where(is_lat, jnp.cos(ang), 1.0)
    sn = jnp.sin(ang)
    sin_ref[...] = jnp.where(is_lat, jnp.where(first_half, -sn, sn), 0.0)


def _rope_tables(t_lat, tall):
    tm = 256
    freqs = ROPE_BASE ** (-jnp.arange(ROPE_PAIRS, dtype=F32) / ROPE_PAIRS)
    freq_lane = jnp.tile(freqs, LANES // ROPE_PAIRS).reshape(1, LANES)
    return pl.pallas_call(
        functools.partial(_rope_body, t_lat=t_lat, tm=tm),
        grid=(tall // tm,),
        in_specs=[pl.BlockSpec((1, LANES), lambda i: (0, 0))],
        out_specs=[pl.BlockSpec((tm, LANES), lambda i: (i, 0))] * 2,
        out_shape=[jax.ShapeDtypeStruct((tall, LANES), F32)] * 2,
        compiler_params=_cparams(("parallel",)),
        name="rope_tables",
    )(freq_lane)


def _is_ctx_rows(r, tm, t_lat):
    t = r * tm + lax.broadcasted_iota(jnp.int32, (tm, 1), 0)
    return t >= t_lat


def _norm_mod(x, g, sc, sh):
    ms = jnp.mean(x * x, axis=-1, keepdims=True)
    return (x * lax.rsqrt(ms + EPS) * g) * (1.0 + sc) + sh


def _segnorm64(z, bd, gain):
    ss = _split_dot(z * z, bd)
    return z * lax.rsqrt(ss * (1.0 / HEAD_DIM) + EPS) * gain


def _hnorm_body(x_ref, g_ref, mb_ref, mc_ref, h_ref, *, t_lat, tm, d):
    r = pl.program_id(1)
    ctx = _is_ctx_rows(r, tm, t_lat)
    sh = jnp.where(ctx, mc_ref[0, :, 0:d], mb_ref[0, :, 0:d])
    sc = jnp.where(ctx, mc_ref[0, :, d:2 * d], mb_ref[0, :, d:2 * d])
    h_ref[0] = _norm_mod(x_ref[0], g_ref[...], sc, sh).astype(BF16)


def _hnorm_call(xall, g, mod3, t_lat):
    b, tall, d = xall.shape
    tm = _row_tile(tall)
    nb = b
    return pl.pallas_call(
        functools.partial(_hnorm_body, t_lat=t_lat, tm=tm, d=d),
        grid=(b, tall // tm),
        in_specs=[pl.BlockSpec((1, tm, d), lambda bi, r: (bi, r, 0)),
                  pl.BlockSpec((1, d), lambda bi, r: (0, 0)),
                  pl.BlockSpec((1, 1, 6 * d), lambda bi, r: (bi, 0, 0)),
                  pl.BlockSpec((1, 1, 6 * d), lambda bi, r: (nb, 0, 0))],
        out_specs=pl.BlockSpec((1, tm, d), lambda bi, r: (bi, r, 0)),
        out_shape=jax.ShapeDtypeStruct((b, tall, d), BF16),
        compiler_params=_cparams(("parallel", "parallel")),
        name="hnorm",
    )(xall, g.reshape(1, d), mod3, mod3)


def _rope128(x, cs, sn, first_half):
    partner = jnp.where(first_half, pltpu.roll(x, LANES - ROPE_PAIRS, 1), pltpu.roll(x, ROPE_PAIRS, 1))
    return x * cs + partner * sn


def _proj_da_body(h_ref, wq_ref, wk_ref, wv_ref, bd_ref, gq_ref, gk_ref, cos_ref, sin_ref,
                  qt_ref, k_ref, vt_ref, nrm_ref):
    h = h_ref[0]
    tm = h.shape[0]
    bd = bd_ref[...]
    cs = cos_ref[...]
    sn = sin_ref[...]
    lane = lax.broadcasted_iota(jnp.int32, cs.shape, 1)
    first_half = jnp.bitwise_and(lane, ROPE_PAIRS) == 0
    zq = _dot(h, wq_ref[...])
    zk = _dot(h, wk_ref[...])
    zv = _dot(h, wv_ref[...])
    scale = math.sqrt(HEAD_DIM ** -0.5 * LOG2E)
    ones_row = (lax.broadcasted_iota(jnp.int32, (DA_VPAD, tm), 0) == 0).astype(BF16)
    low_lanes = lane < HEAD_DIM
    qn2 = jnp.zeros((1, LANES), F32)
    kn2 = jnp.zeros((1, LANES), F32)
    for hd in range(DA_HEADS):
        sl = slice(hd * LANES, (hd + 1) * LANES)
        q = _segnorm64(zq[:, sl], bd, gq_ref[...]) * scale
        q = _rope128(q, cs, sn, first_half)
        q_hi = q.astype(F8).astype(F32)
        qh_t = q_hi.T
        ql_t = (q - q_hi).T
        r0 = hd * 2 * LANES
        for a in range(2):
            qt_ref[0, r0 + a * LANES:r0 + a * LANES + HEAD_DIM, :] = qh_t[a * HEAD_DIM:(a + 1) * HEAD_DIM].astype(F8)
            qt_ref[0, r0 + a * LANES + HEAD_DIM:r0 + (a + 1) * LANES, :] = ql_t[a * HEAD_DIM:(a + 1) * HEAD_DIM].astype(F8)
        k = _segnorm64(zk[:, sl], bd, gk_ref[...]) * scale
        k = _rope128(k, cs, sn, first_half)
        k_hi = k.astype(F8).astype(F32)
        k_lo = k - k_hi
        k_ref[0, :, r0:r0 + LANES] = jnp.where(low_lanes, k_hi, pltpu.roll(k_lo, HEAD_DIM, 1)).astype(F8)
        k_ref[0, :, r0 + LANES:r0 + 2 * LANES] = jnp.where(low_lanes, pltpu.roll(k_hi, HEAD_DIM, 1), k_lo).astype(F8)
        qn2 = jnp.maximum(qn2, jnp.max(_split_dot(q * q, bd), axis=0, keepdims=True))
        kn2 = jnp.maximum(kn2, jnp.max(_split_dot(k * k, bd), axis=0, keepdims=True))
        v0 = hd * DA_VROWS
        vt_ref[0, v0:v0 + DA_V, :] = zv[:, sl].T.astype(BF16)
        vt_ref[0, v0 + DA_V:v0 + DA_VROWS, :] = ones_row
    nrm_ref[0, 0:1, :] = qn2
    nrm_ref[0, 1:2, :] = kn2


def _proj_da_call(h, wq, wk, wv, bd, gq, gk, cos_t, sin_t):
    b, tall, d = h.shape
    tm = _row_tile(tall)
    nt = tall // tm
    w = DA_HEADS * LANES
    const = lambda bi, r: (0, 0)
    return pl.pallas_call(
        _proj_da_body,
        grid=(b, tall // tm),
        in_specs=[pl.BlockSpec((1, tm, d), lambda bi, r: (bi, r, 0)),
                  pl.BlockSpec((d, w), const), pl.BlockSpec((d, w), const), pl.BlockSpec((d, w), const),
                  pl.BlockSpec((LANES, LANES), const),
                  pl.BlockSpec((1, LANES), const), pl.BlockSpec((1, LANES), const),
                  pl.BlockSpec((tm, LANES), lambda bi, r: (r, 0)),
                  pl.BlockSpec((tm, LANES), lambda bi, r: (r, 0))],
        out_specs=[pl.BlockSpec((1, 2 * w, tm), lambda bi, r: (bi, 0, r)),
                   pl.BlockSpec((1, tm, 2 * w), lambda bi, r: (bi, r, 0)),
                   pl.BlockSpec((1, DA_HEADS * DA_VROWS, tm), lambda bi, r: (bi, 0, r)),
                   pl.BlockSpec((1, 2, LANES), lambda bi, r: (bi * nt + r, 0, 0))],
        out_shape=[jax.ShapeDtypeStruct((b, 2 * w, tall), F8),
                   jax.ShapeDtypeStruct((b, tall, 2 * w), F8),
                   jax.ShapeDtypeStruct((b, DA_HEADS * DA_VROWS, tall), BF16),
                   jax.ShapeDtypeStruct((b * nt, 2, LANES), F32)],
        compiler_params=_cparams(("parallel", "parallel")),
        name="proj_da",
    )(h, wq, wk, wv, bd, gq, gk, cos_t, sin_t)


def _proj_gla_body(h_ref, wq_ref, wk_ref, wv_ref, wg_ref, wa_ref, a2_ref, ab_ref,
                   q_ref, k_ref, v_ref, sg_ref, la_ref):
    h = h_ref[0]
    q_ref[0] = _dot(h, wq_ref[...]) * (GLA_DK ** -0.5)
    k_ref[0] = _dot(h, wk_ref[...])
    v_ref[0] = _dot(h, wv_ref[...]).astype(BF16)
    g = _dot(h, wg_ref[...])
    sg_ref[0] = (g * _sigmoid(g)).astype(BF16)
    ga = _dot(h, wa_ref[...])
    z = _dot(ga.astype(BF16), a2_ref[...]) + ab_ref[...]
    la_ref[0] = (jnp.minimum(z, 0.0) - jnp.log1p(jnp.exp(-jnp.abs(z)))) * (1.0 / GLA_TAU)


def _proj_gla_call(h, wq, wk, wv, wg, wa, a2p, ab):
    b, tall, d = h.shape
    tm = _row_tile(tall)
    wqk = GLA_HEADS * GLA_DK
    wv_ = GLA_HEADS * GLA_DV
    const = lambda bi, r: (0, 0)
    row = lambda n: pl.BlockSpec((1, tm, n), lambda bi, r: (bi, r, 0))
    return pl.pallas_call(
        _proj_gla_body,
        grid=(b, tall // tm),
        in_specs=[row(d),
                  pl.BlockSpec((d, wqk), const), pl.BlockSpec((d, wqk), const),
                  pl.BlockSpec((d, wv_), const), pl.BlockSpec((d, wv_), const),
                  pl.BlockSpec((d, LANES), const),
                  pl.BlockSpec((LANES, 2 * wqk), const), pl.BlockSpec((1, 2 * wqk), const)],
        out_specs=[row(wqk), row(wqk), row(wv_), row(wv_), row(2 * wqk)],
        out_shape=[jax.ShapeDtypeStruct((b, tall, wqk), F32),
                   jax.ShapeDtypeStruct((b, tall, wqk), F32),
                   jax.ShapeDtypeStruct((b, tall, wv_), BF16),
                   jax.ShapeDtypeStruct((b, tall, wv_), BF16),
                   jax.ShapeDtypeStruct((b, tall, 2 * wqk), F32)],
        compiler_params=_cparams(("parallel", "parallel")),
        name="proj_gla",
    )(h, wq, wk, wv, wg, wa, a2p, ab)


def _proj_na_body(h_ref, wq_ref, wk_ref, wv_ref, bd_ref, gq_ref, gk_ref, q_ref, k_ref, v_ref, nrm_ref):
    h = h_ref[0]
    bd = bd_ref[...]
    zq = _dot(h, wq_ref[...])
    zk = _dot(h, wk_ref[...])
    scale = HEAD_DIM ** -0.5 * LOG2E
    qn2 = jnp.zeros((1, LANES), F32)
    kn2 = jnp.zeros((1, LANES), F32)
    for j in range(NA_HEADS * HEAD_DIM // LANES):
        sl = slice(j * LANES, (j + 1) * LANES)
        q = _segnorm64(zq[:, sl], bd, gq_ref[...]) * scale
        k = _segnorm64(zk[:, sl], bd, gk_ref[...])
        q_ref[0, :, sl] = q.astype(BF16)
        k_ref[0, :, sl] = k.astype(BF16)
        qn2 = jnp.maximum(qn2, jnp.max(_split_dot(q * q, bd), axis=0, keepdims=True))
        kn2 = jnp.maximum(kn2, jnp.max(_split_dot(k * k, bd), axis=0, keepdims=True))
    v_ref[0] = _dot(h, wv_ref[...]).astype(BF16)
    nrm_ref[0, 0:1, :] = qn2
    nrm_ref[0, 1:2, :] = kn2


def _proj_na_call(h, wq, wk, wv, bd, gq, gk):
    b, tall, d = h.shape
    tm = _row_tile(tall)
    nt = tall // tm
    w = NA_HEADS * HEAD_DIM
    const = lambda bi, r: (0, 0)
    row = lambda n: pl.BlockSpec((1, tm, n), lambda bi, r: (bi, r, 0))
    return pl.pallas_call(
        _proj_na_body,
        grid=(b, tall // tm),
        in_specs=[row(d), pl.BlockSpec((d, w), const), pl.BlockSpec((d, w), const), pl.BlockSpec((d, w), const),
                  pl.BlockSpec((LANES, LANES), const),
                  pl.BlockSpec((1, LANES), const), pl.BlockSpec((1, LANES), const)],
        out_specs=[row(w), row(w), row(w), pl.BlockSpec((1, 2, LANES), lambda bi, r: (bi * nt + r, 0, 0))],
        out_shape=[jax.ShapeDtypeStruct((b, tall, w), BF16)] * 3 + [jax.ShapeDtypeStruct((b * nt, 2, LANES), F32)],
        compiler_params=_cparams(("parallel", "parallel")),
        name="proj_na",
    )(h, wq, wk, wv, bd, gq, gk)


def _proj_gate_body(h_ref, w_ref, o_ref):
    h = h_ref[0]
    n = w_ref.shape[1]
    step = 512
    for j in range(n // step):
        sl = slice(j * step, (j + 1) * step)
        o_ref[0, :, sl] = _sigmoid(_dot(h, w_ref[:, sl])).astype(BF16)


def _proj_gate_call(h, w):
    b, tall, d = h.shape
    tm = _row_tile(tall)
    n = w.shape[1]
    return pl.pallas_call(
        _proj_gate_body,
        grid=(b, tall // tm),
        in_specs=[pl.BlockSpec((1, tm, d), lambda bi, r: (bi, r, 0)),
                  pl.BlockSpec((d, n), lambda bi, r: (0, 0))],
        out_specs=pl.BlockSpec((1, tm, n), lambda bi, r: (bi, r, 0)),
        out_shape=jax.ShapeDtypeStruct((b, tall, n), BF16),
        compiler_params=_cparams(("parallel", "parallel")),
        name="proj_gate",
    )(h, w)


def _da_body(safe_ref, qt_ref, k_ref, vt_ref, lp_ref, sg_ref, o_ref, m_ref, a_ref, acc_ref, *, lam_init, nk):
    j = pl.program_id(3)
    use_lag = safe_ref[0] > 0.0

    @pl.when(j == 0)
    def _():
        m_ref[...] = jnp.full(m_ref.shape, jnp.where(use_lag, -safe_ref[0], -jnp.inf), F32)
        a_ref[...] = jnp.ones(a_ref.shape, F32)
        acc_ref[...] = jnp.zeros(acc_ref.shape, F32)

    def scores(a, rows):
        kb = k_ref[0, rows, a * LANES:(a + 1) * LANES]
        q_hi = qt_ref[0, a * LANES:a * LANES + HEAD_DIM, :]
        q_lo = qt_ref[0, a * LANES + HEAD_DIM:(a + 1) * LANES, :]
        w = jnp.concatenate([q_hi, q_hi, q_lo, q_lo], axis=0)
        return _dot(jnp.concatenate([kb, kb], axis=1), w)

    def lagged_step():
        vt = vt_ref[0]
        for a in range(2):
            s = scores(a, slice(None))
            m_old = m_ref[a:a + 1, :]
            m_new = jnp.maximum(m_old, jnp.max(s, axis=0, keepdims=True))
            p = jnp.exp2(s - m_old)
            acc_ref[a] = a_ref[a:a + 1, :] * acc_ref[a] + _dot(vt, p.astype(BF16))
            a_ref[a:a + 1, :] = jnp.exp2(m_old - m_new)
            m_ref[a:a + 1, :] = m_new

    def plain_step():
        vt = vt_ref[0]
        for a in range(2):
            s = scores(a, slice(None))
            m_old = m_ref[a:a + 1, :]
            m_new = jnp.maximum(m_old, jnp.max(s, axis=0, keepdims=True))
            p = jnp.exp2(s - m_new)
            acc_ref[a] = (a_ref[a:a + 1, :] * jnp.exp2(m_old - m_new)) * acc_ref[a] + _dot(vt, p.astype(BF16))
            a_ref[a:a + 1, :] = jnp.ones_like(m_old)
            m_ref[a:a + 1, :] = m_new

    pl.when(use_lag)(lagged_step)
    pl.when(jnp.logical_not(use_lag))(plain_step)

    @pl.when(j == nk - 1)
    def _():
        lp = lp_ref[...]
        e1 = jnp.exp(jnp.sum(lp[0:1] * lp[1:2], axis=-1, keepdims=True))
        e2 = jnp.exp(jnp.sum(lp[2:3] * lp[3:4], axis=-1, keepdims=True))
        lam = e1 - e2 + lam_init
        o1 = acc_ref[0, 0:DA_V, :] / acc_ref[0, DA_V:DA_V + 1, :]
        o2 = acc_ref[1, 0:DA_V, :] / acc_ref[1, DA_V:DA_V + 1, :]
        o = o1 - lam * o2
        ms = jnp.mean(o * o, axis=0, keepdims=True)
        y = (o * lax.rsqrt(ms + EPS) * sg_ref[...]) * (1.0 - lam_init)
        o_ref[0] = y.T.astype(BF16)


def _da_call(safe, qt, kk, vt, lp, subg, lam_init, *, q_off, nq, k_off, nk, tq, tk):
    b = qt.shape[0]
    return pl.pallas_call(
        functools.partial(_da_body, lam_init=lam_init, nk=nk),
        grid=(b, DA_HEADS, nq, nk),
        in_specs=[pl.BlockSpec(memory_space=pltpu.SMEM),
                  pl.BlockSpec((1, 2 * LANES, tq), lambda bi, h, i, j: (bi, h, i + q_off)),
                  pl.BlockSpec((1, tk, 2 * LANES), lambda bi, h, i, j: (bi, j + k_off, h)),
                  pl.BlockSpec((1, DA_VROWS, tk), lambda bi, h, i, j: (bi, h, j + k_off)),
                  pl.BlockSpec((4, HEAD_DIM), lambda bi, h, i, j: (0, 0)),
                  pl.BlockSpec((LANES, 1), lambda bi, h, i, j: (0, 0))],
        out_specs=pl.BlockSpec((1, tq, LANES), lambda bi, h, i, j: (bi, i, h)),
        out_shape=jax.ShapeDtypeStruct((b, nq * tq, DA_HEADS * LANES), BF16),
        scratch_shapes=[pltpu.VMEM((8, tq), F32), pltpu.VMEM((8, tq), F32),
                        pltpu.VMEM((2, DA_VROWS, tq), F32)],
        compiler_params=_cparams(("parallel", "parallel", "parallel", "arbitrary")),
        name="diff_attn",
    )(safe, qt, kk, vt, lp, subg)


def _gla_body(q_ref, k_ref, v_ref, la_ref, tri_ref, o_ref, s_ref, *, reverse, tb):
    i = pl.program_id(1)

    @pl.when(i == 0)
    def _():
        s_ref[...] = jnp.zeros(s_ref.shape, F32)

    c = GLA_CHUNK
    w = GLA_HEADS * GLA_DK
    tri = tri_ref[...]
    ri = lax.broadcasted_iota(jnp.int32, (c, c), 0)
    ci = lax.broadcasted_iota(jnp.int32, (c, c), 1)
    keep = (ci >= ri) if reverse else (ci <= ri)
    lane = lax.broadcasted_iota(jnp.int32, (1, w), 1)
    chunks = range(tb // c)
    for ch in (reversed(chunks) if reverse else chunks):
        rows = slice(ch * c, (ch + 1) * c)
        la = la_ref[0, rows, :]
        cum = _split_dot_left(tri, la)
        cl = cum[0:1, :] if reverse else cum[c - 1:c, :]
        q = q_ref[0, rows, :]
        k = k_ref[0, rows, :]
        qe = q * jnp.exp(cum)
        ke = (k * jnp.exp(-cum)).astype(BF16)
        kd = (k * jnp.exp(cl - cum)).astype(BF16)
        dec = jnp.exp(cl)
        for hd in range(GLA_HEADS):
            hm = (lane >= hd * GLA_DK) & (lane < (hd + 1) * GLA_DK)
            qh = jnp.where(hm, qe, 0.0).astype(BF16)
            a = jnp.where(keep, _dot_nt(qh, ke), 0.0)
            vh = v_ref[0, rows, hd * GLA_DV:(hd + 1) * GLA_DV]
            st = s_ref[hd]
            o = _dot(a.astype(BF16), vh) + _dot_nt(qh, st.astype(BF16))
            o_ref[0, rows, hd * GLA_DV:(hd + 1) * GLA_DV] = o
            s_ref[hd] = st * dec + _dot_tn(vh, kd)


def _gla_call(gq, gk, gv, la, tri, *, t_lat, reverse):
    b, tall, w = gq.shape
    tb = 256
    n_lat = t_lat // tb
    nblk = tall // tb
    if reverse:
        blk = lambda i: jnp.where(i == 0, n_lat, n_lat - i)
    else:
        blk = lambda i: jnp.where(i == 0, n_lat, i - 1)
    wv_ = GLA_HEADS * GLA_DV
    return pl.pallas_call(
        functools.partial(_gla_body, reverse=reverse, tb=tb),
        grid=(b, nblk),
        in_specs=[pl.BlockSpec((1, tb, w), lambda bi, i: (bi, blk(i), 0)),
                  pl.BlockSpec((1, tb, w), lambda bi, i: (bi, blk(i), 0)),
                  pl.BlockSpec((1, tb, wv_), lambda bi, i: (bi, blk(i), 0)),
                  pl.BlockSpec((1, tb, w), lambda bi, i: (bi, blk(i), 1 if reverse else 0)),
                  pl.BlockSpec((GLA_CHUNK, GLA_CHUNK), lambda bi, i: (0, 0))],
        out_specs=pl.BlockSpec((1, tb, wv_), lambda bi, i: (bi, blk(i), 0)),
        out_shape=jax.ShapeDtypeStruct((b, tall, wv_), F32),
        scratch_shapes=[pltpu.VMEM((GLA_HEADS, GLA_DV, w), F32)],
        compiler_params=_cparams(("parallel", "arbitrary")),
        name="gla_bwd" if reverse else "gla_fwd",
    )(gq, gk, gv, la, tri)


def _na_body(safe_ref, q_ref, k_ref, v_ref, kc_ref, vc_ref, bias_ref, o_ref, *, rows):
    i = pl.program_id(2)
    kb0 = jnp.clip(i * NA_QROWS - NA_KR // 2, 0, rows - NA_BAND)
    start = pl.multiple_of(kb0 * GRID_W, GRID_W)
    nband = NA_BAND * GRID_W

    def attend(fixed_ref):
        q = q_ref[0]
        kb = k_ref[0, pl.ds(start, nband), :]
        kc = kc_ref[0]
        vall = jnp.concatenate([v_ref[0, pl.ds(start, nband), :], vc_ref[0]], axis=0)
        klane = lax.broadcasted_iota(jnp.int32, vall.shape, 1)
        vaug = jnp.concatenate([vall, (klane == 0).astype(BF16)], axis=1)
        lane = lax.broadcasted_iota(jnp.int32, q.shape, 1)
        outs = []
        for hh in range(2):
            qh = jnp.where((lane < HEAD_DIM) if hh == 0 else (lane >= HEAD_DIM), q, jnp.zeros_like(q))
            s_loc = _dot_nt(qh, kb) + bias_ref[0, hh]
            s_ctx = _dot_nt(qh, kc)
            if fixed_ref:
                m = safe_ref[0]
            else:
                m = jnp.maximum(jnp.max(s_loc, axis=-1, keepdims=True), jnp.max(s_ctx, axis=-1, keepdims=True))
            p = jnp.concatenate([jnp.exp2(s_loc - m), jnp.exp2(s_ctx - m)], axis=1).astype(BF16)
            acc = _dot(p, vaug)
            outs.append(acc[:, 0:LANES] / acc[:, LANES:LANES + 1])
        o_ref[0] = jnp.where(lane < HEAD_DIM, outs[0], outs[1]).astype(BF16)

    fixed = safe_ref[0] > 0.0
    pl.when(fixed)(functools.partial(attend, True))
    pl.when(jnp.logical_not(fixed))(functools.partial(attend, False))


def _na_call(safe, nq, nk, nv, bias, *, t_lat):
    b, tall, w = nq.shape
    tc = tall - t_lat
    rows = t_lat // GRID_W
    tq = NA_QROWS * GRID_W
    nsteps = rows // NA_QROWS
    npair = w // LANES
    ctx_blk = t_lat // tc

    def btype(i):
        return jnp.where(i == 0, 0, jnp.where(i == nsteps - 2, 2, jnp.where(i == nsteps - 1, 3, 1)))

    return pl.pallas_call(
        functools.partial(_na_body, rows=rows),
        grid=(b, npair, nsteps),
        in_specs=[pl.BlockSpec(memory_space=pltpu.SMEM),
                  pl.BlockSpec((1, tq, LANES), lambda bi, hp, i: (bi, i, hp)),
                  pl.BlockSpec((1, t_lat, LANES), lambda bi, hp, i: (bi, 0, hp)),
                  pl.BlockSpec((1, t_lat, LANES), lambda bi, hp, i: (bi, 0, hp)),
                  pl.BlockSpec((1, tc, LANES), lambda bi, hp, i: (bi, ctx_blk, hp)),
                  pl.BlockSpec((1, tc, LANES), lambda bi, hp, i: (bi, ctx_blk, hp)),
                  pl.BlockSpec((1, 2, tq, NA_BAND * GRID_W), lambda bi, hp, i: (btype(i), hp, 0, 0))],
        out_specs=pl.BlockSpec((1, tq, LANES), lambda bi, hp, i: (bi, i, hp)),
        out_shape=jax.ShapeDtypeStruct((b, t_lat, w), BF16),
        compiler_params=_cparams(("parallel", "parallel", "arbitrary")),
        name="nbr_attn",
    )(safe, nq, nk, nv, nk, nv, bias)


def _na_bias_tiles(rpb, rows):
    assert rows >= NA_BAND and rows % NA_QROWS == 0
    nl, nh, na, nb = rpb.shape
    cidx = np.arange(GRID_W)
    rel_c = cidx[None, :] - cidx[:, None] + NA_KC - 1
    sel = jnp.asarray(rel_c[None] == np.arange(nb)[:, None, None], F32)
    toep = jnp.einsum('lhab,bqk->lhqak', rpb.astype(F32) * LOG2E, sel, precision=lax.Precision.HIGHEST)
    toep = jnp.pad(toep, ((0, 0), (0, 0), (0, 0), (NA_BAND, NA_BAND), (0, 0)))
    tiles = []
    for r0 in (0, NA_QROWS, rows - 2 * NA_QROWS, rows - NA_QROWS):
        kb0 = min(max(r0 - NA_KR // 2, 0), rows - NA_BAND)
        qi = np.arange(NA_QROWS * GRID_W)
        qr, qc = r0 + qi // GRID_W, qi % GRID_W
        kj = np.arange(NA_BAND * GRID_W)
        kr, kc = kb0 + kj // GRID_W, kj % GRID_W
        rs = np.clip(qr - NA_KR // 2, 0, rows - NA_KR)
        cs = np.clip(qc - NA_KC // 2, 0, GRID_W - NA_KC)
        valid = ((kr[None, :] >= rs[:, None]) & (kr[None, :] < rs[:, None] + NA_KR)
                 & (kc[None, :] >= cs[:, None]) & (kc[None, :] < cs[:, None] + NA_KC))
        parts = []
        for q_row in range(NA_QROWS):
            a0 = kb0 - (r0 + q_row) + NA_KR - 1 + NA_BAND
            blk = toep[:, :, :, a0:a0 + NA_BAND, :]
            parts.append(blk.reshape(nl, nh, GRID_W, NA_BAND * GRID_W))
        tile = jnp.concatenate(parts, axis=2)
        tiles.append(jnp.where(jnp.asarray(valid)[None, None], tile, NEG_BIG))
    return jnp.stack(tiles, axis=1)


def _ctx_attn_body(q_ref, k_ref, v_ref, o_ref):
    q = q_ref[0]
    k = k_ref[0]
    v = v_ref[0]
    lane = lax.broadcasted_iota(jnp.int32, q.shape, 1)
    outs = []
    for hh in range(2):
        qh = jnp.where((lane < HEAD_DIM) if hh == 0 else (lane >= HEAD_DIM), q, jnp.zeros_like(q))
        s = _dot_nt(qh, k)
        m = jnp.max(s, axis=-1, keepdims=True)
        p = jnp.exp2(s - m)
        outs.append(_dot(p.astype(BF16), v) / jnp.sum(p, axis=-1, keepdims=True))
    o_ref[0] = jnp.where(lane < HEAD_DIM, outs[0], outs[1]).astype(BF16)


def _ctx_attn_call(nq, nk, nv, *, t_lat):
    b, tall, w = nq.shape
    tc = tall - t_lat
    ctx_blk = t_lat // tc
    spec = pl.BlockSpec((1, tc, LANES), lambda bi, hp: (bi, ctx_blk, hp))
    return pl.pallas_call(
        _ctx_attn_body,
        grid=(b, w // LANES),
        in_specs=[spec, spec, spec],
        out_specs=pl.BlockSpec((1, tc, LANES), lambda bi, hp: (bi, 0, hp)),
        out_shape=jax.ShapeDtypeStruct((b, tc, w), BF16),
        compiler_params=_cparams(("parallel", "parallel")),
        name="ctx_attn",
    )(nq, nk, nv)


def _merge_body(x_ref, yd_ref, of_ref, ob_ref, sg_ref, yn_ref, gt_ref, wd_ref, wg_ref, wn_ref, wo_ref,
                gn_ref, mb_ref, mc_ref, o_ref, *, t_lat, tm, d):
    r = pl.program_id(1)
    ctx = _is_ctx_rows(r, tm, t_lat)
    g1 = jnp.where(ctx, mc_ref[0, :, 2 * d:3 * d], mb_ref[0, :, 2 * d:3 * d])
    og = of_ref[0] + ob_ref[0]
    parts = []
    for hd in range(GLA_HEADS):
        oh = og[:, hd * GLA_DV:(hd + 1) * GLA_DV]
        ms = jnp.mean(oh * oh, axis=-1, keepdims=True)
        parts.append(oh * lax.rsqrt(ms + EPS) * gn_ref[...])
    yg = (jnp.concatenate(parts, axis=-1) * sg_ref[0].astype(F32)).astype(BF16)
    m = (gt_ref[0, :, 0:d].astype(F32) * _dot(yd_ref[0], wd_ref[...])
         + gt_ref[0, :, d:2 * d].astype(F32) * _dot(yg, wg_ref[...])
         + gt_ref[0, :, 2 * d:3 * d].astype(F32) * _dot(yn_ref[0], wn_ref[...]))
    o_ref[0] = x_ref[0] + g1 * _dot(m.astype(BF16), wo_ref[...])


def _merge_call(xall, yd, of, ob, sg, yn, gates, wd, wg, wn, wo, gn, mod3, t_lat):
    b, tall, d = xall.shape
    tm = _row_tile(tall)
    nb = b
    const = lambda bi, r: (0, 0)
    row = lambda n: pl.BlockSpec((1, tm, n), lambda bi, r: (bi, r, 0))
    return pl.pallas_call(
        functools.partial(_merge_body, t_lat=t_lat, tm=tm, d=d),
        grid=(b, tall // tm),
        in_specs=[row(d), row(BR_W), row(BR_W), row(BR_W), row(BR_W), row(BR_W), row(3 * d),
                  pl.BlockSpec((BR_W, d), const), pl.BlockSpec((BR_W, d), const), pl.BlockSpec((BR_W, d), const),
                  pl.BlockSpec((d, d), const), pl.BlockSpec((1, GLA_DV), const),
                  pl.BlockSpec((1, 1, 6 * d), lambda bi, r: (bi, 0, 0)),
                  pl.BlockSpec((1, 1, 6 * d), lambda bi, r: (nb, 0, 0))],
        out_specs=row(d),
        out_shape=jax.ShapeDtypeStruct((b, tall, d), F32),
        compiler_params=_cparams(("parallel", "parallel")),
        name="merge",
    )(xall, yd, of, ob, sg, yn, gates, wd, wg, wn, wo, gn, mod3, mod3)


def _mlp_body(x_ref, g_ref, w1_ref, w2_ref, mb_ref, mc_ref, o_ref, *, t_lat, tm, d):
    r = pl.program_id(1)
    ctx = _is_ctx_rows(r, tm, t_lat)
    sh = jnp.where(ctx, mc_ref[0, :, 3 * d:4 * d], mb_ref[0, :, 3 * d:4 * d])
    sc = jnp.where(ctx, mc_ref[0, :, 4 * d:5 * d], mb_ref[0, :, 4 * d:5 * d])
    g2 = jnp.where(ctx, mc_ref[0, :, 5 * d:6 * d], mb_ref[0, :, 5 * d:6 * d])
    x = x_ref[0]
    h = _norm_mod(x, g_ref[...], sc, sh).astype(BF16)
    acc = jnp.zeros((tm, d), F32)
    step = 1024
    for j in range(w1_ref.shape[1] // step):
        a = jnp.maximum(_dot(h, w1_ref[:, j * step:(j + 1) * step]), 0.0)
        acc = acc + _dot((a * a).astype(BF16), w2_ref[j * step:(j + 1) * step, :])
    o_ref[0] = x + g2 * acc


def _mlp_call(xall, g, w1, w2, mod3, t_lat, *, lat_only):
    b, tall, d = xall.shape
    tm = next(t for t in (512, 256) if t_lat % t == 0) if lat_only else _row_tile(tall)
    n_rows = t_lat if lat_only else tall
    nb = b
    dff = w1.shape[1]
    const = lambda bi, r: (0, 0)
    return pl.pallas_call(
        functools.partial(_mlp_body, t_lat=t_lat, tm=tm, d=d),
        grid=(b, n_rows // tm),
        in_specs=[pl.BlockSpec((1, tm, d), lambda bi, r: (bi, r, 0)),
                  pl.BlockSpec((1, d), const),
                  pl.BlockSpec((d, dff), const, pipeline_mode=pl.Buffered(1)),
                  pl.BlockSpec((dff, d), const, pipeline_mode=pl.Buffered(1)),
                  pl.BlockSpec((1, 1, 6 * d), lambda bi, r: (bi, 0, 0)),
                  pl.BlockSpec((1, 1, 6 * d), lambda bi, r: (nb, 0, 0))],
        out_specs=pl.BlockSpec((1, tm, d), lambda bi, r: (bi, r, 0)),
        out_shape=jax.ShapeDtypeStruct((b, n_rows, d), F32),
        compiler_params=_cparams(("parallel", "parallel")),
        name="mlp",
    )(xall, g.reshape(1, d), w1, w2, mod3, mod3)


def kernel(x, c, ctx, c_ctx, w_mod, b_mod, norm1_g, norm2_g, w_in, da_qn_g, da_kn_g, da_lambda, da_subln_g,
           gla_a2, gla_a_b, gla_gn_g, na_qn_g, na_kn_g, na_rpb, w_br_da, w_br_gla, w_br_na, w_out, w_ff1, w_ff2):
    b, t_lat, d = x.shape
    tc = ctx.shape[1]
    tall = t_lat + tc
    depth = w_mod.shape[0]
    rows = t_lat // GRID_W
    assert d == D_MODEL and t_lat % (2 * tc) == 0 and tc == 256 and b < 8

    xall = jnp.concatenate([x, ctx], axis=1)
    cvec = jnp.zeros((8, d), F32).at[0:b].set(c).at[b].set(c_ctx)
    mod = _mod_call(cvec, w_mod, b_mod)
    cos_t, sin_t = _rope_tables(t_lat, tall)

    seg = np.arange(LANES) // HEAD_DIM
    bd = jnp.asarray(seg[:, None] == seg[None, :], BF16)
    ci = np.arange(GLA_CHUNK)
    tri_f = jnp.asarray(ci[None, :] <= ci[:, None], BF16)
    tri_b = jnp.asarray(ci[None, :] >= ci[:, None], BF16)

    o_dq, o_dk, o_dv = 0, 512, 1024
    o_gq, o_gk, o_gv, o_gg, o_ga = 1536, 1792, 2048, 2560, 3072
    o_nq, o_nk, o_nv = 3104, 3616, 4128
    o_gate = 4640
    tq_da = 512 if t_lat % 512 == 0 else 256
    tk_da = next(t for t in (3328, 1280, 256) if tall % t == 0)
    na_bias = _na_bias_tiles(na_rpb, rows)

    for l in range(depth):
        need_ctx = l < depth - 1
        lam_init = 0.8 - 0.6 * math.exp(-0.3 * l)
        mod3 = mod[l].reshape(8, 1, 6 * d)
        wl = w_in[l].astype(BF16)
        cut = lambda a, n: wl[:, a:a + n]
        wa = jnp.concatenate([cut(o_ga, 2 * GLA_RANK), jnp.zeros((d, LANES - 2 * GLA_RANK), BF16)], axis=1)
        a2p = jnp.zeros((LANES, 2 * GLA_HEADS * GLA_DK), F32)
        a2p = a2p.at[0:GLA_RANK, 0:256].set(gla_a2[l, 0]).at[GLA_RANK:2 * GLA_RANK, 256:512].set(gla_a2[l, 1])
        ab = gla_a_b[l].reshape(1, 2 * GLA_HEADS * GLA_DK)
        tile2 = lambda g: jnp.tile(g, LANES // HEAD_DIM).reshape(1, LANES)

        h = _hnorm_call(xall, norm1_g[l], mod3, t_lat)
        qt, kk, vt, nrm = _proj_da_call(h, cut(o_dq, 512), cut(o_dk, 512), cut(o_dv, 512), bd,
                                        tile2(da_qn_g[l]), tile2(da_kn_g[l]), cos_t, sin_t)
        bound = jnp.sqrt(jnp.max(nrm[:, 0, :]) * jnp.max(nrm[:, 1, :])) * 1.01 + 1e-3
        safe = jnp.where(bound <= DA_SAFE_LOG2, bound, -1.0).astype(F32).reshape(1)
        gq, gk, gv, sgg, la = _proj_gla_call(h, cut(o_gq, 256), cut(o_gk, 256), cut(o_gv, 512), cut(o_gg, 512),
                                             wa, a2p.astype(BF16), ab)
        nq, nk, nv, nrm_na = _proj_na_call(h, cut(o_nq, 512), cut(o_nk, 512), cut(o_nv, 512), bd,
                                           tile2(na_qn_g[l]), tile2(na_kn_g[l]))
        bound_na = (jnp.sqrt(jnp.max(nrm_na[:, 0, :]) * jnp.max(nrm_na[:, 1, :])) * 1.01 + 1e-3
                    + jnp.max(jnp.abs(na_rpb[l])) * LOG2E)
        safe_na = jnp.where(bound_na <= DA_SAFE_LOG2, bound_na, -1.0).astype(F32).reshape(1)
        gates = _proj_gate_call(h, cut(o_gate, 3 * d))

        subg = da_subln_g[l].reshape(LANES, 1)
        y_da = _da_call(safe, qt, kk, vt, da_lambda[l], subg, lam_init,
                        q_off=0, nq=t_lat // tq_da, k_off=0, nk=tall // tk_da, tq=tq_da, tk=tk_da)
        o_f = _gla_call(gq, gk, gv, la, tri_f, t_lat=t_lat, reverse=False)
        o_b = _gla_call(gq, gk, gv, la, tri_b, t_lat=t_lat, reverse=True)
        y_na = _na_call(safe_na, nq, nk, nv, na_bias[l], t_lat=t_lat)
        if need_ctx:
            y_da_c = _da_call(safe, qt, kk, vt, da_lambda[l], subg, lam_init,
                              q_off=t_lat // tc, nq=1, k_off=t_lat // tc, nk=1, tq=tc, tk=tc)
            y_na_c = _ctx_attn_call(nq, nk, nv, t_lat=t_lat)
        else:
            y_da_c = jnp.zeros((b, tc, BR_W), BF16)
            y_na_c = jnp.zeros((b, tc, BR_W), BF16)
        y_da = jnp.concatenate([y_da, y_da_c], axis=1)
        y_na = jnp.concatenate([y_na, y_na_c], axis=1)

        xall = _merge_call(xall, y_da, o_f, o_b, sgg, y_na, gates,
                           w_br_da[l].astype(BF16), w_br_gla[l].astype(BF16), w_br_na[l].astype(BF16),
                           w_out[l].astype(BF16), gla_gn_g[l].reshape(1, GLA_DV), mod3, t_lat)
        xall = _mlp_call(xall, norm2_g[l], w_ff1[l].astype(BF16), w_ff2[l].astype(BF16), mod3, t_lat,
                         lat_only=not need_ctx)
    return xall
```

```python
import functools
import math

import numpy as np
import jax
import jax.numpy as jnp
from jax import lax
from jax.experimental import pallas as pl
from jax.experimental.pallas import tpu as pltpu

F32 = jnp.float32
BF16 = jnp.bfloat16
F8 = jnp.float8_e4m3fn

D_MODEL = 1024
GRID_W = 64
HEAD_DIM = 64
EPS = 1e-6
ROPE_BASE = 10000.0
ROPE_PAIRS = HEAD_DIM // 4
DA_HEADS = 4
GLA_HEADS = 4
GLA_DK = 64
GLA_DV = 128
GLA_RANK = 16
GLA_TAU = 16.0
GLA_CHUNK = 64
NA_HEADS = 8
NA_KR = 8
NA_KC = 16
BR_W = 512
D_FF = 4 * D_MODEL

DA_V = 2 * HEAD_DIM
DA_VPAD = 16
DA_VROWS = DA_V + DA_VPAD
LOG2E = math.log2(math.e)
DA_SAFE_LOG2 = 45.0

GLA_BLOCK = 256
NORM_LANES = 256
LANES = 128
NA_QROWS = 4
NA_BAND = 16
NEG_BIG = -1e30
VMEM_LIMIT = 48 * 1024 * 1024


def _cparams(sem):
    return pltpu.CompilerParams(dimension_semantics=sem, vmem_limit_bytes=VMEM_LIMIT)


def _dot(a, b):
    return jnp.dot(a, b, preferred_element_type=F32)


def _dot_nt(a, b):
    return lax.dot_general(a, b, (((1,), (1,)), ((), ())), preferred_element_type=F32)


def _dot_tn(a, b):
    return lax.dot_general(a, b, (((0,), (0,)), ((), ())), preferred_element_type=F32)


def _split_dot(x, w_exact):
    hi = x.astype(BF16)
    lo = (x - hi.astype(F32)).astype(BF16)
    return _dot(hi, w_exact) + _dot(lo, w_exact)


def _split_dot_left(w_exact, x):
    hi = x.astype(BF16)
    lo = (x - hi.astype(F32)).astype(BF16)
    return _dot(w_exact, hi) + _dot(w_exact, lo)


def _sigmoid(x):
    return 1.0 / (1.0 + jnp.exp(-x))


def _row_tile(tall):
    for tm in (640, 256):
        if tall % tm == 0:
            return tm
    raise ValueError(f"unsupported token count {tall}")


def _mod_body(c_ref, w_ref, b_ref, o_ref):
    cv = c_ref[...]
    s = cv * _sigmoid(cv)
    o_ref[0] = _dot(s.astype(BF16), w_ref[0].astype(BF16)) + b_ref[0]


def _mod_call(cvec, w_mod, b_mod):
    depth, d, n = w_mod.shape
    tn = 1536
    return pl.pallas_call(
        _mod_body,
        grid=(depth, n // tn),
        in_specs=[pl.BlockSpec((8, d), lambda l, j: (0, 0)),
                  pl.BlockSpec((1, d, tn), lambda l, j: (l, 0, j)),
                  pl.BlockSpec((1, 1, tn), lambda l, j: (l, 0, j))],
        out_specs=pl.BlockSpec((1, 8, tn), lambda l, j: (l, 0, j)),
        out_shape=jax.ShapeDtypeStruct((depth, 8, n), F32),
        compiler_params=_cparams(("parallel", "parallel")),
        name="mod",
    )(cvec, w_mod, b_mod.reshape(depth, 1, n))


def _rope_body(freq_ref, cos_ref, sin_ref, *, t_lat, tm):
    i = pl.program_id(0)
    t = i * tm + lax.broadcasted_iota(jnp.int32, (tm, LANES), 0)
    lane = lax.broadcasted_iota(jnp.int32, (tm, LANES), 1)
    shift = int(math.log2(GRID_W))
    row = lax.shift_right_logical(t, shift).astype(F32)
    col = jnp.bitwise_and(t, GRID_W - 1).astype(F32)
    use_row = jnp.bitwise_and(lane, 2 * ROPE_PAIRS) == 0
    first_half = jnp.bitwise_and(lane, ROPE_PAIRS) == 0
    ang = jnp.where(use_row, row, col) * freq_ref[...]
    is_lat = t < t_lat
    cos_ref[...] = jnp.where(is_lat, jnp.cos(ang), 1.0)
    sn = jnp.sin(ang)
    sin_ref[...] = jnp.where(is_lat, jnp.where(first_half, -sn, sn), 0.0)


def _rope_tables(t_lat, tall):
    tm = 256
    freqs = ROPE_BASE ** (-jnp.arange(ROPE_PAIRS, dtype=F32) / ROPE_PAIRS)
    freq_lane = jnp.tile(freqs, LANES // ROPE_PAIRS).reshape(1, LANES)
    return pl.pallas_call(
        functools.partial(_rope_body, t_lat=t_lat, tm=tm),
        grid=(tall // tm,),
        in_specs=[pl.BlockSpec((1, LANES), lambda i: (0, 0))],
        out_specs=[pl.BlockSpec((tm, LANES), lambda i: (i, 0))] * 2,
        out_shape=[jax.ShapeDtypeStruct((tall, LANES), F32)] * 2,
        compiler_params=_cparams(("parallel",)),
        name="rope_tables",
    )(freq_lane)


def _is_ctx_rows(r, tm, t_lat):
    t = r * tm + lax.broadcasted_iota(jnp.int32, (tm, 1), 0)
    return t >= t_lat


def _norm_mod(x, g, sc, sh):
    ms = jnp.mean(x * x, axis=-1, keepdims=True)
    return (x * lax.rsqrt(ms + EPS) * g) * (1.0 + sc) + sh


def _segnorm64(z, bd, gain):
    ss = _split_dot(z * z, bd)
    return z * lax.rsqrt(ss * (1.0 / HEAD_DIM) + EPS) * gain


def _hnorm_body(x_ref, g_ref, mb_ref, mc_ref, h_ref, *, t_lat, tm, d):
    r = pl.program_id(1)
    ctx = _is_ctx_rows(r, tm, t_lat)
    sh = jnp.where(ctx, mc_ref[0, :, 0:d], mb_ref[0, :, 0:d])
    sc = jnp.where(ctx, mc_ref[0, :, d:2 * d], mb_ref[0, :, d:2 * d])
    h_ref[0] = _norm_mod(x_ref[0], g_ref[...], sc, sh).astype(BF16)


def _hnorm_call(xall, g, mod3, t_lat):
    b, tall, d = xall.shape
    tm = _row_tile(tall)
    nb = b
    return pl.pallas_call(
        functools.partial(_hnorm_body, t_lat=t_lat, tm=tm, d=d),
        grid=(b, tall // tm),
        in_specs=[pl.BlockSpec((1, tm, d), lambda bi, r: (bi, r, 0)),
                  pl.BlockSpec((1, d), lambda bi, r: (0, 0)),
                  pl.BlockSpec((1, 1, 6 * d), lambda bi, r: (bi, 0, 0)),
                  pl.BlockSpec((1, 1, 6 * d), lambda bi, r: (nb, 0, 0))],
        out_specs=pl.BlockSpec((1, tm, d), lambda bi, r: (bi, r, 0)),
        out_shape=jax.ShapeDtypeStruct((b, tall, d), BF16),
        compiler_params=_cparams(("parallel", "parallel")),
        name="hnorm",
    )(xall, g.reshape(1, d), mod3, mod3)


def _rope128(x, cs, sn, first_half):
    partner = jnp.where(first_half, pltpu.roll(x, x.shape[1] - ROPE_PAIRS, 1), pltpu.roll(x, ROPE_PAIRS, 1))
    return x * cs + partner * sn


def _proj_da_body(h_ref, wq_ref, wk_ref, wv_ref, bd_ref, gq_ref, gk_ref, cos_ref, sin_ref,
                  qt_ref, k_ref, vt_ref):
    h = h_ref[0]
    tm = h.shape[0]
    bd = bd_ref[...]
    wide = bd.shape[0]
    cs = jnp.concatenate([cos_ref[...]] * (wide // LANES), axis=1)
    sn = jnp.concatenate([sin_ref[...]] * (wide // LANES), axis=1)
    lane = lax.broadcasted_iota(jnp.int32, cs.shape, 1)
    first_half = jnp.bitwise_and(lane, ROPE_PAIRS) == 0
    low_lanes = jnp.bitwise_and(lane, HEAD_DIM) == 0
    zq = _dot(h, wq_ref[...])
    zk = _dot(h, wk_ref[...])
    zv = _dot(h, wv_ref[...])
    scale = math.sqrt(HEAD_DIM ** -0.5 * LOG2E)
    ones_row = (lax.broadcasted_iota(jnp.int32, (DA_VPAD, tm), 0) == 0).astype(BF16)
    for j in range(zq.shape[1] // wide):
        sl = slice(j * wide, (j + 1) * wide)
        q = _rope128(_segnorm64(zq[:, sl], bd, gq_ref[...]) * scale, cs, sn, first_half)
        k = _rope128(_segnorm64(zk[:, sl], bd, gk_ref[...]) * scale, cs, sn, first_half)
        q_t = q.T
        qh_t = q_t.astype(F8).astype(F32)
        ql_t = q_t - qh_t
        k_hi = k.astype(F8).astype(F32)
        k_lo = k - k_hi
        k_map1 = jnp.where(low_lanes, k_hi, pltpu.roll(k_lo, HEAD_DIM, 1)).astype(F8)
        k_map2 = jnp.where(low_lanes, pltpu.roll(k_hi, wide - HEAD_DIM, 1), k_lo).astype(F8)
        for hh in range(wide // LANES):
            hd = j * (wide // LANES) + hh
            r0 = hd * 2 * LANES
            for a in range(2):
                src = slice(hh * LANES + a * HEAD_DIM, hh * LANES + (a + 1) * HEAD_DIM)
                qt_ref[0, r0 + a * LANES:r0 + a * LANES + HEAD_DIM, :] = qh_t[src].astype(F8)
                qt_ref[0, r0 + a * LANES + HEAD_DIM:r0 + (a + 1) * LANES, :] = ql_t[src].astype(F8)
            k_ref[0, :, r0:r0 + LANES] = k_map1[:, hh * LANES:(hh + 1) * LANES]
            k_ref[0, :, r0 + LANES:r0 + 2 * LANES] = k_map2[:, hh * LANES:(hh + 1) * LANES]
            v0 = hd * DA_VROWS
            vt_ref[0, v0:v0 + DA_V, :] = zv[:, hd * LANES:(hd + 1) * LANES].T.astype(BF16)
            vt_ref[0, v0 + DA_V:v0 + DA_VROWS, :] = ones_row


def _proj_da_call(h, wq, wk, wv, bd, gq, gk, cos_t, sin_t):
    b, tall, d = h.shape
    tm = _row_tile(tall)
    w = DA_HEADS * LANES
    nbd = bd.shape[0]
    const = lambda bi, r: (0, 0)
    return pl.pallas_call(
        _proj_da_body,
        grid=(b, tall // tm),
        in_specs=[pl.BlockSpec((1, tm, d), lambda bi, r: (bi, r, 0)),
                  pl.BlockSpec((d, w), const), pl.BlockSpec((d, w), const), pl.BlockSpec((d, w), const),
                  pl.BlockSpec((nbd, nbd), const),
                  pl.BlockSpec((1, nbd), const), pl.BlockSpec((1, nbd), const),
                  pl.BlockSpec((tm, LANES), lambda bi, r: (r, 0)),
                  pl.BlockSpec((tm, LANES), lambda bi, r: (r, 0))],
        out_specs=[pl.BlockSpec((1, 2 * w, tm), lambda bi, r: (bi, 0, r)),
                   pl.BlockSpec((1, tm, 2 * w), lambda bi, r: (bi, r, 0)),
                   pl.BlockSpec((1, DA_HEADS * DA_VROWS, tm), lambda bi, r: (bi, 0, r))],
        out_shape=[jax.ShapeDtypeStruct((b, 2 * w, tall), F8),
                   jax.ShapeDtypeStruct((b, tall, 2 * w), F8),
                   jax.ShapeDtypeStruct((b, DA_HEADS * DA_VROWS, tall), BF16)],
        compiler_params=_cparams(("parallel", "parallel")),
        name="proj_da",
    )(h, wq, wk, wv, bd, gq, gk, cos_t, sin_t)


def _proj_gla_body(h_ref, wq_ref, wk_ref, wv_ref, wg_ref, wa_ref, a2_ref, ab_ref,
                   q_ref, k_ref, v_ref, sg_ref, la_ref):
    h = h_ref[0]
    q_ref[0] = _dot(h, wq_ref[...]) * (GLA_DK ** -0.5)
    k_ref[0] = _dot(h, wk_ref[...])
    v_ref[0] = _dot(h, wv_ref[...]).astype(BF16)
    g = _dot(h, wg_ref[...])
    sg_ref[0] = (g * _sigmoid(g)).astype(BF16)
    ga = _dot(h, wa_ref[...])
    z = _dot(ga.astype(BF16), a2_ref[...]) + ab_ref[...]
    la_ref[0] = (jnp.minimum(z, 0.0) - jnp.log1p(jnp.exp(-jnp.abs(z)))) * (1.0 / GLA_TAU)


def _proj_gla_call(h, wq, wk, wv, wg, wa, a2p, ab):
    b, tall, d = h.shape
    tm = _row_tile(tall)
    wqk = GLA_HEADS * GLA_DK
    wv_ = GLA_HEADS * GLA_DV
    const = lambda bi, r: (0, 0)
    row = lambda n: pl.BlockSpec((1, tm, n), lambda bi, r: (bi, r, 0))
    return pl.pallas_call(
        _proj_gla_body,
        grid=(b, tall // tm),
        in_specs=[row(d),
                  pl.BlockSpec((d, wqk), const), pl.BlockSpec((d, wqk), const),
                  pl.BlockSpec((d, wv_), const), pl.BlockSpec((d, wv_), const),
                  pl.BlockSpec((d, LANES), const),
                  pl.BlockSpec((LANES, 2 * wqk), const), pl.BlockSpec((1, 2 * wqk), const)],
        out_specs=[row(wqk), row(wqk), row(wv_), row(wv_), row(2 * wqk)],
        out_shape=[jax.ShapeDtypeStruct((b, tall, wqk), F32),
                   jax.ShapeDtypeStruct((b, tall, wqk), F32),
                   jax.ShapeDtypeStruct((b, tall, wv_), BF16),
                   jax.ShapeDtypeStruct((b, tall, wv_), BF16),
                   jax.ShapeDtypeStruct((b, tall, 2 * wqk), F32)],
        compiler_params=_cparams(("parallel", "parallel")),
        name="proj_gla",
    )(h, wq, wk, wv, wg, wa, a2p, ab)


def _proj_na_body(h_ref, wq_ref, wk_ref, wv_ref, bd_ref, gq_ref, gk_ref, q_ref, k_ref, v_ref):
    h = h_ref[0]
    bd = bd_ref[...]
    wide = bd.shape[0]
    zq = _dot(h, wq_ref[...])
    zk = _dot(h, wk_ref[...])
    scale = HEAD_DIM ** -0.5 * LOG2E
    for j in range(NA_HEADS * HEAD_DIM // wide):
        sl = slice(j * wide, (j + 1) * wide)
        q_ref[0, :, sl] = (_segnorm64(zq[:, sl], bd, gq_ref[...]) * scale).astype(BF16)
        k_ref[0, :, sl] = _segnorm64(zk[:, sl], bd, gk_ref[...]).astype(BF16)
    v_ref[0] = _dot(h, wv_ref[...]).astype(BF16)


def _proj_na_call(h, wq, wk, wv, bd, gq, gk):
    b, tall, d = h.shape
    tm = _row_tile(tall)
    w = NA_HEADS * HEAD_DIM
    nbd = bd.shape[0]
    const = lambda bi, r: (0, 0)
    row = lambda n: pl.BlockSpec((1, tm, n), lambda bi, r: (bi, r, 0))
    return pl.pallas_call(
        _proj_na_body,
        grid=(b, tall // tm),
        in_specs=[row(d), pl.BlockSpec((d, w), const), pl.BlockSpec((d, w), const), pl.BlockSpec((d, w), const),
                  pl.BlockSpec((nbd, nbd), const),
                  pl.BlockSpec((1, nbd), const), pl.BlockSpec((1, nbd), const)],
        out_specs=[row(w), row(w), row(w)],
        out_shape=[jax.ShapeDtypeStruct((b, tall, w), BF16)] * 3,
        compiler_params=_cparams(("parallel", "parallel")),
        name="proj_na",
    )(h, wq, wk, wv, bd, gq, gk)


def _proj_gate_body(h_ref, w_ref, o_ref):
    h = h_ref[0]
    n = w_ref.shape[1]
    step = 512
    for j in range(n // step):
        sl = slice(j * step, (j + 1) * step)
        o_ref[0, :, sl] = _sigmoid(_dot(h, w_ref[:, sl])).astype(BF16)


def _proj_gate_call(h, w):
    b, tall, d = h.shape
    tm = _row_tile(tall)
    n = w.shape[1]
    return pl.pallas_call(
        _proj_gate_body,
        grid=(b, tall // tm),
        in_specs=[pl.BlockSpec((1, tm, d), lambda bi, r: (bi, r, 0)),
                  pl.BlockSpec((d, n), lambda bi, r: (0, 0))],
        out_specs=pl.BlockSpec((1, tm, n), lambda bi, r: (bi, r, 0)),
        out_shape=jax.ShapeDtypeStruct((b, tall, n), BF16),
        compiler_params=_cparams(("parallel", "parallel")),
        name="proj_gate",
    )(h, w)


def _da_body(safe_ref, qt_ref, k_ref, vt_ref, lp_ref, sg_ref, o_ref, m_ref, a_ref, acc_ref, *, lam_init, nk):
    j = pl.program_id(3)
    use_lag = safe_ref[0] > 0.0

    @pl.when(j == 0)
    def _():
        m_ref[...] = jnp.full(m_ref.shape, jnp.where(use_lag, -safe_ref[0], -jnp.inf), F32)
        a_ref[...] = jnp.ones(a_ref.shape, F32)
        acc_ref[...] = jnp.zeros(acc_ref.shape, F32)

    def scores(a, rows):
        kb = k_ref[0, rows, a * LANES:(a + 1) * LANES]
        q_hi = qt_ref[0, a * LANES:a * LANES + HEAD_DIM, :]
        q_lo = qt_ref[0, a * LANES + HEAD_DIM:(a + 1) * LANES, :]
        w = jnp.concatenate([q_hi, q_hi, q_lo, q_lo], axis=0)
        return _dot(jnp.concatenate([kb, kb], axis=1), w)

    def lagged_step():
        vt = vt_ref[0]
        for a in range(2):
            s = scores(a, slice(None))
            m_old = m_ref[a:a + 1, :]
            m_new = jnp.maximum(m_old, jnp.max(s, axis=0, keepdims=True))
            p = jnp.exp2(s - m_old)
            acc_ref[a] = a_ref[a:a + 1, :] * acc_ref[a] + _dot(vt, p.astype(BF16))
            a_ref[a:a + 1, :] = jnp.exp2(m_old - m_new)
            m_ref[a:a + 1, :] = m_new

    def plain_step():
        vt = vt_ref[0]
        for a in range(2):
            s = scores(a, slice(None))
            m_old = m_ref[a:a + 1, :]
            m_new = jnp.maximum(m_old, jnp.max(s, axis=0, keepdims=True))
            p = jnp.exp2(s - m_new)
            acc_ref[a] = (a_ref[a:a + 1, :] * jnp.exp2(m_old - m_new)) * acc_ref[a] + _dot(vt, p.astype(BF16))
            a_ref[a:a + 1, :] = jnp.ones_like(m_old)
            m_ref[a:a + 1, :] = m_new

    pl.when(use_lag)(lagged_step)
    pl.when(jnp.logical_not(use_lag))(plain_step)

    @pl.when(j == nk - 1)
    def _():
        lp = lp_ref[...]
        e1 = jnp.exp(jnp.sum(lp[0:1] * lp[1:2], axis=-1, keepdims=True))
        e2 = jnp.exp(jnp.sum(lp[2:3] * lp[3:4], axis=-1, keepdims=True))
        lam = e1 - e2 + lam_init
        o1 = acc_ref[0, 0:DA_V, :] / acc_ref[0, DA_V:DA_V + 1, :]
        o2 = acc_ref[1, 0:DA_V, :] / acc_ref[1, DA_V:DA_V + 1, :]
        o = o1 - lam * o2
        ms = jnp.mean(o * o, axis=0, keepdims=True)
        y = (o * lax.rsqrt(ms + EPS) * sg_ref[...]) * (1.0 - lam_init)
        o_ref[0] = y.T.astype(BF16)


def _da_call(safe, qt, kk, vt, lp, subg, lam_init, *, q_off, nq, k_off, nk, tq, tk):
    b = qt.shape[0]
    return pl.pallas_call(
        functools.partial(_da_body, lam_init=lam_init, nk=nk),
        grid=(b, DA_HEADS, nq, nk),
        in_specs=[pl.BlockSpec(memory_space=pltpu.SMEM),
                  pl.BlockSpec((1, 2 * LANES, tq), lambda bi, h, i, j: (bi, h, i + q_off)),
                  pl.BlockSpec((1, tk, 2 * LANES), lambda bi, h, i, j: (bi, j + k_off, h)),
                  pl.BlockSpec((1, DA_VROWS, tk), lambda bi, h, i, j: (bi, h, j + k_off)),
                  pl.BlockSpec((4, HEAD_DIM), lambda bi, h, i, j: (0, 0)),
                  pl.BlockSpec((LANES, 1), lambda bi, h, i, j: (0, 0))],
        out_specs=pl.BlockSpec((1, tq, LANES), lambda bi, h, i, j: (bi, i, h)),
        out_shape=jax.ShapeDtypeStruct((b, nq * tq, DA_HEADS * LANES), BF16),
        scratch_shapes=[pltpu.VMEM((8, tq), F32), pltpu.VMEM((8, tq), F32),
                        pltpu.VMEM((2, DA_VROWS, tq), F32)],
        compiler_params=_cparams(("parallel", "parallel", "parallel", "arbitrary")),
        name="diff_attn",
    )(safe, qt, kk, vt, lp, subg)


def _gla_body(q_ref, k_ref, v_ref, la_ref, tri_ref, o_ref, s_ref, *, reverse, tb):
    i = pl.program_id(1)

    @pl.when(i == 0)
    def _():
        s_ref[...] = jnp.zeros(s_ref.shape, F32)

    c = GLA_CHUNK
    w = GLA_HEADS * GLA_DK
    nch = tb // c
    order = list(reversed(range(nch))) if reverse else list(range(nch))
    tri = tri_ref[...]
    keep = tri > 0
    cum = _split_dot_left(tri, la_ref[0])
    tot = [cum[ch * c:ch * c + 1, :] if reverse else cum[(ch + 1) * c - 1:(ch + 1) * c, :] for ch in range(nch)]
    tot_rows = jnp.concatenate([jnp.broadcast_to(t, (c, w)) for t in tot], axis=0)
    q = q_ref[0]
    k = k_ref[0]
    qe = q * jnp.exp(cum)
    ke = (k * jnp.exp(-cum)).astype(BF16)
    kd = (k * jnp.exp(tot_rows - cum)).astype(BF16)
    dec = [jnp.exp(t) for t in tot]
    lane = lax.broadcasted_iota(jnp.int32, (1, w), 1)
    heads = range(GLA_HEADS)
    qh = [jnp.where((lane >= hd * GLA_DK) & (lane < (hd + 1) * GLA_DK), qe, 0.0).astype(BF16) for hd in heads]
    vh = [v_ref[0, :, hd * GLA_DV:(hd + 1) * GLA_DV] for hd in heads]
    inc = [[_dot_tn(vh[hd][ch * c:(ch + 1) * c], kd[ch * c:(ch + 1) * c]) for ch in range(nch)] for hd in heads]
    a = [jnp.where(keep, _dot_nt(qh[hd], ke), 0.0).astype(BF16) for hd in heads]
    entry = []
    for hd in heads:
        st = s_ref[hd]
        seen = {}
        for ch in order:
            seen[ch] = st.astype(BF16)
            st = st * dec[ch] + inc[hd][ch]
        s_ref[hd] = st
        entry.append(seen)
    o_intra = [_dot(a[hd], vh[hd]) for hd in heads]
    for hd in heads:
        for ch in range(nch):
            rows = slice(ch * c, (ch + 1) * c)
            o_ref[0, rows, hd * GLA_DV:(hd + 1) * GLA_DV] = o_intra[hd][rows] + _dot_nt(qh[hd][rows], entry[hd][ch])


def _gla_call(gq, gk, gv, la, tri, *, t_lat, reverse):
    b, tall, w = gq.shape
    tb = GLA_BLOCK
    n_lat = t_lat // tb
    nblk = tall // tb
    if reverse:
        blk = lambda i: jnp.where(i == 0, n_lat, n_lat - i)
    else:
        blk = lambda i: jnp.where(i == 0, n_lat, i - 1)
    wv_ = GLA_HEADS * GLA_DV
    return pl.pallas_call(
        functools.partial(_gla_body, reverse=reverse, tb=tb),
        grid=(b, nblk),
        in_specs=[pl.BlockSpec((1, tb, w), lambda bi, i: (bi, blk(i), 0)),
                  pl.BlockSpec((1, tb, w), lambda bi, i: (bi, blk(i), 0)),
                  pl.BlockSpec((1, tb, wv_), lambda bi, i: (bi, blk(i), 0)),
                  pl.BlockSpec((1, tb, w), lambda bi, i: (bi, blk(i), 1 if reverse else 0)),
                  pl.BlockSpec((tb, tb), lambda bi, i: (0, 0))],
        out_specs=pl.BlockSpec((1, tb, wv_), lambda bi, i: (bi, blk(i), 0)),
        out_shape=jax.ShapeDtypeStruct((b, tall, wv_), F32),
        scratch_shapes=[pltpu.VMEM((GLA_HEADS, GLA_DV, w), F32)],
        compiler_params=_cparams(("parallel", "arbitrary")),
        name="gla_bwd" if reverse else "gla_fwd",
    )(gq, gk, gv, la, tri)


def _na_body(safe_ref, q_ref, k_ref, v_ref, kc_ref, vc_ref, bias_ref, o_ref, *, rows):
    i = pl.program_id(2)
    kb0 = jnp.clip(i * NA_QROWS - NA_KR // 2, 0, rows - NA_BAND)
    start = pl.multiple_of(kb0 * GRID_W, GRID_W)
    nband = NA_BAND * GRID_W

    def attend(fixed_ref):
        q = q_ref[0]
        kb = k_ref[0, pl.ds(start, nband), :]
        kc = kc_ref[0]
        vall = jnp.concatenate([v_ref[0, pl.ds(start, nband), :], vc_ref[0]], axis=0)
        klane = lax.broadcasted_iota(jnp.int32, vall.shape, 1)
        vaug = jnp.concatenate([vall, (klane == 0).astype(BF16)], axis=1)
        lane = lax.broadcasted_iota(jnp.int32, q.shape, 1)
        qh = [jnp.where((lane < HEAD_DIM) if hh == 0 else (lane >= HEAD_DIM), q, jnp.zeros_like(q))
              for hh in range(2)]
        s_loc = [_dot_nt(qh[hh], kb) for hh in range(2)]
        s_ctx = [_dot_nt(qh[hh], kc) for hh in range(2)]
        p = []
        for hh in range(2):
            sl = s_loc[hh] + bias_ref[0, hh]
            if fixed_ref:
                m = safe_ref[0]
            else:
                m = jnp.maximum(jnp.max(sl, axis=-1, keepdims=True), jnp.max(s_ctx[hh], axis=-1, keepdims=True))
            p.append(jnp.concatenate([jnp.exp2(sl - m), jnp.exp2(s_ctx[hh] - m)], axis=1).astype(BF16))
        acc = [_dot(p[hh], vaug) for hh in range(2)]
        outs = [acc[hh][:, 0:LANES] / acc[hh][:, LANES:LANES + 1] for hh in range(2)]
        o_ref[0] = jnp.where(lane < HEAD_DIM, outs[0], outs[1]).astype(BF16)

    fixed = safe_ref[0] > 0.0
    pl.when(fixed)(functools.partial(attend, True))
    pl.when(jnp.logical_not(fixed))(functools.partial(attend, False))


def _na_call(safe, nq, nk, nv, bias, *, t_lat):
    b, tall, w = nq.shape
    tc = tall - t_lat
    rows = t_lat // GRID_W
    tq = NA_QROWS * GRID_W
    nsteps = rows // NA_QROWS
    npair = w // LANES
    ctx_blk = t_lat // tc

    def btype(i):
        return jnp.where(i == 0, 0, jnp.where(i == nsteps - 2, 2, jnp.where(i == nsteps - 1, 3, 1)))

    return pl.pallas_call(
        functools.partial(_na_body, rows=rows),
        grid=(b, npair, nsteps),
        in_specs=[pl.BlockSpec(memory_space=pltpu.SMEM),
                  pl.BlockSpec((1, tq, LANES), lambda bi, hp, i: (bi, i, hp)),
                  pl.BlockSpec((1, t_lat, LANES), lambda bi, hp, i: (bi, 0, hp)),
                  pl.BlockSpec((1, t_lat, LANES), lambda bi, hp, i: (bi, 0, hp)),
                  pl.BlockSpec((1, tc, LANES), lambda bi, hp, i: (bi, ctx_blk, hp)),
                  pl.BlockSpec((1, tc, LANES), lambda bi, hp, i: (bi, ctx_blk, hp)),
                  pl.BlockSpec((1, 2, tq, NA_BAND * GRID_W), lambda bi, hp, i: (btype(i), hp, 0, 0))],
        out_specs=pl.BlockSpec((1, tq, LANES), lambda bi, hp, i: (bi, i, hp)),
        out_shape=jax.ShapeDtypeStruct((b, t_lat, w), BF16),
        compiler_params=_cparams(("parallel", "parallel", "arbitrary")),
        name="nbr_attn",
    )(safe, nq, nk, nv, nk, nv, bias)


def _na_bias_tiles(rpb, rows):
    assert rows >= NA_BAND and rows % NA_QROWS == 0
    nl, nh, na, nb = rpb.shape
    cidx = np.arange(GRID_W)
    rel_c = cidx[None, :] - cidx[:, None] + NA_KC - 1
    sel = jnp.asarray(rel_c[None] == np.arange(nb)[:, None, None], F32)
    toep = jnp.einsum('lhab,bqk->lhqak', rpb.astype(F32) * LOG2E, sel, precision=lax.Precision.HIGHEST)
    toep = jnp.pad(toep, ((0, 0), (0, 0), (0, 0), (NA_BAND, NA_BAND), (0, 0)))
    tiles = []
    for r0 in (0, NA_QROWS, rows - 2 * NA_QROWS, rows - NA_QROWS):
        kb0 = min(max(r0 - NA_KR // 2, 0), rows - NA_BAND)
        qi = np.arange(NA_QROWS * GRID_W)
        qr, qc = r0 + qi // GRID_W, qi % GRID_W
        kj = np.arange(NA_BAND * GRID_W)
        kr, kc = kb0 + kj // GRID_W, kj % GRID_W
        rs = np.clip(qr - NA_KR // 2, 0, rows - NA_KR)
        cs = np.clip(qc - NA_KC // 2, 0, GRID_W - NA_KC)
        valid = ((kr[None, :] >= rs[:, None]) & (kr[None, :] < rs[:, None] + NA_KR)
                 & (kc[None, :] >= cs[:, None]) & (kc[None, :] < cs[:, None] + NA_KC))
        parts = []
        for q_row in range(NA_QROWS):
            a0 = kb0 - (r0 + q_row) + NA_KR - 1 + NA_BAND
            blk = toep[:, :, :, a0:a0 + NA_BAND, :]
            parts.append(blk.reshape(nl, nh, GRID_W, NA_BAND * GRID_W))
        tile = jnp.concatenate(parts, axis=2)
        tiles.append(jnp.where(jnp.asarray(valid)[None, None], tile, NEG_BIG))
    return jnp.stack(tiles, axis=1)


def _ctx_attn_body(q_ref, k_ref, v_ref, o_ref):
    q = q_ref[0]
    k = k_ref[0]
    v = v_ref[0]
    lane = lax.broadcasted_iota(jnp.int32, q.shape, 1)
    outs = []
    for hh in range(2):
        qh = jnp.where((lane < HEAD_DIM) if hh == 0 else (lane >= HEAD_DIM), q, jnp.zeros_like(q))
        s = _dot_nt(qh, k)
        m = jnp.max(s, axis=-1, keepdims=True)
        p = jnp.exp2(s - m)
        outs.append(_dot(p.astype(BF16), v) / jnp.sum(p, axis=-1, keepdims=True))
    o_ref[0] = jnp.where(lane < HEAD_DIM, outs[0], outs[1]).astype(BF16)


def _ctx_attn_call(nq, nk, nv, *, t_lat):
    b, tall, w = nq.shape
    tc = tall - t_lat
    ctx_blk = t_lat // tc
    spec = pl.BlockSpec((1, tc, LANES), lambda bi, hp: (bi, ctx_blk, hp))
    return pl.pallas_call(
        _ctx_attn_body,
        grid=(b, w // LANES),
        in_specs=[spec, spec, spec],
        out_specs=pl.BlockSpec((1, tc, LANES), lambda bi, hp: (bi, 0, hp)),
        out_shape=jax.ShapeDtypeStruct((b, tc, w), BF16),
        compiler_params=_cparams(("parallel", "parallel")),
        name="ctx_attn",
    )(nq, nk, nv)


def _merge_body(x_ref, yd_ref, of_ref, ob_ref, sg_ref, yn_ref, gt_ref, wd_ref, wg_ref, wn_ref, wo_ref,
                gn_ref, mb_ref, mc_ref, o_ref, *, t_lat, tm, d):
    r = pl.program_id(1)
    ctx = _is_ctx_rows(r, tm, t_lat)
    g1 = jnp.where(ctx, mc_ref[0, :, 2 * d:3 * d], mb_ref[0, :, 2 * d:3 * d])
    og = of_ref[0] + ob_ref[0]
    parts = []
    for hd in range(GLA_HEADS):
        oh = og[:, hd * GLA_DV:(hd + 1) * GLA_DV]
        ms = jnp.mean(oh * oh, axis=-1, keepdims=True)
        parts.append(oh * lax.rsqrt(ms + EPS) * gn_ref[...])
    yg = (jnp.concatenate(parts, axis=-1) * sg_ref[0].astype(F32)).astype(BF16)
    m = (gt_ref[0, :, 0:d].astype(F32) * _dot(yd_ref[0], wd_ref[...])
         + gt_ref[0, :, d:2 * d].astype(F32) * _dot(yg, wg_ref[...])
         + gt_ref[0, :, 2 * d:3 * d].astype(F32) * _dot(yn_ref[0], wn_ref[...]))
    o_ref[0] = x_ref[0] + g1 * _dot(m.astype(BF16), wo_ref[...])


def _merge_call(xall, yd, of, ob, sg, yn, gates, wd, wg, wn, wo, gn, mod3, t_lat):
    b, tall, d = xall.shape
    tm = _row_tile(tall)
    nb = b
    const = lambda bi, r: (0, 0)
    row = lambda n: pl.BlockSpec((1, tm, n), lambda bi, r: (bi, r, 0))
    return pl.pallas_call(
        functools.partial(_merge_body, t_lat=t_lat, tm=tm, d=d),
        grid=(b, tall // tm),
        in_specs=[row(d), row(BR_W), row(BR_W), row(BR_W), row(BR_W), row(BR_W), row(3 * d),
                  pl.BlockSpec((BR_W, d), const), pl.BlockSpec((BR_W, d), const), pl.BlockSpec((BR_W, d), const),
                  pl.BlockSpec((d, d), const), pl.BlockSpec((1, GLA_DV), const),
                  pl.BlockSpec((1, 1, 6 * d), lambda bi, r: (bi, 0, 0)),
                  pl.BlockSpec((1, 1, 6 * d), lambda bi, r: (nb, 0, 0))],
        out_specs=row(d),
        out_shape=jax.ShapeDtypeStruct((b, tall, d), F32),
        compiler_params=_cparams(("parallel", "parallel")),
        name="merge",
    )(xall, yd, of, ob, sg, yn, gates, wd, wg, wn, wo, gn, mod3, mod3)


def _mlp_body(x_ref, g_ref, w1_ref, w2_ref, mb_ref, mc_ref, o_ref, *, t_lat, tm, d):
    r = pl.program_id(1)
    ctx = _is_ctx_rows(r, tm, t_lat)
    sh = jnp.where(ctx, mc_ref[0, :, 3 * d:4 * d], mb_ref[0, :, 3 * d:4 * d])
    sc = jnp.where(ctx, mc_ref[0, :, 4 * d:5 * d], mb_ref[0, :, 4 * d:5 * d])
    g2 = jnp.where(ctx, mc_ref[0, :, 5 * d:6 * d], mb_ref[0, :, 5 * d:6 * d])
    x = x_ref[0]
    h = _norm_mod(x, g_ref[...], sc, sh).astype(BF16)
    acc = jnp.zeros((tm, d), F32)
    step = 1024
    for j in range(w1_ref.shape[1] // step):
        a = jnp.maximum(_dot(h, w1_ref[:, j * step:(j + 1) * step]), 0.0)
        acc = acc + _dot((a * a).astype(BF16), w2_ref[j * step:(j + 1) * step, :])
    o_ref[0] = x + g2 * acc


def _mlp_call(xall, g, w1, w2, mod3, t_lat, *, lat_only):
    b, tall, d = xall.shape
    tm = next(t for t in (512, 256) if t_lat % t == 0) if lat_only else _row_tile(tall)
    n_rows = t_lat if lat_only else tall
    nb = b
    dff = w1.shape[1]
    const = lambda bi, r: (0, 0)
    return pl.pallas_call(
        functools.partial(_mlp_body, t_lat=t_lat, tm=tm, d=d),
        grid=(b, n_rows // tm),
        in_specs=[pl.BlockSpec((1, tm, d), lambda bi, r: (bi, r, 0)),
                  pl.BlockSpec((1, d), const),
                  pl.BlockSpec((d, dff), const, pipeline_mode=pl.Buffered(1)),
                  pl.BlockSpec((dff, d), const, pipeline_mode=pl.Buffered(1)),
                  pl.BlockSpec((1, 1, 6 * d), lambda bi, r: (bi, 0, 0)),
                  pl.BlockSpec((1, 1, 6 * d), lambda bi, r: (nb, 0, 0))],
        out_specs=pl.BlockSpec((1, tm, d), lambda bi, r: (bi, r, 0)),
        out_shape=jax.ShapeDtypeStruct((b, n_rows, d), F32),
        compiler_params=_cparams(("parallel", "parallel")),
        name="mlp",
    )(xall, g.reshape(1, d), w1, w2, mod3, mod3)


def kernel(x, c, ctx, c_ctx, w_mod, b_mod, norm1_g, norm2_g, w_in, da_qn_g, da_kn_g, da_lambda, da_subln_g,
           gla_a2, gla_a_b, gla_gn_g, na_qn_g, na_kn_g, na_rpb, w_br_da, w_br_gla, w_br_na, w_out, w_ff1, w_ff2):
    b, t_lat, d = x.shape
    tc = ctx.shape[1]
    tall = t_lat + tc
    depth = w_mod.shape[0]
    rows = t_lat // GRID_W
    assert d == D_MODEL and t_lat % (2 * tc) == 0 and tc == 256 and b < 8

    xall = jnp.concatenate([x, ctx], axis=1)
    cvec = jnp.zeros((8, d), F32).at[0:b].set(c).at[b].set(c_ctx)
    mod = _mod_call(cvec, w_mod, b_mod)
    cos_t, sin_t = _rope_tables(t_lat, tall)

    seg = np.arange(NORM_LANES) // HEAD_DIM
    bd = jnp.asarray(seg[:, None] == seg[None, :], BF16)
    ci = np.arange(GLA_BLOCK)
    same_chunk = (ci[None, :] // GLA_CHUNK) == (ci[:, None] // GLA_CHUNK)
    tri_f = jnp.asarray(same_chunk & (ci[None, :] <= ci[:, None]), BF16)
    tri_b = jnp.asarray(same_chunk & (ci[None, :] >= ci[:, None]), BF16)

    o_dq, o_dk, o_dv = 0, 512, 1024
    o_gq, o_gk, o_gv, o_gg, o_ga = 1536, 1792, 2048, 2560, 3072
    o_nq, o_nk, o_nv = 3104, 3616, 4128
    o_gate = 4640
    tq_da = 512 if t_lat % 512 == 0 else 256
    tk_da = next(t for t in (3328, 1280, 256) if tall % t == 0)
    na_bias = _na_bias_tiles(na_rpb, rows)

    for l in range(depth):
        need_ctx = l < depth - 1
        lam_init = 0.8 - 0.6 * math.exp(-0.3 * l)
        mod3 = mod[l].reshape(8, 1, 6 * d)
        wl = w_in[l].astype(BF16)
        cut = lambda a, n: wl[:, a:a + n]
        wa = jnp.concatenate([cut(o_ga, 2 * GLA_RANK), jnp.zeros((d, LANES - 2 * GLA_RANK), BF16)], axis=1)
        a2p = jnp.zeros((LANES, 2 * GLA_HEADS * GLA_DK), F32)
        a2p = a2p.at[0:GLA_RANK, 0:256].set(gla_a2[l, 0]).at[GLA_RANK:2 * GLA_RANK, 256:512].set(gla_a2[l, 1])
        ab = gla_a_b[l].reshape(1, 2 * GLA_HEADS * GLA_DK)
        tile2 = lambda g: jnp.tile(g, NORM_LANES // HEAD_DIM).reshape(1, NORM_LANES)

        h = _hnorm_call(xall, norm1_g[l], mod3, t_lat)
        qt, kk, vt = _proj_da_call(h, cut(o_dq, 512), cut(o_dk, 512), cut(o_dv, 512), bd,
                                   tile2(da_qn_g[l]), tile2(da_kn_g[l]), cos_t, sin_t)
        gq, gk, gv, sgg, la = _proj_gla_call(h, cut(o_gq, 256), cut(o_gk, 256), cut(o_gv, 512), cut(o_gg, 512),
                                             wa, a2p.astype(BF16), ab)
        nq, nk, nv = _proj_na_call(h, cut(o_nq, 512), cut(o_nk, 512), cut(o_nv, 512), bd,
                                   tile2(na_qn_g[l]), tile2(na_kn_g[l]))
        def score_bound(gq_, gk_, extra):
            bnd = (math.sqrt(HEAD_DIM) * LOG2E * 1.01) * jnp.max(jnp.abs(gq_)) * jnp.max(jnp.abs(gk_)) + extra + 1e-3
            return jnp.where(bnd <= DA_SAFE_LOG2, bnd, -1.0).astype(F32).reshape(1)
        safe = score_bound(da_qn_g[l], da_kn_g[l], 0.0)
        safe_na = score_bound(na_qn_g[l], na_kn_g[l], jnp.max(jnp.abs(na_rpb[l])) * LOG2E)
        gates = _proj_gate_call(h, cut(o_gate, 3 * d))

        subg = da_subln_g[l].reshape(LANES, 1)
        y_da = _da_call(safe, qt, kk, vt, da_lambda[l], subg, lam_init,
                        q_off=0, nq=t_lat // tq_da, k_off=0, nk=tall // tk_da, tq=tq_da, tk=tk_da)
        o_f = _gla_call(gq, gk, gv, la, tri_f, t_lat=t_lat, reverse=False)
        o_b = _gla_call(gq, gk, gv, la, tri_b, t_lat=t_lat, reverse=True)
        y_na = _na_call(safe_na, nq, nk, nv, na_bias[l], t_lat=t_lat)
        if need_ctx:
            y_da_c = _da_call(safe, qt, kk, vt, da_lambda[l], subg, lam_init,
                              q_off=t_lat // tc, nq=1, k_off=t_lat // tc, nk=1, tq=tc, tk=tc)
            y_na_c = _ctx_attn_call(nq, nk, nv, t_lat=t_lat)
        else:
            y_da_c = jnp.zeros((b, tc, BR_W), BF16)
            y_na_c = jnp.zeros((b, tc, BR_W), BF16)
        y_da = jnp.concatenate([y_da, y_da_c], axis=1)
        y_na = jnp.concatenate([y_na, y_na_c], axis=1)

        xall = _merge_call(xall, y_da, o_f, o_b, sgg, y_na, gates,
                           w_br_da[l].astype(BF16), w_br_gla[l].astype(BF16), w_br_na[l].astype(BF16),
                           w_out[l].astype(BF16), gla_gn_g[l].reshape(1, GLA_DV), mod3, t_lat)
        xall = _mlp_call(xall, norm2_g[l], w_ff1[l].astype(BF16), w_ff2[l].astype(BF16), mod3, t_lat,
                         lat_only=not need_ctx)
    return xall
```

```python
import functools
import math

import numpy as np
import jax
import jax.numpy as jnp
from jax import lax
from jax.experimental import pallas as pl
from jax.experimental.pallas import tpu as pltpu

F32 = jnp.float32
BF16 = jnp.bfloat16
F8 = jnp.float8_e4m3fn

D_MODEL = 1024
GRID_W = 64
HEAD_DIM = 64
EPS = 1e-6
ROPE_BASE = 10000.0
ROPE_PAIRS = HEAD_DIM // 4
DA_HEADS = 4
GLA_HEADS = 4
GLA_DK = 64
GLA_DV = 128
GLA_RANK = 16
GLA_TAU = 16.0
GLA_CHUNK = 64
NA_HEADS = 8
NA_KR = 8
NA_KC = 16
BR_W = 512
D_FF = 4 * D_MODEL

DA_V = 2 * HEAD_DIM
DA_VPAD = 16
DA_VROWS = DA_V + DA_VPAD
LOG2E = math.log2(math.e)
DA_SAFE_LOG2 = 45.0

MXU_DIM = 256
GLA_BLOCK = 256
NORM_LANES = 256
LANES = 128
NA_QROWS = 4
NA_BAND = 16
NEG_BIG = -1e30
VMEM_LIMIT = 48 * 1024 * 1024


def _cparams(sem):
    return pltpu.CompilerParams(dimension_semantics=sem, vmem_limit_bytes=VMEM_LIMIT)


def _dot(a, b):
    return jnp.dot(a, b, preferred_element_type=F32)


def _dot_nt(a, b):
    return lax.dot_general(a, b, (((1,), (1,)), ((), ())), preferred_element_type=F32)


def _dot_tn(a, b):
    return lax.dot_general(a, b, (((0,), (0,)), ((), ())), preferred_element_type=F32)


def _split_dot(x, w_exact):
    hi = x.astype(BF16)
    lo = (x - hi.astype(F32)).astype(BF16)
    return _dot(hi, w_exact) + _dot(lo, w_exact)


def _split_dot_left(w_exact, x):
    hi = x.astype(BF16)
    lo = (x - hi.astype(F32)).astype(BF16)
    return _dot(w_exact, hi) + _dot(w_exact, lo)


def _sigmoid(x):
    return 1.0 / (1.0 + jnp.exp(-x))


def _row_tile(tall):
    for tm in (640, 256):
        if tall % tm == 0:
            return tm
    raise ValueError(f"unsupported token count {tall}")


def _mod_body(c_ref, w_ref, b_ref, o_ref):
    cv = c_ref[...]
    s = cv * _sigmoid(cv)
    o_ref[0] = _dot(s.astype(BF16), w_ref[0].astype(BF16)) + b_ref[0]


def _mod_call(cvec, w_mod, b_mod):
    depth, d, n = w_mod.shape
    tn = 1536
    return pl.pallas_call(
        _mod_body,
        grid=(depth, n // tn),
        in_specs=[pl.BlockSpec((8, d), lambda l, j: (0, 0)),
                  pl.BlockSpec((1, d, tn), lambda l, j: (l, 0, j)),
                  pl.BlockSpec((1, 1, tn), lambda l, j: (l, 0, j))],
        out_specs=pl.BlockSpec((1, 8, tn), lambda l, j: (l, 0, j)),
        out_shape=jax.ShapeDtypeStruct((depth, 8, n), F32),
        compiler_params=_cparams(("parallel", "parallel")),
        name="mod",
    )(cvec, w_mod, b_mod.reshape(depth, 1, n))


def _rope_body(freq_ref, cos_ref, sin_ref, *, t_lat, tm):
    i = pl.program_id(0)
    t = i * tm + lax.broadcasted_iota(jnp.int32, (tm, LANES), 0)
    lane = lax.broadcasted_iota(jnp.int32, (tm, LANES), 1)
    shift = int(math.log2(GRID_W))
    row = lax.shift_right_logical(t, shift).astype(F32)
    col = jnp.bitwise_and(t, GRID_W - 1).astype(F32)
    use_row = jnp.bitwise_and(lane, 2 * ROPE_PAIRS) == 0
    first_half = jnp.bitwise_and(lane, ROPE_PAIRS) == 0
    ang = jnp.where(use_row, row, col) * freq_ref[...]
    is_lat = t < t_lat
    cos_ref[...] = jnp.where(is_lat, jnp.cos(ang), 1.0)
    sn = jnp.sin(ang)
    sin_ref[...] = jnp.where(is_lat, jnp.where(first_half, -sn, sn), 0.0)


def _rope_tables(t_lat, tall):
    tm = 256
    freqs = ROPE_BASE ** (-jnp.arange(ROPE_PAIRS, dtype=F32) / ROPE_PAIRS)
    freq_lane = jnp.tile(freqs, LANES // ROPE_PAIRS).reshape(1, LANES)
    return pl.pallas_call(
        functools.partial(_rope_body, t_lat=t_lat, tm=tm),
        grid=(tall // tm,),
        in_specs=[pl.BlockSpec((1, LANES), lambda i: (0, 0))],
        out_specs=[pl.BlockSpec((tm, LANES), lambda i: (i, 0))] * 2,
        out_shape=[jax.ShapeDtypeStruct((tall, LANES), F32)] * 2,
        compiler_params=_cparams(("parallel",)),
        name="rope_tables",
    )(freq_lane)


def _is_ctx_rows(r, tm, t_lat):
    t = r * tm + lax.broadcasted_iota(jnp.int32, (tm, 1), 0)
    return t >= t_lat


def _norm_mod(x, g, sc, sh):
    ms = jnp.mean(x * x, axis=-1, keepdims=True)
    return (x * lax.rsqrt(ms + EPS) * g) * (1.0 + sc) + sh


def _segnorm64(z, bd, gain):
    ss = _split_dot(z * z, bd)
    return z * lax.rsqrt(ss * (1.0 / HEAD_DIM) + EPS) * gain


def _hnorm_body(x_ref, g_ref, mb_ref, mc_ref, h_ref, *, t_lat, tm, d):
    r = pl.program_id(1)
    ctx = _is_ctx_rows(r, tm, t_lat)
    sh = jnp.where(ctx, mc_ref[0, :, 0:d], mb_ref[0, :, 0:d])
    sc = jnp.where(ctx, mc_ref[0, :, d:2 * d], mb_ref[0, :, d:2 * d])
    h_ref[0] = _norm_mod(x_ref[0], g_ref[...], sc, sh).astype(BF16)


def _hnorm_call(xall, g, mod3, t_lat):
    b, tall, d = xall.shape
    tm = _row_tile(tall)
    nb = b
    return pl.pallas_call(
        functools.partial(_hnorm_body, t_lat=t_lat, tm=tm, d=d),
        grid=(b, tall // tm),
        in_specs=[pl.BlockSpec((1, tm, d), lambda bi, r: (bi, r, 0)),
                  pl.BlockSpec((1, d), lambda bi, r: (0, 0)),
                  pl.BlockSpec((1, 1, 6 * d), lambda bi, r: (bi, 0, 0)),
                  pl.BlockSpec((1, 1, 6 * d), lambda bi, r: (nb, 0, 0))],
        out_specs=pl.BlockSpec((1, tm, d), lambda bi, r: (bi, r, 0)),
        out_shape=jax.ShapeDtypeStruct((b, tall, d), BF16),
        compiler_params=_cparams(("parallel", "parallel")),
        name="hnorm",
    )(xall, g.reshape(1, d), mod3, mod3)


def _rope128(x, cs, sn, first_half):
    partner = jnp.where(first_half, pltpu.roll(x, x.shape[1] - ROPE_PAIRS, 1), pltpu.roll(x, ROPE_PAIRS, 1))
    return x * cs + partner * sn


def _proj_da_body(h_ref, wq_ref, wk_ref, wv_ref, bd_ref, gq_ref, gk_ref, cos_ref, sin_ref,
                  qt_ref, k_ref, vt_ref):
    h = h_ref[0]
    tm = h.shape[0]
    bd = bd_ref[...]
    wide = bd.shape[0]
    cs = jnp.concatenate([cos_ref[...]] * (wide // LANES), axis=1)
    sn = jnp.concatenate([sin_ref[...]] * (wide // LANES), axis=1)
    lane = lax.broadcasted_iota(jnp.int32, cs.shape, 1)
    first_half = jnp.bitwise_and(lane, ROPE_PAIRS) == 0
    low_lanes = jnp.bitwise_and(lane, HEAD_DIM) == 0
    zq = _dot(h, wq_ref[...])
    zk = _dot(h, wk_ref[...])
    zv = _dot(h, wv_ref[...])
    scale = math.sqrt(HEAD_DIM ** -0.5 * LOG2E)
    ones_row = (lax.broadcasted_iota(jnp.int32, (DA_VPAD, tm), 0) == 0).astype(BF16)
    for j in range(zq.shape[1] // wide):
        sl = slice(j * wide, (j + 1) * wide)
        q = _rope128(_segnorm64(zq[:, sl], bd, gq_ref[...]) * scale, cs, sn, first_half)
        k = _rope128(_segnorm64(zk[:, sl], bd, gk_ref[...]) * scale, cs, sn, first_half)
        q_t = q.T
        qh_t = q_t.astype(F8).astype(F32)
        ql_t = q_t - qh_t
        k_hi = k.astype(F8).astype(F32)
        k_lo = k - k_hi
        k_map1 = jnp.where(low_lanes, k_hi, pltpu.roll(k_lo, HEAD_DIM, 1)).astype(F8)
        k_map2 = jnp.where(low_lanes, pltpu.roll(k_hi, wide - HEAD_DIM, 1), k_lo).astype(F8)
        for hh in range(wide // LANES):
            hd = j * (wide // LANES) + hh
            r0 = hd * 2 * LANES
            for a in range(2):
                src = slice(hh * LANES + a * HEAD_DIM, hh * LANES + (a + 1) * HEAD_DIM)
                qt_ref[0, r0 + a * LANES:r0 + a * LANES + HEAD_DIM, :] = qh_t[src].astype(F8)
                qt_ref[0, r0 + a * LANES + HEAD_DIM:r0 + (a + 1) * LANES, :] = ql_t[src].astype(F8)
            k_ref[0, :, r0:r0 + LANES] = k_map1[:, hh * LANES:(hh + 1) * LANES]
            k_ref[0, :, r0 + LANES:r0 + 2 * LANES] = k_map2[:, hh * LANES:(hh + 1) * LANES]
            v0 = hd * DA_VROWS
            vt_ref[0, v0:v0 + DA_V, :] = zv[:, hd * LANES:(hd + 1) * LANES].T.astype(BF16)
            vt_ref[0, v0 + DA_V:v0 + DA_VROWS, :] = ones_row


def _proj_da_call(h, wq, wk, wv, bd, gq, gk, cos_t, sin_t):
    b, tall, d = h.shape
    tm = _row_tile(tall)
    w = DA_HEADS * LANES
    nbd = bd.shape[0]
    const = lambda bi, r: (0, 0)
    return pl.pallas_call(
        _proj_da_body,
        grid=(b, tall // tm),
        in_specs=[pl.BlockSpec((1, tm, d), lambda bi, r: (bi, r, 0)),
                  pl.BlockSpec((d, w), const), pl.BlockSpec((d, w), const), pl.BlockSpec((d, w), const),
                  pl.BlockSpec((nbd, nbd), const),
                  pl.BlockSpec((1, nbd), const), pl.BlockSpec((1, nbd), const),
                  pl.BlockSpec((tm, LANES), lambda bi, r: (r, 0)),
                  pl.BlockSpec((tm, LANES), lambda bi, r: (r, 0))],
        out_specs=[pl.BlockSpec((1, 2 * w, tm), lambda bi, r: (bi, 0, r)),
                   pl.BlockSpec((1, tm, 2 * w), lambda bi, r: (bi, r, 0)),
                   pl.BlockSpec((1, DA_HEADS * DA_VROWS, tm), lambda bi, r: (bi, 0, r))],
        out_shape=[jax.ShapeDtypeStruct((b, 2 * w, tall), F8),
                   jax.ShapeDtypeStruct((b, tall, 2 * w), F8),
                   jax.ShapeDtypeStruct((b, DA_HEADS * DA_VROWS, tall), BF16)],
        compiler_params=_cparams(("parallel", "parallel")),
        name="proj_da",
    )(h, wq, wk, wv, bd, gq, gk, cos_t, sin_t)


def _proj_gla_body(h_ref, wq_ref, wk_ref, wv_ref, wg_ref, wa_ref, a2_ref, ab_ref,
                   q_ref, k_ref, v_ref, sg_ref, la_ref):
    h = h_ref[0]
    q_ref[0] = _dot(h, wq_ref[...]) * (GLA_DK ** -0.5)
    k_ref[0] = _dot(h, wk_ref[...])
    v_ref[0] = _dot(h, wv_ref[...]).astype(BF16)
    g = _dot(h, wg_ref[...])
    sg_ref[0] = (g * _sigmoid(g)).astype(BF16)
    ga = _dot(h, wa_ref[...])
    z = _dot(ga.astype(BF16), a2_ref[...]) + ab_ref[...]
    la_ref[0] = (jnp.minimum(z, 0.0) - jnp.log1p(jnp.exp(-jnp.abs(z)))) * (1.0 / GLA_TAU)


def _proj_gla_call(h, wq, wk, wv, wg, wa, a2p, ab):
    b, tall, d = h.shape
    tm = _row_tile(tall)
    wqk = GLA_HEADS * GLA_DK
    wv_ = GLA_HEADS * GLA_DV
    const = lambda bi, r: (0, 0)
    row = lambda n: pl.BlockSpec((1, tm, n), lambda bi, r: (bi, r, 0))
    return pl.pallas_call(
        _proj_gla_body,
        grid=(b, tall // tm),
        in_specs=[row(d),
                  pl.BlockSpec((d, wqk), const), pl.BlockSpec((d, wqk), const),
                  pl.BlockSpec((d, wv_), const), pl.BlockSpec((d, wv_), const),
                  pl.BlockSpec((d, LANES), const),
                  pl.BlockSpec((LANES, 2 * wqk), const), pl.BlockSpec((1, 2 * wqk), const)],
        out_specs=[row(wqk), row(wqk), row(wv_), row(wv_), row(2 * wqk)],
        out_shape=[jax.ShapeDtypeStruct((b, tall, wqk), F32),
                   jax.ShapeDtypeStruct((b, tall, wqk), F32),
                   jax.ShapeDtypeStruct((b, tall, wv_), BF16),
                   jax.ShapeDtypeStruct((b, tall, wv_), BF16),
                   jax.ShapeDtypeStruct((b, tall, 2 * wqk), F32)],
        compiler_params=_cparams(("parallel", "parallel")),
        name="proj_gla",
    )(h, wq, wk, wv, wg, wa, a2p, ab)


def _proj_na_body(h_ref, wq_ref, wk_ref, wv_ref, bd_ref, gq_ref, gk_ref, q_ref, k_ref, v_ref):
    h = h_ref[0]
    bd = bd_ref[...]
    wide = bd.shape[0]
    zq = _dot(h, wq_ref[...])
    zk = _dot(h, wk_ref[...])
    scale = HEAD_DIM ** -0.5 * LOG2E
    for j in range(NA_HEADS * HEAD_DIM // wide):
        sl = slice(j * wide, (j + 1) * wide)
        q_ref[0, :, sl] = (_segnorm64(zq[:, sl], bd, gq_ref[...]) * scale).astype(BF16)
        k_ref[0, :, sl] = _segnorm64(zk[:, sl], bd, gk_ref[...]).astype(BF16)
    v_ref[0] = _dot(h, wv_ref[...]).astype(BF16)


def _proj_na_call(h, wq, wk, wv, bd, gq, gk):
    b, tall, d = h.shape
    tm = _row_tile(tall)
    w = NA_HEADS * HEAD_DIM
    nbd = bd.shape[0]
    const = lambda bi, r: (0, 0)
    row = lambda n: pl.BlockSpec((1, tm, n), lambda bi, r: (bi, r, 0))
    return pl.pallas_call(
        _proj_na_body,
        grid=(b, tall // tm),
        in_specs=[row(d), pl.BlockSpec((d, w), const), pl.BlockSpec((d, w), const), pl.BlockSpec((d, w), const),
                  pl.BlockSpec((nbd, nbd), const),
                  pl.BlockSpec((1, nbd), const), pl.BlockSpec((1, nbd), const)],
        out_specs=[row(w), row(w), row(w)],
        out_shape=[jax.ShapeDtypeStruct((b, tall, w), BF16)] * 3,
        compiler_params=_cparams(("parallel", "parallel")),
        name="proj_na",
    )(h, wq, wk, wv, bd, gq, gk)


def _proj_gate_body(h_ref, w_ref, o_ref):
    h = h_ref[0]
    n = w_ref.shape[1]
    step = 512
    for j in range(n // step):
        sl = slice(j * step, (j + 1) * step)
        o_ref[0, :, sl] = _sigmoid(_dot(h, w_ref[:, sl])).astype(BF16)


def _proj_gate_call(h, w):
    b, tall, d = h.shape
    tm = _row_tile(tall)
    n = w.shape[1]
    return pl.pallas_call(
        _proj_gate_body,
        grid=(b, tall // tm),
        in_specs=[pl.BlockSpec((1, tm, d), lambda bi, r: (bi, r, 0)),
                  pl.BlockSpec((d, n), lambda bi, r: (0, 0))],
        out_specs=pl.BlockSpec((1, tm, n), lambda bi, r: (bi, r, 0)),
        out_shape=jax.ShapeDtypeStruct((b, tall, n), BF16),
        compiler_params=_cparams(("parallel", "parallel")),
        name="proj_gate",
    )(h, w)


def _da_body(safe_ref, qt_ref, k_ref, vt_ref, lp_ref, sg_ref, o_ref, m_ref, a_ref, g_ref, acc_ref, p_ref, *,
             lam_init, nk):
    j = pl.program_id(3)
    use_lag = safe_ref[0] > 0.0
    tk = k_ref.shape[1]
    pieces = [slice(c, min(c + MXU_DIM, tk)) for c in range(0, tk, MXU_DIM)]

    @pl.when(j == 0)
    def _():
        m_ref[...] = jnp.full(m_ref.shape, jnp.where(use_lag, -safe_ref[0], -jnp.inf), F32)
        a_ref[...] = jnp.ones(a_ref.shape, F32)
        acc_ref[...] = jnp.zeros(acc_ref.shape, F32)

    def q_weights(a):
        q_hi = qt_ref[0, a * LANES:a * LANES + HEAD_DIM, :]
        q_lo = qt_ref[0, a * LANES + HEAD_DIM:(a + 1) * LANES, :]
        return jnp.concatenate([q_hi, q_hi, q_lo, q_lo], axis=0)

    def scores(a, rows, w):
        kb = k_ref[0, rows, a * LANES:(a + 1) * LANES]
        return _dot(jnp.concatenate([kb, kb], axis=1), w)

    def lagged_step(do_exp, do_pv, slot_w, slot_r):
        m_old = [m_ref[a:a + 1, :] for a in range(2)]
        w = [q_weights(a) for a in range(2)] if do_exp else None
        pv = [None, None]
        smax = [None, None]
        for rows in pieces:
            if do_exp:
                for a in range(2):
                    s = scores(a, rows, w[a])
                    p_ref[slot_w, a, rows, :] = jnp.exp2(s - m_old[a]).astype(BF16)
                    mx = jnp.max(s, axis=0, keepdims=True)
                    smax[a] = mx if smax[a] is None else jnp.maximum(smax[a], mx)
            if do_pv:
                for a in range(2):
                    d = _dot(vt_ref[0, :, rows], p_ref[slot_r, a, rows, :])
                    pv[a] = d if pv[a] is None else pv[a] + d
        for a in range(2):
            if do_pv:
                acc_ref[a] = g_ref[slot_r, a:a + 1, :] * acc_ref[a] + pv[a]
            if do_exp:
                m_new = jnp.maximum(m_old[a], smax[a])
                g_ref[slot_w, a:a + 1, :] = a_ref[a:a + 1, :]
                a_ref[a:a + 1, :] = jnp.exp2(m_old[a] - m_new)
                m_ref[a:a + 1, :] = m_new

    def plain_step(do_exp, do_pv, slot_w, slot_r):
        if do_pv:
            vt = vt_ref[0]
            for a in range(2):
                acc_ref[a] = g_ref[slot_r, a:a + 1, :] * acc_ref[a] + _dot(vt, p_ref[slot_r, a])
        if do_exp:
            for a in range(2):
                s = scores(a, slice(None), q_weights(a))
                m_old = m_ref[a:a + 1, :]
                m_new = jnp.maximum(m_old, jnp.max(s, axis=0, keepdims=True))
                p_ref[slot_w, a] = jnp.exp2(s - m_new).astype(BF16)
                g_ref[slot_w, a:a + 1, :] = jnp.exp2(m_old - m_new)
                m_ref[a:a + 1, :] = m_new

    first = j == 0
    last = j == nk
    middle = jnp.logical_and(j > 0, j < nk)
    for flag, step in ((use_lag, lagged_step), (jnp.logical_not(use_lag), plain_step)):
        pl.when(jnp.logical_and(flag, first))(functools.partial(step, True, False, 0, 1))
        for par in range(2):
            on = jnp.logical_and(jnp.logical_and(flag, middle), j % 2 == par)
            pl.when(on)(functools.partial(step, True, True, par, 1 - par))
        pl.when(jnp.logical_and(flag, last))(functools.partial(step, False, True, nk % 2, (nk - 1) % 2))

    @pl.when(last)
    def _():
        lp = lp_ref[...]
        e1 = jnp.exp(jnp.sum(lp[0:1] * lp[1:2], axis=-1, keepdims=True))
        e2 = jnp.exp(jnp.sum(lp[2:3] * lp[3:4], axis=-1, keepdims=True))
        lam = e1 - e2 + lam_init
        o1 = acc_ref[0, 0:DA_V, :] / acc_ref[0, DA_V:DA_V + 1, :]
        o2 = acc_ref[1, 0:DA_V, :] / acc_ref[1, DA_V:DA_V + 1, :]
        o = o1 - lam * o2
        ms = jnp.mean(o * o, axis=0, keepdims=True)
        y = (o * lax.rsqrt(ms + EPS) * sg_ref[...]) * (1.0 - lam_init)
        o_ref[0] = y.T.astype(BF16)


def _da_call(safe, qt, kk, vt, lp, subg, lam_init, *, q_off, nq, k_off, nk, tq, tk):
    b = qt.shape[0]
    return pl.pallas_call(
        functools.partial(_da_body, lam_init=lam_init, nk=nk),
        grid=(b, DA_HEADS, nq, nk + 1),
        in_specs=[pl.BlockSpec(memory_space=pltpu.SMEM),
                  pl.BlockSpec((1, 2 * LANES, tq), lambda bi, h, i, j: (bi, h, i + q_off)),
                  pl.BlockSpec((1, tk, 2 * LANES), lambda bi, h, i, j: (bi, jnp.minimum(j, nk - 1) + k_off, h)),
                  pl.BlockSpec((1, DA_VROWS, tk), lambda bi, h, i, j: (bi, h, jnp.maximum(j - 1, 0) + k_off)),
                  pl.BlockSpec((4, HEAD_DIM), lambda bi, h, i, j: (0, 0)),
                  pl.BlockSpec((LANES, 1), lambda bi, h, i, j: (0, 0))],
        out_specs=pl.BlockSpec((1, tq, LANES), lambda bi, h, i, j: (bi, i, h)),
        out_shape=jax.ShapeDtypeStruct((b, nq * tq, DA_HEADS * LANES), BF16),
        scratch_shapes=[pltpu.VMEM((8, tq), F32), pltpu.VMEM((8, tq), F32), pltpu.VMEM((2, 8, tq), F32),
                        pltpu.VMEM((2, DA_VROWS, tq), F32),
                        pltpu.VMEM((2, 2, tk, tq), BF16)],
        compiler_params=_cparams(("parallel", "parallel", "parallel", "arbitrary")),
        name="diff_attn",
    )(safe, qt, kk, vt, lp, subg)


def _gla_body(q_ref, k_ref, v_ref, la_ref, tri_ref, o_ref, s_ref, *, reverse, tb):
    i = pl.program_id(1)

    @pl.when(i == 0)
    def _():
        s_ref[...] = jnp.zeros(s_ref.shape, F32)

    c = GLA_CHUNK
    w = GLA_HEADS * GLA_DK
    nch = tb // c
    order = list(reversed(range(nch))) if reverse else list(range(nch))
    tri = tri_ref[...]
    keep = tri > 0
    cum = _split_dot_left(tri, la_ref[0])
    tot = [cum[ch * c:ch * c + 1, :] if reverse else cum[(ch + 1) * c - 1:(ch + 1) * c, :] for ch in range(nch)]
    tot_rows = jnp.concatenate([jnp.broadcast_to(t, (c, w)) for t in tot], axis=0)
    q = q_ref[0]
    k = k_ref[0]
    qe = q * jnp.exp(cum)
    ke = (k * jnp.exp(-cum)).astype(BF16)
    kd = (k * jnp.exp(tot_rows - cum)).astype(BF16)
    dec = [jnp.exp(t) for t in tot]
    lane = lax.broadcasted_iota(jnp.int32, (1, w), 1)
    heads = range(GLA_HEADS)
    qh = [jnp.where((lane >= hd * GLA_DK) & (lane < (hd + 1) * GLA_DK), qe, 0.0).astype(BF16) for hd in heads]
    vh = [v_ref[0, :, hd * GLA_DV:(hd + 1) * GLA_DV] for hd in heads]
    inc = [[_dot_tn(vh[hd][ch * c:(ch + 1) * c], kd[ch * c:(ch + 1) * c]) for ch in range(nch)] for hd in heads]
    a = [jnp.where(keep, _dot_nt(qh[hd], ke), 0.0).astype(BF16) for hd in heads]
    entry = []
    for hd in heads:
        st = s_ref[hd]
        seen = {}
        for ch in order:
            seen[ch] = st.astype(BF16)
            st = st * dec[ch] + inc[hd][ch]
        s_ref[hd] = st
        entry.append(seen)
    o_intra = [_dot(a[hd], vh[hd]) for hd in heads]
    for hd in heads:
        for ch in range(nch):
            rows = slice(ch * c, (ch + 1) * c)
            o_ref[0, rows, hd * GLA_DV:(hd + 1) * GLA_DV] = o_intra[hd][rows] + _dot_nt(qh[hd][rows], entry[hd][ch])


def _gla_call(gq, gk, gv, la, tri, *, t_lat, reverse):
    b, tall, w = gq.shape
    tb = GLA_BLOCK
    n_lat = t_lat // tb
    nblk = tall // tb
    if reverse:
        blk = lambda i: jnp.where(i == 0, n_lat, n_lat - i)
    else:
        blk = lambda i: jnp.where(i == 0, n_lat, i - 1)
    wv_ = GLA_HEADS * GLA_DV
    return pl.pallas_call(
        functools.partial(_gla_body, reverse=reverse, tb=tb),
        grid=(b, nblk),
        in_specs=[pl.BlockSpec((1, tb, w), lambda bi, i: (bi, blk(i), 0)),
                  pl.BlockSpec((1, tb, w), lambda bi, i: (bi, blk(i), 0)),
                  pl.BlockSpec((1, tb, wv_), lambda bi, i: (bi, blk(i), 0)),
                  pl.BlockSpec((1, tb, w), lambda bi, i: (bi, blk(i), 1 if reverse else 0)),
                  pl.BlockSpec((tb, tb), lambda bi, i: (0, 0))],
        out_specs=pl.BlockSpec((1, tb, wv_), lambda bi, i: (bi, blk(i), 0)),
        out_shape=jax.ShapeDtypeStruct((b, tall, wv_), F32),
        scratch_shapes=[pltpu.VMEM((GLA_HEADS, GLA_DV, w), F32)],
        compiler_params=_cparams(("parallel", "arbitrary")),
        name="gla_bwd" if reverse else "gla_fwd",
    )(gq, gk, gv, la, tri)


def _na_body(safe_ref, q_ref, k_ref, v_ref, kc_ref, vc_ref, bias_ref, o_ref, *, rows):
    i = pl.program_id(2)
    kb0 = jnp.clip(i * NA_QROWS - NA_KR // 2, 0, rows - NA_BAND)
    start = pl.multiple_of(kb0 * GRID_W, GRID_W)
    nband = NA_BAND * GRID_W

    def attend(fixed_ref):
        q = q_ref[0]
        kb = k_ref[0, pl.ds(start, nband), :]
        kc = kc_ref[0]
        vall = jnp.concatenate([v_ref[0, pl.ds(start, nband), :], vc_ref[0]], axis=0)
        klane = lax.broadcasted_iota(jnp.int32, vall.shape, 1)
        vaug = jnp.concatenate([vall, (klane == 0).astype(BF16)], axis=1)
        lane = lax.broadcasted_iota(jnp.int32, q.shape, 1)
        qh = [jnp.where((lane < HEAD_DIM) if hh == 0 else (lane >= HEAD_DIM), q, jnp.zeros_like(q))
              for hh in range(2)]
        s_loc = [_dot_nt(qh[hh], kb) for hh in range(2)]
        s_ctx = [_dot_nt(qh[hh], kc) for hh in range(2)]
        p = []
        for hh in range(2):
            sl = s_loc[hh] + bias_ref[0, hh]
            if fixed_ref:
                m = safe_ref[0]
            else:
                m = jnp.maximum(jnp.max(sl, axis=-1, keepdims=True), jnp.max(s_ctx[hh], axis=-1, keepdims=True))
            p.append(jnp.concatenate([jnp.exp2(sl - m), jnp.exp2(s_ctx[hh] - m)], axis=1).astype(BF16))
        acc = [_dot(p[hh], vaug) for hh in range(2)]
        outs = [acc[hh][:, 0:LANES] / acc[hh][:, LANES:LANES + 1] for hh in range(2)]
        o_ref[0] = jnp.where(lane < HEAD_DIM, outs[0], outs[1]).astype(BF16)

    fixed = safe_ref[0] > 0.0
    pl.when(fixed)(functools.partial(attend, True))
    pl.when(jnp.logical_not(fixed))(functools.partial(attend, False))


def _na_call(safe, nq, nk, nv, bias, *, t_lat):
    b, tall, w = nq.shape
    tc = tall - t_lat
    rows = t_lat // GRID_W
    tq = NA_QROWS * GRID_W
    nsteps = rows // NA_QROWS
    npair = w // LANES
    ctx_blk = t_lat // tc

    def btype(i):
        return jnp.where(i == 0, 0, jnp.where(i == nsteps - 2, 2, jnp.where(i == nsteps - 1, 3, 1)))

    return pl.pallas_call(
        functools.partial(_na_body, rows=rows),
        grid=(b, npair, nsteps),
        in_specs=[pl.BlockSpec(memory_space=pltpu.SMEM),
                  pl.BlockSpec((1, tq, LANES), lambda bi, hp, i: (bi, i, hp)),
                  pl.BlockSpec((1, t_lat, LANES), lambda bi, hp, i: (bi, 0, hp)),
                  pl.BlockSpec((1, t_lat, LANES), lambda bi, hp, i: (bi, 0, hp)),
                  pl.BlockSpec((1, tc, LANES), lambda bi, hp, i: (bi, ctx_blk, hp)),
                  pl.BlockSpec((1, tc, LANES), lambda bi, hp, i: (bi, ctx_blk, hp)),
                  pl.BlockSpec((1, 2, tq, NA_BAND * GRID_W), lambda bi, hp, i: (btype(i), hp, 0, 0))],
        out_specs=pl.BlockSpec((1, tq, LANES), lambda bi, hp, i: (bi, i, hp)),
        out_shape=jax.ShapeDtypeStruct((b, t_lat, w), BF16),
        compiler_params=_cparams(("parallel", "parallel", "arbitrary")),
        name="nbr_attn",
    )(safe, nq, nk, nv, nk, nv, bias)


def _na_bias_tiles(rpb, rows):
    assert rows >= NA_BAND and rows % NA_QROWS == 0
    nl, nh, na, nb = rpb.shape
    cidx = np.arange(GRID_W)
    rel_c = cidx[None, :] - cidx[:, None] + NA_KC - 1
    sel = jnp.asarray(rel_c[None] == np.arange(nb)[:, None, None], F32)
    toep = jnp.einsum('lhab,bqk->lhqak', rpb.astype(F32) * LOG2E, sel, precision=lax.Precision.HIGHEST)
    toep = jnp.pad(toep, ((0, 0), (0, 0), (0, 0), (NA_BAND, NA_BAND), (0, 0)))
    tiles = []
    for r0 in (0, NA_QROWS, rows - 2 * NA_QROWS, rows - NA_QROWS):
        kb0 = min(max(r0 - NA_KR // 2, 0), rows - NA_BAND)
        qi = np.arange(NA_QROWS * GRID_W)
        qr, qc = r0 + qi // GRID_W, qi % GRID_W
        kj = np.arange(NA_BAND * GRID_W)
        kr, kc = kb0 + kj // GRID_W, kj % GRID_W
        rs = np.clip(qr - NA_KR // 2, 0, rows - NA_KR)
        cs = np.clip(qc - NA_KC // 2, 0, GRID_W - NA_KC)
        valid = ((kr[None, :] >= rs[:, None]) & (kr[None, :] < rs[:, None] + NA_KR)
                 & (kc[None, :] >= cs[:, None]) & (kc[None, :] < cs[:, None] + NA_KC))
        parts = []
        for q_row in range(NA_QROWS):
            a0 = kb0 - (r0 + q_row) + NA_KR - 1 + NA_BAND
            blk = toep[:, :, :, a0:a0 + NA_BAND, :]
            parts.append(blk.reshape(nl, nh, GRID_W, NA_BAND * GRID_W))
        tile = jnp.concatenate(parts, axis=2)
        tiles.append(jnp.where(jnp.asarray(valid)[None, None], tile, NEG_BIG))
    return jnp.stack(tiles, axis=1)


def _ctx_attn_body(q_ref, k_ref, v_ref, o_ref):
    q = q_ref[0]
    k = k_ref[0]
    v = v_ref[0]
    lane = lax.broadcasted_iota(jnp.int32, q.shape, 1)
    outs = []
    for hh in range(2):
        qh = jnp.where((lane < HEAD_DIM) if hh == 0 else (lane >= HEAD_DIM), q, jnp.zeros_like(q))
        s = _dot_nt(qh, k)
        m = jnp.max(s, axis=-1, keepdims=True)
        p = jnp.exp2(s - m)
        outs.append(_dot(p.astype(BF16), v) / jnp.sum(p, axis=-1, keepdims=True))
    o_ref[0] = jnp.where(lane < HEAD_DIM, outs[0], outs[1]).astype(BF16)


def _ctx_attn_call(nq, nk, nv, *, t_lat):
    b, tall, w = nq.shape
    tc = tall - t_lat
    ctx_blk = t_lat // tc
    spec = pl.BlockSpec((1, tc, LANES), lambda bi, hp: (bi, ctx_blk, hp))
    return pl.pallas_call(
        _ctx_attn_body,
        grid=(b, w // LANES),
        in_specs=[spec, spec, spec],
        out_specs=pl.BlockSpec((1, tc, LANES), lambda bi, hp: (bi, 0, hp)),
        out_shape=jax.ShapeDtypeStruct((b, tc, w), BF16),
        compiler_params=_cparams(("parallel", "parallel")),
        name="ctx_attn",
    )(nq, nk, nv)


def _merge_body(x_ref, yd_ref, of_ref, ob_ref, sg_ref, yn_ref, gt_ref, wd_ref, wg_ref, wn_ref, wo_ref,
                gn_ref, mb_ref, mc_ref, o_ref, *, t_lat, tm, d):
    r = pl.program_id(1)
    ctx = _is_ctx_rows(r, tm, t_lat)
    g1 = jnp.where(ctx, mc_ref[0, :, 2 * d:3 * d], mb_ref[0, :, 2 * d:3 * d])
    og = of_ref[0] + ob_ref[0]
    parts = []
    for hd in range(GLA_HEADS):
        oh = og[:, hd * GLA_DV:(hd + 1) * GLA_DV]
        ms = jnp.mean(oh * oh, axis=-1, keepdims=True)
        parts.append(oh * lax.rsqrt(ms + EPS) * gn_ref[...])
    yg = (jnp.concatenate(parts, axis=-1) * sg_ref[0].astype(F32)).astype(BF16)
    m = (gt_ref[0, :, 0:d].astype(F32) * _dot(yd_ref[0], wd_ref[...])
         + gt_ref[0, :, d:2 * d].astype(F32) * _dot(yg, wg_ref[...])
         + gt_ref[0, :, 2 * d:3 * d].astype(F32) * _dot(yn_ref[0], wn_ref[...]))
    o_ref[0] = x_ref[0] + g1 * _dot(m.astype(BF16), wo_ref[...])


def _merge_call(xall, yd, of, ob, sg, yn, gates, wd, wg, wn, wo, gn, mod3, t_lat):
    b, tall, d = xall.shape
    tm = _row_tile(tall)
    nb = b
    const = lambda bi, r: (0, 0)
    row = lambda n: pl.BlockSpec((1, tm, n), lambda bi, r: (bi, r, 0))
    return pl.pallas_call(
        functools.partial(_merge_body, t_lat=t_lat, tm=tm, d=d),
        grid=(b, tall // tm),
        in_specs=[row(d), row(BR_W), row(BR_W), row(BR_W), row(BR_W), row(BR_W), row(3 * d),
                  pl.BlockSpec((BR_W, d), const), pl.BlockSpec((BR_W, d), const), pl.BlockSpec((BR_W, d), const),
                  pl.BlockSpec((d, d), const), pl.BlockSpec((1, GLA_DV), const),
                  pl.BlockSpec((1, 1, 6 * d), lambda bi, r: (bi, 0, 0)),
                  pl.BlockSpec((1, 1, 6 * d), lambda bi, r: (nb, 0, 0))],
        out_specs=row(d),
        out_shape=jax.ShapeDtypeStruct((b, tall, d), F32),
        compiler_params=_cparams(("parallel", "parallel")),
        name="merge",
    )(xall, yd, of, ob, sg, yn, gates, wd, wg, wn, wo, gn, mod3, mod3)


def _mlp_body(x_ref, g_ref, w1_ref, w2_ref, mb_ref, mc_ref, o_ref, *, t_lat, tm, d):
    r = pl.program_id(1)
    ctx = _is_ctx_rows(r, tm, t_lat)
    sh = jnp.where(ctx, mc_ref[0, :, 3 * d:4 * d], mb_ref[0, :, 3 * d:4 * d])
    sc = jnp.where(ctx, mc_ref[0, :, 4 * d:5 * d], mb_ref[0, :, 4 * d:5 * d])
    g2 = jnp.where(ctx, mc_ref[0, :, 5 * d:6 * d], mb_ref[0, :, 5 * d:6 * d])
    x = x_ref[0]
    h = _norm_mod(x, g_ref[...], sc, sh).astype(BF16)
    acc = jnp.zeros((tm, d), F32)
    step = 1024
    for j in range(w1_ref.shape[1] // step):
        a = jnp.maximum(_dot(h, w1_ref[:, j * step:(j + 1) * step]), 0.0)
        acc = acc + _dot((a * a).astype(BF16), w2_ref[j * step:(j + 1) * step, :])
    o_ref[0] = x + g2 * acc


def _mlp_call(xall, g, w1, w2, mod3, t_lat, *, lat_only):
    b, tall, d = xall.shape
    tm = next(t for t in (512, 256) if t_lat % t == 0) if lat_only else _row_tile(tall)
    n_rows = t_lat if lat_only else tall
    nb = b
    dff = w1.shape[1]
    const = lambda bi, r: (0, 0)
    return pl.pallas_call(
        functools.partial(_mlp_body, t_lat=t_lat, tm=tm, d=d),
        grid=(b, n_rows // tm),
        in_specs=[pl.BlockSpec((1, tm, d), lambda bi, r: (bi, r, 0)),
                  pl.BlockSpec((1, d), const),
                  pl.BlockSpec((d, dff), const, pipeline_mode=pl.Buffered(1)),
                  pl.BlockSpec((dff, d), const, pipeline_mode=pl.Buffered(1)),
                  pl.BlockSpec((1, 1, 6 * d), lambda bi, r: (bi, 0, 0)),
                  pl.BlockSpec((1, 1, 6 * d), lambda bi, r: (nb, 0, 0))],
        out_specs=pl.BlockSpec((1, tm, d), lambda bi, r: (bi, r, 0)),
        out_shape=jax.ShapeDtypeStruct((b, n_rows, d), F32),
        compiler_params=_cparams(("parallel", "parallel")),
        name="mlp",
    )(xall, g.reshape(1, d), w1, w2, mod3, mod3)


def kernel(x, c, ctx, c_ctx, w_mod, b_mod, norm1_g, norm2_g, w_in, da_qn_g, da_kn_g, da_lambda, da_subln_g,
           gla_a2, gla_a_b, gla_gn_g, na_qn_g, na_kn_g, na_rpb, w_br_da, w_br_gla, w_br_na, w_out, w_ff1, w_ff2):
    b, t_lat, d = x.shape
    tc = ctx.shape[1]
    tall = t_lat + tc
    depth = w_mod.shape[0]
    rows = t_lat // GRID_W
    assert d == D_MODEL and t_lat % (2 * tc) == 0 and tc == 256 and b < 8

    xall = jnp.concatenate([x, ctx], axis=1)
    cvec = jnp.zeros((8, d), F32).at[0:b].set(c).at[b].set(c_ctx)
    mod = _mod_call(cvec, w_mod, b_mod)
    cos_t, sin_t = _rope_tables(t_lat, tall)

    seg = np.arange(NORM_LANES) // HEAD_DIM
    bd = jnp.asarray(seg[:, None] == seg[None, :], BF16)
    ci = np.arange(GLA_BLOCK)
    same_chunk = (ci[None, :] // GLA_CHUNK) == (ci[:, None] // GLA_CHUNK)
    tri_f = jnp.asarray(same_chunk & (ci[None, :] <= ci[:, None]), BF16)
    tri_b = jnp.asarray(same_chunk & (ci[None, :] >= ci[:, None]), BF16)

    o_dq, o_dk, o_dv = 0, 512, 1024
    o_gq, o_gk, o_gv, o_gg, o_ga = 1536, 1792, 2048, 2560, 3072
    o_nq, o_nk, o_nv = 3104, 3616, 4128
    o_gate = 4640
    tq_da = 512 if t_lat % 512 == 0 else 256
    tk_da = next(t for t in (3328, 1280, 256) if tall % t == 0)
    na_bias = _na_bias_tiles(na_rpb, rows)

    for l in range(depth):
        need_ctx = l < depth - 1
        lam_init = 0.8 - 0.6 * math.exp(-0.3 * l)
        mod3 = mod[l].reshape(8, 1, 6 * d)
        wl = w_in[l].astype(BF16)
        cut = lambda a, n: wl[:, a:a + n]
        wa = jnp.concatenate([cut(o_ga, 2 * GLA_RANK), jnp.zeros((d, LANES - 2 * GLA_RANK), BF16)], axis=1)
        a2p = jnp.zeros((LANES, 2 * GLA_HEADS * GLA_DK), F32)
        a2p = a2p.at[0:GLA_RANK, 0:256].set(gla_a2[l, 0]).at[GLA_RANK:2 * GLA_RANK, 256:512].set(gla_a2[l, 1])
        ab = gla_a_b[l].reshape(1, 2 * GLA_HEADS * GLA_DK)
        tile2 = lambda g: jnp.tile(g, NORM_LANES // HEAD_DIM).reshape(1, NORM_LANES)

        h = _hnorm_call(xall, norm1_g[l], mod3, t_lat)
        qt, kk, vt = _proj_da_call(h, cut(o_dq, 512), cut(o_dk, 512), cut(o_dv, 512), bd,
                                   tile2(da_qn_g[l]), tile2(da_kn_g[l]), cos_t, sin_t)
        gq, gk, gv, sgg, la = _proj_gla_call(h, cut(o_gq, 256), cut(o_gk, 256), cut(o_gv, 512), cut(o_gg, 512),
                                             wa, a2p.astype(BF16), ab)
        nq, nk, nv = _proj_na_call(h, cut(o_nq, 512), cut(o_nk, 512), cut(o_nv, 512), bd,
                                   tile2(na_qn_g[l]), tile2(na_kn_g[l]))
        def score_bound(gq_, gk_, extra):
            bnd = (math.sqrt(HEAD_DIM) * LOG2E * 1.01) * jnp.max(jnp.abs(gq_)) * jnp.max(jnp.abs(gk_)) + extra + 1e-3
            return jnp.where(bnd <= DA_SAFE_LOG2, bnd, -1.0).astype(F32).reshape(1)
        safe = score_bound(da_qn_g[l], da_kn_g[l], 0.0)
        safe_na = score_bound(na_qn_g[l], na_kn_g[l], jnp.max(jnp.abs(na_rpb[l])) * LOG2E)
        gates = _proj_gate_call(h, cut(o_gate, 3 * d))

        subg = da_subln_g[l].reshape(LANES, 1)
        y_da = _da_call(safe, qt, kk, vt, da_lambda[l], subg, lam_init,
                        q_off=0, nq=t_lat // tq_da, k_off=0, nk=tall // tk_da, tq=tq_da, tk=tk_da)
        o_f = _gla_call(gq, gk, gv, la, tri_f, t_lat=t_lat, reverse=False)
        o_b = _gla_call(gq, gk, gv, la, tri_b, t_lat=t_lat, reverse=True)
        y_na = _na_call(safe_na, nq, nk, nv, na_bias[l], t_lat=t_lat)
        if need_ctx:
            y_da_c = _da_call(safe, qt, kk, vt, da_lambda[l], subg, lam_init,
                              q_off=t_lat // tc, nq=1, k_off=t_lat // tc, nk=1, tq=tc, tk=tc)
            y_na_c = _ctx_attn_call(nq, nk, nv, t_lat=t_lat)
        else:
            y_da_c = jnp.zeros((b, tc, BR_W), BF16)
            y_na_c = jnp.zeros((b, tc, BR_W), BF16)
        y_da = jnp.concatenate([y_da, y_da_c], axis=1)
        y_na = jnp.concatenate([y_na, y_na_c], axis=1)

        xall = _merge_call(xall, y_da, o_f, o_b, sgg, y_na, gates,
                           w_br_da[l].astype(BF16), w_br_gla[l].astype(BF16), w_br_na[l].astype(BF16),
                           w_out[l].astype(BF16), gla_gn_g[l].reshape(1, GLA_DV), mod3, t_lat)
        xall = _mlp_call(xall, norm2_g[l], w_ff1[l].astype(BF16), w_ff2[l].astype(BF16), mod3, t_lat,
                         lat_only=not need_ctx)
    return xall
```

```python
import functools
import math

import numpy as np
import jax
import jax.numpy as jnp
from jax import lax
from jax.experimental import pallas as pl
from jax.experimental.pallas import tpu as pltpu

F32 = jnp.float32
BF16 = jnp.bfloat16
F8 = jnp.float8_e4m3fn

D_MODEL = 1024
GRID_W = 64
HEAD_DIM = 64
EPS = 1e-6
ROPE_BASE = 10000.0
ROPE_PAIRS = HEAD_DIM // 4
DA_HEADS = 4
GLA_HEADS = 4
GLA_DK = 64
GLA_DV = 128
GLA_RANK = 16
GLA_TAU = 16.0
GLA_CHUNK = 64
NA_HEADS = 8
NA_KR = 8
NA_KC = 16
BR_W = 512
D_FF = 4 * D_MODEL

DA_V = 2 * HEAD_DIM
DA_VPAD = 16
DA_VROWS = DA_V + DA_VPAD
LOG2E = math.log2(math.e)
DA_SAFE_LOG2 = 45.0

GLA_BLOCK = 256
NORM_LANES = 256
LANES = 128
NA_QROWS = 8
NA_BAND = 16
NEG_BIG = -1e30
VMEM_LIMIT = 48 * 1024 * 1024


def _cparams(sem):
    return pltpu.CompilerParams(dimension_semantics=sem, vmem_limit_bytes=VMEM_LIMIT)


def _dot(a, b):
    return jnp.dot(a, b, preferred_element_type=F32)


def _dot_nt(a, b):
    return lax.dot_general(a, b, (((1,), (1,)), ((), ())), preferred_element_type=F32)


def _dot_tn(a, b):
    return lax.dot_general(a, b, (((0,), (0,)), ((), ())), preferred_element_type=F32)


def _split_dot(x, w_exact):
    hi = x.astype(BF16)
    lo = (x - hi.astype(F32)).astype(BF16)
    return _dot(hi, w_exact) + _dot(lo, w_exact)


def _split_dot_left(w_exact, x):
    hi = x.astype(BF16)
    lo = (x - hi.astype(F32)).astype(BF16)
    return _dot(w_exact, hi) + _dot(w_exact, lo)


def _sigmoid(x):
    return 1.0 / (1.0 + jnp.exp(-x))


def _row_tile(tall):
    for tm in (640, 256):
        if tall % tm == 0:
            return tm
    raise ValueError(f"unsupported token count {tall}")


def _mod_body(c_ref, w_ref, b_ref, o_ref):
    cv = c_ref[...]
    s = cv * _sigmoid(cv)
    o_ref[0] = _dot(s.astype(BF16), w_ref[0].astype(BF16)) + b_ref[0]


def _mod_call(cvec, w_mod, b_mod):
    depth, d, n = w_mod.shape
    tn = 1536
    return pl.pallas_call(
        _mod_body,
        grid=(depth, n // tn),
        in_specs=[pl.BlockSpec((8, d), lambda l, j: (0, 0)),
                  pl.BlockSpec((1, d, tn), lambda l, j: (l, 0, j)),
                  pl.BlockSpec((1, 1, tn), lambda l, j: (l, 0, j))],
        out_specs=pl.BlockSpec((1, 8, tn), lambda l, j: (l, 0, j)),
        out_shape=jax.ShapeDtypeStruct((depth, 8, n), F32),
        compiler_params=_cparams(("parallel", "parallel")),
        name="mod",
    )(cvec, w_mod, b_mod.reshape(depth, 1, n))


def _rope_body(freq_ref, cos_ref, sin_ref, *, t_lat, tm):
    i = pl.program_id(0)
    t = i * tm + lax.broadcasted_iota(jnp.int32, (tm, LANES), 0)
    lane = lax.broadcasted_iota(jnp.int32, (tm, LANES), 1)
    shift = int(math.log2(GRID_W))
    row = lax.shift_right_logical(t, shift).astype(F32)
    col = jnp.bitwise_and(t, GRID_W - 1).astype(F32)
    use_row = jnp.bitwise_and(lane, 2 * ROPE_PAIRS) == 0
    first_half = jnp.bitwise_and(lane, ROPE_PAIRS) == 0
    ang = jnp.where(use_row, row, col) * freq_ref[...]
    is_lat = t < t_lat
    cos_ref[...] = jnp.where(is_lat, jnp.cos(ang), 1.0)
    sn = jnp.sin(ang)
    sin_ref[...] = jnp.where(is_lat, jnp.where(first_half, -sn, sn), 0.0)


def _rope_tables(t_lat, tall):
    tm = 256
    freqs = ROPE_BASE ** (-jnp.arange(ROPE_PAIRS, dtype=F32) / ROPE_PAIRS)
    freq_lane = jnp.tile(freqs, LANES // ROPE_PAIRS).reshape(1, LANES)
    return pl.pallas_call(
        functools.partial(_rope_body, t_lat=t_lat, tm=tm),
        grid=(tall // tm,),
        in_specs=[pl.BlockSpec((1, LANES), lambda i: (0, 0))],
        out_specs=[pl.BlockSpec((tm, LANES), lambda i: (i, 0))] * 2,
        out_shape=[jax.ShapeDtypeStruct((tall, LANES), F32)] * 2,
        compiler_params=_cparams(("parallel",)),
        name="rope_tables",
    )(freq_lane)


def _is_ctx_rows(r, tm, t_lat):
    t = r * tm + lax.broadcasted_iota(jnp.int32, (tm, 1), 0)
    return t >= t_lat


def _norm_mod(x, g, sc, sh):
    ms = jnp.mean(x * x, axis=-1, keepdims=True)
    return (x * lax.rsqrt(ms + EPS) * g) * (1.0 + sc) + sh


def _segnorm64(z, bd, gain):
    ss = _split_dot(z * z, bd)
    return z * lax.rsqrt(ss * (1.0 / HEAD_DIM) + EPS) * gain


def _hnorm_body(x_ref, g_ref, mb_ref, mc_ref, h_ref, *, t_lat, tm, d):
    r = pl.program_id(1)
    ctx = _is_ctx_rows(r, tm, t_lat)
    sh = jnp.where(ctx, mc_ref[0, :, 0:d], mb_ref[0, :, 0:d])
    sc = jnp.where(ctx, mc_ref[0, :, d:2 * d], mb_ref[0, :, d:2 * d])
    h_ref[0] = _norm_mod(x_ref[0], g_ref[...], sc, sh).astype(BF16)


def _hnorm_call(xall, g, mod3, t_lat):
    b, tall, d = xall.shape
    tm = _row_tile(tall)
    nb = b
    return pl.pallas_call(
        functools.partial(_hnorm_body, t_lat=t_lat, tm=tm, d=d),
        grid=(b, tall // tm),
        in_specs=[pl.BlockSpec((1, tm, d), lambda bi, r: (bi, r, 0)),
                  pl.BlockSpec((1, d), lambda bi, r: (0, 0)),
                  pl.BlockSpec((1, 1, 6 * d), lambda bi, r: (bi, 0, 0)),
                  pl.BlockSpec((1, 1, 6 * d), lambda bi, r: (nb, 0, 0))],
        out_specs=pl.BlockSpec((1, tm, d), lambda bi, r: (bi, r, 0)),
        out_shape=jax.ShapeDtypeStruct((b, tall, d), BF16),
        compiler_params=_cparams(("parallel", "parallel")),
        name="hnorm",
    )(xall, g.reshape(1, d), mod3, mod3)


def _rope128(x, cs, sn, first_half):
    partner = jnp.where(first_half, pltpu.roll(x, x.shape[1] - ROPE_PAIRS, 1), pltpu.roll(x, ROPE_PAIRS, 1))
    return x * cs + partner * sn


def _proj_da_body(h_ref, wq_ref, wk_ref, wv_ref, bd_ref, gq_ref, gk_ref, cos_ref, sin_ref,
                  qt_ref, k_ref, vt_ref):
    h = h_ref[0]
    tm = h.shape[0]
    bd = bd_ref[...]
    wide = bd.shape[0]
    cs = jnp.concatenate([cos_ref[...]] * (wide // LANES), axis=1)
    sn = jnp.concatenate([sin_ref[...]] * (wide // LANES), axis=1)
    lane = lax.broadcasted_iota(jnp.int32, cs.shape, 1)
    first_half = jnp.bitwise_and(lane, ROPE_PAIRS) == 0
    low_lanes = jnp.bitwise_and(lane, HEAD_DIM) == 0
    zq = _dot(h, wq_ref[...])
    zk = _dot(h, wk_ref[...])
    zv = _dot(h, wv_ref[...])
    scale = math.sqrt(HEAD_DIM ** -0.5 * LOG2E)
    ones_row = (lax.broadcasted_iota(jnp.int32, (DA_VPAD, tm), 0) == 0).astype(BF16)
    for j in range(zq.shape[1] // wide):
        sl = slice(j * wide, (j + 1) * wide)
        q = _rope128(_segnorm64(zq[:, sl], bd, gq_ref[...]) * scale, cs, sn, first_half)
        k = _rope128(_segnorm64(zk[:, sl], bd, gk_ref[...]) * scale, cs, sn, first_half)
        q_t = q.T
        qh_t = q_t.astype(F8).astype(F32)
        ql_t = q_t - qh_t
        k_hi = k.astype(F8).astype(F32)
        k_lo = k - k_hi
        k_map1 = jnp.where(low_lanes, k_hi, pltpu.roll(k_lo, HEAD_DIM, 1)).astype(F8)
        k_map2 = jnp.where(low_lanes, pltpu.roll(k_hi, wide - HEAD_DIM, 1), k_lo).astype(F8)
        for hh in range(wide // LANES):
            hd = j * (wide // LANES) + hh
            r0 = hd * 2 * LANES
            for a in range(2):
                src = slice(hh * LANES + a * HEAD_DIM, hh * LANES + (a + 1) * HEAD_DIM)
                qt_ref[0, r0 + a * LANES:r0 + a * LANES + HEAD_DIM, :] = qh_t[src].astype(F8)
                qt_ref[0, r0 + a * LANES + HEAD_DIM:r0 + (a + 1) * LANES, :] = ql_t[src].astype(F8)
            k_ref[0, :, r0:r0 + LANES] = k_map1[:, hh * LANES:(hh + 1) * LANES]
            k_ref[0, :, r0 + LANES:r0 + 2 * LANES] = k_map2[:, hh * LANES:(hh + 1) * LANES]
            v0 = hd * DA_VROWS
            vt_ref[0, v0:v0 + DA_V, :] = zv[:, hd * LANES:(hd + 1) * LANES].T.astype(BF16)
            vt_ref[0, v0 + DA_V:v0 + DA_VROWS, :] = ones_row


def _proj_da_call(h, wq, wk, wv, bd, gq, gk, cos_t, sin_t):
    b, tall, d = h.shape
    tm = _row_tile(tall)
    w = DA_HEADS * LANES
    nbd = bd.shape[0]
    const = lambda bi, r: (0, 0)
    return pl.pallas_call(
        _proj_da_body,
        grid=(b, tall // tm),
        in_specs=[pl.BlockSpec((1, tm, d), lambda bi, r: (bi, r, 0)),
                  pl.BlockSpec((d, w), const), pl.BlockSpec((d, w), const), pl.BlockSpec((d, w), const),
                  pl.BlockSpec((nbd, nbd), const),
                  pl.BlockSpec((1, nbd), const), pl.BlockSpec((1, nbd), const),
                  pl.BlockSpec((tm, LANES), lambda bi, r: (r, 0)),
                  pl.BlockSpec((tm, LANES), lambda bi, r: (r, 0))],
        out_specs=[pl.BlockSpec((1, 2 * w, tm), lambda bi, r: (bi, 0, r)),
                   pl.BlockSpec((1, tm, 2 * w), lambda bi, r: (bi, r, 0)),
                   pl.BlockSpec((1, DA_HEADS * DA_VROWS, tm), lambda bi, r: (bi, 0, r))],
        out_shape=[jax.ShapeDtypeStruct((b, 2 * w, tall), F8),
                   jax.ShapeDtypeStruct((b, tall, 2 * w), F8),
                   jax.ShapeDtypeStruct((b, DA_HEADS * DA_VROWS, tall), BF16)],
        compiler_params=_cparams(("parallel", "parallel")),
        name="proj_da",
    )(h, wq, wk, wv, bd, gq, gk, cos_t, sin_t)


def _proj_gla_body(h_ref, wq_ref, wk_ref, wv_ref, wg_ref, wa_ref, a2_ref, ab_ref,
                   q_ref, k_ref, v_ref, sg_ref, la_ref):
    h = h_ref[0]
    q_ref[0] = _dot(h, wq_ref[...]) * (GLA_DK ** -0.5)
    k_ref[0] = _dot(h, wk_ref[...])
    v_ref[0] = _dot(h, wv_ref[...]).astype(BF16)
    g = _dot(h, wg_ref[...])
    sg_ref[0] = (g * _sigmoid(g)).astype(BF16)
    ga = _dot(h, wa_ref[...])
    z = _dot(ga.astype(BF16), a2_ref[...]) + ab_ref[...]
    la_ref[0] = (jnp.minimum(z, 0.0) - jnp.log1p(jnp.exp(-jnp.abs(z)))) * (1.0 / GLA_TAU)


def _proj_gla_call(h, wq, wk, wv, wg, wa, a2p, ab):
    b, tall, d = h.shape
    tm = _row_tile(tall)
    wqk = GLA_HEADS * GLA_DK
    wv_ = GLA_HEADS * GLA_DV
    const = lambda bi, r: (0, 0)
    row = lambda n: pl.BlockSpec((1, tm, n), lambda bi, r: (bi, r, 0))
    return pl.pallas_call(
        _proj_gla_body,
        grid=(b, tall // tm),
        in_specs=[row(d),
                  pl.BlockSpec((d, wqk), const), pl.BlockSpec((d, wqk), const),
                  pl.BlockSpec((d, wv_), const), pl.BlockSpec((d, wv_), const),
                  pl.BlockSpec((d, LANES), const),
                  pl.BlockSpec((LANES, 2 * wqk), const), pl.BlockSpec((1, 2 * wqk), const)],
        out_specs=[row(wqk), row(wqk), row(wv_), row(wv_), row(2 * wqk)],
        out_shape=[jax.ShapeDtypeStruct((b, tall, wqk), F32),
                   jax.ShapeDtypeStruct((b, tall, wqk), F32),
                   jax.ShapeDtypeStruct((b, tall, wv_), BF16),
                   jax.ShapeDtypeStruct((b, tall, wv_), BF16),
                   jax.ShapeDtypeStruct((b, tall, 2 * wqk), F32)],
        compiler_params=_cparams(("parallel", "parallel")),
        name="proj_gla",
    )(h, wq, wk, wv, wg, wa, a2p, ab)


def _proj_na_body(h_ref, wq_ref, wk_ref, wv_ref, bd_ref, gq_ref, gk_ref, q_ref, k_ref, v_ref):
    h = h_ref[0]
    bd = bd_ref[...]
    wide = bd.shape[0]
    zq = _dot(h, wq_ref[...])
    zk = _dot(h, wk_ref[...])
    scale = HEAD_DIM ** -0.5 * LOG2E
    for j in range(NA_HEADS * HEAD_DIM // wide):
        sl = slice(j * wide, (j + 1) * wide)
        q_ref[0, :, sl] = (_segnorm64(zq[:, sl], bd, gq_ref[...]) * scale).astype(BF16)
        k_ref[0, :, sl] = _segnorm64(zk[:, sl], bd, gk_ref[...]).astype(BF16)
    v_ref[0] = _dot(h, wv_ref[...]).astype(BF16)


def _proj_na_call(h, wq, wk, wv, bd, gq, gk):
    b, tall, d = h.shape
    tm = _row_tile(tall)
    w = NA_HEADS * HEAD_DIM
    nbd = bd.shape[0]
    const = lambda bi, r: (0, 0)
    row = lambda n: pl.BlockSpec((1, tm, n), lambda bi, r: (bi, r, 0))
    return pl.pallas_call(
        _proj_na_body,
        grid=(b, tall // tm),
        in_specs=[row(d), pl.BlockSpec((d, w), const), pl.BlockSpec((d, w), const), pl.BlockSpec((d, w), const),
                  pl.BlockSpec((nbd, nbd), const),
                  pl.BlockSpec((1, nbd), const), pl.BlockSpec((1, nbd), const)],
        out_specs=[row(w), row(w), row(w)],
        out_shape=[jax.ShapeDtypeStruct((b, tall, w), BF16)] * 3,
        compiler_params=_cparams(("parallel", "parallel")),
        name="proj_na",
    )(h, wq, wk, wv, bd, gq, gk)


def _proj_gate_body(h_ref, w_ref, o_ref):
    h = h_ref[0]
    n = w_ref.shape[1]
    step = 512
    for j in range(n // step):
        sl = slice(j * step, (j + 1) * step)
        o_ref[0, :, sl] = _sigmoid(_dot(h, w_ref[:, sl])).astype(BF16)


def _proj_gate_call(h, w):
    b, tall, d = h.shape
    tm = _row_tile(tall)
    n = w.shape[1]
    return pl.pallas_call(
        _proj_gate_body,
        grid=(b, tall // tm),
        in_specs=[pl.BlockSpec((1, tm, d), lambda bi, r: (bi, r, 0)),
                  pl.BlockSpec((d, n), lambda bi, r: (0, 0))],
        out_specs=pl.BlockSpec((1, tm, n), lambda bi, r: (bi, r, 0)),
        out_shape=jax.ShapeDtypeStruct((b, tall, n), BF16),
        compiler_params=_cparams(("parallel", "parallel")),
        name="proj_gate",
    )(h, w)


def _da_body(safe_ref, qt_ref, k_ref, vt_ref, lp_ref, sg_ref, o_ref, m_ref, acc_ref, *, lam_init, nk):
    j = pl.program_id(3)
    bounded = safe_ref[0] > 0.0

    @pl.when(j == 0)
    def _():
        m_ref[...] = jnp.full(m_ref.shape, -jnp.inf, F32)
        acc_ref[...] = jnp.zeros(acc_ref.shape, F32)

    def scores(a):
        kb = k_ref[0, :, a * LANES:(a + 1) * LANES]
        q_hi = qt_ref[0, a * LANES:a * LANES + HEAD_DIM, :]
        q_lo = qt_ref[0, a * LANES + HEAD_DIM:(a + 1) * LANES, :]
        w = jnp.concatenate([q_hi, q_hi, q_lo, q_lo], axis=0)
        return _dot(jnp.concatenate([kb, kb], axis=1), w)

    def bounded_step():
        vt = vt_ref[0]
        for a in range(2):
            p = jnp.exp2(scores(a) - safe_ref[0])
            acc_ref[a] += _dot(vt, p.astype(BF16))

    def plain_step():
        vt = vt_ref[0]
        for a in range(2):
            s = scores(a)
            m_old = m_ref[a:a + 1, :]
            m_new = jnp.maximum(m_old, jnp.max(s, axis=0, keepdims=True))
            p = jnp.exp2(s - m_new)
            acc_ref[a] = jnp.exp2(m_old - m_new) * acc_ref[a] + _dot(vt, p.astype(BF16))
            m_ref[a:a + 1, :] = m_new

    pl.when(bounded)(bounded_step)
    pl.when(jnp.logical_not(bounded))(plain_step)

    @pl.when(j == nk - 1)
    def _():
        lp = lp_ref[...]
        e1 = jnp.exp(jnp.sum(lp[0:1] * lp[1:2], axis=-1, keepdims=True))
        e2 = jnp.exp(jnp.sum(lp[2:3] * lp[3:4], axis=-1, keepdims=True))
        lam = e1 - e2 + lam_init
        o1 = acc_ref[0, 0:DA_V, :] / acc_ref[0, DA_V:DA_V + 1, :]
        o2 = acc_ref[1, 0:DA_V, :] / acc_ref[1, DA_V:DA_V + 1, :]
        o = o1 - lam * o2
        ms = jnp.mean(o * o, axis=0, keepdims=True)
        y = (o * lax.rsqrt(ms + EPS) * sg_ref[...]) * (1.0 - lam_init)
        o_ref[0] = y.T.astype(BF16)


def _da_call(safe, qt, kk, vt, lp, subg, lam_init, *, q_off, nq, k_off, nk, tq, tk):
    b = qt.shape[0]
    return pl.pallas_call(
        functools.partial(_da_body, lam_init=lam_init, nk=nk),
        grid=(b, DA_HEADS, nq, nk),
        in_specs=[pl.BlockSpec(memory_space=pltpu.SMEM),
                  pl.BlockSpec((1, 2 * LANES, tq), lambda bi, h, i, j: (bi, h, i + q_off)),
                  pl.BlockSpec((1, tk, 2 * LANES), lambda bi, h, i, j: (bi, j + k_off, h)),
                  pl.BlockSpec((1, DA_VROWS, tk), lambda bi, h, i, j: (bi, h, j + k_off)),
                  pl.BlockSpec((4, HEAD_DIM), lambda bi, h, i, j: (0, 0)),
                  pl.BlockSpec((LANES, 1), lambda bi, h, i, j: (0, 0))],
        out_specs=pl.BlockSpec((1, tq, LANES), lambda bi, h, i, j: (bi, i, h)),
        out_shape=jax.ShapeDtypeStruct((b, nq * tq, DA_HEADS * LANES), BF16),
        scratch_shapes=[pltpu.VMEM((8, tq), F32), pltpu.VMEM((2, DA_VROWS, tq), F32)],
        compiler_params=_cparams(("parallel", "parallel", "parallel", "arbitrary")),
        name="diff_attn",
    )(safe, qt, kk, vt, lp, subg)


def _gla_body(q_ref, k_ref, v_ref, la_ref, tri_ref, o_ref, s_ref, *, reverse, tb):
    i = pl.program_id(1)

    @pl.when(i == 0)
    def _():
        s_ref[...] = jnp.zeros(s_ref.shape, F32)

    c = GLA_CHUNK
    w = GLA_HEADS * GLA_DK
    nch = tb // c
    order = list(reversed(range(nch))) if reverse else list(range(nch))
    tri = tri_ref[...]
    keep = tri > 0
    cum = _split_dot_left(tri, la_ref[0])
    tot = [cum[ch * c:ch * c + 1, :] if reverse else cum[(ch + 1) * c - 1:(ch + 1) * c, :] for ch in range(nch)]
    tot_rows = jnp.concatenate([jnp.broadcast_to(t, (c, w)) for t in tot], axis=0)
    q = q_ref[0]
    k = k_ref[0]
    qe = q * jnp.exp(cum)
    ke = (k * jnp.exp(-cum)).astype(BF16)
    kd = (k * jnp.exp(tot_rows - cum)).astype(BF16)
    dec = [jnp.exp(t) for t in tot]
    lane = lax.broadcasted_iota(jnp.int32, (1, w), 1)
    heads = range(GLA_HEADS)
    qh = [jnp.where((lane >= hd * GLA_DK) & (lane < (hd + 1) * GLA_DK), qe, 0.0).astype(BF16) for hd in heads]
    vh = [v_ref[0, :, hd * GLA_DV:(hd + 1) * GLA_DV] for hd in heads]
    inc = [[_dot_tn(vh[hd][ch * c:(ch + 1) * c], kd[ch * c:(ch + 1) * c]) for ch in range(nch)] for hd in heads]
    a = [jnp.where(keep, _dot_nt(qh[hd], ke), 0.0).astype(BF16) for hd in heads]
    entry = []
    for hd in heads:
        st = s_ref[hd]
        seen = {}
        for ch in order:
            seen[ch] = st.astype(BF16)
            st = st * dec[ch] + inc[hd][ch]
        s_ref[hd] = st
        entry.append(seen)
    o_intra = [_dot(a[hd], vh[hd]) for hd in heads]
    for hd in heads:
        for ch in range(nch):
            rows = slice(ch * c, (ch + 1) * c)
            o_ref[0, rows, hd * GLA_DV:(hd + 1) * GLA_DV] = o_intra[hd][rows] + _dot_nt(qh[hd][rows], entry[hd][ch])


def _gla_call(gq, gk, gv, la, tri, *, t_lat, reverse):
    b, tall, w = gq.shape
    tb = GLA_BLOCK
    n_lat = t_lat // tb
    nblk = tall // tb
    if reverse:
        blk = lambda i: jnp.where(i == 0, n_lat, n_lat - i)
    else:
        blk = lambda i: jnp.where(i == 0, n_lat, i - 1)
    wv_ = GLA_HEADS * GLA_DV
    return pl.pallas_call(
        functools.partial(_gla_body, reverse=reverse, tb=tb),
        grid=(b, nblk),
        in_specs=[pl.BlockSpec((1, tb, w), lambda bi, i: (bi, blk(i), 0)),
                  pl.BlockSpec((1, tb, w), lambda bi, i: (bi, blk(i), 0)),
                  pl.BlockSpec((1, tb, wv_), lambda bi, i: (bi, blk(i), 0)),
                  pl.BlockSpec((1, tb, w), lambda bi, i: (bi, blk(i), 1 if reverse else 0)),
                  pl.BlockSpec((tb, tb), lambda bi, i: (0, 0))],
        out_specs=pl.BlockSpec((1, tb, wv_), lambda bi, i: (bi, blk(i), 0)),
        out_shape=jax.ShapeDtypeStruct((b, tall, wv_), F32),
        scratch_shapes=[pltpu.VMEM((GLA_HEADS, GLA_DV, w), F32)],
        compiler_params=_cparams(("parallel", "arbitrary")),
        name="gla_bwd" if reverse else "gla_fwd",
    )(gq, gk, gv, la, tri)


def _na_body(safe_ref, q_ref, k_ref, v_ref, kc_ref, vc_ref, bias_ref, o_ref, *, rows):
    i = pl.program_id(2)
    kb0 = jnp.clip(i * NA_QROWS - NA_KR // 2, 0, rows - NA_BAND)
    start = pl.multiple_of(kb0 * GRID_W, GRID_W)
    nband = NA_BAND * GRID_W

    def attend(fixed_ref):
        q = q_ref[0]
        kb = k_ref[0, pl.ds(start, nband), :]
        kc = kc_ref[0]
        vall = jnp.concatenate([v_ref[0, pl.ds(start, nband), :], vc_ref[0]], axis=0)
        klane = lax.broadcasted_iota(jnp.int32, vall.shape, 1)
        vaug = jnp.concatenate([vall, (klane == 0).astype(BF16)], axis=1)
        lane = lax.broadcasted_iota(jnp.int32, q.shape, 1)
        qh = [jnp.where((lane < HEAD_DIM) if hh == 0 else (lane >= HEAD_DIM), q, jnp.zeros_like(q))
              for hh in range(2)]
        s_loc = [_dot_nt(qh[hh], kb) for hh in range(2)]
        s_ctx = [_dot_nt(qh[hh], kc) for hh in range(2)]
        p = []
        for hh in range(2):
            sl = s_loc[hh] + bias_ref[0, hh]
            if fixed_ref:
                m = safe_ref[0]
            else:
                m = jnp.maximum(jnp.max(sl, axis=-1, keepdims=True), jnp.max(s_ctx[hh], axis=-1, keepdims=True))
            p.append(jnp.concatenate([jnp.exp2(sl - m), jnp.exp2(s_ctx[hh] - m)], axis=1).astype(BF16))
        acc = [_dot(p[hh], vaug) for hh in range(2)]
        outs = [acc[hh][:, 0:LANES] / acc[hh][:, LANES:LANES + 1] for hh in range(2)]
        o_ref[0] = jnp.where(lane < HEAD_DIM, outs[0], outs[1]).astype(BF16)

    fixed = safe_ref[0] > 0.0
    pl.when(fixed)(functools.partial(attend, True))
    pl.when(jnp.logical_not(fixed))(functools.partial(attend, False))


def _na_call(safe, nq, nk, nv, bias, *, t_lat):
    b, tall, w = nq.shape
    tc = tall - t_lat
    rows = t_lat // GRID_W
    tq = NA_QROWS * GRID_W
    nsteps = rows // NA_QROWS
    npair = w // LANES
    ctx_blk = t_lat // tc

    step_types, _ = _na_block_types(rows)
    common = max(set(step_types), key=step_types.count)

    def btype(i):
        t = common
        for step, kind in enumerate(step_types):
            if kind != common:
                t = jnp.where(i == step, kind, t)
        return t

    return pl.pallas_call(
        functools.partial(_na_body, rows=rows),
        grid=(b, npair, nsteps),
        in_specs=[pl.BlockSpec(memory_space=pltpu.SMEM),
                  pl.BlockSpec((1, tq, LANES), lambda bi, hp, i: (bi, i, hp)),
                  pl.BlockSpec((1, t_lat, LANES), lambda bi, hp, i: (bi, 0, hp)),
                  pl.BlockSpec((1, t_lat, LANES), lambda bi, hp, i: (bi, 0, hp)),
                  pl.BlockSpec((1, tc, LANES), lambda bi, hp, i: (bi, ctx_blk, hp)),
                  pl.BlockSpec((1, tc, LANES), lambda bi, hp, i: (bi, ctx_blk, hp)),
                  pl.BlockSpec((1, 2, tq, NA_BAND * GRID_W), lambda bi, hp, i: (btype(i), hp, 0, 0))],
        out_specs=pl.BlockSpec((1, tq, LANES), lambda bi, hp, i: (bi, i, hp)),
        out_shape=jax.ShapeDtypeStruct((b, t_lat, w), BF16),
        compiler_params=_cparams(("parallel", "parallel", "arbitrary")),
        name="nbr_attn",
    )(safe, nq, nk, nv, nk, nv, bias)


def _na_block_types(rows):
    kinds, reps, step_types = {}, [], []
    for r0 in range(0, rows, NA_QROWS):
        kb0 = min(max(r0 - NA_KR // 2, 0), rows - NA_BAND)
        r = r0 + np.arange(NA_QROWS)
        sig = (kb0 - r0, tuple(np.clip(r - NA_KR // 2, 0, rows - NA_KR) - r))
        if sig not in kinds:
            kinds[sig] = len(reps)
            reps.append(r0)
        step_types.append(kinds[sig])
    return step_types, reps


def _na_bias_tiles(rpb, rows):
    assert rows >= NA_BAND and rows % NA_QROWS == 0 and NA_QROWS + NA_KR - 1 <= NA_BAND
    nl, nh, na, nb = rpb.shape
    cidx = np.arange(GRID_W)
    rel_c = cidx[None, :] - cidx[:, None] + NA_KC - 1
    sel = jnp.asarray(rel_c[None] == np.arange(nb)[:, None, None], F32)
    toep = jnp.einsum('lhab,bqk->lhqak', rpb.astype(F32) * LOG2E, sel, precision=lax.Precision.HIGHEST)
    toep = jnp.pad(toep, ((0, 0), (0, 0), (0, 0), (NA_BAND, NA_BAND), (0, 0)))
    tiles = []
    for r0 in _na_block_types(rows)[1]:
        kb0 = min(max(r0 - NA_KR // 2, 0), rows - NA_BAND)
        qi = np.arange(NA_QROWS * GRID_W)
        qr, qc = r0 + qi // GRID_W, qi % GRID_W
        kj = np.arange(NA_BAND * GRID_W)
        kr, kc = kb0 + kj // GRID_W, kj % GRID_W
        rs = np.clip(qr - NA_KR // 2, 0, rows - NA_KR)
        cs = np.clip(qc - NA_KC // 2, 0, GRID_W - NA_KC)
        valid = ((kr[None, :] >= rs[:, None]) & (kr[None, :] < rs[:, None] + NA_KR)
                 & (kc[None, :] >= cs[:, None]) & (kc[None, :] < cs[:, None] + NA_KC))
        parts = []
        for q_row in range(NA_QROWS):
            a0 = kb0 - (r0 + q_row) + NA_KR - 1 + NA_BAND
            blk = toep[:, :, :, a0:a0 + NA_BAND, :]
            parts.append(blk.reshape(nl, nh, GRID_W, NA_BAND * GRID_W))
        tile = jnp.concatenate(parts, axis=2)
        tiles.append(jnp.where(jnp.asarray(valid)[None, None], tile, NEG_BIG))
    return jnp.stack(tiles, axis=1)


def _ctx_attn_body(q_ref, k_ref, v_ref, o_ref):
    q = q_ref[0]
    k = k_ref[0]
    v = v_ref[0]
    lane = lax.broadcasted_iota(jnp.int32, q.shape, 1)
    outs = []
    for hh in range(2):
        qh = jnp.where((lane < HEAD_DIM) if hh == 0 else (lane >= HEAD_DIM), q, jnp.zeros_like(q))
        s = _dot_nt(qh, k)
        m = jnp.max(s, axis=-1, keepdims=True)
        p = jnp.exp2(s - m)
        outs.append(_dot(p.astype(BF16), v) / jnp.sum(p, axis=-1, keepdims=True))
    o_ref[0] = jnp.where(lane < HEAD_DIM, outs[0], outs[1]).astype(BF16)


def _ctx_attn_call(nq, nk, nv, *, t_lat):
    b, tall, w = nq.shape
    tc = tall - t_lat
    ctx_blk = t_lat // tc
    spec = pl.BlockSpec((1, tc, LANES), lambda bi, hp: (bi, ctx_blk, hp))
    return pl.pallas_call(
        _ctx_attn_body,
        grid=(b, w // LANES),
        in_specs=[spec, spec, spec],
        out_specs=pl.BlockSpec((1, tc, LANES), lambda bi, hp: (bi, 0, hp)),
        out_shape=jax.ShapeDtypeStruct((b, tc, w), BF16),
        compiler_params=_cparams(("parallel", "parallel")),
        name="ctx_attn",
    )(nq, nk, nv)


def _merge_body(x_ref, yd_ref, of_ref, ob_ref, sg_ref, yn_ref, gt_ref, wd_ref, wg_ref, wn_ref, wo_ref,
                gn_ref, mb_ref, mc_ref, o_ref, *, t_lat, tm, d):
    r = pl.program_id(1)
    ctx = _is_ctx_rows(r, tm, t_lat)
    g1 = jnp.where(ctx, mc_ref[0, :, 2 * d:3 * d], mb_ref[0, :, 2 * d:3 * d])
    og = of_ref[0] + ob_ref[0]
    parts = []
    for hd in range(GLA_HEADS):
        oh = og[:, hd * GLA_DV:(hd + 1) * GLA_DV]
        ms = jnp.mean(oh * oh, axis=-1, keepdims=True)
        parts.append(oh * lax.rsqrt(ms + EPS) * gn_ref[...])
    yg = (jnp.concatenate(parts, axis=-1) * sg_ref[0].astype(F32)).astype(BF16)
    m = (gt_ref[0, :, 0:d].astype(F32) * _dot(yd_ref[0], wd_ref[...])
         + gt_ref[0, :, d:2 * d].astype(F32) * _dot(yg, wg_ref[...])
         + gt_ref[0, :, 2 * d:3 * d].astype(F32) * _dot(yn_ref[0], wn_ref[...]))
    o_ref[0] = x_ref[0] + g1 * _dot(m.astype(BF16), wo_ref[...])


def _merge_call(xall, yd, of, ob, sg, yn, gates, wd, wg, wn, wo, gn, mod3, t_lat):
    b, tall, d = xall.shape
    tm = _row_tile(tall)
    nb = b
    const = lambda bi, r: (0, 0)
    row = lambda n: pl.BlockSpec((1, tm, n), lambda bi, r: (bi, r, 0))
    return pl.pallas_call(
        functools.partial(_merge_body, t_lat=t_lat, tm=tm, d=d),
        grid=(b, tall // tm),
        in_specs=[row(d), row(BR_W), row(BR_W), row(BR_W), row(BR_W), row(BR_W), row(3 * d),
                  pl.BlockSpec((BR_W, d), const), pl.BlockSpec((BR_W, d), const), pl.BlockSpec((BR_W, d), const),
                  pl.BlockSpec((d, d), const), pl.BlockSpec((1, GLA_DV), const),
                  pl.BlockSpec((1, 1, 6 * d), lambda bi, r: (bi, 0, 0)),
                  pl.BlockSpec((1, 1, 6 * d), lambda bi, r: (nb, 0, 0))],
        out_specs=row(d),
        out_shape=jax.ShapeDtypeStruct((b, tall, d), F32),
        compiler_params=_cparams(("parallel", "parallel")),
        name="merge",
    )(xall, yd, of, ob, sg, yn, gates, wd, wg, wn, wo, gn, mod3, mod3)


def _mlp_body(x_ref, g_ref, w1_ref, w2_ref, mb_ref, mc_ref, o_ref, *, t_lat, tm, d):
    r = pl.program_id(1)
    ctx = _is_ctx_rows(r, tm, t_lat)
    sh = jnp.where(ctx, mc_ref[0, :, 3 * d:4 * d], mb_ref[0, :, 3 * d:4 * d])
    sc = jnp.where(ctx, mc_ref[0, :, 4 * d:5 * d], mb_ref[0, :, 4 * d:5 * d])
    g2 = jnp.where(ctx, mc_ref[0, :, 5 * d:6 * d], mb_ref[0, :, 5 * d:6 * d])
    x = x_ref[0]
    h = _norm_mod(x, g_ref[...], sc, sh).astype(BF16)
    acc = jnp.zeros((tm, d), F32)
    step = 1024
    for j in range(w1_ref.shape[1] // step):
        a = jnp.maximum(_dot(h, w1_ref[:, j * step:(j + 1) * step]), 0.0)
        acc = acc + _dot((a * a).astype(BF16), w2_ref[j * step:(j + 1) * step, :])
    o_ref[0] = x + g2 * acc


def _mlp_call(xall, g, w1, w2, mod3, t_lat, *, lat_only):
    b, tall, d = xall.shape
    tm = next(t for t in (512, 256) if t_lat % t == 0) if lat_only else _row_tile(tall)
    n_rows = t_lat if lat_only else tall
    nb = b
    dff = w1.shape[1]
    const = lambda bi, r: (0, 0)
    return pl.pallas_call(
        functools.partial(_mlp_body, t_lat=t_lat, tm=tm, d=d),
        grid=(b, n_rows // tm),
        in_specs=[pl.BlockSpec((1, tm, d), lambda bi, r: (bi, r, 0)),
                  pl.BlockSpec((1, d), const),
                  pl.BlockSpec((d, dff), const, pipeline_mode=pl.Buffered(1)),
                  pl.BlockSpec((dff, d), const, pipeline_mode=pl.Buffered(1)),
                  pl.BlockSpec((1, 1, 6 * d), lambda bi, r: (bi, 0, 0)),
                  pl.BlockSpec((1, 1, 6 * d), lambda bi, r: (nb, 0, 0))],
        out_specs=pl.BlockSpec((1, tm, d), lambda bi, r: (bi, r, 0)),
        out_shape=jax.ShapeDtypeStruct((b, n_rows, d), F32),
        compiler_params=_cparams(("parallel", "parallel")),
        name="mlp",
    )(xall, g.reshape(1, d), w1, w2, mod3, mod3)


def kernel(x, c, ctx, c_ctx, w_mod, b_mod, norm1_g, norm2_g, w_in, da_qn_g, da_kn_g, da_lambda, da_subln_g,
           gla_a2, gla_a_b, gla_gn_g, na_qn_g, na_kn_g, na_rpb, w_br_da, w_br_gla, w_br_na, w_out, w_ff1, w_ff2):
    b, t_lat, d = x.shape
    tc = ctx.shape[1]
    tall = t_lat + tc
    depth = w_mod.shape[0]
    rows = t_lat // GRID_W
    assert d == D_MODEL and t_lat % (2 * tc) == 0 and tc == 256 and b < 8

    xall = jnp.concatenate([x, ctx], axis=1)
    cvec = jnp.zeros((8, d), F32).at[0:b].set(c).at[b].set(c_ctx)
    mod = _mod_call(cvec, w_mod, b_mod)
    cos_t, sin_t = _rope_tables(t_lat, tall)

    seg = np.arange(NORM_LANES) // HEAD_DIM
    bd = jnp.asarray(seg[:, None] == seg[None, :], BF16)
    ci = np.arange(GLA_BLOCK)
    same_chunk = (ci[None, :] // GLA_CHUNK) == (ci[:, None] // GLA_CHUNK)
    tri_f = jnp.asarray(same_chunk & (ci[None, :] <= ci[:, None]), BF16)
    tri_b = jnp.asarray(same_chunk & (ci[None, :] >= ci[:, None]), BF16)

    o_dq, o_dk, o_dv = 0, 512, 1024
    o_gq, o_gk, o_gv, o_gg, o_ga = 1536, 1792, 2048, 2560, 3072
    o_nq, o_nk, o_nv = 3104, 3616, 4128
    o_gate = 4640
    tq_da = 512 if t_lat % 512 == 0 else 256
    tk_da = next(t for t in (3328, 1280, 256) if tall % t == 0)
    na_bias = _na_bias_tiles(na_rpb, rows)

    for l in range(depth):
        need_ctx = l < depth - 1
        lam_init = 0.8 - 0.6 * math.exp(-0.3 * l)
        mod3 = mod[l].reshape(8, 1, 6 * d)
        wl = w_in[l].astype(BF16)
        cut = lambda a, n: wl[:, a:a + n]
        wa = jnp.concatenate([cut(o_ga, 2 * GLA_RANK), jnp.zeros((d, LANES - 2 * GLA_RANK), BF16)], axis=1)
        a2p = jnp.zeros((LANES, 2 * GLA_HEADS * GLA_DK), F32)
        a2p = a2p.at[0:GLA_RANK, 0:256].set(gla_a2[l, 0]).at[GLA_RANK:2 * GLA_RANK, 256:512].set(gla_a2[l, 1])
        ab = gla_a_b[l].reshape(1, 2 * GLA_HEADS * GLA_DK)
        tile2 = lambda g: jnp.tile(g, NORM_LANES // HEAD_DIM).reshape(1, NORM_LANES)

        h = _hnorm_call(xall, norm1_g[l], mod3, t_lat)
        qt, kk, vt = _proj_da_call(h, cut(o_dq, 512), cut(o_dk, 512), cut(o_dv, 512), bd,
                                   tile2(da_qn_g[l]), tile2(da_kn_g[l]), cos_t, sin_t)
        gq, gk, gv, sgg, la = _proj_gla_call(h, cut(o_gq, 256), cut(o_gk, 256), cut(o_gv, 512), cut(o_gg, 512),
                                             wa, a2p.astype(BF16), ab)
        nq, nk, nv = _proj_na_call(h, cut(o_nq, 512), cut(o_nk, 512), cut(o_nv, 512), bd,
                                   tile2(na_qn_g[l]), tile2(na_kn_g[l]))
        def score_bound(gq_, gk_, extra):
            bnd = (math.sqrt(HEAD_DIM) * LOG2E * 1.01) * jnp.max(jnp.abs(gq_)) * jnp.max(jnp.abs(gk_)) + extra + 1e-3
            return jnp.where(bnd <= DA_SAFE_LOG2, bnd, -1.0).astype(F32).reshape(1)
        safe = score_bound(da_qn_g[l], da_kn_g[l], 0.0)
        safe_na = score_bound(na_qn_g[l], na_kn_g[l], jnp.max(jnp.abs(na_rpb[l])) * LOG2E)
        gates = _proj_gate_call(h, cut(o_gate, 3 * d))

        subg = da_subln_g[l].reshape(LANES, 1)
        y_da = _da_call(safe, qt, kk, vt, da_lambda[l], subg, lam_init,
                        q_off=0, nq=t_lat // tq_da, k_off=0, nk=tall // tk_da, tq=tq_da, tk=tk_da)
        o_f = _gla_call(gq, gk, gv, la, tri_f, t_lat=t_lat, reverse=False)
        o_b = _gla_call(gq, gk, gv, la, tri_b, t_lat=t_lat, reverse=True)
        y_na = _na_call(safe_na, nq, nk, nv, na_bias[l], t_lat=t_lat)
        if need_ctx:
            y_da_c = _da_call(safe, qt, kk, vt, da_lambda[l], subg, lam_init,
                              q_off=t_lat // tc, nq=1, k_off=t_lat // tc, nk=1, tq=tc, tk=tc)
            y_na_c = _ctx_attn_call(nq, nk, nv, t_lat=t_lat)
        else:
            y_da_c = jnp.zeros((b, tc, BR_W), BF16)
            y_na_c = jnp.zeros((b, tc, BR_W), BF16)
        y_da = jnp.concatenate([y_da, y_da_c], axis=1)
        y_na = jnp.concatenate([y_na, y_na_c], axis=1)

        xall = _merge_call(xall, y_da, o_f, o_b, sgg, y_na, gates,
                           w_br_da[l].astype(BF16), w_br_gla[l].astype(BF16), w_br_na[l].astype(BF16),
                           w_out[l].astype(BF16), gla_gn_g[l].reshape(1, GLA_DV), mod3, t_lat)
        xall = _mlp_call(xall, norm2_g[l], w_ff1[l].astype(BF16), w_ff2[l].astype(BF16), mod3, t_lat,
                         lat_only=not need_ctx)
    return xall
```

```python
import functools
import math

import numpy as np
import jax
import jax.numpy as jnp
from jax import lax
from jax.experimental import pallas as pl
from jax.experimental.pallas import tpu as pltpu

F32 = jnp.float32
BF16 = jnp.bfloat16
F8 = jnp.float8_e4m3fn

D_MODEL = 1024
GRID_W = 64
HEAD_DIM = 64
EPS = 1e-6
ROPE_BASE = 10000.0
ROPE_PAIRS = HEAD_DIM // 4
DA_HEADS = 4
GLA_HEADS = 4
GLA_DK = 64
GLA_DV = 128
GLA_RANK = 16
GLA_TAU = 16.0
GLA_CHUNK = 64
NA_HEADS = 8
NA_KR = 8
NA_KC = 16
BR_W = 512
D_FF = 4 * D_MODEL

DA_V = 2 * HEAD_DIM
DA_VPAD = 16
DA_VROWS = DA_V + DA_VPAD
LOG2E = math.log2(math.e)
DA_SAFE_LOG2 = 45.0

GLA_BLOCK = 256
NORM_LANES = 256
LANES = 128
NA_QROWS = 8
NA_BAND = 16
NEG_BIG = -1e30
VMEM_LIMIT = 48 * 1024 * 1024


def _cparams(sem):
    return pltpu.CompilerParams(dimension_semantics=sem, vmem_limit_bytes=VMEM_LIMIT)


def _dot(a, b):
    return jnp.dot(a, b, preferred_element_type=F32)


def _dot_nt(a, b):
    return lax.dot_general(a, b, (((1,), (1,)), ((), ())), preferred_element_type=F32)


def _dot_tn(a, b):
    return lax.dot_general(a, b, (((0,), (0,)), ((), ())), preferred_element_type=F32)


def _split_dot(x, w_exact):
    hi = x.astype(BF16)
    lo = (x - hi.astype(F32)).astype(BF16)
    return _dot(hi, w_exact) + _dot(lo, w_exact)


def _split_dot_left(w_exact, x):
    hi = x.astype(BF16)
    lo = (x - hi.astype(F32)).astype(BF16)
    return _dot(w_exact, hi) + _dot(w_exact, lo)


def _sigmoid(x):
    return 1.0 / (1.0 + jnp.exp(-x))


def _row_tile(tall):
    for tm in (640, 256):
        if tall % tm == 0:
            return tm
    raise ValueError(f"unsupported token count {tall}")


def _mod_body(c_ref, w_ref, b_ref, o_ref):
    cv = c_ref[...]
    s = cv * _sigmoid(cv)
    o_ref[0] = _dot(s.astype(BF16), w_ref[0].astype(BF16)) + b_ref[0]


def _mod_call(cvec, w_mod, b_mod):
    depth, d, n = w_mod.shape
    tn = 1536
    return pl.pallas_call(
        _mod_body,
        grid=(depth, n // tn),
        in_specs=[pl.BlockSpec((8, d), lambda l, j: (0, 0)),
                  pl.BlockSpec((1, d, tn), lambda l, j: (l, 0, j)),
                  pl.BlockSpec((1, 1, tn), lambda l, j: (l, 0, j))],
        out_specs=pl.BlockSpec((1, 8, tn), lambda l, j: (l, 0, j)),
        out_shape=jax.ShapeDtypeStruct((depth, 8, n), F32),
        compiler_params=_cparams(("parallel", "parallel")),
        name="mod",
    )(cvec, w_mod, b_mod.reshape(depth, 1, n))


def _rope_body(freq_ref, cos_ref, sin_ref, *, t_lat, tm):
    i = pl.program_id(0)
    t = i * tm + lax.broadcasted_iota(jnp.int32, (tm, LANES), 0)
    lane = lax.broadcasted_iota(jnp.int32, (tm, LANES), 1)
    shift = int(math.log2(GRID_W))
    row = lax.shift_right_logical(t, shift).astype(F32)
    col = jnp.bitwise_and(t, GRID_W - 1).astype(F32)
    use_row = jnp.bitwise_and(lane, 2 * ROPE_PAIRS) == 0
    first_half = jnp.bitwise_and(lane, ROPE_PAIRS) == 0
    ang = jnp.where(use_row, row, col) * freq_ref[...]
    is_lat = t < t_lat
    cos_ref[...] = jnp.where(is_lat, jnp.cos(ang), 1.0)
    sn = jnp.sin(ang)
    sin_ref[...] = jnp.where(is_lat, jnp.where(first_half, -sn, sn), 0.0)


def _rope_tables(t_lat, tall):
    tm = 256
    freqs = ROPE_BASE ** (-jnp.arange(ROPE_PAIRS, dtype=F32) / ROPE_PAIRS)
    freq_lane = jnp.tile(freqs, LANES // ROPE_PAIRS).reshape(1, LANES)
    return pl.pallas_call(
        functools.partial(_rope_body, t_lat=t_lat, tm=tm),
        grid=(tall // tm,),
        in_specs=[pl.BlockSpec((1, LANES), lambda i: (0, 0))],
        out_specs=[pl.BlockSpec((tm, LANES), lambda i: (i, 0))] * 2,
        out_shape=[jax.ShapeDtypeStruct((tall, LANES), F32)] * 2,
        compiler_params=_cparams(("parallel",)),
        name="rope_tables",
    )(freq_lane)


def _is_ctx_rows(r, tm, t_lat):
    t = r * tm + lax.broadcasted_iota(jnp.int32, (tm, 1), 0)
    return t >= t_lat


def _norm_mod(x, g, sc, sh):
    ms = jnp.mean(x * x, axis=-1, keepdims=True)
    return (x * lax.rsqrt(ms + EPS) * g) * (1.0 + sc) + sh


def _segnorm64(z, bd, gain):
    ss = _split_dot(z * z, bd)
    return z * lax.rsqrt(ss * (1.0 / HEAD_DIM) + EPS) * gain


def _hnorm_body(x_ref, g_ref, mb_ref, mc_ref, h_ref, *, t_lat, tm, d):
    r = pl.program_id(1)
    ctx = _is_ctx_rows(r, tm, t_lat)
    sh = jnp.where(ctx, mc_ref[0, :, 0:d], mb_ref[0, :, 0:d])
    sc = jnp.where(ctx, mc_ref[0, :, d:2 * d], mb_ref[0, :, d:2 * d])
    h_ref[0] = _norm_mod(x_ref[0], g_ref[...], sc, sh).astype(BF16)


def _hnorm_call(xall, g, mod3, t_lat):
    b, tall, d = xall.shape
    tm = _row_tile(tall)
    nb = b
    return pl.pallas_call(
        functools.partial(_hnorm_body, t_lat=t_lat, tm=tm, d=d),
        grid=(b, tall // tm),
        in_specs=[pl.BlockSpec((1, tm, d), lambda bi, r: (bi, r, 0)),
                  pl.BlockSpec((1, d), lambda bi, r: (0, 0)),
                  pl.BlockSpec((1, 1, 6 * d), lambda bi, r: (bi, 0, 0)),
                  pl.BlockSpec((1, 1, 6 * d), lambda bi, r: (nb, 0, 0))],
        out_specs=pl.BlockSpec((1, tm, d), lambda bi, r: (bi, r, 0)),
        out_shape=jax.ShapeDtypeStruct((b, tall, d), BF16),
        compiler_params=_cparams(("parallel", "parallel")),
        name="hnorm",
    )(xall, g.reshape(1, d), mod3, mod3)


def _rope128(x, cs, sn, first_half):
    partner = jnp.where(first_half, pltpu.roll(x, x.shape[1] - ROPE_PAIRS, 1), pltpu.roll(x, ROPE_PAIRS, 1))
    return x * cs + partner * sn


def _proj_da_body(h_ref, wq_ref, wk_ref, wv_ref, bd_ref, gq_ref, gk_ref, cos_ref, sin_ref,
                  qt_ref, k_ref, vt_ref):
    h = h_ref[0]
    tm = h.shape[0]
    bd = bd_ref[...]
    wide = bd.shape[0]
    cs = jnp.concatenate([cos_ref[...]] * (wide // LANES), axis=1)
    sn = jnp.concatenate([sin_ref[...]] * (wide // LANES), axis=1)
    lane = lax.broadcasted_iota(jnp.int32, cs.shape, 1)
    first_half = jnp.bitwise_and(lane, ROPE_PAIRS) == 0
    low_lanes = jnp.bitwise_and(lane, HEAD_DIM) == 0
    zq = _dot(h, wq_ref[...])
    zk = _dot(h, wk_ref[...])
    zv = _dot(h, wv_ref[...])
    scale = math.sqrt(HEAD_DIM ** -0.5 * LOG2E)
    ones_row = (lax.broadcasted_iota(jnp.int32, (DA_VPAD, tm), 0) == 0).astype(BF16)
    for j in range(zq.shape[1] // wide):
        sl = slice(j * wide, (j + 1) * wide)
        q = _rope128(_segnorm64(zq[:, sl], bd, gq_ref[...]) * scale, cs, sn, first_half)
        k = _rope128(_segnorm64(zk[:, sl], bd, gk_ref[...]) * scale, cs, sn, first_half)
        q_t = q.T
        qh_t = q_t.astype(F8).astype(F32)
        ql_t = q_t - qh_t
        k_hi = k.astype(F8).astype(F32)
        k_lo = k - k_hi
        k_map1 = jnp.where(low_lanes, k_hi, pltpu.roll(k_lo, HEAD_DIM, 1)).astype(F8)
        k_map2 = jnp.where(low_lanes, pltpu.roll(k_hi, wide - HEAD_DIM, 1), k_lo).astype(F8)
        for hh in range(wide // LANES):
            hd = j * (wide // LANES) + hh
            r0 = hd * 2 * LANES
            for a in range(2):
                src = slice(hh * LANES + a * HEAD_DIM, hh * LANES + (a + 1) * HEAD_DIM)
                qt_ref[0, r0 + a * LANES:r0 + a * LANES + HEAD_DIM, :] = qh_t[src].astype(F8)
                qt_ref[0, r0 + a * LANES + HEAD_DIM:r0 + (a + 1) * LANES, :] = ql_t[src].astype(F8)
            k_ref[0, :, r0:r0 + LANES] = k_map1[:, hh * LANES:(hh + 1) * LANES]
            k_ref[0, :, r0 + LANES:r0 + 2 * LANES] = k_map2[:, hh * LANES:(hh + 1) * LANES]
            v0 = hd * DA_VROWS
            vt_ref[0, v0:v0 + DA_V, :] = zv[:, hd * LANES:(hd + 1) * LANES].T.astype(BF16)
            vt_ref[0, v0 + DA_V:v0 + DA_VROWS, :] = ones_row


def _proj_da_call(h, wq, wk, wv, bd, gq, gk, cos_t, sin_t):
    b, tall, d = h.shape
    tm = _row_tile(tall)
    w = DA_HEADS * LANES
    nbd = bd.shape[0]
    const = lambda bi, r: (0, 0)
    return pl.pallas_call(
        _proj_da_body,
        grid=(b, tall // tm),
        in_specs=[pl.BlockSpec((1, tm, d), lambda bi, r: (bi, r, 0)),
                  pl.BlockSpec((d, w), const), pl.BlockSpec((d, w), const), pl.BlockSpec((d, w), const),
                  pl.BlockSpec((nbd, nbd), const),
                  pl.BlockSpec((1, nbd), const), pl.BlockSpec((1, nbd), const),
                  pl.BlockSpec((tm, LANES), lambda bi, r: (r, 0)),
                  pl.BlockSpec((tm, LANES), lambda bi, r: (r, 0))],
        out_specs=[pl.BlockSpec((1, 2 * w, tm), lambda bi, r: (bi, 0, r)),
                   pl.BlockSpec((1, tm, 2 * w), lambda bi, r: (bi, r, 0)),
                   pl.BlockSpec((1, DA_HEADS * DA_VROWS, tm), lambda bi, r: (bi, 0, r))],
        out_shape=[jax.ShapeDtypeStruct((b, 2 * w, tall), F8),
                   jax.ShapeDtypeStruct((b, tall, 2 * w), F8),
                   jax.ShapeDtypeStruct((b, DA_HEADS * DA_VROWS, tall), BF16)],
        compiler_params=_cparams(("parallel", "parallel")),
        name="proj_da",
    )(h, wq, wk, wv, bd, gq, gk, cos_t, sin_t)


def _proj_gla_body(h_ref, wq_ref, wk_ref, wv_ref, wg_ref, wa_ref, a2_ref, ab_ref,
                   q_ref, k_ref, v_ref, sg_ref, la_ref):
    h = h_ref[0]
    q_ref[0] = _dot(h, wq_ref[...]) * (GLA_DK ** -0.5)
    k_ref[0] = _dot(h, wk_ref[...])
    v_ref[0] = _dot(h, wv_ref[...]).astype(BF16)
    g = _dot(h, wg_ref[...])
    sg_ref[0] = (g * _sigmoid(g)).astype(BF16)
    ga = _dot(h, wa_ref[...])
    z = _dot(ga.astype(BF16), a2_ref[...]) + ab_ref[...]
    la_ref[0] = (jnp.minimum(z, 0.0) - jnp.log1p(jnp.exp(-jnp.abs(z)))) * (1.0 / GLA_TAU)


def _proj_gla_call(h, wq, wk, wv, wg, wa, a2p, ab):
    b, tall, d = h.shape
    tm = _row_tile(tall)
    wqk = GLA_HEADS * GLA_DK
    wv_ = GLA_HEADS * GLA_DV
    const = lambda bi, r: (0, 0)
    row = lambda n: pl.BlockSpec((1, tm, n), lambda bi, r: (bi, r, 0))
    return pl.pallas_call(
        _proj_gla_body,
        grid=(b, tall // tm),
        in_specs=[row(d),
                  pl.BlockSpec((d, wqk), const), pl.BlockSpec((d, wqk), const),
                  pl.BlockSpec((d, wv_), const), pl.BlockSpec((d, wv_), const),
                  pl.BlockSpec((d, LANES), const),
                  pl.BlockSpec((LANES, 2 * wqk), const), pl.BlockSpec((1, 2 * wqk), const)],
        out_specs=[row(wqk), row(wqk), row(wv_), row(wv_), row(2 * wqk)],
        out_shape=[jax.ShapeDtypeStruct((b, tall, wqk), F32),
                   jax.ShapeDtypeStruct((b, tall, wqk), F32),
                   jax.ShapeDtypeStruct((b, tall, wv_), BF16),
                   jax.ShapeDtypeStruct((b, tall, wv_), BF16),
                   jax.ShapeDtypeStruct((b, tall, 2 * wqk), F32)],
        compiler_params=_cparams(("parallel", "parallel")),
        name="proj_gla",
    )(h, wq, wk, wv, wg, wa, a2p, ab)


def _proj_na_body(h_ref, wq_ref, wk_ref, wv_ref, bd_ref, gq_ref, gk_ref, q_ref, k_ref, v_ref):
    h = h_ref[0]
    bd = bd_ref[...]
    wide = bd.shape[0]
    zq = _dot(h, wq_ref[...])
    zk = _dot(h, wk_ref[...])
    scale = HEAD_DIM ** -0.5 * LOG2E
    for j in range(NA_HEADS * HEAD_DIM // wide):
        sl = slice(j * wide, (j + 1) * wide)
        q_ref[0, :, sl] = (_segnorm64(zq[:, sl], bd, gq_ref[...]) * scale).astype(BF16)
        k_ref[0, :, sl] = _segnorm64(zk[:, sl], bd, gk_ref[...]).astype(BF16)
    v_ref[0] = _dot(h, wv_ref[...]).astype(BF16)


def _proj_na_call(h, wq, wk, wv, bd, gq, gk):
    b, tall, d = h.shape
    tm = _row_tile(tall)
    w = NA_HEADS * HEAD_DIM
    nbd = bd.shape[0]
    const = lambda bi, r: (0, 0)
    row = lambda n: pl.BlockSpec((1, tm, n), lambda bi, r: (bi, r, 0))
    return pl.pallas_call(
        _proj_na_body,
        grid=(b, tall // tm),
        in_specs=[row(d), pl.BlockSpec((d, w), const), pl.BlockSpec((d, w), const), pl.BlockSpec((d, w), const),
                  pl.BlockSpec((nbd, nbd), const),
                  pl.BlockSpec((1, nbd), const), pl.BlockSpec((1, nbd), const)],
        out_specs=[row(w), row(w), row(w)],
        out_shape=[jax.ShapeDtypeStruct((b, tall, w), BF16)] * 3,
        compiler_params=_cparams(("parallel", "parallel")),
        name="proj_na",
    )(h, wq, wk, wv, bd, gq, gk)


def _proj_gate_body(h_ref, w_ref, o_ref):
    h = h_ref[0]
    n = w_ref.shape[1]
    step = 512
    for j in range(n // step):
        sl = slice(j * step, (j + 1) * step)
        o_ref[0, :, sl] = _sigmoid(_dot(h, w_ref[:, sl])).astype(BF16)


def _proj_gate_call(h, w):
    b, tall, d = h.shape
    tm = _row_tile(tall)
    n = w.shape[1]
    return pl.pallas_call(
        _proj_gate_body,
        grid=(b, tall // tm),
        in_specs=[pl.BlockSpec((1, tm, d), lambda bi, r: (bi, r, 0)),
                  pl.BlockSpec((d, n), lambda bi, r: (0, 0))],
        out_specs=pl.BlockSpec((1, tm, n), lambda bi, r: (bi, r, 0)),
        out_shape=jax.ShapeDtypeStruct((b, tall, n), BF16),
        compiler_params=_cparams(("parallel", "parallel")),
        name="proj_gate",
    )(h, w)


def _da_body(safe_ref, qt_ref, k_ref, vt_ref, lp_ref, sg_ref, o_ref, m_ref, acc_ref, *, lam_init, nk):
    j = pl.program_id(3)
    bounded = safe_ref[0] > 0.0

    @pl.when(j == 0)
    def _():
        m_ref[...] = jnp.full(m_ref.shape, -jnp.inf, F32)
        acc_ref[...] = jnp.zeros(acc_ref.shape, F32)

    def scores(a):
        kb = k_ref[0, :, a * LANES:(a + 1) * LANES]
        q_hi = qt_ref[0, a * LANES:a * LANES + HEAD_DIM, :]
        q_lo = qt_ref[0, a * LANES + HEAD_DIM:(a + 1) * LANES, :]
        w = jnp.concatenate([q_hi, q_hi, q_lo, q_lo], axis=0)
        return _dot(jnp.concatenate([kb, kb], axis=1), w)

    def bounded_step():
        vt = vt_ref[0]
        for a in range(2):
            p = jnp.exp2(scores(a) - safe_ref[0])
            acc_ref[a] += _dot(vt, p.astype(BF16))

    def plain_step():
        vt = vt_ref[0]
        for a in range(2):
            s = scores(a)
            m_old = m_ref[a:a + 1, :]
            m_new = jnp.maximum(m_old, jnp.max(s, axis=0, keepdims=True))
            p = jnp.exp2(s - m_new)
            acc_ref[a] = jnp.exp2(m_old - m_new) * acc_ref[a] + _dot(vt, p.astype(BF16))
            m_ref[a:a + 1, :] = m_new

    pl.when(bounded)(bounded_step)
    pl.when(jnp.logical_not(bounded))(plain_step)

    @pl.when(j == nk - 1)
    def _():
        lp = lp_ref[...]
        e1 = jnp.exp(jnp.sum(lp[0:1] * lp[1:2], axis=-1, keepdims=True))
        e2 = jnp.exp(jnp.sum(lp[2:3] * lp[3:4], axis=-1, keepdims=True))
        lam = e1 - e2 + lam_init
        o1 = acc_ref[0, 0:DA_V, :] / acc_ref[0, DA_V:DA_V + 1, :]
        o2 = acc_ref[1, 0:DA_V, :] / acc_ref[1, DA_V:DA_V + 1, :]
        o = o1 - lam * o2
        ms = jnp.mean(o * o, axis=0, keepdims=True)
        y = (o * lax.rsqrt(ms + EPS) * sg_ref[...]) * (1.0 - lam_init)
        o_ref[0] = y.T.astype(BF16)


def _da_call(safe, qt, kk, vt, lp, subg, lam_init, *, q_off, nq, k_off, nk, tq, tk):
    b = qt.shape[0]
    return pl.pallas_call(
        functools.partial(_da_body, lam_init=lam_init, nk=nk),
        grid=(b, DA_HEADS, nq, nk),
        in_specs=[pl.BlockSpec(memory_space=pltpu.SMEM),
                  pl.BlockSpec((1, 2 * LANES, tq), lambda bi, h, i, j: (bi, h, i + q_off)),
                  pl.BlockSpec((1, tk, 2 * LANES), lambda bi, h, i, j: (bi, j + k_off, h)),
                  pl.BlockSpec((1, DA_VROWS, tk), lambda bi, h, i, j: (bi, h, j + k_off)),
                  pl.BlockSpec((4, HEAD_DIM), lambda bi, h, i, j: (0, 0)),
                  pl.BlockSpec((LANES, 1), lambda bi, h, i, j: (0, 0))],
        out_specs=pl.BlockSpec((1, tq, LANES), lambda bi, h, i, j: (bi, i, h)),
        out_shape=jax.ShapeDtypeStruct((b, nq * tq, DA_HEADS * LANES), BF16),
        scratch_shapes=[pltpu.VMEM((8, tq), F32), pltpu.VMEM((2, DA_VROWS, tq), F32)],
        compiler_params=_cparams(("parallel", "parallel", "parallel", "arbitrary")),
        name="diff_attn",
    )(safe, qt, kk, vt, lp, subg)


def _gla_body(q_ref, k_ref, v_ref, la_ref, tri_ref, o_ref, s_ref, *, reverse, tb):
    i = pl.program_id(1)

    @pl.when(i == 0)
    def _():
        s_ref[...] = jnp.zeros(s_ref.shape, F32)

    c = GLA_CHUNK
    w = GLA_HEADS * GLA_DK
    nch = tb // c
    order = list(reversed(range(nch))) if reverse else list(range(nch))
    tri = tri_ref[...]
    keep = tri > 0
    cum = _split_dot_left(tri, la_ref[0])
    tot = [cum[ch * c:ch * c + 1, :] if reverse else cum[(ch + 1) * c - 1:(ch + 1) * c, :] for ch in range(nch)]
    tot_rows = jnp.concatenate([jnp.broadcast_to(t, (c, w)) for t in tot], axis=0)
    q = q_ref[0]
    k = k_ref[0]
    qe = q * jnp.exp(cum)
    ke = (k * jnp.exp(-cum)).astype(BF16)
    kd = (k * jnp.exp(tot_rows - cum)).astype(BF16)
    dec = [jnp.exp(t) for t in tot]
    lane = lax.broadcasted_iota(jnp.int32, (1, w), 1)
    heads = range(GLA_HEADS)
    qh = [jnp.where((lane >= hd * GLA_DK) & (lane < (hd + 1) * GLA_DK), qe, 0.0).astype(BF16) for hd in heads]
    vh = [v_ref[0, :, hd * GLA_DV:(hd + 1) * GLA_DV] for hd in heads]
    inc = [[_dot_tn(vh[hd][ch * c:(ch + 1) * c], kd[ch * c:(ch + 1) * c]) for ch in range(nch)] for hd in heads]
    a = [jnp.where(keep, _dot_nt(qh[hd], ke), 0.0).astype(BF16) for hd in heads]
    entry = []
    for hd in heads:
        st = s_ref[hd]
        seen = {}
        for ch in order:
            seen[ch] = st.astype(BF16)
            st = st * dec[ch] + inc[hd][ch]
        s_ref[hd] = st
        entry.append(seen)
    o_intra = [_dot(a[hd], vh[hd]) for hd in heads]
    for hd in heads:
        for ch in range(nch):
            rows = slice(ch * c, (ch + 1) * c)
            o_ref[0, rows, hd * GLA_DV:(hd + 1) * GLA_DV] = o_intra[hd][rows] + _dot_nt(qh[hd][rows], entry[hd][ch])


def _gla_call(gq, gk, gv, la, tri, *, t_lat, reverse):
    b, tall, w = gq.shape
    tb = GLA_BLOCK
    n_lat = t_lat // tb
    nblk = tall // tb
    if reverse:
        blk = lambda i: jnp.where(i == 0, n_lat, n_lat - i)
    else:
        blk = lambda i: jnp.where(i == 0, n_lat, i - 1)
    wv_ = GLA_HEADS * GLA_DV
    return pl.pallas_call(
        functools.partial(_gla_body, reverse=reverse, tb=tb),
        grid=(b, nblk),
        in_specs=[pl.BlockSpec((1, tb, w), lambda bi, i: (bi, blk(i), 0)),
                  pl.BlockSpec((1, tb, w), lambda bi, i: (bi, blk(i), 0)),
                  pl.BlockSpec((1, tb, wv_), lambda bi, i: (bi, blk(i), 0)),
                  pl.BlockSpec((1, tb, w), lambda bi, i: (bi, blk(i), 1 if reverse else 0)),
                  pl.BlockSpec((tb, tb), lambda bi, i: (0, 0))],
        out_specs=pl.BlockSpec((1, tb, wv_), lambda bi, i: (bi, blk(i), 0)),
        out_shape=jax.ShapeDtypeStruct((b, tall, wv_), F32),
        scratch_shapes=[pltpu.VMEM((GLA_HEADS, GLA_DV, w), F32)],
        compiler_params=_cparams(("parallel", "arbitrary")),
        name="gla_bwd" if reverse else "gla_fwd",
    )(gq, gk, gv, la, tri)


def _na_body(safe_ref, q_ref, k_ref, v_ref, kc_ref, vc_ref, bias_ref, o_ref, *, rows):
    i = pl.program_id(2)
    kb0 = jnp.clip(i * NA_QROWS - NA_KR // 2, 0, rows - NA_BAND)
    start = pl.multiple_of(kb0 * GRID_W, GRID_W)
    nband = NA_BAND * GRID_W

    def attend(fixed_ref):
        q = q_ref[0]
        kb = k_ref[0, pl.ds(start, nband), :]
        kc = kc_ref[0]
        vall = jnp.concatenate([v_ref[0, pl.ds(start, nband), :], vc_ref[0]], axis=0)
        klane = lax.broadcasted_iota(jnp.int32, vall.shape, 1)
        vaug = jnp.concatenate([vall, (klane == 0).astype(BF16)], axis=1)
        lane = lax.broadcasted_iota(jnp.int32, q.shape, 1)
        qh = [jnp.where((lane < HEAD_DIM) if hh == 0 else (lane >= HEAD_DIM), q, jnp.zeros_like(q))
              for hh in range(2)]
        s_loc = [_dot_nt(qh[hh], kb) for hh in range(2)]
        s_ctx = [_dot_nt(qh[hh], kc) for hh in range(2)]
        p = []
        for hh in range(2):
            sl = s_loc[hh] + bias_ref[0, hh]
            if fixed_ref:
                m = safe_ref[0]
            else:
                m = jnp.maximum(jnp.max(sl, axis=-1, keepdims=True), jnp.max(s_ctx[hh], axis=-1, keepdims=True))
            p.append(jnp.concatenate([jnp.exp2(sl - m), jnp.exp2(s_ctx[hh] - m)], axis=1).astype(BF16))
        acc = [_dot(p[hh], vaug) for hh in range(2)]
        outs = [acc[hh][:, 0:LANES] / acc[hh][:, LANES:LANES + 1] for hh in range(2)]
        o_ref[0] = jnp.where(lane < HEAD_DIM, outs[0], outs[1]).astype(BF16)

    fixed = safe_ref[0] > 0.0
    pl.when(fixed)(functools.partial(attend, True))
    pl.when(jnp.logical_not(fixed))(functools.partial(attend, False))


def _na_call(safe, nq, nk, nv, bias, *, t_lat):
    b, tall, w = nq.shape
    tc = tall - t_lat
    rows = t_lat // GRID_W
    tq = NA_QROWS * GRID_W
    nsteps = rows // NA_QROWS
    npair = w // LANES
    ctx_blk = t_lat // tc

    step_types, _ = _na_block_types(rows)
    common = max(set(step_types), key=step_types.count)

    def btype(i):
        t = common
        for step, kind in enumerate(step_types):
            if kind != common:
                t = jnp.where(i == step, kind, t)
        return t

    return pl.pallas_call(
        functools.partial(_na_body, rows=rows),
        grid=(b, npair, nsteps),
        in_specs=[pl.BlockSpec(memory_space=pltpu.SMEM),
                  pl.BlockSpec((1, tq, LANES), lambda bi, hp, i: (bi, i, hp)),
                  pl.BlockSpec((1, t_lat, LANES), lambda bi, hp, i: (bi, 0, hp)),
                  pl.BlockSpec((1, t_lat, LANES), lambda bi, hp, i: (bi, 0, hp)),
                  pl.BlockSpec((1, tc, LANES), lambda bi, hp, i: (bi, ctx_blk, hp)),
                  pl.BlockSpec((1, tc, LANES), lambda bi, hp, i: (bi, ctx_blk, hp)),
                  pl.BlockSpec((1, 2, tq, NA_BAND * GRID_W), lambda bi, hp, i: (btype(i), hp, 0, 0))],
        out_specs=pl.BlockSpec((1, tq, LANES), lambda bi, hp, i: (bi, i, hp)),
        out_shape=jax.ShapeDtypeStruct((b, t_lat, w), BF16),
        compiler_params=_cparams(("parallel", "parallel", "arbitrary")),
        name="nbr_attn",
    )(safe, nq, nk, nv, nk, nv, bias)


def _na_block_types(rows):
    kinds, reps, step_types = {}, [], []
    for r0 in range(0, rows, NA_QROWS):
        kb0 = min(max(r0 - NA_KR // 2, 0), rows - NA_BAND)
        r = r0 + np.arange(NA_QROWS)
        sig = (kb0 - r0, tuple(np.clip(r - NA_KR // 2, 0, rows - NA_KR) - r))
        if sig not in kinds:
            kinds[sig] = len(reps)
            reps.append(r0)
        step_types.append(kinds[sig])
    return step_types, reps


def _na_bias_tiles(rpb, rows):
    assert rows >= NA_BAND and rows % NA_QROWS == 0 and NA_QROWS + NA_KR - 1 <= NA_BAND
    nl, nh, na, nb = rpb.shape
    cidx = np.arange(GRID_W)
    rel_c = cidx[None, :] - cidx[:, None] + NA_KC - 1
    sel = jnp.asarray(rel_c[None] == np.arange(nb)[:, None, None], F32)
    toep = jnp.einsum('lhab,bqk->lhqak', rpb.astype(F32) * LOG2E, sel, precision=lax.Precision.HIGHEST)
    toep = jnp.pad(toep, ((0, 0), (0, 0), (0, 0), (NA_BAND, NA_BAND), (0, 0)))
    tiles = []
    for r0 in _na_block_types(rows)[1]:
        kb0 = min(max(r0 - NA_KR // 2, 0), rows - NA_BAND)
        qi = np.arange(NA_QROWS * GRID_W)
        qr, qc = r0 + qi // GRID_W, qi % GRID_W
        kj = np.arange(NA_BAND * GRID_W)
        kr, kc = kb0 + kj // GRID_W, kj % GRID_W
        rs = np.clip(qr - NA_KR // 2, 0, rows - NA_KR)
        cs = np.clip(qc - NA_KC // 2, 0, GRID_W - NA_KC)
        valid = ((kr[None, :] >= rs[:, None]) & (kr[None, :] < rs[:, None] + NA_KR)
                 & (kc[None, :] >= cs[:, None]) & (kc[None, :] < cs[:, None] + NA_KC))
        parts = []
        for q_row in range(NA_QROWS):
            a0 = kb0 - (r0 + q_row) + NA_KR - 1 + NA_BAND
            blk = toep[:, :, :, a0:a0 + NA_BAND, :]
            parts.append(blk.reshape(nl, nh, GRID_W, NA_BAND * GRID_W))
        tile = jnp.concatenate(parts, axis=2)
        tiles.append(jnp.where(jnp.asarray(valid)[None, None], tile, NEG_BIG))
    return jnp.stack(tiles, axis=1)


def _ctx_attn_body(q_ref, k_ref, v_ref, o_ref):
    q = q_ref[0]
    k = k_ref[0]
    v = v_ref[0]
    lane = lax.broadcasted_iota(jnp.int32, q.shape, 1)
    outs = []
    for hh in range(2):
        qh = jnp.where((lane < HEAD_DIM) if hh == 0 else (lane >= HEAD_DIM), q, jnp.zeros_like(q))
        s = _dot_nt(qh, k)
        m = jnp.max(s, axis=-1, keepdims=True)
        p = jnp.exp2(s - m)
        outs.append(_dot(p.astype(BF16), v) / jnp.sum(p, axis=-1, keepdims=True))
    o_ref[0] = jnp.where(lane < HEAD_DIM, outs[0], outs[1]).astype(BF16)


def _ctx_attn_call(nq, nk, nv, *, t_lat):
    b, tall, w = nq.shape
    tc = tall - t_lat
    ctx_blk = t_lat // tc
    spec = pl.BlockSpec((1, tc, LANES), lambda bi, hp: (bi, ctx_blk, hp))
    return pl.pallas_call(
        _ctx_attn_body,
        grid=(b, w // LANES),
        in_specs=[spec, spec, spec],
        out_specs=pl.BlockSpec((1, tc, LANES), lambda bi, hp: (bi, 0, hp)),
        out_shape=jax.ShapeDtypeStruct((b, tc, w), BF16),
        compiler_params=_cparams(("parallel", "parallel")),
        name="ctx_attn",
    )(nq, nk, nv)


def _merge_body(x_ref, yd_ref, of_ref, ob_ref, sg_ref, yn_ref, gt_ref, wd_ref, wg_ref, wn_ref, wo_ref,
                gn_ref, mb_ref, mc_ref, o_ref, *, t_lat, tm, d):
    r = pl.program_id(1)
    ctx = _is_ctx_rows(r, tm, t_lat)
    g1 = jnp.where(ctx, mc_ref[0, :, 2 * d:3 * d], mb_ref[0, :, 2 * d:3 * d])
    og = of_ref[0] + ob_ref[0]
    parts = []
    for hd in range(GLA_HEADS):
        oh = og[:, hd * GLA_DV:(hd + 1) * GLA_DV]
        ms = jnp.mean(oh * oh, axis=-1, keepdims=True)
        parts.append(oh * lax.rsqrt(ms + EPS) * gn_ref[...])
    yg = (jnp.concatenate(parts, axis=-1) * sg_ref[0].astype(F32)).astype(BF16)
    m = (gt_ref[0, :, 0:d].astype(F32) * _dot(yd_ref[0], wd_ref[...])
         + gt_ref[0, :, d:2 * d].astype(F32) * _dot(yg, wg_ref[...])
         + gt_ref[0, :, 2 * d:3 * d].astype(F32) * _dot(yn_ref[0], wn_ref[...]))
    o_ref[0] = x_ref[0] + g1 * _dot(m.astype(BF16), wo_ref[...])


def _merge_call(xall, yd, of, ob, sg, yn, gates, wd, wg, wn, wo, gn, mod3, t_lat):
    b, tall, d = xall.shape
    tm = _row_tile(tall)
    nb = b
    const = lambda bi, r: (0, 0)
    row = lambda n: pl.BlockSpec((1, tm, n), lambda bi, r: (bi, r, 0))
    return pl.pallas_call(
        functools.partial(_merge_body, t_lat=t_lat, tm=tm, d=d),
        grid=(b, tall // tm),
        in_specs=[row(d), row(BR_W), row(BR_W), row(BR_W), row(BR_W), row(BR_W), row(3 * d),
                  pl.BlockSpec((BR_W, d), const), pl.BlockSpec((BR_W, d), const), pl.BlockSpec((BR_W, d), const),
                  pl.BlockSpec((d, d), const), pl.BlockSpec((1, GLA_DV), const),
                  pl.BlockSpec((1, 1, 6 * d), lambda bi, r: (bi, 0, 0)),
                  pl.BlockSpec((1, 1, 6 * d), lambda bi, r: (nb, 0, 0))],
        out_specs=row(d),
        out_shape=jax.ShapeDtypeStruct((b, tall, d), F32),
        compiler_params=_cparams(("parallel", "parallel")),
        name="merge",
    )(xall, yd, of, ob, sg, yn, gates, wd, wg, wn, wo, gn, mod3, mod3)


def _mlp_body(x_ref, g_ref, w1_ref, w2_ref, mb_ref, mc_ref, o_ref, *, t_lat, tm, d):
    r = pl.program_id(1)
    ctx = _is_ctx_rows(r, tm, t_lat)
    sh = jnp.where(ctx, mc_ref[0, :, 3 * d:4 * d], mb_ref[0, :, 3 * d:4 * d])
    sc = jnp.where(ctx, mc_ref[0, :, 4 * d:5 * d], mb_ref[0, :, 4 * d:5 * d])
    g2 = jnp.where(ctx, mc_ref[0, :, 5 * d:6 * d], mb_ref[0, :, 5 * d:6 * d])
    x = x_ref[0]
    h = _norm_mod(x, g_ref[...], sc, sh).astype(BF16)
    acc = jnp.zeros((tm, d), F32)
    step = 1024
    for j in range(w1_ref.shape[1] // step):
        a = jnp.maximum(_dot(h, w1_ref[:, j * step:(j + 1) * step]), 0.0)
        acc = acc + _dot((a * a).astype(BF16), w2_ref[j * step:(j + 1) * step, :])
    o_ref[0] = x + g2 * acc


def _mlp_call(xall, g, w1, w2, mod3, t_lat, *, lat_only):
    b, tall, d = xall.shape
    tm = next(t for t in (512, 256) if t_lat % t == 0) if lat_only else _row_tile(tall)
    n_rows = t_lat if lat_only else tall
    nb = b
    dff = w1.shape[1]
    const = lambda bi, r: (0, 0)
    return pl.pallas_call(
        functools.partial(_mlp_body, t_lat=t_lat, tm=tm, d=d),
        grid=(b, n_rows // tm),
        in_specs=[pl.BlockSpec((1, tm, d), lambda bi, r: (bi, r, 0)),
                  pl.BlockSpec((1, d), const),
                  pl.BlockSpec((d, dff), const, pipeline_mode=pl.Buffered(1)),
                  pl.BlockSpec((dff, d), const, pipeline_mode=pl.Buffered(1)),
                  pl.BlockSpec((1, 1, 6 * d), lambda bi, r: (bi, 0, 0)),
                  pl.BlockSpec((1, 1, 6 * d), lambda bi, r: (nb, 0, 0))],
        out_specs=pl.BlockSpec((1, tm, d), lambda bi, r: (bi, r, 0)),
        out_shape=jax.ShapeDtypeStruct((b, n_rows, d), F32),
        compiler_params=_cparams(("parallel", "parallel")),
        name="mlp",
    )(xall, g.reshape(1, d), w1, w2, mod3, mod3)


def kernel(x, c, ctx, c_ctx, w_mod, b_mod, norm1_g, norm2_g, w_in, da_qn_g, da_kn_g, da_lambda, da_subln_g,
           gla_a2, gla_a_b, gla_gn_g, na_qn_g, na_kn_g, na_rpb, w_br_da, w_br_gla, w_br_na, w_out, w_ff1, w_ff2):
    b, t_lat, d = x.shape
    tc = ctx.shape[1]
    tall = t_lat + tc
    depth = w_mod.shape[0]
    rows = t_lat // GRID_W
    assert d == D_MODEL and t_lat % (2 * tc) == 0 and tc == 256 and b < 8

    xall = jnp.concatenate([x, ctx], axis=1)
    cvec = jnp.zeros((8, d), F32).at[0:b].set(c).at[b].set(c_ctx)
    mod = _mod_call(cvec, w_mod, b_mod)
    cos_t, sin_t = _rope_tables(t_lat, tall)

    seg = np.arange(NORM_LANES) // HEAD_DIM
    bd = jnp.asarray(seg[:, None] == seg[None, :], BF16)
    ci = np.arange(GLA_BLOCK)
    same_chunk = (ci[None, :] // GLA_CHUNK) == (ci[:, None] // GLA_CHUNK)
    tri_f = jnp.asarray(same_chunk & (ci[None, :] <= ci[:, None]), BF16)
    tri_b = jnp.asarray(same_chunk & (ci[None, :] >= ci[:, None]), BF16)

    o_dq, o_dk, o_dv = 0, 512, 1024
    o_gq, o_gk, o_gv, o_gg, o_ga = 1536, 1792, 2048, 2560, 3072
    o_nq, o_nk, o_nv = 3104, 3616, 4128
    o_gate = 4640
    tq_da = next(t for t in (1024, 512, 256) if t_lat % t == 0)
    tk_da = next(t for t in (3328, 1280, 256) if tall % t == 0)
    na_bias = _na_bias_tiles(na_rpb, rows)

    for l in range(depth):
        need_ctx = l < depth - 1
        lam_init = 0.8 - 0.6 * math.exp(-0.3 * l)
        mod3 = mod[l].reshape(8, 1, 6 * d)
        wl = w_in[l].astype(BF16)
        cut = lambda a, n: wl[:, a:a + n]
        wa = jnp.concatenate([cut(o_ga, 2 * GLA_RANK), jnp.zeros((d, LANES - 2 * GLA_RANK), BF16)], axis=1)
        a2p = jnp.zeros((LANES, 2 * GLA_HEADS * GLA_DK), F32)
        a2p = a2p.at[0:GLA_RANK, 0:256].set(gla_a2[l, 0]).at[GLA_RANK:2 * GLA_RANK, 256:512].set(gla_a2[l, 1])
        ab = gla_a_b[l].reshape(1, 2 * GLA_HEADS * GLA_DK)
        tile2 = lambda g: jnp.tile(g, NORM_LANES // HEAD_DIM).reshape(1, NORM_LANES)

        h = _hnorm_call(xall, norm1_g[l], mod3, t_lat)
        qt, kk, vt = _proj_da_call(h, cut(o_dq, 512), cut(o_dk, 512), cut(o_dv, 512), bd,
                                   tile2(da_qn_g[l]), tile2(da_kn_g[l]), cos_t, sin_t)
        gq, gk, gv, sgg, la = _proj_gla_call(h, cut(o_gq, 256), cut(o_gk, 256), cut(o_gv, 512), cut(o_gg, 512),
                                             wa, a2p.astype(BF16), ab)
        nq, nk, nv = _proj_na_call(h, cut(o_nq, 512), cut(o_nk, 512), cut(o_nv, 512), bd,
                                   tile2(na_qn_g[l]), tile2(na_kn_g[l]))
        def score_bound(gq_, gk_, extra):
            bnd = (math.sqrt(HEAD_DIM) * LOG2E * 1.01) * jnp.max(jnp.abs(gq_)) * jnp.max(jnp.abs(gk_)) + extra + 1e-3
            return jnp.where(bnd <= DA_SAFE_LOG2, bnd, -1.0).astype(F32).reshape(1)
        safe = score_bound(da_qn_g[l], da_kn_g[l], 0.0)
        safe_na = score_bound(na_qn_g[l], na_kn_g[l], jnp.max(jnp.abs(na_rpb[l])) * LOG2E)
        gates = _proj_gate_call(h, cut(o_gate, 3 * d))

        subg = da_subln_g[l].reshape(LANES, 1)
        y_da = _da_call(safe, qt, kk, vt, da_lambda[l], subg, lam_init,
                        q_off=0, nq=t_lat // tq_da, k_off=0, nk=tall // tk_da, tq=tq_da, tk=tk_da)
        o_f = _gla_call(gq, gk, gv, la, tri_f, t_lat=t_lat, reverse=False)
        o_b = _gla_call(gq, gk, gv, la, tri_b, t_lat=t_lat, reverse=True)
        y_na = _na_call(safe_na, nq, nk, nv, na_bias[l], t_lat=t_lat)
        if need_ctx:
            y_da_c = _da_call(safe, qt, kk, vt, da_lambda[l], subg, lam_init,
                              q_off=t_lat // tc, nq=1, k_off=t_lat // tc, nk=1, tq=tc, tk=tc)
            y_na_c = _ctx_attn_call(nq, nk, nv, t_lat=t_lat)
        else:
            y_da_c = jnp.zeros((b, tc, BR_W), BF16)
            y_na_c = jnp.zeros((b, tc, BR_W), BF16)
        y_da = jnp.concatenate([y_da, y_da_c], axis=1)
        y_na = jnp.concatenate([y_na, y_na_c], axis=1)

        xall = _merge_call(xall, y_da, o_f, o_b, sgg, y_na, gates,
                           w_br_da[l].astype(BF16), w_br_gla[l].astype(BF16), w_br_na[l].astype(BF16),
                           w_out[l].astype(BF16), gla_gn_g[l].reshape(1, GLA_DV), mod3, t_lat)
        xall = _mlp_call(xall, norm2_g[l], w_ff1[l].astype(BF16), w_ff2[l].astype(BF16), mod3, t_lat,
                         lat_only=not need_ctx)
    return xall
```

```python
import functools
import math

import numpy as np
import jax
import jax.numpy as jnp
from jax import lax
from jax.experimental import pallas as pl
from jax.experimental.pallas import tpu as pltpu

F32 = jnp.float32
BF16 = jnp.bfloat16
F8 = jnp.float8_e4m3fn

D_MODEL = 1024
GRID_W = 64
HEAD_DIM = 64
EPS = 1e-6
ROPE_BASE = 10000.0
ROPE_PAIRS = HEAD_DIM // 4
DA_HEADS = 4
GLA_HEADS = 4
GLA_DK = 64
GLA_DV = 128
GLA_RANK = 16
GLA_TAU = 16.0
GLA_CHUNK = 64
NA_HEADS = 8
NA_KR = 8
NA_KC = 16
BR_W = 512
D_FF = 4 * D_MODEL

DA_V = 2 * HEAD_DIM
DA_VPAD = 16
DA_VROWS = DA_V + DA_VPAD
LOG2E = math.log2(math.e)
DA_SAFE_LOG2 = 45.0

GLA_BLOCK = 256
NORM_LANES = 256
LANES = 128
NA_QROWS = 8
NA_BAND = 16
NEG_BIG = -1e30
VMEM_LIMIT = 48 * 1024 * 1024


def _cparams(sem):
    return pltpu.CompilerParams(dimension_semantics=sem, vmem_limit_bytes=VMEM_LIMIT)


def _dot(a, b):
    return jnp.dot(a, b, preferred_element_type=F32)


def _dot_nt(a, b):
    return lax.dot_general(a, b, (((1,), (1,)), ((), ())), preferred_element_type=F32)


def _dot_tn(a, b):
    return lax.dot_general(a, b, (((0,), (0,)), ((), ())), preferred_element_type=F32)


def _split_dot(x, w_exact):
    hi = x.astype(BF16)
    lo = (x - hi.astype(F32)).astype(BF16)
    return _dot(hi, w_exact) + _dot(lo, w_exact)


def _split_dot_left(w_exact, x):
    hi = x.astype(BF16)
    lo = (x - hi.astype(F32)).astype(BF16)
    return _dot(w_exact, hi) + _dot(w_exact, lo)


def _sigmoid(x):
    return 1.0 / (1.0 + jnp.exp(-x))


def _row_tile(tall):
    for tm in (640, 256):
        if tall % tm == 0:
            return tm
    raise ValueError(f"unsupported token count {tall}")


def _mod_body(c_ref, w_ref, b_ref, o_ref):
    cv = c_ref[...]
    s = cv * _sigmoid(cv)
    o_ref[0] = _dot(s.astype(BF16), w_ref[0].astype(BF16)) + b_ref[0]


def _mod_call(cvec, w_mod, b_mod):
    depth, d, n = w_mod.shape
    tn = 1536
    return pl.pallas_call(
        _mod_body,
        grid=(depth, n // tn),
        in_specs=[pl.BlockSpec((8, d), lambda l, j: (0, 0)),
                  pl.BlockSpec((1, d, tn), lambda l, j: (l, 0, j)),
                  pl.BlockSpec((1, 1, tn), lambda l, j: (l, 0, j))],
        out_specs=pl.BlockSpec((1, 8, tn), lambda l, j: (l, 0, j)),
        out_shape=jax.ShapeDtypeStruct((depth, 8, n), F32),
        compiler_params=_cparams(("parallel", "parallel")),
        name="mod",
    )(cvec, w_mod, b_mod.reshape(depth, 1, n))


def _rope_body(freq_ref, cos_ref, sin_ref, *, t_lat, tm):
    i = pl.program_id(0)
    t = i * tm + lax.broadcasted_iota(jnp.int32, (tm, LANES), 0)
    lane = lax.broadcasted_iota(jnp.int32, (tm, LANES), 1)
    shift = int(math.log2(GRID_W))
    row = lax.shift_right_logical(t, shift).astype(F32)
    col = jnp.bitwise_and(t, GRID_W - 1).astype(F32)
    use_row = jnp.bitwise_and(lane, 2 * ROPE_PAIRS) == 0
    first_half = jnp.bitwise_and(lane, ROPE_PAIRS) == 0
    ang = jnp.where(use_row, row, col) * freq_ref[...]
    is_lat = t < t_lat
    cos_ref[...] = jnp.where(is_lat, jnp.cos(ang), 1.0)
    sn = jnp.sin(ang)
    sin_ref[...] = jnp.where(is_lat, jnp.where(first_half, -sn, sn), 0.0)


def _rope_tables(t_lat, tall):
    tm = 256
    freqs = ROPE_BASE ** (-jnp.arange(ROPE_PAIRS, dtype=F32) / ROPE_PAIRS)
    freq_lane = jnp.tile(freqs, LANES // ROPE_PAIRS).reshape(1, LANES)
    return pl.pallas_call(
        functools.partial(_rope_body, t_lat=t_lat, tm=tm),
        grid=(tall // tm,),
        in_specs=[pl.BlockSpec((1, LANES), lambda i: (0, 0))],
        out_specs=[pl.BlockSpec((tm, LANES), lambda i: (i, 0))] * 2,
        out_shape=[jax.ShapeDtypeStruct((tall, LANES), F32)] * 2,
        compiler_params=_cparams(("parallel",)),
        name="rope_tables",
    )(freq_lane)


def _is_ctx_rows(r, tm, t_lat):
    t = r * tm + lax.broadcasted_iota(jnp.int32, (tm, 1), 0)
    return t >= t_lat


def _norm_mod(x, g, sc, sh):
    ms = jnp.mean(x * x, axis=-1, keepdims=True)
    return (x * lax.rsqrt(ms + EPS) * g) * (1.0 + sc) + sh


def _segnorm64(z, bd, gain):
    ss = _split_dot(z * z, bd)
    return z * lax.rsqrt(ss * (1.0 / HEAD_DIM) + EPS) * gain


def _hnorm_body(x_ref, g_ref, mb_ref, mc_ref, h_ref, *, t_lat, tm, d):
    r = pl.program_id(1)
    ctx = _is_ctx_rows(r, tm, t_lat)
    sh = jnp.where(ctx, mc_ref[0, :, 0:d], mb_ref[0, :, 0:d])
    sc = jnp.where(ctx, mc_ref[0, :, d:2 * d], mb_ref[0, :, d:2 * d])
    h_ref[0] = _norm_mod(x_ref[0], g_ref[...], sc, sh).astype(BF16)


def _hnorm_call(xall, g, mod3, t_lat):
    b, tall, d = xall.shape
    tm = _row_tile(tall)
    nb = b
    return pl.pallas_call(
        functools.partial(_hnorm_body, t_lat=t_lat, tm=tm, d=d),
        grid=(b, tall // tm),
        in_specs=[pl.BlockSpec((1, tm, d), lambda bi, r: (bi, r, 0)),
                  pl.BlockSpec((1, d), lambda bi, r: (0, 0)),
                  pl.BlockSpec((1, 1, 6 * d), lambda bi, r: (bi, 0, 0)),
                  pl.BlockSpec((1, 1, 6 * d), lambda bi, r: (nb, 0, 0))],
        out_specs=pl.BlockSpec((1, tm, d), lambda bi, r: (bi, r, 0)),
        out_shape=jax.ShapeDtypeStruct((b, tall, d), BF16),
        compiler_params=_cparams(("parallel", "parallel")),
        name="hnorm",
    )(xall, g.reshape(1, d), mod3, mod3)


def _rope128(x, cs, sn, first_half):
    partner = jnp.where(first_half, pltpu.roll(x, x.shape[1] - ROPE_PAIRS, 1), pltpu.roll(x, ROPE_PAIRS, 1))
    return x * cs + partner * sn


def _proj_da_body(h_ref, wq_ref, wk_ref, wv_ref, bd_ref, gq_ref, gk_ref, cos_ref, sin_ref,
                  qt_ref, k_ref, vt_ref):
    h = h_ref[0]
    tm = h.shape[0]
    bd = bd_ref[...]
    wide = bd.shape[0]
    cs = jnp.concatenate([cos_ref[...]] * (wide // LANES), axis=1)
    sn = jnp.concatenate([sin_ref[...]] * (wide // LANES), axis=1)
    lane = lax.broadcasted_iota(jnp.int32, cs.shape, 1)
    first_half = jnp.bitwise_and(lane, ROPE_PAIRS) == 0
    low_lanes = jnp.bitwise_and(lane, HEAD_DIM) == 0
    zq = _dot(h, wq_ref[...])
    zk = _dot(h, wk_ref[...])
    zv = _dot(h, wv_ref[...])
    scale = math.sqrt(HEAD_DIM ** -0.5 * LOG2E)
    ones_row = (lax.broadcasted_iota(jnp.int32, (DA_VPAD, tm), 0) == 0).astype(BF16)
    for j in range(zq.shape[1] // wide):
        sl = slice(j * wide, (j + 1) * wide)
        q = _rope128(_segnorm64(zq[:, sl], bd, gq_ref[...]) * scale, cs, sn, first_half)
        k = _rope128(_segnorm64(zk[:, sl], bd, gk_ref[...]) * scale, cs, sn, first_half)
        q_t = q.T
        qh_t = q_t.astype(F8).astype(F32)
        ql_t = q_t - qh_t
        k_hi = k.astype(F8).astype(F32)
        k_lo = k - k_hi
        k_map1 = jnp.where(low_lanes, k_hi, pltpu.roll(k_lo, HEAD_DIM, 1)).astype(F8)
        k_map2 = jnp.where(low_lanes, pltpu.roll(k_hi, wide - HEAD_DIM, 1), k_lo).astype(F8)
        for hh in range(wide // LANES):
            hd = j * (wide // LANES) + hh
            r0 = hd * 2 * LANES
            for a in range(2):
                src = slice(hh * LANES + a * HEAD_DIM, hh * LANES + (a + 1) * HEAD_DIM)
                qt_ref[0, r0 + a * LANES:r0 + a * LANES + HEAD_DIM, :] = qh_t[src].astype(F8)
                qt_ref[0, r0 + a * LANES + HEAD_DIM:r0 + (a + 1) * LANES, :] = ql_t[src].astype(F8)
            k_ref[0, :, r0:r0 + LANES] = k_map1[:, hh * LANES:(hh + 1) * LANES]
            k_ref[0, :, r0 + LANES:r0 + 2 * LANES] = k_map2[:, hh * LANES:(hh + 1) * LANES]
            v0 = hd * DA_VROWS
            vt_ref[0, v0:v0 + DA_V, :] = zv[:, hd * LANES:(hd + 1) * LANES].T.astype(BF16)
            vt_ref[0, v0 + DA_V:v0 + DA_VROWS, :] = ones_row


def _proj_da_call(h, wq, wk, wv, bd, gq, gk, cos_t, sin_t):
    b, tall, d = h.shape
    tm = _row_tile(tall)
    w = DA_HEADS * LANES
    nbd = bd.shape[0]
    const = lambda bi, r: (0, 0)
    return pl.pallas_call(
        _proj_da_body,
        grid=(b, tall // tm),
        in_specs=[pl.BlockSpec((1, tm, d), lambda bi, r: (bi, r, 0)),
                  pl.BlockSpec((d, w), const), pl.BlockSpec((d, w), const), pl.BlockSpec((d, w), const),
                  pl.BlockSpec((nbd, nbd), const),
                  pl.BlockSpec((1, nbd), const), pl.BlockSpec((1, nbd), const),
                  pl.BlockSpec((tm, LANES), lambda bi, r: (r, 0)),
                  pl.BlockSpec((tm, LANES), lambda bi, r: (r, 0))],
        out_specs=[pl.BlockSpec((1, 2 * w, tm), lambda bi, r: (bi, 0, r)),
                   pl.BlockSpec((1, tm, 2 * w), lambda bi, r: (bi, r, 0)),
                   pl.BlockSpec((1, DA_HEADS * DA_VROWS, tm), lambda bi, r: (bi, 0, r))],
        out_shape=[jax.ShapeDtypeStruct((b, 2 * w, tall), F8),
                   jax.ShapeDtypeStruct((b, tall, 2 * w), F8),
                   jax.ShapeDtypeStruct((b, DA_HEADS * DA_VROWS, tall), BF16)],
        compiler_params=_cparams(("parallel", "parallel")),
        name="proj_da",
    )(h, wq, wk, wv, bd, gq, gk, cos_t, sin_t)


def _proj_gla_body(h_ref, wq_ref, wk_ref, wv_ref, wg_ref, wa_ref, a2_ref, ab_ref,
                   q_ref, k_ref, v_ref, sg_ref, la_ref):
    h = h_ref[0]
    q_ref[0] = _dot(h, wq_ref[...]) * (GLA_DK ** -0.5)
    k_ref[0] = _dot(h, wk_ref[...])
    v_ref[0] = _dot(h, wv_ref[...]).astype(BF16)
    g = _dot(h, wg_ref[...])
    sg_ref[0] = (g * _sigmoid(g)).astype(BF16)
    ga = _dot(h, wa_ref[...])
    z = _dot(ga.astype(BF16), a2_ref[...]) + ab_ref[...]
    la_ref[0] = (jnp.minimum(z, 0.0) - jnp.log1p(jnp.exp(-jnp.abs(z)))) * (1.0 / GLA_TAU)


def _proj_gla_call(h, wq, wk, wv, wg, wa, a2p, ab):
    b, tall, d = h.shape
    tm = _row_tile(tall)
    wqk = GLA_HEADS * GLA_DK
    wv_ = GLA_HEADS * GLA_DV
    const = lambda bi, r: (0, 0)
    row = lambda n: pl.BlockSpec((1, tm, n), lambda bi, r: (bi, r, 0))
    return pl.pallas_call(
        _proj_gla_body,
        grid=(b, tall // tm),
        in_specs=[row(d),
                  pl.BlockSpec((d, wqk), const), pl.BlockSpec((d, wqk), const),
                  pl.BlockSpec((d, wv_), const), pl.BlockSpec((d, wv_), const),
                  pl.BlockSpec((d, LANES), const),
                  pl.BlockSpec((LANES, 2 * wqk), const), pl.BlockSpec((1, 2 * wqk), const)],
        out_specs=[row(wqk), row(wqk), row(wv_), row(wv_), row(2 * wqk)],
        out_shape=[jax.ShapeDtypeStruct((b, tall, wqk), F32),
                   jax.ShapeDtypeStruct((b, tall, wqk), F32),
                   jax.ShapeDtypeStruct((b, tall, wv_), BF16),
                   jax.ShapeDtypeStruct((b, tall, wv_), BF16),
                   jax.ShapeDtypeStruct((b, tall, 2 * wqk), F32)],
        compiler_params=_cparams(("parallel", "parallel")),
        name="proj_gla",
    )(h, wq, wk, wv, wg, wa, a2p, ab)


def _proj_na_body(h_ref, wq_ref, wk_ref, wv_ref, bd_ref, gq_ref, gk_ref, q_ref, k_ref, v_ref):
    h = h_ref[0]
    bd = bd_ref[...]
    wide = bd.shape[0]
    zq = _dot(h, wq_ref[...])
    zk = _dot(h, wk_ref[...])
    scale = HEAD_DIM ** -0.5 * LOG2E
    for j in range(NA_HEADS * HEAD_DIM // wide):
        sl = slice(j * wide, (j + 1) * wide)
        q_ref[0, :, sl] = (_segnorm64(zq[:, sl], bd, gq_ref[...]) * scale).astype(BF16)
        k_ref[0, :, sl] = _segnorm64(zk[:, sl], bd, gk_ref[...]).astype(BF16)
    v_ref[0] = _dot(h, wv_ref[...]).astype(BF16)


def _proj_na_call(h, wq, wk, wv, bd, gq, gk):
    b, tall, d = h.shape
    tm = _row_tile(tall)
    w = NA_HEADS * HEAD_DIM
    nbd = bd.shape[0]
    const = lambda bi, r: (0, 0)
    row = lambda n: pl.BlockSpec((1, tm, n), lambda bi, r: (bi, r, 0))
    return pl.pallas_call(
        _proj_na_body,
        grid=(b, tall // tm),
        in_specs=[row(d), pl.BlockSpec((d, w), const), pl.BlockSpec((d, w), const), pl.BlockSpec((d, w), const),
                  pl.BlockSpec((nbd, nbd), const),
                  pl.BlockSpec((1, nbd), const), pl.BlockSpec((1, nbd), const)],
        out_specs=[row(w), row(w), row(w)],
        out_shape=[jax.ShapeDtypeStruct((b, tall, w), BF16)] * 3,
        compiler_params=_cparams(("parallel", "parallel")),
        name="proj_na",
    )(h, wq, wk, wv, bd, gq, gk)


def _proj_gate_body(h_ref, w_ref, o_ref):
    h = h_ref[0]
    n = w_ref.shape[1]
    step = 512
    for j in range(n // step):
        sl = slice(j * step, (j + 1) * step)
        o_ref[0, :, sl] = _sigmoid(_dot(h, w_ref[:, sl])).astype(BF16)


def _proj_gate_call(h, w):
    b, tall, d = h.shape
    tm = _row_tile(tall)
    n = w.shape[1]
    return pl.pallas_call(
        _proj_gate_body,
        grid=(b, tall // tm),
        in_specs=[pl.BlockSpec((1, tm, d), lambda bi, r: (bi, r, 0)),
                  pl.BlockSpec((d, n), lambda bi, r: (0, 0))],
        out_specs=pl.BlockSpec((1, tm, n), lambda bi, r: (bi, r, 0)),
        out_shape=jax.ShapeDtypeStruct((b, tall, n), BF16),
        compiler_params=_cparams(("parallel", "parallel")),
        name="proj_gate",
    )(h, w)


def _da_body(safe_ref, qt_ref, k_ref, vt_ref, lp_ref, sg_ref, o_ref, m_ref, acc_ref, *, lam_init, nk):
    j = pl.program_id(3)
    bounded = safe_ref[0] > 0.0

    @pl.when(j == 0)
    def _():
        m_ref[...] = jnp.full(m_ref.shape, -jnp.inf, F32)
        acc_ref[...] = jnp.zeros(acc_ref.shape, F32)

    def scores(a):
        kb = k_ref[0, :, a * LANES:(a + 1) * LANES]
        q_hi = qt_ref[0, a * LANES:a * LANES + HEAD_DIM, :]
        q_lo = qt_ref[0, a * LANES + HEAD_DIM:(a + 1) * LANES, :]
        w = jnp.concatenate([q_hi, q_hi, q_lo, q_lo], axis=0)
        return _dot(jnp.concatenate([kb, kb], axis=1), w)

    def bounded_step():
        vt = vt_ref[0]
        for a in range(2):
            p = jnp.exp2(scores(a) - safe_ref[0])
            acc_ref[a] += _dot(vt, p.astype(BF16))

    def plain_step():
        vt = vt_ref[0]
        for a in range(2):
            s = scores(a)
            m_old = m_ref[a:a + 1, :]
            m_new = jnp.maximum(m_old, jnp.max(s, axis=0, keepdims=True))
            p = jnp.exp2(s - m_new)
            acc_ref[a] = jnp.exp2(m_old - m_new) * acc_ref[a] + _dot(vt, p.astype(BF16))
            m_ref[a:a + 1, :] = m_new

    pl.when(bounded)(bounded_step)
    pl.when(jnp.logical_not(bounded))(plain_step)

    @pl.when(j == nk - 1)
    def _():
        lp = lp_ref[...]
        e1 = jnp.exp(jnp.sum(lp[0:1] * lp[1:2], axis=-1, keepdims=True))
        e2 = jnp.exp(jnp.sum(lp[2:3] * lp[3:4], axis=-1, keepdims=True))
        lam = e1 - e2 + lam_init
        o1 = acc_ref[0, 0:DA_V, :] / acc_ref[0, DA_V:DA_V + 1, :]
        o2 = acc_ref[1, 0:DA_V, :] / acc_ref[1, DA_V:DA_V + 1, :]
        o = o1 - lam * o2
        ms = jnp.mean(o * o, axis=0, keepdims=True)
        y = (o * lax.rsqrt(ms + EPS) * sg_ref[...]) * (1.0 - lam_init)
        o_ref[0] = y.T.astype(BF16)


def _da_body_into(safe_ref, qt_ref, k_ref, vt_ref, lp_ref, sg_ref, y_ref, o_ref, m_ref, acc_ref, **kw):
    del y_ref
    _da_body(safe_ref, qt_ref, k_ref, vt_ref, lp_ref, sg_ref, o_ref, m_ref, acc_ref, **kw)


def _da_call(safe, qt, kk, vt, lp, subg, lam_init, *, q_off, nq, k_off, nk, tq, tk, into=None):
    b, _, tall = qt.shape
    in_specs = [pl.BlockSpec(memory_space=pltpu.SMEM),
                pl.BlockSpec((1, 2 * LANES, tq), lambda bi, h, i, j: (bi, h, i + q_off)),
                pl.BlockSpec((1, tk, 2 * LANES), lambda bi, h, i, j: (bi, j + k_off, h)),
                pl.BlockSpec((1, DA_VROWS, tk), lambda bi, h, i, j: (bi, h, j + k_off)),
                pl.BlockSpec((4, HEAD_DIM), lambda bi, h, i, j: (0, 0)),
                pl.BlockSpec((LANES, 1), lambda bi, h, i, j: (0, 0))]
    args = [safe, qt, kk, vt, lp, subg]
    body, aliases = _da_body, {}
    if into is not None:
        in_specs.append(pl.BlockSpec(memory_space=pl.ANY))
        args.append(into)
        body, aliases = _da_body_into, {len(args) - 1: 0}
    return pl.pallas_call(
        functools.partial(body, lam_init=lam_init, nk=nk),
        grid=(b, DA_HEADS, nq, nk),
        in_specs=in_specs,
        out_specs=pl.BlockSpec((1, tq, LANES), lambda bi, h, i, j: (bi, i + q_off, h)),
        out_shape=jax.ShapeDtypeStruct((b, tall, DA_HEADS * LANES), BF16),
        input_output_aliases=aliases,
        scratch_shapes=[pltpu.VMEM((8, tq), F32), pltpu.VMEM((2, DA_VROWS, tq), F32)],
        compiler_params=_cparams(("parallel", "parallel", "parallel", "arbitrary")),
        name="diff_attn",
    )(*args)


def _gla_body(q_ref, k_ref, v_ref, la_ref, tri_ref, o_ref, s_ref, *, reverse, tb):
    i = pl.program_id(1)

    @pl.when(i == 0)
    def _():
        s_ref[...] = jnp.zeros(s_ref.shape, F32)

    c = GLA_CHUNK
    w = GLA_HEADS * GLA_DK
    nch = tb // c
    order = list(reversed(range(nch))) if reverse else list(range(nch))
    tri = tri_ref[...]
    keep = tri > 0
    cum = _split_dot_left(tri, la_ref[0])
    tot = [cum[ch * c:ch * c + 1, :] if reverse else cum[(ch + 1) * c - 1:(ch + 1) * c, :] for ch in range(nch)]
    tot_rows = jnp.concatenate([jnp.broadcast_to(t, (c, w)) for t in tot], axis=0)
    q = q_ref[0]
    k = k_ref[0]
    qe = q * jnp.exp(cum)
    ke = (k * jnp.exp(-cum)).astype(BF16)
    kd = (k * jnp.exp(tot_rows - cum)).astype(BF16)
    dec = [jnp.exp(t) for t in tot]
    lane = lax.broadcasted_iota(jnp.int32, (1, w), 1)
    heads = range(GLA_HEADS)
    qh = [jnp.where((lane >= hd * GLA_DK) & (lane < (hd + 1) * GLA_DK), qe, 0.0).astype(BF16) for hd in heads]
    vh = [v_ref[0, :, hd * GLA_DV:(hd + 1) * GLA_DV] for hd in heads]
    inc = [[_dot_tn(vh[hd][ch * c:(ch + 1) * c], kd[ch * c:(ch + 1) * c]) for ch in range(nch)] for hd in heads]
    a = [jnp.where(keep, _dot_nt(qh[hd], ke), 0.0).astype(BF16) for hd in heads]
    entry = []
    for hd in heads:
        st = s_ref[hd]
        seen = {}
        for ch in order:
            seen[ch] = st.astype(BF16)
            st = st * dec[ch] + inc[hd][ch]
        s_ref[hd] = st
        entry.append(seen)
    o_intra = [_dot(a[hd], vh[hd]) for hd in heads]
    for hd in heads:
        for ch in range(nch):
            rows = slice(ch * c, (ch + 1) * c)
            o_ref[0, rows, hd * GLA_DV:(hd + 1) * GLA_DV] = o_intra[hd][rows] + _dot_nt(qh[hd][rows], entry[hd][ch])


def _gla_call(gq, gk, gv, la, tri, *, t_lat, reverse):
    b, tall, w = gq.shape
    tb = GLA_BLOCK
    n_lat = t_lat // tb
    nblk = tall // tb
    if reverse:
        blk = lambda i: jnp.where(i == 0, n_lat, n_lat - i)
    else:
        blk = lambda i: jnp.where(i == 0, n_lat, i - 1)
    wv_ = GLA_HEADS * GLA_DV
    return pl.pallas_call(
        functools.partial(_gla_body, reverse=reverse, tb=tb),
        grid=(b, nblk),
        in_specs=[pl.BlockSpec((1, tb, w), lambda bi, i: (bi, blk(i), 0)),
                  pl.BlockSpec((1, tb, w), lambda bi, i: (bi, blk(i), 0)),
                  pl.BlockSpec((1, tb, wv_), lambda bi, i: (bi, blk(i), 0)),
                  pl.BlockSpec((1, tb, w), lambda bi, i: (bi, blk(i), 1 if reverse else 0)),
                  pl.BlockSpec((tb, tb), lambda bi, i: (0, 0))],
        out_specs=pl.BlockSpec((1, tb, wv_), lambda bi, i: (bi, blk(i), 0)),
        out_shape=jax.ShapeDtypeStruct((b, tall, wv_), F32),
        scratch_shapes=[pltpu.VMEM((GLA_HEADS, GLA_DV, w), F32)],
        compiler_params=_cparams(("parallel", "arbitrary")),
        name="gla_bwd" if reverse else "gla_fwd",
    )(gq, gk, gv, la, tri)


def _na_body(safe_ref, q_ref, k_ref, v_ref, kc_ref, vc_ref, bias_ref, o_ref, *, rows):
    i = pl.program_id(2)
    kb0 = jnp.clip(i * NA_QROWS - NA_KR // 2, 0, rows - NA_BAND)
    start = pl.multiple_of(kb0 * GRID_W, GRID_W)
    nband = NA_BAND * GRID_W

    def attend(fixed_ref):
        q = q_ref[0]
        kb = k_ref[0, pl.ds(start, nband), :]
        kc = kc_ref[0]
        vall = jnp.concatenate([v_ref[0, pl.ds(start, nband), :], vc_ref[0]], axis=0)
        klane = lax.broadcasted_iota(jnp.int32, vall.shape, 1)
        vaug = jnp.concatenate([vall, (klane == 0).astype(BF16)], axis=1)
        lane = lax.broadcasted_iota(jnp.int32, q.shape, 1)
        qh = [jnp.where((lane < HEAD_DIM) if hh == 0 else (lane >= HEAD_DIM), q, jnp.zeros_like(q))
              for hh in range(2)]
        s_loc = [_dot_nt(qh[hh], kb) for hh in range(2)]
        s_ctx = [_dot_nt(qh[hh], kc) for hh in range(2)]
        p = []
        for hh in range(2):
            sl = s_loc[hh] + bias_ref[0, hh].astype(F32)
            if fixed_ref:
                m = safe_ref[0]
            else:
                m = jnp.maximum(jnp.max(sl, axis=-1, keepdims=True), jnp.max(s_ctx[hh], axis=-1, keepdims=True))
            p.append(jnp.concatenate([jnp.exp2(sl - m), jnp.exp2(s_ctx[hh] - m)], axis=1).astype(BF16))
        acc = [_dot(p[hh], vaug) for hh in range(2)]
        outs = [acc[hh][:, 0:LANES] / acc[hh][:, LANES:LANES + 1] for hh in range(2)]
        o_ref[0] = jnp.where(lane < HEAD_DIM, outs[0], outs[1]).astype(BF16)

    fixed = safe_ref[0] > 0.0
    pl.when(fixed)(functools.partial(attend, True))
    pl.when(jnp.logical_not(fixed))(functools.partial(attend, False))


def _na_call(safe, nq, nk, nv, bias, *, t_lat):
    b, tall, w = nq.shape
    tc = tall - t_lat
    rows = t_lat // GRID_W
    tq = NA_QROWS * GRID_W
    nsteps = rows // NA_QROWS
    npair = w // LANES
    ctx_blk = t_lat // tc

    step_types, _ = _na_block_types(rows)
    common = max(set(step_types), key=step_types.count)

    def btype(i):
        t = common
        for step, kind in enumerate(step_types):
            if kind != common:
                t = jnp.where(i == step, kind, t)
        return t

    return pl.pallas_call(
        functools.partial(_na_body, rows=rows),
        grid=(b, npair, nsteps),
        in_specs=[pl.BlockSpec(memory_space=pltpu.SMEM),
                  pl.BlockSpec((1, tq, LANES), lambda bi, hp, i: (bi, i, hp)),
                  pl.BlockSpec((1, t_lat, LANES), lambda bi, hp, i: (bi, 0, hp)),
                  pl.BlockSpec((1, t_lat, LANES), lambda bi, hp, i: (bi, 0, hp)),
                  pl.BlockSpec((1, tc, LANES), lambda bi, hp, i: (bi, ctx_blk, hp)),
                  pl.BlockSpec((1, tc, LANES), lambda bi, hp, i: (bi, ctx_blk, hp)),
                  pl.BlockSpec((1, 2, tq, NA_BAND * GRID_W), lambda bi, hp, i: (btype(i), hp, 0, 0))],
        out_specs=pl.BlockSpec((1, tq, LANES), lambda bi, hp, i: (bi, i, hp)),
        out_shape=jax.ShapeDtypeStruct((b, tall, w), BF16),
        compiler_params=_cparams(("parallel", "parallel", "arbitrary")),
        name="nbr_attn",
    )(safe, nq, nk, nv, nk, nv, bias)


def _na_block_types(rows):
    kinds, reps, step_types = {}, [], []
    for r0 in range(0, rows, NA_QROWS):
        kb0 = min(max(r0 - NA_KR // 2, 0), rows - NA_BAND)
        r = r0 + np.arange(NA_QROWS)
        sig = (kb0 - r0, tuple(np.clip(r - NA_KR // 2, 0, rows - NA_KR) - r))
        if sig not in kinds:
            kinds[sig] = len(reps)
            reps.append(r0)
        step_types.append(kinds[sig])
    return step_types, reps


def _na_bias_tiles(rpb, rows):
    assert rows >= NA_BAND and rows % NA_QROWS == 0 and NA_QROWS + NA_KR - 1 <= NA_BAND
    nl, nh, na, nb = rpb.shape
    cidx = np.arange(GRID_W)
    rel_c = cidx[None, :] - cidx[:, None] + NA_KC - 1
    sel = jnp.asarray(rel_c[None] == np.arange(nb)[:, None, None], F32)
    toep = jnp.einsum('lhab,bqk->lhqak', rpb.astype(F32) * LOG2E, sel, precision=lax.Precision.HIGHEST)
    toep = jnp.pad(toep.astype(BF16), ((0, 0), (0, 0), (0, 0), (NA_BAND, NA_BAND), (0, 0)))
    tiles = []
    for r0 in _na_block_types(rows)[1]:
        kb0 = min(max(r0 - NA_KR // 2, 0), rows - NA_BAND)
        qi = np.arange(NA_QROWS * GRID_W)
        qr, qc = r0 + qi // GRID_W, qi % GRID_W
        kj = np.arange(NA_BAND * GRID_W)
        kr, kc = kb0 + kj // GRID_W, kj % GRID_W
        rs = np.clip(qr - NA_KR // 2, 0, rows - NA_KR)
        cs = np.clip(qc - NA_KC // 2, 0, GRID_W - NA_KC)
        valid = ((kr[None, :] >= rs[:, None]) & (kr[None, :] < rs[:, None] + NA_KR)
                 & (kc[None, :] >= cs[:, None]) & (kc[None, :] < cs[:, None] + NA_KC))
        parts = []
        for q_row in range(NA_QROWS):
            a0 = kb0 - (r0 + q_row) + NA_KR - 1 + NA_BAND
            blk = toep[:, :, :, a0:a0 + NA_BAND, :]
            parts.append(blk.reshape(nl, nh, GRID_W, NA_BAND * GRID_W))
        tile = jnp.concatenate(parts, axis=2)
        tiles.append(jnp.where(jnp.asarray(valid)[None, None], tile, NEG_BIG))
    return jnp.stack(tiles, axis=1)


def _ctx_attn_body(q_ref, k_ref, v_ref, y_ref, o_ref):
    del y_ref
    q = q_ref[0]
    k = k_ref[0]
    v = v_ref[0]
    lane = lax.broadcasted_iota(jnp.int32, q.shape, 1)
    outs = []
    for hh in range(2):
        qh = jnp.where((lane < HEAD_DIM) if hh == 0 else (lane >= HEAD_DIM), q, jnp.zeros_like(q))
        s = _dot_nt(qh, k)
        m = jnp.max(s, axis=-1, keepdims=True)
        p = jnp.exp2(s - m)
        outs.append(_dot(p.astype(BF16), v) / jnp.sum(p, axis=-1, keepdims=True))
    o_ref[0] = jnp.where(lane < HEAD_DIM, outs[0], outs[1]).astype(BF16)


def _ctx_attn_call(nq, nk, nv, into, *, t_lat):
    b, tall, w = nq.shape
    tc = tall - t_lat
    ctx_blk = t_lat // tc
    spec = pl.BlockSpec((1, tc, LANES), lambda bi, hp: (bi, ctx_blk, hp))
    return pl.pallas_call(
        _ctx_attn_body,
        grid=(b, w // LANES),
        in_specs=[spec, spec, spec, pl.BlockSpec(memory_space=pl.ANY)],
        out_specs=spec,
        out_shape=jax.ShapeDtypeStruct((b, tall, w), BF16),
        input_output_aliases={3: 0},
        compiler_params=_cparams(("parallel", "parallel")),
        name="ctx_attn",
    )(nq, nk, nv, into)


def _merge_body(x_ref, yd_ref, of_ref, ob_ref, sg_ref, yn_ref, gt_ref, wd_ref, wg_ref, wn_ref, wo_ref,
                gn_ref, mb_ref, mc_ref, o_ref, *, t_lat, tm, d):
    r = pl.program_id(1)
    ctx = _is_ctx_rows(r, tm, t_lat)
    g1 = jnp.where(ctx, mc_ref[0, :, 2 * d:3 * d], mb_ref[0, :, 2 * d:3 * d])
    og = of_ref[0] + ob_ref[0]
    parts = []
    for hd in range(GLA_HEADS):
        oh = og[:, hd * GLA_DV:(hd + 1) * GLA_DV]
        ms = jnp.mean(oh * oh, axis=-1, keepdims=True)
        parts.append(oh * lax.rsqrt(ms + EPS) * gn_ref[...])
    yg = (jnp.concatenate(parts, axis=-1) * sg_ref[0].astype(F32)).astype(BF16)
    m = (gt_ref[0, :, 0:d].astype(F32) * _dot(yd_ref[0], wd_ref[...])
         + gt_ref[0, :, d:2 * d].astype(F32) * _dot(yg, wg_ref[...])
         + gt_ref[0, :, 2 * d:3 * d].astype(F32) * _dot(yn_ref[0], wn_ref[...]))
    o_ref[0] = x_ref[0] + g1 * _dot(m.astype(BF16), wo_ref[...])


def _merge_call(xall, yd, of, ob, sg, yn, gates, wd, wg, wn, wo, gn, mod3, t_lat, *, lat_only):
    b, tall, d = xall.shape
    tm = next(t for t in (512, 256) if t_lat % t == 0) if lat_only else _row_tile(tall)
    n_rows = t_lat if lat_only else tall
    nb = b
    const = lambda bi, r: (0, 0)
    row = lambda n: pl.BlockSpec((1, tm, n), lambda bi, r: (bi, r, 0))
    return pl.pallas_call(
        functools.partial(_merge_body, t_lat=t_lat, tm=tm, d=d),
        grid=(b, n_rows // tm),
        in_specs=[row(d), row(BR_W), row(BR_W), row(BR_W), row(BR_W), row(BR_W), row(3 * d),
                  pl.BlockSpec((BR_W, d), const), pl.BlockSpec((BR_W, d), const), pl.BlockSpec((BR_W, d), const),
                  pl.BlockSpec((d, d), const), pl.BlockSpec((1, GLA_DV), const),
                  pl.BlockSpec((1, 1, 6 * d), lambda bi, r: (bi, 0, 0)),
                  pl.BlockSpec((1, 1, 6 * d), lambda bi, r: (nb, 0, 0))],
        out_specs=row(d),
        out_shape=jax.ShapeDtypeStruct((b, n_rows, d), F32),
        compiler_params=_cparams(("parallel", "parallel")),
        name="merge",
    )(xall, yd, of, ob, sg, yn, gates, wd, wg, wn, wo, gn, mod3, mod3)


def _mlp_body(x_ref, g_ref, w1_ref, w2_ref, mb_ref, mc_ref, o_ref, *, t_lat, tm, d):
    r = pl.program_id(1)
    ctx = _is_ctx_rows(r, tm, t_lat)
    sh = jnp.where(ctx, mc_ref[0, :, 3 * d:4 * d], mb_ref[0, :, 3 * d:4 * d])
    sc = jnp.where(ctx, mc_ref[0, :, 4 * d:5 * d], mb_ref[0, :, 4 * d:5 * d])
    g2 = jnp.where(ctx, mc_ref[0, :, 5 * d:6 * d], mb_ref[0, :, 5 * d:6 * d])
    x = x_ref[0]
    h = _norm_mod(x, g_ref[...], sc, sh).astype(BF16)
    acc = jnp.zeros((tm, d), F32)
    step = 1024
    for j in range(w1_ref.shape[1] // step):
        a = jnp.maximum(_dot(h, w1_ref[:, j * step:(j + 1) * step]), 0.0)
        acc = acc + _dot((a * a).astype(BF16), w2_ref[j * step:(j + 1) * step, :])
    o_ref[0] = x + g2 * acc


def _mlp_call(xall, g, w1, w2, mod3, t_lat, *, lat_only):
    b, tall, d = xall.shape
    tm = next(t for t in (512, 256) if t_lat % t == 0) if lat_only else _row_tile(tall)
    n_rows = t_lat if lat_only else tall
    nb = b
    dff = w1.shape[1]
    const = lambda bi, r: (0, 0)
    return pl.pallas_call(
        functools.partial(_mlp_body, t_lat=t_lat, tm=tm, d=d),
        grid=(b, n_rows // tm),
        in_specs=[pl.BlockSpec((1, tm, d), lambda bi, r: (bi, r, 0)),
                  pl.BlockSpec((1, d), const),
                  pl.BlockSpec((d, dff), const, pipeline_mode=pl.Buffered(1)),
                  pl.BlockSpec((dff, d), const, pipeline_mode=pl.Buffered(1)),
                  pl.BlockSpec((1, 1, 6 * d), lambda bi, r: (bi, 0, 0)),
                  pl.BlockSpec((1, 1, 6 * d), lambda bi, r: (nb, 0, 0))],
        out_specs=pl.BlockSpec((1, tm, d), lambda bi, r: (bi, r, 0)),
        out_shape=jax.ShapeDtypeStruct((b, n_rows, d), F32),
        compiler_params=_cparams(("parallel", "parallel")),
        name="mlp",
    )(xall, g.reshape(1, d), w1, w2, mod3, mod3)


def kernel(x, c, ctx, c_ctx, w_mod, b_mod, norm1_g, norm2_g, w_in, da_qn_g, da_kn_g, da_lambda, da_subln_g,
           gla_a2, gla_a_b, gla_gn_g, na_qn_g, na_kn_g, na_rpb, w_br_da, w_br_gla, w_br_na, w_out, w_ff1, w_ff2):
    b, t_lat, d = x.shape
    tc = ctx.shape[1]
    tall = t_lat + tc
    depth = w_mod.shape[0]
    rows = t_lat // GRID_W
    assert d == D_MODEL and t_lat % (2 * tc) == 0 and tc == 256 and b < 8

    xall = jnp.concatenate([x, ctx], axis=1)
    cvec = jnp.zeros((8, d), F32).at[0:b].set(c).at[b].set(c_ctx)
    mod = _mod_call(cvec, w_mod, b_mod)
    cos_t, sin_t = _rope_tables(t_lat, tall)

    seg = np.arange(NORM_LANES) // HEAD_DIM
    bd = jnp.asarray(seg[:, None] == seg[None, :], BF16)
    ci = np.arange(GLA_BLOCK)
    same_chunk = (ci[None, :] // GLA_CHUNK) == (ci[:, None] // GLA_CHUNK)
    tri_f = jnp.asarray(same_chunk & (ci[None, :] <= ci[:, None]), BF16)
    tri_b = jnp.asarray(same_chunk & (ci[None, :] >= ci[:, None]), BF16)

    o_dq, o_dk, o_dv = 0, 512, 1024
    o_gq, o_gk, o_gv, o_gg, o_ga = 1536, 1792, 2048, 2560, 3072
    o_nq, o_nk, o_nv = 3104, 3616, 4128
    o_gate = 4640
    tq_da = next(t for t in (2048, 1024, 512, 256) if t_lat % t == 0)
    tk_da = next(t for t in (3328, 1280, 256) if tall % t == 0)
    na_bias = _na_bias_tiles(na_rpb, rows)

    for l in range(depth):
        need_ctx = l < depth - 1
        lam_init = 0.8 - 0.6 * math.exp(-0.3 * l)
        mod3 = mod[l].reshape(8, 1, 6 * d)
        wl = w_in[l].astype(BF16)
        cut = lambda a, n: wl[:, a:a + n]
        wa = jnp.concatenate([cut(o_ga, 2 * GLA_RANK), jnp.zeros((d, LANES - 2 * GLA_RANK), BF16)], axis=1)
        a2p = jnp.zeros((LANES, 2 * GLA_HEADS * GLA_DK), F32)
        a2p = a2p.at[0:GLA_RANK, 0:256].set(gla_a2[l, 0]).at[GLA_RANK:2 * GLA_RANK, 256:512].set(gla_a2[l, 1])
        ab = gla_a_b[l].reshape(1, 2 * GLA_HEADS * GLA_DK)
        tile2 = lambda g: jnp.tile(g, NORM_LANES // HEAD_DIM).reshape(1, NORM_LANES)

        h = _hnorm_call(xall, norm1_g[l], mod3, t_lat)
        qt, kk, vt = _proj_da_call(h, cut(o_dq, 512), cut(o_dk, 512), cut(o_dv, 512), bd,
                                   tile2(da_qn_g[l]), tile2(da_kn_g[l]), cos_t, sin_t)
        gq, gk, gv, sgg, la = _proj_gla_call(h, cut(o_gq, 256), cut(o_gk, 256), cut(o_gv, 512), cut(o_gg, 512),
                                             wa, a2p.astype(BF16), ab)
        nq, nk, nv = _proj_na_call(h, cut(o_nq, 512), cut(o_nk, 512), cut(o_nv, 512), bd,
                                   tile2(na_qn_g[l]), tile2(na_kn_g[l]))
        def score_bound(gq_, gk_, extra):
            bnd = (math.sqrt(HEAD_DIM) * LOG2E * 1.01) * jnp.max(jnp.abs(gq_)) * jnp.max(jnp.abs(gk_)) + extra + 1e-3
            return jnp.where(bnd <= DA_SAFE_LOG2, bnd, -1.0).astype(F32).reshape(1)
        safe = score_bound(da_qn_g[l], da_kn_g[l], 0.0)
        safe_na = score_bound(na_qn_g[l], na_kn_g[l], jnp.max(jnp.abs(na_rpb[l])) * LOG2E)
        gates = _proj_gate_call(h, cut(o_gate, 3 * d))

        subg = da_subln_g[l].reshape(LANES, 1)
        y_da = _da_call(safe, qt, kk, vt, da_lambda[l], subg, lam_init,
                        q_off=0, nq=t_lat // tq_da, k_off=0, nk=tall // tk_da, tq=tq_da, tk=tk_da)
        o_f = _gla_call(gq, gk, gv, la, tri_f, t_lat=t_lat, reverse=False)
        o_b = _gla_call(gq, gk, gv, la, tri_b, t_lat=t_lat, reverse=True)
        y_na = _na_call(safe_na, nq, nk, nv, na_bias[l], t_lat=t_lat)
        if need_ctx:
            y_da = _da_call(safe, qt, kk, vt, da_lambda[l], subg, lam_init,
                            q_off=t_lat // tc, nq=1, k_off=t_lat // tc, nk=1, tq=tc, tk=tc, into=y_da)
            y_na = _ctx_attn_call(nq, nk, nv, y_na, t_lat=t_lat)

        xall = _merge_call(xall, y_da, o_f, o_b, sgg, y_na, gates,
                           w_br_da[l].astype(BF16), w_br_gla[l].astype(BF16), w_br_na[l].astype(BF16),
                           w_out[l].astype(BF16), gla_gn_g[l].reshape(1, GLA_DV), mod3, t_lat,
                           lat_only=not need_ctx)
        xall = _mlp_call(xall, norm2_g[l], w_ff1[l].astype(BF16), w_ff2[l].astype(BF16), mod3, t_lat,
                         lat_only=not need_ctx)
    return xall
```

```python
import functools
import math

import numpy as np
import jax
import jax.numpy as jnp
from jax import lax
from jax.experimental import pallas as pl
from jax.experimental.pallas import tpu as pltpu

F32 = jnp.float32
BF16 = jnp.bfloat16
F8 = jnp.float8_e4m3fn

D_MODEL = 1024
GRID_W = 64
HEAD_DIM = 64
EPS = 1e-6
ROPE_BASE = 10000.0
ROPE_PAIRS = HEAD_DIM // 4
DA_HEADS = 4
GLA_HEADS = 4
GLA_DK = 64
GLA_DV = 128
GLA_RANK = 16
GLA_TAU = 16.0
GLA_CHUNK = 64
NA_HEADS = 8
NA_KR = 8
NA_KC = 16
BR_W = 512
D_FF = 4 * D_MODEL

DA_V = 2 * HEAD_DIM
LOG2E = math.log2(math.e)
DA_SAFE_LOG2 = 45.0

GLA_BLOCK = 256
NORM_LANES = 256
LANES = 128
NA_QROWS = 8
NA_BAND = 16
NEG_BIG = -1e30
VMEM_LIMIT = 48 * 1024 * 1024


def _cparams(sem):
    return pltpu.CompilerParams(dimension_semantics=sem, vmem_limit_bytes=VMEM_LIMIT)


def _dot(a, b):
    return jnp.dot(a, b, preferred_element_type=F32)


def _dot_nt(a, b):
    return lax.dot_general(a, b, (((1,), (1,)), ((), ())), preferred_element_type=F32)


def _dot_tn(a, b):
    return lax.dot_general(a, b, (((0,), (0,)), ((), ())), preferred_element_type=F32)


def _split_dot(x, w_exact):
    hi = x.astype(BF16)
    lo = (x - hi.astype(F32)).astype(BF16)
    return _dot(hi, w_exact) + _dot(lo, w_exact)


def _split_dot_left(w_exact, x):
    hi = x.astype(BF16)
    lo = (x - hi.astype(F32)).astype(BF16)
    return _dot(w_exact, hi) + _dot(w_exact, lo)


def _sigmoid(x):
    return 1.0 / (1.0 + jnp.exp(-x))


def _row_tile(tall):
    for tm in (640, 256):
        if tall % tm == 0:
            return tm
    raise ValueError(f"unsupported token count {tall}")


def _mod_body(c_ref, w_ref, b_ref, o_ref):
    cv = c_ref[...]
    s = cv * _sigmoid(cv)
    o_ref[0] = _dot(s.astype(BF16), w_ref[0].astype(BF16)) + b_ref[0]


def _mod_call(cvec, w_mod, b_mod):
    depth, d, n = w_mod.shape
    tn = 1536
    return pl.pallas_call(
        _mod_body,
        grid=(depth, n // tn),
        in_specs=[pl.BlockSpec((8, d), lambda l, j: (0, 0)),
                  pl.BlockSpec((1, d, tn), lambda l, j: (l, 0, j)),
                  pl.BlockSpec((1, 1, tn), lambda l, j: (l, 0, j))],
        out_specs=pl.BlockSpec((1, 8, tn), lambda l, j: (l, 0, j)),
        out_shape=jax.ShapeDtypeStruct((depth, 8, n), F32),
        compiler_params=_cparams(("parallel", "parallel")),
        name="mod",
    )(cvec, w_mod, b_mod.reshape(depth, 1, n))


def _rope_body(freq_ref, cos_ref, sin_ref, *, t_lat, tm):
    i = pl.program_id(0)
    t = i * tm + lax.broadcasted_iota(jnp.int32, (tm, LANES), 0)
    lane = lax.broadcasted_iota(jnp.int32, (tm, LANES), 1)
    shift = int(math.log2(GRID_W))
    row = lax.shift_right_logical(t, shift).astype(F32)
    col = jnp.bitwise_and(t, GRID_W - 1).astype(F32)
    use_row = jnp.bitwise_and(lane, 2 * ROPE_PAIRS) == 0
    first_half = jnp.bitwise_and(lane, ROPE_PAIRS) == 0
    ang = jnp.where(use_row, row, col) * freq_ref[...]
    is_lat = t < t_lat
    cos_ref[...] = jnp.where(is_lat, jnp.cos(ang), 1.0)
    sn = jnp.sin(ang)
    sin_ref[...] = jnp.where(is_lat, jnp.where(first_half, -sn, sn), 0.0)


def _rope_tables(t_lat, tall):
    tm = 256
    freqs = ROPE_BASE ** (-jnp.arange(ROPE_PAIRS, dtype=F32) / ROPE_PAIRS)
    freq_lane = jnp.tile(freqs, LANES // ROPE_PAIRS).reshape(1, LANES)
    return pl.pallas_call(
        functools.partial(_rope_body, t_lat=t_lat, tm=tm),
        grid=(tall // tm,),
        in_specs=[pl.BlockSpec((1, LANES), lambda i: (0, 0))],
        out_specs=[pl.BlockSpec((tm, LANES), lambda i: (i, 0))] * 2,
        out_shape=[jax.ShapeDtypeStruct((tall, LANES), F32)] * 2,
        compiler_params=_cparams(("parallel",)),
        name="rope_tables",
    )(freq_lane)


def _is_ctx_rows(r, tm, t_lat):
    t = r * tm + lax.broadcasted_iota(jnp.int32, (tm, 1), 0)
    return t >= t_lat


def _norm_mod(x, g, sc, sh):
    ms = jnp.mean(x * x, axis=-1, keepdims=True)
    return (x * lax.rsqrt(ms + EPS) * g) * (1.0 + sc) + sh


def _segnorm64(z, bd, gain):
    ss = _split_dot(z * z, bd)
    return z * lax.rsqrt(ss * (1.0 / HEAD_DIM) + EPS) * gain


def _hnorm_body(x_ref, g_ref, mb_ref, mc_ref, h_ref, *, t_lat, tm, d):
    r = pl.program_id(1)
    ctx = _is_ctx_rows(r, tm, t_lat)
    sh = jnp.where(ctx, mc_ref[0, :, 0:d], mb_ref[0, :, 0:d])
    sc = jnp.where(ctx, mc_ref[0, :, d:2 * d], mb_ref[0, :, d:2 * d])
    h_ref[0] = _norm_mod(x_ref[0], g_ref[...], sc, sh).astype(BF16)


def _hnorm_call(xall, g, mod3, t_lat):
    b, tall, d = xall.shape
    tm = _row_tile(tall)
    nb = b
    return pl.pallas_call(
        functools.partial(_hnorm_body, t_lat=t_lat, tm=tm, d=d),
        grid=(b, tall // tm),
        in_specs=[pl.BlockSpec((1, tm, d), lambda bi, r: (bi, r, 0)),
                  pl.BlockSpec((1, d), lambda bi, r: (0, 0)),
                  pl.BlockSpec((1, 1, 6 * d), lambda bi, r: (bi, 0, 0)),
                  pl.BlockSpec((1, 1, 6 * d), lambda bi, r: (nb, 0, 0))],
        out_specs=pl.BlockSpec((1, tm, d), lambda bi, r: (bi, r, 0)),
        out_shape=jax.ShapeDtypeStruct((b, tall, d), BF16),
        compiler_params=_cparams(("parallel", "parallel")),
        name="hnorm",
    )(xall, g.reshape(1, d), mod3, mod3)


def _rope128(x, cs, sn, first_half):
    partner = jnp.where(first_half, pltpu.roll(x, x.shape[1] - ROPE_PAIRS, 1), pltpu.roll(x, ROPE_PAIRS, 1))
    return x * cs + partner * sn


def _proj_da_body(h_ref, wq_ref, wk_ref, wv_ref, bd_ref, gq_ref, gk_ref, cos_ref, sin_ref,
                  qt_ref, k_ref, vt_ref):
    h = h_ref[0]
    tm = h.shape[0]
    bd = bd_ref[...]
    wide = bd.shape[0]
    cs = jnp.concatenate([cos_ref[...]] * (wide // LANES), axis=1)
    sn = jnp.concatenate([sin_ref[...]] * (wide // LANES), axis=1)
    lane = lax.broadcasted_iota(jnp.int32, cs.shape, 1)
    first_half = jnp.bitwise_and(lane, ROPE_PAIRS) == 0
    low_lanes = jnp.bitwise_and(lane, HEAD_DIM) == 0
    zq = _dot(h, wq_ref[...])
    zk = _dot(h, wk_ref[...])
    zv = _dot(h, wv_ref[...])
    scale = math.sqrt(HEAD_DIM ** -0.5 * LOG2E)
    for j in range(zq.shape[1] // wide):
        sl = slice(j * wide, (j + 1) * wide)
        q = _rope128(_segnorm64(zq[:, sl], bd, gq_ref[...]) * scale, cs, sn, first_half)
        k = _rope128(_segnorm64(zk[:, sl], bd, gk_ref[...]) * scale, cs, sn, first_half)
        q_t = q.T
        qh_t = q_t.astype(F8).astype(F32)
        ql_t = q_t - qh_t
        k_hi = k.astype(F8).astype(F32)
        k_lo = k - k_hi
        k_map1 = jnp.where(low_lanes, k_hi, pltpu.roll(k_lo, HEAD_DIM, 1)).astype(F8)
        k_map2 = jnp.where(low_lanes, pltpu.roll(k_hi, wide - HEAD_DIM, 1), k_lo).astype(F8)
        for hh in range(wide // LANES):
            hd = j * (wide // LANES) + hh
            r0 = hd * 2 * LANES
            for a in range(2):
                src = slice(hh * LANES + a * HEAD_DIM, hh * LANES + (a + 1) * HEAD_DIM)
                qt_ref[0, r0 + a * LANES:r0 + a * LANES + HEAD_DIM, :] = qh_t[src].astype(F8)
                qt_ref[0, r0 + a * LANES + HEAD_DIM:r0 + (a + 1) * LANES, :] = ql_t[src].astype(F8)
            k_ref[0, :, r0:r0 + LANES] = k_map1[:, hh * LANES:(hh + 1) * LANES]
            k_ref[0, :, r0 + LANES:r0 + 2 * LANES] = k_map2[:, hh * LANES:(hh + 1) * LANES]
            vt_ref[0, hd * DA_V:(hd + 1) * DA_V, :] = zv[:, hd * LANES:(hd + 1) * LANES].T.astype(BF16)


def _proj_da_call(h, wq, wk, wv, bd, gq, gk, cos_t, sin_t):
    b, tall, d = h.shape
    tm = _row_tile(tall)
    w = DA_HEADS * LANES
    nbd = bd.shape[0]
    const = lambda bi, r: (0, 0)
    return pl.pallas_call(
        _proj_da_body,
        grid=(b, tall // tm),
        in_specs=[pl.BlockSpec((1, tm, d), lambda bi, r: (bi, r, 0)),
                  pl.BlockSpec((d, w), const), pl.BlockSpec((d, w), const), pl.BlockSpec((d, w), const),
                  pl.BlockSpec((nbd, nbd), const),
                  pl.BlockSpec((1, nbd), const), pl.BlockSpec((1, nbd), const),
                  pl.BlockSpec((tm, LANES), lambda bi, r: (r, 0)),
                  pl.BlockSpec((tm, LANES), lambda bi, r: (r, 0))],
        out_specs=[pl.BlockSpec((1, 2 * w, tm), lambda bi, r: (bi, 0, r)),
                   pl.BlockSpec((1, tm, 2 * w), lambda bi, r: (bi, r, 0)),
                   pl.BlockSpec((1, DA_HEADS * DA_V, tm), lambda bi, r: (bi, 0, r))],
        out_shape=[jax.ShapeDtypeStruct((b, 2 * w, tall), F8),
                   jax.ShapeDtypeStruct((b, tall, 2 * w), F8),
                   jax.ShapeDtypeStruct((b, DA_HEADS * DA_V, tall), BF16)],
        compiler_params=_cparams(("parallel", "parallel")),
        name="proj_da",
    )(h, wq, wk, wv, bd, gq, gk, cos_t, sin_t)


def _proj_gla_body(h_ref, wq_ref, wk_ref, wv_ref, wg_ref, wa_ref, a2_ref, ab_ref,
                   q_ref, k_ref, v_ref, sg_ref, la_ref):
    h = h_ref[0]
    q_ref[0] = _dot(h, wq_ref[...]) * (GLA_DK ** -0.5)
    k_ref[0] = _dot(h, wk_ref[...])
    v_ref[0] = _dot(h, wv_ref[...]).astype(BF16)
    g = _dot(h, wg_ref[...])
    sg_ref[0] = (g * _sigmoid(g)).astype(BF16)
    ga = _dot(h, wa_ref[...])
    z = _dot(ga.astype(BF16), a2_ref[...]) + ab_ref[...]
    la_ref[0] = (jnp.minimum(z, 0.0) - jnp.log1p(jnp.exp(-jnp.abs(z)))) * (1.0 / GLA_TAU)


def _proj_gla_call(h, wq, wk, wv, wg, wa, a2p, ab):
    b, tall, d = h.shape
    tm = _row_tile(tall)
    wqk = GLA_HEADS * GLA_DK
    wv_ = GLA_HEADS * GLA_DV
    const = lambda bi, r: (0, 0)
    row = lambda n: pl.BlockSpec((1, tm, n), lambda bi, r: (bi, r, 0))
    return pl.pallas_call(
        _proj_gla_body,
        grid=(b, tall // tm),
        in_specs=[row(d),
                  pl.BlockSpec((d, wqk), const), pl.BlockSpec((d, wqk), const),
                  pl.BlockSpec((d, wv_), const), pl.BlockSpec((d, wv_), const),
                  pl.BlockSpec((d, LANES), const),
                  pl.BlockSpec((LANES, 2 * wqk), const), pl.BlockSpec((1, 2 * wqk), const)],
        out_specs=[row(wqk), row(wqk), row(wv_), row(wv_), row(2 * wqk)],
        out_shape=[jax.ShapeDtypeStruct((b, tall, wqk), F32),
                   jax.ShapeDtypeStruct((b, tall, wqk), F32),
                   jax.ShapeDtypeStruct((b, tall, wv_), BF16),
                   jax.ShapeDtypeStruct((b, tall, wv_), BF16),
                   jax.ShapeDtypeStruct((b, tall, 2 * wqk), F32)],
        compiler_params=_cparams(("parallel", "parallel")),
        name="proj_gla",
    )(h, wq, wk, wv, wg, wa, a2p, ab)


def _proj_na_body(h_ref, wq_ref, wk_ref, wv_ref, bd_ref, gq_ref, gk_ref, q_ref, k_ref, v_ref):
    h = h_ref[0]
    bd = bd_ref[...]
    wide = bd.shape[0]
    zq = _dot(h, wq_ref[...])
    zk = _dot(h, wk_ref[...])
    scale = HEAD_DIM ** -0.5 * LOG2E
    for j in range(NA_HEADS * HEAD_DIM // wide):
        sl = slice(j * wide, (j + 1) * wide)
        q_ref[0, :, sl] = (_segnorm64(zq[:, sl], bd, gq_ref[...]) * scale).astype(BF16)
        k_ref[0, :, sl] = _segnorm64(zk[:, sl], bd, gk_ref[...]).astype(BF16)
    v_ref[0] = _dot(h, wv_ref[...]).astype(BF16)


def _proj_na_call(h, wq, wk, wv, bd, gq, gk):
    b, tall, d = h.shape
    tm = _row_tile(tall)
    w = NA_HEADS * HEAD_DIM
    nbd = bd.shape[0]
    const = lambda bi, r: (0, 0)
    row = lambda n: pl.BlockSpec((1, tm, n), lambda bi, r: (bi, r, 0))
    return pl.pallas_call(
        _proj_na_body,
        grid=(b, tall // tm),
        in_specs=[row(d), pl.BlockSpec((d, w), const), pl.BlockSpec((d, w), const), pl.BlockSpec((d, w), const),
                  pl.BlockSpec((nbd, nbd), const),
                  pl.BlockSpec((1, nbd), const), pl.BlockSpec((1, nbd), const)],
        out_specs=[row(w), row(w), row(w)],
        out_shape=[jax.ShapeDtypeStruct((b, tall, w), BF16)] * 3,
        compiler_params=_cparams(("parallel", "parallel")),
        name="proj_na",
    )(h, wq, wk, wv, bd, gq, gk)


def _proj_gate_body(h_ref, w_ref, o_ref):
    h = h_ref[0]
    n = w_ref.shape[1]
    step = 512
    for j in range(n // step):
        sl = slice(j * step, (j + 1) * step)
        o_ref[0, :, sl] = _sigmoid(_dot(h, w_ref[:, sl])).astype(BF16)


def _proj_gate_call(h, w):
    b, tall, d = h.shape
    tm = _row_tile(tall)
    n = w.shape[1]
    return pl.pallas_call(
        _proj_gate_body,
        grid=(b, tall // tm),
        in_specs=[pl.BlockSpec((1, tm, d), lambda bi, r: (bi, r, 0)),
                  pl.BlockSpec((d, n), lambda bi, r: (0, 0))],
        out_specs=pl.BlockSpec((1, tm, n), lambda bi, r: (bi, r, 0)),
        out_shape=jax.ShapeDtypeStruct((b, tall, n), BF16),
        compiler_params=_cparams(("parallel", "parallel")),
        name="proj_gate",
    )(h, w)


def _da_body(safe_ref, qt_ref, k_ref, vt_ref, lp_ref, sg_ref, o_ref, m_ref, l_ref, acc_ref, *, lam_init, nk):
    j = pl.program_id(3)
    bounded = safe_ref[0] > 0.0

    @pl.when(j == 0)
    def _():
        m_ref[...] = jnp.full(m_ref.shape, -jnp.inf, F32)
        l_ref[...] = jnp.zeros(l_ref.shape, F32)
        acc_ref[...] = jnp.zeros(acc_ref.shape, F32)

    def scores(a):
        kb = k_ref[0, :, a * LANES:(a + 1) * LANES]
        q_hi = qt_ref[0, a * LANES:a * LANES + HEAD_DIM, :]
        q_lo = qt_ref[0, a * LANES + HEAD_DIM:(a + 1) * LANES, :]
        w = jnp.concatenate([q_hi, q_hi, q_lo, q_lo], axis=0)
        return _dot(jnp.concatenate([kb, kb], axis=1), w)

    def bounded_step():
        vt = vt_ref[0]
        for a in range(2):
            p = jnp.exp2(scores(a) - safe_ref[0])
            l_ref[a:a + 1, :] += jnp.sum(p, axis=0, keepdims=True)
            acc_ref[a] += _dot(vt, p.astype(BF16))

    def plain_step():
        vt = vt_ref[0]
        for a in range(2):
            s = scores(a)
            m_old = m_ref[a:a + 1, :]
            m_new = jnp.maximum(m_old, jnp.max(s, axis=0, keepdims=True))
            alpha = jnp.exp2(m_old - m_new)
            p = jnp.exp2(s - m_new)
            l_ref[a:a + 1, :] = alpha * l_ref[a:a + 1, :] + jnp.sum(p, axis=0, keepdims=True)
            acc_ref[a] = alpha * acc_ref[a] + _dot(vt, p.astype(BF16))
            m_ref[a:a + 1, :] = m_new

    pl.when(bounded)(bounded_step)
    pl.when(jnp.logical_not(bounded))(plain_step)

    @pl.when(j == nk - 1)
    def _():
        lp = lp_ref[...]
        e1 = jnp.exp(jnp.sum(lp[0:1] * lp[1:2], axis=-1, keepdims=True))
        e2 = jnp.exp(jnp.sum(lp[2:3] * lp[3:4], axis=-1, keepdims=True))
        lam = e1 - e2 + lam_init
        o1 = acc_ref[0] / l_ref[0:1, :]
        o2 = acc_ref[1] / l_ref[1:2, :]
        o = o1 - lam * o2
        ms = jnp.mean(o * o, axis=0, keepdims=True)
        y = (o * lax.rsqrt(ms + EPS) * sg_ref[...]) * (1.0 - lam_init)
        o_ref[0] = y.T.astype(BF16)


def _da_body_into(safe_ref, qt_ref, k_ref, vt_ref, lp_ref, sg_ref, y_ref, o_ref, m_ref, l_ref, acc_ref, **kw):
    del y_ref
    _da_body(safe_ref, qt_ref, k_ref, vt_ref, lp_ref, sg_ref, o_ref, m_ref, l_ref, acc_ref, **kw)


def _da_call(safe, qt, kk, vt, lp, subg, lam_init, *, q_off, nq, k_off, nk, tq, tk, into=None):
    b, _, tall = qt.shape
    in_specs = [pl.BlockSpec(memory_space=pltpu.SMEM),
                pl.BlockSpec((1, 2 * LANES, tq), lambda bi, h, i, j: (bi, h, i + q_off)),
                pl.BlockSpec((1, tk, 2 * LANES), lambda bi, h, i, j: (bi, j + k_off, h)),
                pl.BlockSpec((1, DA_V, tk), lambda bi, h, i, j: (bi, h, j + k_off)),
                pl.BlockSpec((4, HEAD_DIM), lambda bi, h, i, j: (0, 0)),
                pl.BlockSpec((LANES, 1), lambda bi, h, i, j: (0, 0))]
    args = [safe, qt, kk, vt, lp, subg]
    body, aliases = _da_body, {}
    if into is not None:
        in_specs.append(pl.BlockSpec(memory_space=pl.ANY))
        args.append(into)
        body, aliases = _da_body_into, {len(args) - 1: 0}
    return pl.pallas_call(
        functools.partial(body, lam_init=lam_init, nk=nk),
        grid=(b, DA_HEADS, nq, nk),
        in_specs=in_specs,
        out_specs=pl.BlockSpec((1, tq, LANES), lambda bi, h, i, j: (bi, i + q_off, h)),
        out_shape=jax.ShapeDtypeStruct((b, tall, DA_HEADS * LANES), BF16),
        input_output_aliases=aliases,
        scratch_shapes=[pltpu.VMEM((8, tq), F32), pltpu.VMEM((8, tq), F32), pltpu.VMEM((2, DA_V, tq), F32)],
        compiler_params=_cparams(("parallel", "parallel", "parallel", "arbitrary")),
        name="diff_attn",
    )(*args)


def _gla_body(q_ref, k_ref, v_ref, la_ref, tri_ref, o_ref, s_ref, *, reverse, tb):
    i = pl.program_id(1)

    @pl.when(i == 0)
    def _():
        s_ref[...] = jnp.zeros(s_ref.shape, F32)

    c = GLA_CHUNK
    w = GLA_HEADS * GLA_DK
    nch = tb // c
    order = list(reversed(range(nch))) if reverse else list(range(nch))
    tri = tri_ref[...]
    keep = tri > 0
    cum = _split_dot_left(tri, la_ref[0])
    tot = [cum[ch * c:ch * c + 1, :] if reverse else cum[(ch + 1) * c - 1:(ch + 1) * c, :] for ch in range(nch)]
    tot_rows = jnp.concatenate([jnp.broadcast_to(t, (c, w)) for t in tot], axis=0)
    q = q_ref[0]
    k = k_ref[0]
    qe = q * jnp.exp(cum)
    ke = (k * jnp.exp(-cum)).astype(BF16)
    kd = (k * jnp.exp(tot_rows - cum)).astype(BF16)
    dec = [jnp.exp(t) for t in tot]
    lane = lax.broadcasted_iota(jnp.int32, (1, w), 1)
    heads = range(GLA_HEADS)
    qh = [jnp.where((lane >= hd * GLA_DK) & (lane < (hd + 1) * GLA_DK), qe, 0.0).astype(BF16) for hd in heads]
    vh = [v_ref[0, :, hd * GLA_DV:(hd + 1) * GLA_DV] for hd in heads]
    inc = [[_dot_tn(vh[hd][ch * c:(ch + 1) * c], kd[ch * c:(ch + 1) * c]) for ch in range(nch)] for hd in heads]
    a = [jnp.where(keep, _dot_nt(qh[hd], ke), 0.0).astype(BF16) for hd in heads]
    entry = []
    for hd in heads:
        st = s_ref[hd]
        seen = {}
        for ch in order:
            seen[ch] = st.astype(BF16)
            st = st * dec[ch] + inc[hd][ch]
        s_ref[hd] = st
        entry.append(seen)
    o_intra = [_dot(a[hd], vh[hd]) for hd in heads]
    for hd in heads:
        for ch in range(nch):
            rows = slice(ch * c, (ch + 1) * c)
            o_ref[0, rows, hd * GLA_DV:(hd + 1) * GLA_DV] = o_intra[hd][rows] + _dot_nt(qh[hd][rows], entry[hd][ch])


def _gla_call(gq, gk, gv, la, tri, *, t_lat, reverse):
    b, tall, w = gq.shape
    tb = GLA_BLOCK
    n_lat = t_lat // tb
    nblk = tall // tb
    if reverse:
        blk = lambda i: jnp.where(i == 0, n_lat, n_lat - i)
    else:
        blk = lambda i: jnp.where(i == 0, n_lat, i - 1)
    wv_ = GLA_HEADS * GLA_DV
    return pl.pallas_call(
        functools.partial(_gla_body, reverse=reverse, tb=tb),
        grid=(b, nblk),
        in_specs=[pl.BlockSpec((1, tb, w), lambda bi, i: (bi, blk(i), 0)),
                  pl.BlockSpec((1, tb, w), lambda bi, i: (bi, blk(i), 0)),
                  pl.BlockSpec((1, tb, wv_), lambda bi, i: (bi, blk(i), 0)),
                  pl.BlockSpec((1, tb, w), lambda bi, i: (bi, blk(i), 1 if reverse else 0)),
                  pl.BlockSpec((tb, tb), lambda bi, i: (0, 0))],
        out_specs=pl.BlockSpec((1, tb, wv_), lambda bi, i: (bi, blk(i), 0)),
        out_shape=jax.ShapeDtypeStruct((b, tall, wv_), F32),
        scratch_shapes=[pltpu.VMEM((GLA_HEADS, GLA_DV, w), F32)],
        compiler_params=_cparams(("parallel", "arbitrary")),
        name="gla_bwd" if reverse else "gla_fwd",
    )(gq, gk, gv, la, tri)


def _na_body(safe_ref, q_ref, k_ref, v_ref, kc_ref, vc_ref, bias_ref, o_ref, *, rows):
    i = pl.program_id(2)
    kb0 = jnp.clip(i * NA_QROWS - NA_KR // 2, 0, rows - NA_BAND)
    start = pl.multiple_of(kb0 * GRID_W, GRID_W)
    nband = NA_BAND * GRID_W

    def attend(fixed_ref):
        q = q_ref[0]
        kb = k_ref[0, pl.ds(start, nband), :]
        kc = kc_ref[0]
        vall = jnp.concatenate([v_ref[0, pl.ds(start, nband), :], vc_ref[0]], axis=0)
        klane = lax.broadcasted_iota(jnp.int32, vall.shape, 1)
        vaug = jnp.concatenate([vall, (klane == 0).astype(BF16)], axis=1)
        lane = lax.broadcasted_iota(jnp.int32, q.shape, 1)
        qh = [jnp.where((lane < HEAD_DIM) if hh == 0 else (lane >= HEAD_DIM), q, jnp.zeros_like(q))
              for hh in range(2)]
        s_loc = [_dot_nt(qh[hh], kb) for hh in range(2)]
        s_ctx = [_dot_nt(qh[hh], kc) for hh in range(2)]
        p = []
        for hh in range(2):
            sl = s_loc[hh] + bias_ref[0, hh].astype(F32)
            if fixed_ref:
                m = safe_ref[0]
            else:
                m = jnp.maximum(jnp.max(sl, axis=-1, keepdims=True), jnp.max(s_ctx[hh], axis=-1, keepdims=True))
            p.append(jnp.concatenate([jnp.exp2(sl - m), jnp.exp2(s_ctx[hh] - m)], axis=1).astype(BF16))
        acc = [_dot(p[hh], vaug) for hh in range(2)]
        outs = [acc[hh][:, 0:LANES] / acc[hh][:, LANES:LANES + 1] for hh in range(2)]
        o_ref[0] = jnp.where(lane < HEAD_DIM, outs[0], outs[1]).astype(BF16)

    fixed = safe_ref[0] > 0.0
    pl.when(fixed)(functools.partial(attend, True))
    pl.when(jnp.logical_not(fixed))(functools.partial(attend, False))


def _na_call(safe, nq, nk, nv, bias, *, t_lat):
    b, tall, w = nq.shape
    tc = tall - t_lat
    rows = t_lat // GRID_W
    tq = NA_QROWS * GRID_W
    nsteps = rows // NA_QROWS
    npair = w // LANES
    ctx_blk = t_lat // tc

    step_types, _ = _na_block_types(rows)
    common = max(set(step_types), key=step_types.count)

    def btype(i):
        t = common
        for step, kind in enumerate(step_types):
            if kind != common:
                t = jnp.where(i == step, kind, t)
        return t

    return pl.pallas_call(
        functools.partial(_na_body, rows=rows),
        grid=(b, npair, nsteps),
        in_specs=[pl.BlockSpec(memory_space=pltpu.SMEM),
                  pl.BlockSpec((1, tq, LANES), lambda bi, hp, i: (bi, i, hp)),
                  pl.BlockSpec((1, t_lat, LANES), lambda bi, hp, i: (bi, 0, hp)),
                  pl.BlockSpec((1, t_lat, LANES), lambda bi, hp, i: (bi, 0, hp)),
                  pl.BlockSpec((1, tc, LANES), lambda bi, hp, i: (bi, ctx_blk, hp)),
                  pl.BlockSpec((1, tc, LANES), lambda bi, hp, i: (bi, ctx_blk, hp)),
                  pl.BlockSpec((1, 2, tq, NA_BAND * GRID_W), lambda bi, hp, i: (btype(i), hp, 0, 0))],
        out_specs=pl.BlockSpec((1, tq, LANES), lambda bi, hp, i: (bi, i, hp)),
        out_shape=jax.ShapeDtypeStruct((b, tall, w), BF16),
        compiler_params=_cparams(("parallel", "parallel", "arbitrary")),
        name="nbr_attn",
    )(safe, nq, nk, nv, nk, nv, bias)


def _na_block_types(rows):
    kinds, reps, step_types = {}, [], []
    for r0 in range(0, rows, NA_QROWS):
        kb0 = min(max(r0 - NA_KR // 2, 0), rows - NA_BAND)
        r = r0 + np.arange(NA_QROWS)
        sig = (kb0 - r0, tuple(np.clip(r - NA_KR // 2, 0, rows - NA_KR) - r))
        if sig not in kinds:
            kinds[sig] = len(reps)
            reps.append(r0)
        step_types.append(kinds[sig])
    return step_types, reps


def _na_bias_tiles(rpb, rows):
    assert rows >= NA_BAND and rows % NA_QROWS == 0 and NA_QROWS + NA_KR - 1 <= NA_BAND
    nl, nh, na, nb = rpb.shape
    cidx = np.arange(GRID_W)
    rel_c = cidx[None, :] - cidx[:, None] + NA_KC - 1
    sel = jnp.asarray(rel_c[None] == np.arange(nb)[:, None, None], F32)
    toep = jnp.einsum('lhab,bqk->lhqak', rpb.astype(F32) * LOG2E, sel, precision=lax.Precision.HIGHEST)
    toep = jnp.pad(toep.astype(BF16), ((0, 0), (0, 0), (0, 0), (NA_BAND, NA_BAND), (0, 0)))
    tiles = []
    for r0 in _na_block_types(rows)[1]:
        kb0 = min(max(r0 - NA_KR // 2, 0), rows - NA_BAND)
        qi = np.arange(NA_QROWS * GRID_W)
        qr, qc = r0 + qi // GRID_W, qi % GRID_W
        kj = np.arange(NA_BAND * GRID_W)
        kr, kc = kb0 + kj // GRID_W, kj % GRID_W
        rs = np.clip(qr - NA_KR // 2, 0, rows - NA_KR)
        cs = np.clip(qc - NA_KC // 2, 0, GRID_W - NA_KC)
        valid = ((kr[None, :] >= rs[:, None]) & (kr[None, :] < rs[:, None] + NA_KR)
                 & (kc[None, :] >= cs[:, None]) & (kc[None, :] < cs[:, None] + NA_KC))
        parts = []
        for q_row in range(NA_QROWS):
            a0 = kb0 - (r0 + q_row) + NA_KR - 1 + NA_BAND
            blk = toep[:, :, :, a0:a0 + NA_BAND, :]
            parts.append(blk.reshape(nl, nh, GRID_W, NA_BAND * GRID_W))
        tile = jnp.concatenate(parts, axis=2)
        tiles.append(jnp.where(jnp.asarray(valid)[None, None], tile, NEG_BIG))
    return jnp.stack(tiles, axis=1)


def _ctx_attn_body(q_ref, k_ref, v_ref, y_ref, o_ref):
    del y_ref
    q = q_ref[0]
    k = k_ref[0]
    v = v_ref[0]
    lane = lax.broadcasted_iota(jnp.int32, q.shape, 1)
    outs = []
    for hh in range(2):
        qh = jnp.where((lane < HEAD_DIM) if hh == 0 else (lane >= HEAD_DIM), q, jnp.zeros_like(q))
        s = _dot_nt(qh, k)
        m = jnp.max(s, axis=-1, keepdims=True)
        p = jnp.exp2(s - m)
        outs.append(_dot(p.astype(BF16), v) / jnp.sum(p, axis=-1, keepdims=True))
    o_ref[0] = jnp.where(lane < HEAD_DIM, outs[0], outs[1]).astype(BF16)


def _ctx_attn_call(nq, nk, nv, into, *, t_lat):
    b, tall, w = nq.shape
    tc = tall - t_lat
    ctx_blk = t_lat // tc
    spec = pl.BlockSpec((1, tc, LANES), lambda bi, hp: (bi, ctx_blk, hp))
    return pl.pallas_call(
        _ctx_attn_body,
        grid=(b, w // LANES),
        in_specs=[spec, spec, spec, pl.BlockSpec(memory_space=pl.ANY)],
        out_specs=spec,
        out_shape=jax.ShapeDtypeStruct((b, tall, w), BF16),
        input_output_aliases={3: 0},
        compiler_params=_cparams(("parallel", "parallel")),
        name="ctx_attn",
    )(nq, nk, nv, into)


def _merge_body(x_ref, yd_ref, of_ref, ob_ref, sg_ref, yn_ref, gt_ref, wd_ref, wg_ref, wn_ref, wo_ref,
                gn_ref, mb_ref, mc_ref, o_ref, *, t_lat, tm, d):
    r = pl.program_id(1)
    ctx = _is_ctx_rows(r, tm, t_lat)
    g1 = jnp.where(ctx, mc_ref[0, :, 2 * d:3 * d], mb_ref[0, :, 2 * d:3 * d])
    og = of_ref[0] + ob_ref[0]
    parts = []
    for hd in range(GLA_HEADS):
        oh = og[:, hd * GLA_DV:(hd + 1) * GLA_DV]
        ms = jnp.mean(oh * oh, axis=-1, keepdims=True)
        parts.append(oh * lax.rsqrt(ms + EPS) * gn_ref[...])
    yg = (jnp.concatenate(parts, axis=-1) * sg_ref[0].astype(F32)).astype(BF16)
    m = (gt_ref[0, :, 0:d].astype(F32) * _dot(yd_ref[0], wd_ref[...])
         + gt_ref[0, :, d:2 * d].astype(F32) * _dot(yg, wg_ref[...])
         + gt_ref[0, :, 2 * d:3 * d].astype(F32) * _dot(yn_ref[0], wn_ref[...]))
    o_ref[0] = x_ref[0] + g1 * _dot(m.astype(BF16), wo_ref[...])


def _merge_call(xall, yd, of, ob, sg, yn, gates, wd, wg, wn, wo, gn, mod3, t_lat, *, lat_only):
    b, tall, d = xall.shape
    tm = next(t for t in (512, 256) if t_lat % t == 0) if lat_only else _row_tile(tall)
    n_rows = t_lat if lat_only else tall
    nb = b
    const = lambda bi, r: (0, 0)
    row = lambda n: pl.BlockSpec((1, tm, n), lambda bi, r: (bi, r, 0))
    return pl.pallas_call(
        functools.partial(_merge_body, t_lat=t_lat, tm=tm, d=d),
        grid=(b, n_rows // tm),
        in_specs=[row(d), row(BR_W), row(BR_W), row(BR_W), row(BR_W), row(BR_W), row(3 * d),
                  pl.BlockSpec((BR_W, d), const), pl.BlockSpec((BR_W, d), const), pl.BlockSpec((BR_W, d), const),
                  pl.BlockSpec((d, d), const), pl.BlockSpec((1, GLA_DV), const),
                  pl.BlockSpec((1, 1, 6 * d), lambda bi, r: (bi, 0, 0)),
                  pl.BlockSpec((1, 1, 6 * d), lambda bi, r: (nb, 0, 0))],
        out_specs=row(d),
        out_shape=jax.ShapeDtypeStruct((b, n_rows, d), F32),
        compiler_params=_cparams(("parallel", "parallel")),
        name="merge",
    )(xall, yd, of, ob, sg, yn, gates, wd, wg, wn, wo, gn, mod3, mod3)


def _mlp_body(x_ref, g_ref, w1_ref, w2_ref, mb_ref, mc_ref, o_ref, *, t_lat, tm, d):
    r = pl.program_id(1)
    ctx = _is_ctx_rows(r, tm, t_lat)
    sh = jnp.where(ctx, mc_ref[0, :, 3 * d:4 * d], mb_ref[0, :, 3 * d:4 * d])
    sc = jnp.where(ctx, mc_ref[0, :, 4 * d:5 * d], mb_ref[0, :, 4 * d:5 * d])
    g2 = jnp.where(ctx, mc_ref[0, :, 5 * d:6 * d], mb_ref[0, :, 5 * d:6 * d])
    x = x_ref[0]
    h = _norm_mod(x, g_ref[...], sc, sh).astype(BF16)
    acc = jnp.zeros((tm, d), F32)
    step = 1024
    for j in range(w1_ref.shape[1] // step):
        a = jnp.maximum(_dot(h, w1_ref[:, j * step:(j + 1) * step]), 0.0)
        acc = acc + _dot((a * a).astype(BF16), w2_ref[j * step:(j + 1) * step, :])
    o_ref[0] = x + g2 * acc


def _mlp_call(xall, g, w1, w2, mod3, t_lat, *, lat_only):
    b, tall, d = xall.shape
    tm = next(t for t in (512, 256) if t_lat % t == 0) if lat_only else _row_tile(tall)
    n_rows = t_lat if lat_only else tall
    nb = b
    dff = w1.shape[1]
    const = lambda bi, r: (0, 0)
    return pl.pallas_call(
        functools.partial(_mlp_body, t_lat=t_lat, tm=tm, d=d),
        grid=(b, n_rows // tm),
        in_specs=[pl.BlockSpec((1, tm, d), lambda bi, r: (bi, r, 0)),
                  pl.BlockSpec((1, d), const),
                  pl.BlockSpec((d, dff), const, pipeline_mode=pl.Buffered(1)),
                  pl.BlockSpec((dff, d), const, pipeline_mode=pl.Buffered(1)),
                  pl.BlockSpec((1, 1, 6 * d), lambda bi, r: (bi, 0, 0)),
                  pl.BlockSpec((1, 1, 6 * d), lambda bi, r: (nb, 0, 0))],
        out_specs=pl.BlockSpec((1, tm, d), lambda bi, r: (bi, r, 0)),
        out_shape=jax.ShapeDtypeStruct((b, n_rows, d), F32),
        compiler_params=_cparams(("parallel", "parallel")),
        name="mlp",
    )(xall, g.reshape(1, d), w1, w2, mod3, mod3)


def kernel(x, c, ctx, c_ctx, w_mod, b_mod, norm1_g, norm2_g, w_in, da_qn_g, da_kn_g, da_lambda, da_subln_g,
           gla_a2, gla_a_b, gla_gn_g, na_qn_g, na_kn_g, na_rpb, w_br_da, w_br_gla, w_br_na, w_out, w_ff1, w_ff2):
    b, t_lat, d = x.shape
    tc = ctx.shape[1]
    tall = t_lat + tc
    depth = w_mod.shape[0]
    rows = t_lat // GRID_W
    assert d == D_MODEL and t_lat % (2 * tc) == 0 and tc == 256 and b < 8

    xall = jnp.concatenate([x, ctx], axis=1)
    cvec = jnp.zeros((8, d), F32).at[0:b].set(c).at[b].set(c_ctx)
    mod = _mod_call(cvec, w_mod, b_mod)
    cos_t, sin_t = _rope_tables(t_lat, tall)

    seg = np.arange(NORM_LANES) // HEAD_DIM
    bd = jnp.asarray(seg[:, None] == seg[None, :], BF16)
    ci = np.arange(GLA_BLOCK)
    same_chunk = (ci[None, :] // GLA_CHUNK) == (ci[:, None] // GLA_CHUNK)
    tri_f = jnp.asarray(same_chunk & (ci[None, :] <= ci[:, None]), BF16)
    tri_b = jnp.asarray(same_chunk & (ci[None, :] >= ci[:, None]), BF16)

    o_dq, o_dk, o_dv = 0, 512, 1024
    o_gq, o_gk, o_gv, o_gg, o_ga = 1536, 1792, 2048, 2560, 3072
    o_nq, o_nk, o_nv = 3104, 3616, 4128
    o_gate = 4640
    tq_da = next(t for t in (2048, 1024, 512, 256) if t_lat % t == 0)
    tk_da = next(t for t in (3328, 1280, 256) if tall % t == 0)
    na_bias = _na_bias_tiles(na_rpb, rows)

    for l in range(depth):
        need_ctx = l < depth - 1
        lam_init = 0.8 - 0.6 * math.exp(-0.3 * l)
        mod3 = mod[l].reshape(8, 1, 6 * d)
        wl = w_in[l].astype(BF16)
        cut = lambda a, n: wl[:, a:a + n]
        wa = jnp.concatenate([cut(o_ga, 2 * GLA_RANK), jnp.zeros((d, LANES - 2 * GLA_RANK), BF16)], axis=1)
        a2p = jnp.zeros((LANES, 2 * GLA_HEADS * GLA_DK), F32)
        a2p = a2p.at[0:GLA_RANK, 0:256].set(gla_a2[l, 0]).at[GLA_RANK:2 * GLA_RANK, 256:512].set(gla_a2[l, 1])
        ab = gla_a_b[l].reshape(1, 2 * GLA_HEADS * GLA_DK)
        tile2 = lambda g: jnp.tile(g, NORM_LANES // HEAD_DIM).reshape(1, NORM_LANES)

        h = _hnorm_call(xall, norm1_g[l], mod3, t_lat)
        qt, kk, vt = _proj_da_call(h, cut(o_dq, 512), cut(o_dk, 512), cut(o_dv, 512), bd,
                                   tile2(da_qn_g[l]), tile2(da_kn_g[l]), cos_t, sin_t)
        gq, gk, gv, sgg, la = _proj_gla_call(h, cut(o_gq, 256), cut(o_gk, 256), cut(o_gv, 512), cut(o_gg, 512),
                                             wa, a2p.astype(BF16), ab)
        nq, nk, nv = _proj_na_call(h, cut(o_nq, 512), cut(o_nk, 512), cut(o_nv, 512), bd,
                                   tile2(na_qn_g[l]), tile2(na_kn_g[l]))
        def score_bound(gq_, gk_, extra):
            bnd = (math.sqrt(HEAD_DIM) * LOG2E * 1.01) * jnp.max(jnp.abs(gq_)) * jnp.max(jnp.abs(gk_)) + extra + 1e-3
            return jnp.where(bnd <= DA_SAFE_LOG2, bnd, -1.0).astype(F32).reshape(1)
        safe = score_bound(da_qn_g[l], da_kn_g[l], 0.0)
        safe_na = score_bound(na_qn_g[l], na_kn_g[l], jnp.max(jnp.abs(na_rpb[l])) * LOG2E)
        gates = _proj_gate_call(h, cut(o_gate, 3 * d))

        subg = da_subln_g[l].reshape(LANES, 1)
        y_da = _da_call(safe, qt, kk, vt, da_lambda[l], subg, lam_init,
                        q_off=0, nq=t_lat // tq_da, k_off=0, nk=tall // tk_da, tq=tq_da, tk=tk_da)
        o_f = _gla_call(gq, gk, gv, la, tri_f, t_lat=t_lat, reverse=False)
        o_b = _gla_call(gq, gk, gv, la, tri_b, t_lat=t_lat, reverse=True)
        y_na = _na_call(safe_na, nq, nk, nv, na_bias[l], t_lat=t_lat)
        if need_ctx:
            y_da = _da_call(safe, qt, kk, vt, da_lambda[l], subg, lam_init,
                            q_off=t_lat // tc, nq=1, k_off=t_lat // tc, nk=1, tq=tc, tk=tc, into=y_da)
            y_na = _ctx_attn_call(nq, nk, nv, y_na, t_lat=t_lat)

        xall = _merge_call(xall, y_da, o_f, o_b, sgg, y_na, gates,
                           w_br_da[l].astype(BF16), w_br_gla[l].astype(BF16), w_br_na[l].astype(BF16),
                           w_out[l].astype(BF16), gla_gn_g[l].reshape(1, GLA_DV), mod3, t_lat,
                           lat_only=not need_ctx)
        xall = _mlp_call(xall, norm2_g[l], w_ff1[l].astype(BF16), w_ff2[l].astype(BF16), mod3, t_lat,
                         lat_only=not need_ctx)
    return xall
```

```python
import functools
import math

import numpy as np
import jax
import jax.numpy as jnp
from jax import lax
from jax.experimental import pallas as pl
from jax.experimental.pallas import tpu as pltpu

F32 = jnp.float32
BF16 = jnp.bfloat16
F8 = jnp.float8_e4m3fn

D_MODEL = 1024
GRID_W = 64
HEAD_DIM = 64
EPS = 1e-6
ROPE_BASE = 10000.0
ROPE_PAIRS = HEAD_DIM // 4
DA_HEADS = 4
GLA_HEADS = 4
GLA_DK = 64
GLA_DV = 128
GLA_RANK = 16
GLA_TAU = 16.0
GLA_CHUNK = 64
NA_HEADS = 8
NA_KR = 8
NA_KC = 16
BR_W = 512
D_FF = 4 * D_MODEL

DA_V = 2 * HEAD_DIM
LOG2E = math.log2(math.e)
DA_SAFE_LOG2 = 45.0

GLA_BLOCK = 256
NORM_LANES = 256
LANES = 128
NA_QROWS = 8
NA_BAND = 16
NEG_BIG = -1e30
VMEM_LIMIT = 48 * 1024 * 1024


def _cparams(sem):
    return pltpu.CompilerParams(dimension_semantics=sem, vmem_limit_bytes=VMEM_LIMIT)


def _dot(a, b):
    return jnp.dot(a, b, preferred_element_type=F32)


def _dot_nt(a, b):
    return lax.dot_general(a, b, (((1,), (1,)), ((), ())), preferred_element_type=F32)


def _dot_tn(a, b):
    return lax.dot_general(a, b, (((0,), (0,)), ((), ())), preferred_element_type=F32)


def _split_dot(x, w_exact):
    hi = x.astype(BF16)
    lo = (x - hi.astype(F32)).astype(BF16)
    return _dot(hi, w_exact) + _dot(lo, w_exact)


def _split_dot_left(w_exact, x):
    hi = x.astype(BF16)
    lo = (x - hi.astype(F32)).astype(BF16)
    return _dot(w_exact, hi) + _dot(w_exact, lo)


def _sigmoid(x):
    return 1.0 / (1.0 + jnp.exp(-x))


def _row_tile(tall):
    for tm in (640, 256):
        if tall % tm == 0:
            return tm
    raise ValueError(f"unsupported token count {tall}")


def _mod_body(c_ref, w_ref, b_ref, o_ref):
    cv = c_ref[...]
    s = cv * _sigmoid(cv)
    o_ref[0] = _dot(s.astype(BF16), w_ref[0].astype(BF16)) + b_ref[0]


def _mod_call(cvec, w_mod, b_mod):
    depth, d, n = w_mod.shape
    tn = 1536
    return pl.pallas_call(
        _mod_body,
        grid=(depth, n // tn),
        in_specs=[pl.BlockSpec((8, d), lambda l, j: (0, 0)),
                  pl.BlockSpec((1, d, tn), lambda l, j: (l, 0, j)),
                  pl.BlockSpec((1, 1, tn), lambda l, j: (l, 0, j))],
        out_specs=pl.BlockSpec((1, 8, tn), lambda l, j: (l, 0, j)),
        out_shape=jax.ShapeDtypeStruct((depth, 8, n), F32),
        compiler_params=_cparams(("parallel", "parallel")),
        name="mod",
    )(cvec, w_mod, b_mod.reshape(depth, 1, n))


def _rope_body(freq_ref, cos_ref, sin_ref, *, t_lat, tm):
    i = pl.program_id(0)
    t = i * tm + lax.broadcasted_iota(jnp.int32, (tm, LANES), 0)
    lane = lax.broadcasted_iota(jnp.int32, (tm, LANES), 1)
    shift = int(math.log2(GRID_W))
    row = lax.shift_right_logical(t, shift).astype(F32)
    col = jnp.bitwise_and(t, GRID_W - 1).astype(F32)
    use_row = jnp.bitwise_and(lane, 2 * ROPE_PAIRS) == 0
    first_half = jnp.bitwise_and(lane, ROPE_PAIRS) == 0
    ang = jnp.where(use_row, row, col) * freq_ref[...]
    is_lat = t < t_lat
    cos_ref[...] = jnp.where(is_lat, jnp.cos(ang), 1.0)
    sn = jnp.sin(ang)
    sin_ref[...] = jnp.where(is_lat, jnp.where(first_half, -sn, sn), 0.0)


def _rope_tables(t_lat, tall):
    tm = 256
    freqs = ROPE_BASE ** (-jnp.arange(ROPE_PAIRS, dtype=F32) / ROPE_PAIRS)
    freq_lane = jnp.tile(freqs, LANES // ROPE_PAIRS).reshape(1, LANES)
    return pl.pallas_call(
        functools.partial(_rope_body, t_lat=t_lat, tm=tm),
        grid=(tall // tm,),
        in_specs=[pl.BlockSpec((1, LANES), lambda i: (0, 0))],
        out_specs=[pl.BlockSpec((tm, LANES), lambda i: (i, 0))] * 2,
        out_shape=[jax.ShapeDtypeStruct((tall, LANES), F32)] * 2,
        compiler_params=_cparams(("parallel",)),
        name="rope_tables",
    )(freq_lane)


def _is_ctx_rows(r, tm, t_lat):
    t = r * tm + lax.broadcasted_iota(jnp.int32, (tm, 1), 0)
    return t >= t_lat


def _norm_mod(x, g, sc, sh):
    ms = jnp.mean(x * x, axis=-1, keepdims=True)
    return (x * lax.rsqrt(ms + EPS) * g) * (1.0 + sc) + sh


def _segnorm64(z, bd, gain):
    ss = _split_dot(z * z, bd)
    return z * lax.rsqrt(ss * (1.0 / HEAD_DIM) + EPS) * gain


def _hnorm_body(x_ref, g_ref, mb_ref, mc_ref, h_ref, *, t_lat, tm, d):
    r = pl.program_id(1)
    ctx = _is_ctx_rows(r, tm, t_lat)
    sh = jnp.where(ctx, mc_ref[0, :, 0:d], mb_ref[0, :, 0:d])
    sc = jnp.where(ctx, mc_ref[0, :, d:2 * d], mb_ref[0, :, d:2 * d])
    h_ref[0] = _norm_mod(x_ref[0], g_ref[...], sc, sh).astype(BF16)


def _hnorm_call(xall, g, mod3, t_lat):
    b, tall, d = xall.shape
    tm = _row_tile(tall)
    nb = b
    return pl.pallas_call(
        functools.partial(_hnorm_body, t_lat=t_lat, tm=tm, d=d),
        grid=(b, tall // tm),
        in_specs=[pl.BlockSpec((1, tm, d), lambda bi, r: (bi, r, 0)),
                  pl.BlockSpec((1, d), lambda bi, r: (0, 0)),
                  pl.BlockSpec((1, 1, 6 * d), lambda bi, r: (bi, 0, 0)),
                  pl.BlockSpec((1, 1, 6 * d), lambda bi, r: (nb, 0, 0))],
        out_specs=pl.BlockSpec((1, tm, d), lambda bi, r: (bi, r, 0)),
        out_shape=jax.ShapeDtypeStruct((b, tall, d), BF16),
        compiler_params=_cparams(("parallel", "parallel")),
        name="hnorm",
    )(xall, g.reshape(1, d), mod3, mod3)


def _rope128(x, cs, sn, first_half):
    partner = jnp.where(first_half, pltpu.roll(x, x.shape[1] - ROPE_PAIRS, 1), pltpu.roll(x, ROPE_PAIRS, 1))
    return x * cs + partner * sn


def _proj_da_body(h_ref, wq_ref, wk_ref, wv_ref, bd_ref, gq_ref, gk_ref, cos_ref, sin_ref,
                  qt_ref, k_ref, vt_ref):
    h = h_ref[0]
    tm = h.shape[0]
    bd = bd_ref[...]
    wide = bd.shape[0]
    cs = jnp.concatenate([cos_ref[...]] * (wide // LANES), axis=1)
    sn = jnp.concatenate([sin_ref[...]] * (wide // LANES), axis=1)
    lane = lax.broadcasted_iota(jnp.int32, cs.shape, 1)
    first_half = jnp.bitwise_and(lane, ROPE_PAIRS) == 0
    low_lanes = jnp.bitwise_and(lane, HEAD_DIM) == 0
    zq = _dot(h, wq_ref[...])
    zk = _dot(h, wk_ref[...])
    zv = _dot(h, wv_ref[...])
    scale = math.sqrt(HEAD_DIM ** -0.5 * LOG2E)
    for j in range(zq.shape[1] // wide):
        sl = slice(j * wide, (j + 1) * wide)
        q = _rope128(_segnorm64(zq[:, sl], bd, gq_ref[...]) * scale, cs, sn, first_half)
        k = _rope128(_segnorm64(zk[:, sl], bd, gk_ref[...]) * scale, cs, sn, first_half)
        q_t = q.T
        qh_t = q_t.astype(F8).astype(F32)
        ql_t = q_t - qh_t
        k_hi = k.astype(F8).astype(F32)
        k_lo = k - k_hi
        k_map1 = jnp.where(low_lanes, k_hi, pltpu.roll(k_lo, HEAD_DIM, 1)).astype(F8)
        k_map2 = jnp.where(low_lanes, pltpu.roll(k_hi, wide - HEAD_DIM, 1), k_lo).astype(F8)
        for hh in range(wide // LANES):
            hd = j * (wide // LANES) + hh
            r0 = hd * 2 * LANES
            for a in range(2):
                src = slice(hh * LANES + a * HEAD_DIM, hh * LANES + (a + 1) * HEAD_DIM)
                qt_ref[0, r0 + a * LANES:r0 + a * LANES + HEAD_DIM, :] = qh_t[src].astype(F8)
                qt_ref[0, r0 + a * LANES + HEAD_DIM:r0 + (a + 1) * LANES, :] = ql_t[src].astype(F8)
            k_ref[0, :, r0:r0 + LANES] = k_map1[:, hh * LANES:(hh + 1) * LANES]
            k_ref[0, :, r0 + LANES:r0 + 2 * LANES] = k_map2[:, hh * LANES:(hh + 1) * LANES]
            vt_ref[0, hd * DA_V:(hd + 1) * DA_V, :] = zv[:, hd * LANES:(hd + 1) * LANES].T.astype(BF16)


def _proj_da_call(h, wq, wk, wv, bd, gq, gk, cos_t, sin_t):
    b, tall, d = h.shape
    tm = _row_tile(tall)
    w = DA_HEADS * LANES
    nbd = bd.shape[0]
    const = lambda bi, r: (0, 0)
    return pl.pallas_call(
        _proj_da_body,
        grid=(b, tall // tm),
        in_specs=[pl.BlockSpec((1, tm, d), lambda bi, r: (bi, r, 0)),
                  pl.BlockSpec((d, w), const), pl.BlockSpec((d, w), const), pl.BlockSpec((d, w), const),
                  pl.BlockSpec((nbd, nbd), const),
                  pl.BlockSpec((1, nbd), const), pl.BlockSpec((1, nbd), const),
                  pl.BlockSpec((tm, LANES), lambda bi, r: (r, 0)),
                  pl.BlockSpec((tm, LANES), lambda bi, r: (r, 0))],
        out_specs=[pl.BlockSpec((1, 2 * w, tm), lambda bi, r: (bi, 0, r)),
                   pl.BlockSpec((1, tm, 2 * w), lambda bi, r: (bi, r, 0)),
                   pl.BlockSpec((1, DA_HEADS * DA_V, tm), lambda bi, r: (bi, 0, r))],
        out_shape=[jax.ShapeDtypeStruct((b, 2 * w, tall), F8),
                   jax.ShapeDtypeStruct((b, tall, 2 * w), F8),
                   jax.ShapeDtypeStruct((b, DA_HEADS * DA_V, tall), BF16)],
        compiler_params=_cparams(("parallel", "parallel")),
        name="proj_da",
    )(h, wq, wk, wv, bd, gq, gk, cos_t, sin_t)


def _proj_gla_body(h_ref, wq_ref, wk_ref, wv_ref, wg_ref, wa_ref, a2_ref, ab_ref,
                   q_ref, k_ref, v_ref, sg_ref, la_ref):
    h = h_ref[0]
    q_ref[0] = _dot(h, wq_ref[...]) * (GLA_DK ** -0.5)
    k_ref[0] = _dot(h, wk_ref[...])
    v_ref[0] = _dot(h, wv_ref[...]).astype(BF16)
    g = _dot(h, wg_ref[...])
    sg_ref[0] = (g * _sigmoid(g)).astype(BF16)
    ga = _dot(h, wa_ref[...])
    z = _dot(ga.astype(BF16), a2_ref[...]) + ab_ref[...]
    la_ref[0] = (jnp.minimum(z, 0.0) - jnp.log1p(jnp.exp(-jnp.abs(z)))) * (1.0 / GLA_TAU)


def _proj_gla_call(h, wq, wk, wv, wg, wa, a2p, ab):
    b, tall, d = h.shape
    tm = _row_tile(tall)
    wqk = GLA_HEADS * GLA_DK
    wv_ = GLA_HEADS * GLA_DV
    const = lambda bi, r: (0, 0)
    row = lambda n: pl.BlockSpec((1, tm, n), lambda bi, r: (bi, r, 0))
    return pl.pallas_call(
        _proj_gla_body,
        grid=(b, tall // tm),
        in_specs=[row(d),
                  pl.BlockSpec((d, wqk), const), pl.BlockSpec((d, wqk), const),
                  pl.BlockSpec((d, wv_), const), pl.BlockSpec((d, wv_), const),
                  pl.BlockSpec((d, LANES), const),
                  pl.BlockSpec((LANES, 2 * wqk), const), pl.BlockSpec((1, 2 * wqk), const)],
        out_specs=[row(wqk), row(wqk), row(wv_), row(wv_), row(2 * wqk)],
        out_shape=[jax.ShapeDtypeStruct((b, tall, wqk), F32),
                   jax.ShapeDtypeStruct((b, tall, wqk), F32),
                   jax.ShapeDtypeStruct((b, tall, wv_), BF16),
                   jax.ShapeDtypeStruct((b, tall, wv_), BF16),
                   jax.ShapeDtypeStruct((b, tall, 2 * wqk), F32)],
        compiler_params=_cparams(("parallel", "parallel")),
        name="proj_gla",
    )(h, wq, wk, wv, wg, wa, a2p, ab)


def _proj_na_body(h_ref, wq_ref, wk_ref, wv_ref, bd_ref, gq_ref, gk_ref, q_ref, k_ref, v_ref):
    h = h_ref[0]
    bd = bd_ref[...]
    wide = bd.shape[0]
    zq = _dot(h, wq_ref[...])
    zk = _dot(h, wk_ref[...])
    scale = HEAD_DIM ** -0.5 * LOG2E
    for j in range(NA_HEADS * HEAD_DIM // wide):
        sl = slice(j * wide, (j + 1) * wide)
        q_ref[0, :, sl] = (_segnorm64(zq[:, sl], bd, gq_ref[...]) * scale).astype(BF16)
        k_ref[0, :, sl] = _segnorm64(zk[:, sl], bd, gk_ref[...]).astype(BF16)
    v_ref[0] = _dot(h, wv_ref[...]).astype(BF16)


def _proj_na_call(h, wq, wk, wv, bd, gq, gk):
    b, tall, d = h.shape
    tm = _row_tile(tall)
    w = NA_HEADS * HEAD_DIM
    nbd = bd.shape[0]
    const = lambda bi, r: (0, 0)
    row = lambda n: pl.BlockSpec((1, tm, n), lambda bi, r: (bi, r, 0))
    return pl.pallas_call(
        _proj_na_body,
        grid=(b, tall // tm),
        in_specs=[row(d), pl.BlockSpec((d, w), const), pl.BlockSpec((d, w), const), pl.BlockSpec((d, w), const),
                  pl.BlockSpec((nbd, nbd), const),
                  pl.BlockSpec((1, nbd), const), pl.BlockSpec((1, nbd), const)],
        out_specs=[row(w), row(w), row(w)],
        out_shape=[jax.ShapeDtypeStruct((b, tall, w), BF16)] * 3,
        compiler_params=_cparams(("parallel", "parallel")),
        name="proj_na",
    )(h, wq, wk, wv, bd, gq, gk)


def _proj_gate_body(h_ref, w_ref, o_ref):
    h = h_ref[0]
    n = w_ref.shape[1]
    step = 512
    for j in range(n // step):
        sl = slice(j * step, (j + 1) * step)
        o_ref[0, :, sl] = _sigmoid(_dot(h, w_ref[:, sl])).astype(BF16)


def _proj_gate_call(h, w):
    b, tall, d = h.shape
    tm = _row_tile(tall)
    n = w.shape[1]
    return pl.pallas_call(
        _proj_gate_body,
        grid=(b, tall // tm),
        in_specs=[pl.BlockSpec((1, tm, d), lambda bi, r: (bi, r, 0)),
                  pl.BlockSpec((d, n), lambda bi, r: (0, 0))],
        out_specs=pl.BlockSpec((1, tm, n), lambda bi, r: (bi, r, 0)),
        out_shape=jax.ShapeDtypeStruct((b, tall, n), BF16),
        compiler_params=_cparams(("parallel", "parallel")),
        name="proj_gate",
    )(h, w)


def _da_body(safe_ref, qt_ref, k_ref, vt_ref, lp_ref, sg_ref, o_ref, m_ref, l_ref, acc_ref, *, lam_init, nk):
    j = pl.program_id(3)
    bounded = safe_ref[0] > 0.0

    @pl.when(j == 0)
    def _():
        m_ref[...] = jnp.full(m_ref.shape, -jnp.inf, F32)
        l_ref[...] = jnp.zeros(l_ref.shape, F32)
        acc_ref[...] = jnp.zeros(acc_ref.shape, F32)

    def scores(a):
        kb = k_ref[0, :, a * LANES:(a + 1) * LANES]
        q_hi = qt_ref[0, a * LANES:a * LANES + HEAD_DIM, :]
        q_lo = qt_ref[0, a * LANES + HEAD_DIM:(a + 1) * LANES, :]
        w = jnp.concatenate([q_hi, q_hi, q_lo, q_lo], axis=0)
        return _dot(jnp.concatenate([kb, kb], axis=1), w)

    def bounded_step():
        vt = vt_ref[0]
        for a in range(2):
            p = jnp.exp2(scores(a) - safe_ref[0])
            l_ref[a:a + 1, :] += jnp.sum(p, axis=0, keepdims=True)
            acc_ref[a] += _dot(vt, p.astype(BF16))

    def plain_step():
        vt = vt_ref[0]
        for a in range(2):
            s = scores(a)
            m_old = m_ref[a:a + 1, :]
            m_new = jnp.maximum(m_old, jnp.max(s, axis=0, keepdims=True))
            alpha = jnp.exp2(m_old - m_new)
            p = jnp.exp2(s - m_new)
            l_ref[a:a + 1, :] = alpha * l_ref[a:a + 1, :] + jnp.sum(p, axis=0, keepdims=True)
            acc_ref[a] = alpha * acc_ref[a] + _dot(vt, p.astype(BF16))
            m_ref[a:a + 1, :] = m_new

    pl.when(bounded)(bounded_step)
    pl.when(jnp.logical_not(bounded))(plain_step)

    @pl.when(j == nk - 1)
    def _():
        lp = lp_ref[...]
        e1 = jnp.exp(jnp.sum(lp[0:1] * lp[1:2], axis=-1, keepdims=True))
        e2 = jnp.exp(jnp.sum(lp[2:3] * lp[3:4], axis=-1, keepdims=True))
        lam = e1 - e2 + lam_init
        o1 = acc_ref[0] / l_ref[0:1, :]
        o2 = acc_ref[1] / l_ref[1:2, :]
        o = o1 - lam * o2
        ms = jnp.mean(o * o, axis=0, keepdims=True)
        y = (o * lax.rsqrt(ms + EPS) * sg_ref[...]) * (1.0 - lam_init)
        o_ref[0] = y.T.astype(BF16)


def _da_body_into(safe_ref, qt_ref, k_ref, vt_ref, lp_ref, sg_ref, y_ref, o_ref, m_ref, l_ref, acc_ref, **kw):
    del y_ref
    _da_body(safe_ref, qt_ref, k_ref, vt_ref, lp_ref, sg_ref, o_ref, m_ref, l_ref, acc_ref, **kw)


def _da_call(safe, qt, kk, vt, lp, subg, lam_init, *, q_off, nq, k_off, nk, tq, tk, into=None):
    b, _, tall = qt.shape
    in_specs = [pl.BlockSpec(memory_space=pltpu.SMEM),
                pl.BlockSpec((1, 2 * LANES, tq), lambda bi, h, i, j: (bi, h, i + q_off)),
                pl.BlockSpec((1, tk, 2 * LANES), lambda bi, h, i, j: (bi, j + k_off, h)),
                pl.BlockSpec((1, DA_V, tk), lambda bi, h, i, j: (bi, h, j + k_off)),
                pl.BlockSpec((4, HEAD_DIM), lambda bi, h, i, j: (0, 0)),
                pl.BlockSpec((LANES, 1), lambda bi, h, i, j: (0, 0))]
    args = [safe, qt, kk, vt, lp, subg]
    body, aliases = _da_body, {}
    if into is not None:
        in_specs.append(pl.BlockSpec(memory_space=pl.ANY))
        args.append(into)
        body, aliases = _da_body_into, {len(args) - 1: 0}
    return pl.pallas_call(
        functools.partial(body, lam_init=lam_init, nk=nk),
        grid=(b, DA_HEADS, nq, nk),
        in_specs=in_specs,
        out_specs=pl.BlockSpec((1, tq, LANES), lambda bi, h, i, j: (bi, i + q_off, h)),
        out_shape=jax.ShapeDtypeStruct((b, tall, DA_HEADS * LANES), BF16),
        input_output_aliases=aliases,
        scratch_shapes=[pltpu.VMEM((8, tq), F32), pltpu.VMEM((8, tq), F32), pltpu.VMEM((2, DA_V, tq), F32)],
        compiler_params=_cparams(("parallel", "parallel", "parallel", "arbitrary")),
        name="diff_attn",
    )(*args)


def _gla_body(qf_ref, kf_ref, vf_ref, laf_ref, qb_ref, kb_ref, vb_ref, lab_ref, trif_ref, trib_ref,
              of_ref, ob_ref, s_ref, *, tb):
    i = pl.program_id(1)

    @pl.when(i == 0)
    def _():
        s_ref[...] = jnp.zeros(s_ref.shape, F32)

    c = GLA_CHUNK
    w = GLA_HEADS * GLA_DK
    nch = tb // c
    lane = lax.broadcasted_iota(jnp.int32, (1, w), 1)
    heads = range(GLA_HEADS)
    dirs = ((qf_ref, kf_ref, vf_ref, laf_ref, trif_ref, of_ref, False),
            (qb_ref, kb_ref, vb_ref, lab_ref, trib_ref, ob_ref, True))
    cums = [_split_dot_left(tri_ref[...], la_ref[0]) for _, _, _, la_ref, tri_ref, _, _ in dirs]
    prep = []
    for (q_ref, k_ref, v_ref, _, tri_ref, _, reverse), cum in zip(dirs, cums):
        tot = [cum[ch * c:ch * c + 1, :] if reverse else cum[(ch + 1) * c - 1:(ch + 1) * c, :]
               for ch in range(nch)]
        tot_rows = jnp.concatenate([jnp.broadcast_to(t, (c, w)) for t in tot], axis=0)
        k = k_ref[0]
        qe = q_ref[0] * jnp.exp(cum)
        prep.append(dict(
            keep=tri_ref[...] > 0,
            ke=(k * jnp.exp(-cum)).astype(BF16),
            kd=(k * jnp.exp(tot_rows - cum)).astype(BF16),
            dec=[jnp.exp(t) for t in tot],
            order=list(reversed(range(nch))) if reverse else list(range(nch)),
            qh=[jnp.where((lane >= hd * GLA_DK) & (lane < (hd + 1) * GLA_DK), qe, 0.0).astype(BF16)
                for hd in heads],
            vh=[v_ref[0, :, hd * GLA_DV:(hd + 1) * GLA_DV] for hd in heads]))
    inc = [[[_dot_tn(p["vh"][hd][ch * c:(ch + 1) * c], p["kd"][ch * c:(ch + 1) * c]) for ch in range(nch)]
            for hd in heads] for p in prep]
    a = [[jnp.where(p["keep"], _dot_nt(p["qh"][hd], p["ke"]), 0.0).astype(BF16) for hd in heads] for p in prep]
    entry = []
    for d, p in enumerate(prep):
        per_head = []
        for hd in heads:
            st = s_ref[d, hd]
            seen = {}
            for ch in p["order"]:
                seen[ch] = st.astype(BF16)
                st = st * p["dec"][ch] + inc[d][hd][ch]
            s_ref[d, hd] = st
            per_head.append(seen)
        entry.append(per_head)
    o_intra = [[_dot(a[d][hd], p["vh"][hd]) for hd in heads] for d, p in enumerate(prep)]
    for d, p in enumerate(prep):
        o_ref = dirs[d][5]
        for hd in heads:
            for ch in range(nch):
                rows = slice(ch * c, (ch + 1) * c)
                o_ref[0, rows, hd * GLA_DV:(hd + 1) * GLA_DV] = (
                    o_intra[d][hd][rows] + _dot_nt(p["qh"][hd][rows], entry[d][hd][ch]))


def _gla_call(gq, gk, gv, la, tri_f, tri_b, *, t_lat):
    b, tall, w = gq.shape
    tb = GLA_BLOCK
    n_lat = t_lat // tb
    nblk = tall // tb
    blk_f = lambda i: jnp.where(i == 0, n_lat, i - 1)
    blk_b = lambda i: jnp.where(i == 0, n_lat, n_lat - i)
    wv_ = GLA_HEADS * GLA_DV

    def specs(blk, la_col):
        return [pl.BlockSpec((1, tb, w), lambda bi, i: (bi, blk(i), 0)),
                pl.BlockSpec((1, tb, w), lambda bi, i: (bi, blk(i), 0)),
                pl.BlockSpec((1, tb, wv_), lambda bi, i: (bi, blk(i), 0)),
                pl.BlockSpec((1, tb, w), lambda bi, i: (bi, blk(i), la_col))]

    return pl.pallas_call(
        functools.partial(_gla_body, tb=tb),
        grid=(b, nblk),
        in_specs=specs(blk_f, 0) + specs(blk_b, 1) + [pl.BlockSpec((tb, tb), lambda bi, i: (0, 0))] * 2,
        out_specs=[pl.BlockSpec((1, tb, wv_), lambda bi, i: (bi, blk_f(i), 0)),
                   pl.BlockSpec((1, tb, wv_), lambda bi, i: (bi, blk_b(i), 0))],
        out_shape=[jax.ShapeDtypeStruct((b, tall, wv_), F32)] * 2,
        scratch_shapes=[pltpu.VMEM((2, GLA_HEADS, GLA_DV, w), F32)],
        compiler_params=_cparams(("parallel", "arbitrary")),
        name="gla",
    )(gq, gk, gv, la, gq, gk, gv, la, tri_f, tri_b)


def _na_body(safe_ref, q_ref, k_ref, v_ref, kc_ref, vc_ref, bias_ref, o_ref, *, rows):
    i = pl.program_id(2)
    kb0 = jnp.clip(i * NA_QROWS - NA_KR // 2, 0, rows - NA_BAND)
    start = pl.multiple_of(kb0 * GRID_W, GRID_W)
    nband = NA_BAND * GRID_W

    def attend(fixed_ref):
        q = q_ref[0]
        kb = k_ref[0, pl.ds(start, nband), :]
        kc = kc_ref[0]
        vall = jnp.concatenate([v_ref[0, pl.ds(start, nband), :], vc_ref[0]], axis=0)
        klane = lax.broadcasted_iota(jnp.int32, vall.shape, 1)
        vaug = jnp.concatenate([vall, (klane == 0).astype(BF16)], axis=1)
        lane = lax.broadcasted_iota(jnp.int32, q.shape, 1)
        qh = [jnp.where((lane < HEAD_DIM) if hh == 0 else (lane >= HEAD_DIM), q, jnp.zeros_like(q))
              for hh in range(2)]
        s_loc = [_dot_nt(qh[hh], kb) for hh in range(2)]
        s_ctx = [_dot_nt(qh[hh], kc) for hh in range(2)]
        p = []
        for hh in range(2):
            sl = s_loc[hh] + bias_ref[0, hh].astype(F32)
            if fixed_ref:
                m = safe_ref[0]
            else:
                m = jnp.maximum(jnp.max(sl, axis=-1, keepdims=True), jnp.max(s_ctx[hh], axis=-1, keepdims=True))
            p.append(jnp.concatenate([jnp.exp2(sl - m), jnp.exp2(s_ctx[hh] - m)], axis=1).astype(BF16))
        acc = [_dot(p[hh], vaug) for hh in range(2)]
        outs = [acc[hh][:, 0:LANES] / acc[hh][:, LANES:LANES + 1] for hh in range(2)]
        o_ref[0] = jnp.where(lane < HEAD_DIM, outs[0], outs[1]).astype(BF16)

    fixed = safe_ref[0] > 0.0
    pl.when(fixed)(functools.partial(attend, True))
    pl.when(jnp.logical_not(fixed))(functools.partial(attend, False))


def _na_call(safe, nq, nk, nv, bias, *, t_lat):
    b, tall, w = nq.shape
    tc = tall - t_lat
    rows = t_lat // GRID_W
    tq = NA_QROWS * GRID_W
    nsteps = rows // NA_QROWS
    npair = w // LANES
    ctx_blk = t_lat // tc

    step_types, _ = _na_block_types(rows)
    common = max(set(step_types), key=step_types.count)

    def btype(i):
        t = common
        for step, kind in enumerate(step_types):
            if kind != common:
                t = jnp.where(i == step, kind, t)
        return t

    return pl.pallas_call(
        functools.partial(_na_body, rows=rows),
        grid=(b, npair, nsteps),
        in_specs=[pl.BlockSpec(memory_space=pltpu.SMEM),
                  pl.BlockSpec((1, tq, LANES), lambda bi, hp, i: (bi, i, hp)),
                  pl.BlockSpec((1, t_lat, LANES), lambda bi, hp, i: (bi, 0, hp)),
                  pl.BlockSpec((1, t_lat, LANES), lambda bi, hp, i: (bi, 0, hp)),
                  pl.BlockSpec((1, tc, LANES), lambda bi, hp, i: (bi, ctx_blk, hp)),
                  pl.BlockSpec((1, tc, LANES), lambda bi, hp, i: (bi, ctx_blk, hp)),
                  pl.BlockSpec((1, 2, tq, NA_BAND * GRID_W), lambda bi, hp, i: (btype(i), hp, 0, 0))],
        out_specs=pl.BlockSpec((1, tq, LANES), lambda bi, hp, i: (bi, i, hp)),
        out_shape=jax.ShapeDtypeStruct((b, tall, w), BF16),
        compiler_params=_cparams(("parallel", "parallel", "arbitrary")),
        name="nbr_attn",
    )(safe, nq, nk, nv, nk, nv, bias)


def _na_block_types(rows):
    kinds, reps, step_types = {}, [], []
    for r0 in range(0, rows, NA_QROWS):
        kb0 = min(max(r0 - NA_KR // 2, 0), rows - NA_BAND)
        r = r0 + np.arange(NA_QROWS)
        sig = (kb0 - r0, tuple(np.clip(r - NA_KR // 2, 0, rows - NA_KR) - r))
        if sig not in kinds:
            kinds[sig] = len(reps)
            reps.append(r0)
        step_types.append(kinds[sig])
    return step_types, reps


def _na_bias_tiles(rpb, rows):
    assert rows >= NA_BAND and rows % NA_QROWS == 0 and NA_QROWS + NA_KR - 1 <= NA_BAND
    nl, nh, na, nb = rpb.shape
    cidx = np.arange(GRID_W)
    rel_c = cidx[None, :] - cidx[:, None] + NA_KC - 1
    sel = jnp.asarray(rel_c[None] == np.arange(nb)[:, None, None], F32)
    toep = jnp.einsum('lhab,bqk->lhqak', rpb.astype(F32) * LOG2E, sel, precision=lax.Precision.HIGHEST)
    toep = jnp.pad(toep.astype(BF16), ((0, 0), (0, 0), (0, 0), (NA_BAND, NA_BAND), (0, 0)))
    tiles = []
    for r0 in _na_block_types(rows)[1]:
        kb0 = min(max(r0 - NA_KR // 2, 0), rows - NA_BAND)
        qi = np.arange(NA_QROWS * GRID_W)
        qr, qc = r0 + qi // GRID_W, qi % GRID_W
        kj = np.arange(NA_BAND * GRID_W)
        kr, kc = kb0 + kj // GRID_W, kj % GRID_W
        rs = np.clip(qr - NA_KR // 2, 0, rows - NA_KR)
        cs = np.clip(qc - NA_KC // 2, 0, GRID_W - NA_KC)
        valid = ((kr[None, :] >= rs[:, None]) & (kr[None, :] < rs[:, None] + NA_KR)
                 & (kc[None, :] >= cs[:, None]) & (kc[None, :] < cs[:, None] + NA_KC))
        parts = []
        for q_row in range(NA_QROWS):
            a0 = kb0 - (r0 + q_row) + NA_KR - 1 + NA_BAND
            blk = toep[:, :, :, a0:a0 + NA_BAND, :]
            parts.append(blk.reshape(nl, nh, GRID_W, NA_BAND * GRID_W))
        tile = jnp.concatenate(parts, axis=2)
        tiles.append(jnp.where(jnp.asarray(valid)[None, None], tile, NEG_BIG))
    return jnp.stack(tiles, axis=1)


def _ctx_attn_body(q_ref, k_ref, v_ref, y_ref, o_ref):
    del y_ref
    q = q_ref[0]
    k = k_ref[0]
    v = v_ref[0]
    lane = lax.broadcasted_iota(jnp.int32, q.shape, 1)
    outs = []
    for hh in range(2):
        qh = jnp.where((lane < HEAD_DIM) if hh == 0 else (lane >= HEAD_DIM), q, jnp.zeros_like(q))
        s = _dot_nt(qh, k)
        m = jnp.max(s, axis=-1, keepdims=True)
        p = jnp.exp2(s - m)
        outs.append(_dot(p.astype(BF16), v) / jnp.sum(p, axis=-1, keepdims=True))
    o_ref[0] = jnp.where(lane < HEAD_DIM, outs[0], outs[1]).astype(BF16)


def _ctx_attn_call(nq, nk, nv, into, *, t_lat):
    b, tall, w = nq.shape
    tc = tall - t_lat
    ctx_blk = t_lat // tc
    spec = pl.BlockSpec((1, tc, LANES), lambda bi, hp: (bi, ctx_blk, hp))
    return pl.pallas_call(
        _ctx_attn_body,
        grid=(b, w // LANES),
        in_specs=[spec, spec, spec, pl.BlockSpec(memory_space=pl.ANY)],
        out_specs=spec,
        out_shape=jax.ShapeDtypeStruct((b, tall, w), BF16),
        input_output_aliases={3: 0},
        compiler_params=_cparams(("parallel", "parallel")),
        name="ctx_attn",
    )(nq, nk, nv, into)


def _merge_body(x_ref, yd_ref, of_ref, ob_ref, sg_ref, yn_ref, gt_ref, wd_ref, wg_ref, wn_ref, wo_ref,
                gn_ref, mb_ref, mc_ref, o_ref, *, t_lat, tm, d):
    r = pl.program_id(1)
    ctx = _is_ctx_rows(r, tm, t_lat)
    g1 = jnp.where(ctx, mc_ref[0, :, 2 * d:3 * d], mb_ref[0, :, 2 * d:3 * d])
    og = of_ref[0] + ob_ref[0]
    parts = []
    for hd in range(GLA_HEADS):
        oh = og[:, hd * GLA_DV:(hd + 1) * GLA_DV]
        ms = jnp.mean(oh * oh, axis=-1, keepdims=True)
        parts.append(oh * lax.rsqrt(ms + EPS) * gn_ref[...])
    yg = (jnp.concatenate(parts, axis=-1) * sg_ref[0].astype(F32)).astype(BF16)
    m = (gt_ref[0, :, 0:d].astype(F32) * _dot(yd_ref[0], wd_ref[...])
         + gt_ref[0, :, d:2 * d].astype(F32) * _dot(yg, wg_ref[...])
         + gt_ref[0, :, 2 * d:3 * d].astype(F32) * _dot(yn_ref[0], wn_ref[...]))
    o_ref[0] = x_ref[0] + g1 * _dot(m.astype(BF16), wo_ref[...])


def _merge_call(xall, yd, of, ob, sg, yn, gates, wd, wg, wn, wo, gn, mod3, t_lat, *, lat_only):
    b, tall, d = xall.shape
    tm = next(t for t in (512, 256) if t_lat % t == 0) if lat_only else _row_tile(tall)
    n_rows = t_lat if lat_only else tall
    nb = b
    const = lambda bi, r: (0, 0)
    row = lambda n: pl.BlockSpec((1, tm, n), lambda bi, r: (bi, r, 0))
    return pl.pallas_call(
        functools.partial(_merge_body, t_lat=t_lat, tm=tm, d=d),
        grid=(b, n_rows // tm),
        in_specs=[row(d), row(BR_W), row(BR_W), row(BR_W), row(BR_W), row(BR_W), row(3 * d),
                  pl.BlockSpec((BR_W, d), const), pl.BlockSpec((BR_W, d), const), pl.BlockSpec((BR_W, d), const),
                  pl.BlockSpec((d, d), const), pl.BlockSpec((1, GLA_DV), const),
                  pl.BlockSpec((1, 1, 6 * d), lambda bi, r: (bi, 0, 0)),
                  pl.BlockSpec((1, 1, 6 * d), lambda bi, r: (nb, 0, 0))],
        out_specs=row(d),
        out_shape=jax.ShapeDtypeStruct((b, n_rows, d), F32),
        compiler_params=_cparams(("parallel", "parallel")),
        name="merge",
    )(xall, yd, of, ob, sg, yn, gates, wd, wg, wn, wo, gn, mod3, mod3)


def _mlp_body(x_ref, g_ref, w1_ref, w2_ref, mb_ref, mc_ref, o_ref, *, t_lat, tm, d):
    r = pl.program_id(1)
    ctx = _is_ctx_rows(r, tm, t_lat)
    sh = jnp.where(ctx, mc_ref[0, :, 3 * d:4 * d], mb_ref[0, :, 3 * d:4 * d])
    sc = jnp.where(ctx, mc_ref[0, :, 4 * d:5 * d], mb_ref[0, :, 4 * d:5 * d])
    g2 = jnp.where(ctx, mc_ref[0, :, 5 * d:6 * d], mb_ref[0, :, 5 * d:6 * d])
    x = x_ref[0]
    h = _norm_mod(x, g_ref[...], sc, sh).astype(BF16)
    acc = jnp.zeros((tm, d), F32)
    step = 1024
    for j in range(w1_ref.shape[1] // step):
        a = jnp.maximum(_dot(h, w1_ref[:, j * step:(j + 1) * step]), 0.0)
        acc = acc + _dot((a * a).astype(BF16), w2_ref[j * step:(j + 1) * step, :])
    o_ref[0] = x + g2 * acc


def _mlp_call(xall, g, w1, w2, mod3, t_lat, *, lat_only):
    b, tall, d = xall.shape
    tm = next(t for t in (512, 256) if t_lat % t == 0) if lat_only else _row_tile(tall)
    n_rows = t_lat if lat_only else tall
    nb = b
    dff = w1.shape[1]
    const = lambda bi, r: (0, 0)
    return pl.pallas_call(
        functools.partial(_mlp_body, t_lat=t_lat, tm=tm, d=d),
        grid=(b, n_rows // tm),
        in_specs=[pl.BlockSpec((1, tm, d), lambda bi, r: (bi, r, 0)),
                  pl.BlockSpec((1, d), const),
                  pl.BlockSpec((d, dff), const, pipeline_mode=pl.Buffered(1)),
                  pl.BlockSpec((dff, d), const, pipeline_mode=pl.Buffered(1)),
                  pl.BlockSpec((1, 1, 6 * d), lambda bi, r: (bi, 0, 0)),
                  pl.BlockSpec((1, 1, 6 * d), lambda bi, r: (nb, 0, 0))],
        out_specs=pl.BlockSpec((1, tm, d), lambda bi, r: (bi, r, 0)),
        out_shape=jax.ShapeDtypeStruct((b, n_rows, d), F32),
        compiler_params=_cparams(("parallel", "parallel")),
        name="mlp",
    )(xall, g.reshape(1, d), w1, w2, mod3, mod3)


def kernel(x, c, ctx, c_ctx, w_mod, b_mod, norm1_g, norm2_g, w_in, da_qn_g, da_kn_g, da_lambda, da_subln_g,
           gla_a2, gla_a_b, gla_gn_g, na_qn_g, na_kn_g, na_rpb, w_br_da, w_br_gla, w_br_na, w_out, w_ff1, w_ff2):
    b, t_lat, d = x.shape
    tc = ctx.shape[1]
    tall = t_lat + tc
    depth = w_mod.shape[0]
    rows = t_lat // GRID_W
    assert d == D_MODEL and t_lat % (2 * tc) == 0 and tc == 256 and b < 8

    xall = jnp.concatenate([x, ctx], axis=1)
    cvec = jnp.zeros((8, d), F32).at[0:b].set(c).at[b].set(c_ctx)
    mod = _mod_call(cvec, w_mod, b_mod)
    cos_t, sin_t = _rope_tables(t_lat, tall)

    seg = np.arange(NORM_LANES) // HEAD_DIM
    bd = jnp.asarray(seg[:, None] == seg[None, :], BF16)
    ci = np.arange(GLA_BLOCK)
    same_chunk = (ci[None, :] // GLA_CHUNK) == (ci[:, None] // GLA_CHUNK)
    tri_f = jnp.asarray(same_chunk & (ci[None, :] <= ci[:, None]), BF16)
    tri_b = jnp.asarray(same_chunk & (ci[None, :] >= ci[:, None]), BF16)

    o_dq, o_dk, o_dv = 0, 512, 1024
    o_gq, o_gk, o_gv, o_gg, o_ga = 1536, 1792, 2048, 2560, 3072
    o_nq, o_nk, o_nv = 3104, 3616, 4128
    o_gate = 4640
    tq_da = next(t for t in (2048, 1024, 512, 256) if t_lat % t == 0)
    tk_da = next(t for t in (3328, 1280, 256) if tall % t == 0)
    na_bias = _na_bias_tiles(na_rpb, rows)

    for l in range(depth):
        need_ctx = l < depth - 1
        lam_init = 0.8 - 0.6 * math.exp(-0.3 * l)
        mod3 = mod[l].reshape(8, 1, 6 * d)
        wl = w_in[l].astype(BF16)
        cut = lambda a, n: wl[:, a:a + n]
        wa = jnp.concatenate([cut(o_ga, 2 * GLA_RANK), jnp.zeros((d, LANES - 2 * GLA_RANK), BF16)], axis=1)
        a2p = jnp.zeros((LANES, 2 * GLA_HEADS * GLA_DK), F32)
        a2p = a2p.at[0:GLA_RANK, 0:256].set(gla_a2[l, 0]).at[GLA_RANK:2 * GLA_RANK, 256:512].set(gla_a2[l, 1])
        ab = gla_a_b[l].reshape(1, 2 * GLA_HEADS * GLA_DK)
        tile2 = lambda g: jnp.tile(g, NORM_LANES // HEAD_DIM).reshape(1, NORM_LANES)

        h = _hnorm_call(xall, norm1_g[l], mod3, t_lat)
        qt, kk, vt = _proj_da_call(h, cut(o_dq, 512), cut(o_dk, 512), cut(o_dv, 512), bd,
                                   tile2(da_qn_g[l]), tile2(da_kn_g[l]), cos_t, sin_t)
        gq, gk, gv, sgg, la = _proj_gla_call(h, cut(o_gq, 256), cut(o_gk, 256), cut(o_gv, 512), cut(o_gg, 512),
                                             wa, a2p.astype(BF16), ab)
        nq, nk, nv = _proj_na_call(h, cut(o_nq, 512), cut(o_nk, 512), cut(o_nv, 512), bd,
                                   tile2(na_qn_g[l]), tile2(na_kn_g[l]))
        def score_bound(gq_, gk_, extra):
            bnd = (math.sqrt(HEAD_DIM) * LOG2E * 1.01) * jnp.max(jnp.abs(gq_)) * jnp.max(jnp.abs(gk_)) + extra + 1e-3
            return jnp.where(bnd <= DA_SAFE_LOG2, bnd, -1.0).astype(F32).reshape(1)
        safe = score_bound(da_qn_g[l], da_kn_g[l], 0.0)
        safe_na = score_bound(na_qn_g[l], na_kn_g[l], jnp.max(jnp.abs(na_rpb[l])) * LOG2E)
        gates = _proj_gate_call(h, cut(o_gate, 3 * d))

        subg = da_subln_g[l].reshape(LANES, 1)
        y_da = _da_call(safe, qt, kk, vt, da_lambda[l], subg, lam_init,
                        q_off=0, nq=t_lat // tq_da, k_off=0, nk=tall // tk_da, tq=tq_da, tk=tk_da)
        o_f, o_b = _gla_call(gq, gk, gv, la, tri_f, tri_b, t_lat=t_lat)
        y_na = _na_call(safe_na, nq, nk, nv, na_bias[l], t_lat=t_lat)
        if need_ctx:
            y_da = _da_call(safe, qt, kk, vt, da_lambda[l], subg, lam_init,
                            q_off=t_lat // tc, nq=1, k_off=t_lat // tc, nk=1, tq=tc, tk=tc, into=y_da)
            y_na = _ctx_attn_call(nq, nk, nv, y_na, t_lat=t_lat)

        xall = _merge_call(xall, y_da, o_f, o_b, sgg, y_na, gates,
                           w_br_da[l].astype(BF16), w_br_gla[l].astype(BF16), w_br_na[l].astype(BF16),
                           w_out[l].astype(BF16), gla_gn_g[l].reshape(1, GLA_DV), mod3, t_lat,
                           lat_only=not need_ctx)
        xall = _mlp_call(xall, norm2_g[l], w_ff1[l].astype(BF16), w_ff2[l].astype(BF16), mod3, t_lat,
                         lat_only=not need_ctx)
    return xall
```

```python
import functools
import math

import numpy as np
import jax
import jax.numpy as jnp
from jax import lax
from jax.experimental import pallas as pl
from jax.experimental.pallas import tpu as pltpu

F32 = jnp.float32
BF16 = jnp.bfloat16
F8 = jnp.float8_e4m3fn

D_MODEL = 1024
GRID_W = 64
HEAD_DIM = 64
EPS = 1e-6
ROPE_BASE = 10000.0
ROPE_PAIRS = HEAD_DIM // 4
DA_HEADS = 4
GLA_HEADS = 4
GLA_DK = 64
GLA_DV = 128
GLA_RANK = 16
GLA_TAU = 16.0
GLA_CHUNK = 64
NA_HEADS = 8
NA_KR = 8
NA_KC = 16
BR_W = 512
D_FF = 4 * D_MODEL

DA_V = 2 * HEAD_DIM
LOG2E = math.log2(math.e)
DA_SAFE_LOG2 = 45.0

GLA_BLOCK = 256
NORM_LANES = 256
LANES = 128
NA_QROWS = 8
NA_BAND = 16
NEG_BIG = -1e30
VMEM_LIMIT = 48 * 1024 * 1024
VMEM_LIMIT_MERGE_MLP = 56 * 1024 * 1024


def _cparams(sem):
    return pltpu.CompilerParams(dimension_semantics=sem, vmem_limit_bytes=VMEM_LIMIT)


def _dot(a, b):
    return jnp.dot(a, b, preferred_element_type=F32)


def _dot_nt(a, b):
    return lax.dot_general(a, b, (((1,), (1,)), ((), ())), preferred_element_type=F32)


def _dot_tn(a, b):
    return lax.dot_general(a, b, (((0,), (0,)), ((), ())), preferred_element_type=F32)


def _split_dot(x, w_exact):
    hi = x.astype(BF16)
    lo = (x - hi.astype(F32)).astype(BF16)
    return _dot(hi, w_exact) + _dot(lo, w_exact)


def _split_dot_left(w_exact, x):
    hi = x.astype(BF16)
    lo = (x - hi.astype(F32)).astype(BF16)
    return _dot(w_exact, hi) + _dot(w_exact, lo)


def _sigmoid(x):
    return 1.0 / (1.0 + jnp.exp(-x))


def _row_tile(tall):
    for tm in (640, 256):
        if tall % tm == 0:
            return tm
    raise ValueError(f"unsupported token count {tall}")


def _mod_body(c_ref, w_ref, b_ref, o_ref):
    cv = c_ref[...]
    s = cv * _sigmoid(cv)
    o_ref[0] = _dot(s.astype(BF16), w_ref[0].astype(BF16)) + b_ref[0]


def _mod_call(cvec, w_mod, b_mod):
    depth, d, n = w_mod.shape
    tn = 1536
    return pl.pallas_call(
        _mod_body,
        grid=(depth, n // tn),
        in_specs=[pl.BlockSpec((8, d), lambda l, j: (0, 0)),
                  pl.BlockSpec((1, d, tn), lambda l, j: (l, 0, j)),
                  pl.BlockSpec((1, 1, tn), lambda l, j: (l, 0, j))],
        out_specs=pl.BlockSpec((1, 8, tn), lambda l, j: (l, 0, j)),
        out_shape=jax.ShapeDtypeStruct((depth, 8, n), F32),
        compiler_params=_cparams(("parallel", "parallel")),
        name="mod",
    )(cvec, w_mod, b_mod.reshape(depth, 1, n))


def _rope_body(freq_ref, cos_ref, sin_ref, *, t_lat, tm):
    i = pl.program_id(0)
    t = i * tm + lax.broadcasted_iota(jnp.int32, (tm, LANES), 0)
    lane = lax.broadcasted_iota(jnp.int32, (tm, LANES), 1)
    shift = int(math.log2(GRID_W))
    row = lax.shift_right_logical(t, shift).astype(F32)
    col = jnp.bitwise_and(t, GRID_W - 1).astype(F32)
    use_row = jnp.bitwise_and(lane, 2 * ROPE_PAIRS) == 0
    first_half = jnp.bitwise_and(lane, ROPE_PAIRS) == 0
    ang = jnp.where(use_row, row, col) * freq_ref[...]
    is_lat = t < t_lat
    cos_ref[...] = jnp.where(is_lat, jnp.cos(ang), 1.0)
    sn = jnp.sin(ang)
    sin_ref[...] = jnp.where(is_lat, jnp.where(first_half, -sn, sn), 0.0)


def _rope_tables(t_lat, tall):
    tm = 256
    freqs = ROPE_BASE ** (-jnp.arange(ROPE_PAIRS, dtype=F32) / ROPE_PAIRS)
    freq_lane = jnp.tile(freqs, LANES // ROPE_PAIRS).reshape(1, LANES)
    return pl.pallas_call(
        functools.partial(_rope_body, t_lat=t_lat, tm=tm),
        grid=(tall // tm,),
        in_specs=[pl.BlockSpec((1, LANES), lambda i: (0, 0))],
        out_specs=[pl.BlockSpec((tm, LANES), lambda i: (i, 0))] * 2,
        out_shape=[jax.ShapeDtypeStruct((tall, LANES), F32)] * 2,
        compiler_params=_cparams(("parallel",)),
        name="rope_tables",
    )(freq_lane)


def _is_ctx_rows(r, tm, t_lat):
    t = r * tm + lax.broadcasted_iota(jnp.int32, (tm, 1), 0)
    return t >= t_lat


def _norm_mod(x, g, sc, sh):
    ms = jnp.mean(x * x, axis=-1, keepdims=True)
    return (x * lax.rsqrt(ms + EPS) * g) * (1.0 + sc) + sh


def _segnorm64(z, bd, gain):
    ss = _split_dot(z * z, bd)
    return z * lax.rsqrt(ss * (1.0 / HEAD_DIM) + EPS) * gain


def _hnorm_body(x_ref, g_ref, mb_ref, mc_ref, h_ref, *, t_lat, tm, d):
    r = pl.program_id(1)
    ctx = _is_ctx_rows(r, tm, t_lat)
    sh = jnp.where(ctx, mc_ref[0, :, 0:d], mb_ref[0, :, 0:d])
    sc = jnp.where(ctx, mc_ref[0, :, d:2 * d], mb_ref[0, :, d:2 * d])
    h_ref[0] = _norm_mod(x_ref[0], g_ref[...], sc, sh).astype(BF16)


def _hnorm_call(xall, g, mod3, t_lat):
    b, tall, d = xall.shape
    tm = _row_tile(tall)
    nb = b
    return pl.pallas_call(
        functools.partial(_hnorm_body, t_lat=t_lat, tm=tm, d=d),
        grid=(b, tall // tm),
        in_specs=[pl.BlockSpec((1, tm, d), lambda bi, r: (bi, r, 0)),
                  pl.BlockSpec((1, d), lambda bi, r: (0, 0)),
                  pl.BlockSpec((1, 1, 6 * d), lambda bi, r: (bi, 0, 0)),
                  pl.BlockSpec((1, 1, 6 * d), lambda bi, r: (nb, 0, 0))],
        out_specs=pl.BlockSpec((1, tm, d), lambda bi, r: (bi, r, 0)),
        out_shape=jax.ShapeDtypeStruct((b, tall, d), BF16),
        compiler_params=_cparams(("parallel", "parallel")),
        name="hnorm",
    )(xall, g.reshape(1, d), mod3, mod3)


def _rope128(x, cs, sn, first_half):
    partner = jnp.where(first_half, pltpu.roll(x, x.shape[1] - ROPE_PAIRS, 1), pltpu.roll(x, ROPE_PAIRS, 1))
    return x * cs + partner * sn


def _proj_da_body(h_ref, wq_ref, wk_ref, wv_ref, bd_ref, gq_ref, gk_ref, cos_ref, sin_ref,
                  qt_ref, k_ref, vt_ref):
    h = h_ref[0]
    tm = h.shape[0]
    bd = bd_ref[...]
    wide = bd.shape[0]
    cs = jnp.concatenate([cos_ref[...]] * (wide // LANES), axis=1)
    sn = jnp.concatenate([sin_ref[...]] * (wide // LANES), axis=1)
    lane = lax.broadcasted_iota(jnp.int32, cs.shape, 1)
    first_half = jnp.bitwise_and(lane, ROPE_PAIRS) == 0
    low_lanes = jnp.bitwise_and(lane, HEAD_DIM) == 0
    zq = _dot(h, wq_ref[...])
    zk = _dot(h, wk_ref[...])
    zv = _dot(h, wv_ref[...])
    scale = math.sqrt(HEAD_DIM ** -0.5 * LOG2E)
    for j in range(zq.shape[1] // wide):
        sl = slice(j * wide, (j + 1) * wide)
        q = _rope128(_segnorm64(zq[:, sl], bd, gq_ref[...]) * scale, cs, sn, first_half)
        k = _rope128(_segnorm64(zk[:, sl], bd, gk_ref[...]) * scale, cs, sn, first_half)
        q_t = q.T
        qh_t = q_t.astype(F8).astype(F32)
        ql_t = q_t - qh_t
        k_hi = k.astype(F8).astype(F32)
        k_lo = k - k_hi
        k_map1 = jnp.where(low_lanes, k_hi, pltpu.roll(k_lo, HEAD_DIM, 1)).astype(F8)
        k_map2 = jnp.where(low_lanes, pltpu.roll(k_hi, wide - HEAD_DIM, 1), k_lo).astype(F8)
        for hh in range(wide // LANES):
            hd = j * (wide // LANES) + hh
            r0 = hd * 2 * LANES
            for a in range(2):
                src = slice(hh * LANES + a * HEAD_DIM, hh * LANES + (a + 1) * HEAD_DIM)
                qt_ref[0, r0 + a * LANES:r0 + a * LANES + HEAD_DIM, :] = qh_t[src].astype(F8)
                qt_ref[0, r0 + a * LANES + HEAD_DIM:r0 + (a + 1) * LANES, :] = ql_t[src].astype(F8)
            k_ref[0, :, r0:r0 + LANES] = k_map1[:, hh * LANES:(hh + 1) * LANES]
            k_ref[0, :, r0 + LANES:r0 + 2 * LANES] = k_map2[:, hh * LANES:(hh + 1) * LANES]
            vt_ref[0, hd * DA_V:(hd + 1) * DA_V, :] = zv[:, hd * LANES:(hd + 1) * LANES].T.astype(BF16)


def _proj_da_call(h, wq, wk, wv, bd, gq, gk, cos_t, sin_t):
    b, tall, d = h.shape
    tm = _row_tile(tall)
    w = DA_HEADS * LANES
    nbd = bd.shape[0]
    const = lambda bi, r: (0, 0)
    return pl.pallas_call(
        _proj_da_body,
        grid=(b, tall // tm),
        in_specs=[pl.BlockSpec((1, tm, d), lambda bi, r: (bi, r, 0)),
                  pl.BlockSpec((d, w), const), pl.BlockSpec((d, w), const), pl.BlockSpec((d, w), const),
                  pl.BlockSpec((nbd, nbd), const),
                  pl.BlockSpec((1, nbd), const), pl.BlockSpec((1, nbd), const),
                  pl.BlockSpec((tm, LANES), lambda bi, r: (r, 0)),
                  pl.BlockSpec((tm, LANES), lambda bi, r: (r, 0))],
        out_specs=[pl.BlockSpec((1, 2 * w, tm), lambda bi, r: (bi, 0, r)),
                   pl.BlockSpec((1, tm, 2 * w), lambda bi, r: (bi, r, 0)),
                   pl.BlockSpec((1, DA_HEADS * DA_V, tm), lambda bi, r: (bi, 0, r))],
        out_shape=[jax.ShapeDtypeStruct((b, 2 * w, tall), F8),
                   jax.ShapeDtypeStruct((b, tall, 2 * w), F8),
                   jax.ShapeDtypeStruct((b, DA_HEADS * DA_V, tall), BF16)],
        compiler_params=_cparams(("parallel", "parallel")),
        name="proj_da",
    )(h, wq, wk, wv, bd, gq, gk, cos_t, sin_t)


def _proj_gla_body(h_ref, wq_ref, wk_ref, wv_ref, wg_ref, wa_ref, a2_ref, ab_ref,
                   q_ref, k_ref, v_ref, sg_ref, la_ref):
    h = h_ref[0]
    q_ref[0] = _dot(h, wq_ref[...]) * (GLA_DK ** -0.5)
    k_ref[0] = _dot(h, wk_ref[...])
    v_ref[0] = _dot(h, wv_ref[...]).astype(BF16)
    g = _dot(h, wg_ref[...])
    sg_ref[0] = (g * _sigmoid(g)).astype(BF16)
    ga = _dot(h, wa_ref[...])
    z = _dot(ga.astype(BF16), a2_ref[...]) + ab_ref[...]
    la_ref[0] = (jnp.minimum(z, 0.0) - jnp.log1p(jnp.exp(-jnp.abs(z)))) * (1.0 / GLA_TAU)


def _proj_gla_call(h, wq, wk, wv, wg, wa, a2p, ab):
    b, tall, d = h.shape
    tm = _row_tile(tall)
    wqk = GLA_HEADS * GLA_DK
    wv_ = GLA_HEADS * GLA_DV
    const = lambda bi, r: (0, 0)
    row = lambda n: pl.BlockSpec((1, tm, n), lambda bi, r: (bi, r, 0))
    return pl.pallas_call(
        _proj_gla_body,
        grid=(b, tall // tm),
        in_specs=[row(d),
                  pl.BlockSpec((d, wqk), const), pl.BlockSpec((d, wqk), const),
                  pl.BlockSpec((d, wv_), const), pl.BlockSpec((d, wv_), const),
                  pl.BlockSpec((d, LANES), const),
                  pl.BlockSpec((LANES, 2 * wqk), const), pl.BlockSpec((1, 2 * wqk), const)],
        out_specs=[row(wqk), row(wqk), row(wv_), row(wv_), row(2 * wqk)],
        out_shape=[jax.ShapeDtypeStruct((b, tall, wqk), F32),
                   jax.ShapeDtypeStruct((b, tall, wqk), F32),
                   jax.ShapeDtypeStruct((b, tall, wv_), BF16),
                   jax.ShapeDtypeStruct((b, tall, wv_), BF16),
                   jax.ShapeDtypeStruct((b, tall, 2 * wqk), F32)],
        compiler_params=_cparams(("parallel", "parallel")),
        name="proj_gla",
    )(h, wq, wk, wv, wg, wa, a2p, ab)


def _proj_na_body(h_ref, wq_ref, wk_ref, wv_ref, bd_ref, gq_ref, gk_ref, q_ref, k_ref, v_ref):
    h = h_ref[0]
    bd = bd_ref[...]
    wide = bd.shape[0]
    zq = _dot(h, wq_ref[...])
    zk = _dot(h, wk_ref[...])
    scale = HEAD_DIM ** -0.5 * LOG2E
    for j in range(NA_HEADS * HEAD_DIM // wide):
        sl = slice(j * wide, (j + 1) * wide)
        q_ref[0, :, sl] = (_segnorm64(zq[:, sl], bd, gq_ref[...]) * scale).astype(BF16)
        k_ref[0, :, sl] = _segnorm64(zk[:, sl], bd, gk_ref[...]).astype(BF16)
    v_ref[0] = _dot(h, wv_ref[...]).astype(BF16)


def _proj_na_call(h, wq, wk, wv, bd, gq, gk):
    b, tall, d = h.shape
    tm = _row_tile(tall)
    w = NA_HEADS * HEAD_DIM
    nbd = bd.shape[0]
    const = lambda bi, r: (0, 0)
    row = lambda n: pl.BlockSpec((1, tm, n), lambda bi, r: (bi, r, 0))
    return pl.pallas_call(
        _proj_na_body,
        grid=(b, tall // tm),
        in_specs=[row(d), pl.BlockSpec((d, w), const), pl.BlockSpec((d, w), const), pl.BlockSpec((d, w), const),
                  pl.BlockSpec((nbd, nbd), const),
                  pl.BlockSpec((1, nbd), const), pl.BlockSpec((1, nbd), const)],
        out_specs=[row(w), row(w), row(w)],
        out_shape=[jax.ShapeDtypeStruct((b, tall, w), BF16)] * 3,
        compiler_params=_cparams(("parallel", "parallel")),
        name="proj_na",
    )(h, wq, wk, wv, bd, gq, gk)


def _proj_gate_body(h_ref, w_ref, o_ref):
    h = h_ref[0]
    n = w_ref.shape[1]
    step = 512
    for j in range(n // step):
        sl = slice(j * step, (j + 1) * step)
        o_ref[0, :, sl] = _sigmoid(_dot(h, w_ref[:, sl])).astype(BF16)


def _proj_gate_call(h, w):
    b, tall, d = h.shape
    tm = _row_tile(tall)
    n = w.shape[1]
    return pl.pallas_call(
        _proj_gate_body,
        grid=(b, tall // tm),
        in_specs=[pl.BlockSpec((1, tm, d), lambda bi, r: (bi, r, 0)),
                  pl.BlockSpec((d, n), lambda bi, r: (0, 0))],
        out_specs=pl.BlockSpec((1, tm, n), lambda bi, r: (bi, r, 0)),
        out_shape=jax.ShapeDtypeStruct((b, tall, n), BF16),
        compiler_params=_cparams(("parallel", "parallel")),
        name="proj_gate",
    )(h, w)


def _da_body(safe_ref, qt_ref, k_ref, vt_ref, lp_ref, sg_ref, o_ref, m_ref, l_ref, acc_ref, *, lam_init, nk):
    j = pl.program_id(3)
    bounded = safe_ref[0] > 0.0

    @pl.when(j == 0)
    def _():
        m_ref[...] = jnp.full(m_ref.shape, -jnp.inf, F32)
        l_ref[...] = jnp.zeros(l_ref.shape, F32)
        acc_ref[...] = jnp.zeros(acc_ref.shape, F32)

    def scores(a):
        kb = k_ref[0, :, a * LANES:(a + 1) * LANES]
        q_hi = qt_ref[0, a * LANES:a * LANES + HEAD_DIM, :]
        q_lo = qt_ref[0, a * LANES + HEAD_DIM:(a + 1) * LANES, :]
        w = jnp.concatenate([q_hi, q_hi, q_lo, q_lo], axis=0)
        return _dot(jnp.concatenate([kb, kb], axis=1), w)

    def bounded_step():
        vt = vt_ref[0]
        for a in range(2):
            p = jnp.exp2(scores(a) - safe_ref[0])
            l_ref[a:a + 1, :] += jnp.sum(p, axis=0, keepdims=True)
            acc_ref[a] += _dot(vt, p.astype(BF16))

    def plain_step():
        vt = vt_ref[0]
        for a in range(2):
            s = scores(a)
            m_old = m_ref[a:a + 1, :]
            m_new = jnp.maximum(m_old, jnp.max(s, axis=0, keepdims=True))
            alpha = jnp.exp2(m_old - m_new)
            p = jnp.exp2(s - m_new)
            l_ref[a:a + 1, :] = alpha * l_ref[a:a + 1, :] + jnp.sum(p, axis=0, keepdims=True)
            acc_ref[a] = alpha * acc_ref[a] + _dot(vt, p.astype(BF16))
            m_ref[a:a + 1, :] = m_new

    pl.when(bounded)(bounded_step)
    pl.when(jnp.logical_not(bounded))(plain_step)

    @pl.when(j == nk - 1)
    def _():
        lp = lp_ref[...]
        e1 = jnp.exp(jnp.sum(lp[0:1] * lp[1:2], axis=-1, keepdims=True))
        e2 = jnp.exp(jnp.sum(lp[2:3] * lp[3:4], axis=-1, keepdims=True))
        lam = e1 - e2 + lam_init
        o1 = acc_ref[0] / l_ref[0:1, :]
        o2 = acc_ref[1] / l_ref[1:2, :]
        o = o1 - lam * o2
        ms = jnp.mean(o * o, axis=0, keepdims=True)
        y = (o * lax.rsqrt(ms + EPS) * sg_ref[...]) * (1.0 - lam_init)
        o_ref[0] = y.T.astype(BF16)


def _da_body_into(safe_ref, qt_ref, k_ref, vt_ref, lp_ref, sg_ref, y_ref, o_ref, m_ref, l_ref, acc_ref, **kw):
    del y_ref
    _da_body(safe_ref, qt_ref, k_ref, vt_ref, lp_ref, sg_ref, o_ref, m_ref, l_ref, acc_ref, **kw)


def _da_call(safe, qt, kk, vt, lp, subg, lam_init, *, q_off, nq, k_off, nk, tq, tk, into=None):
    b, _, tall = qt.shape
    in_specs = [pl.BlockSpec(memory_space=pltpu.SMEM),
                pl.BlockSpec((1, 2 * LANES, tq), lambda bi, h, i, j: (bi, h, i + q_off)),
                pl.BlockSpec((1, tk, 2 * LANES), lambda bi, h, i, j: (bi, j + k_off, h)),
                pl.BlockSpec((1, DA_V, tk), lambda bi, h, i, j: (bi, h, j + k_off)),
                pl.BlockSpec((4, HEAD_DIM), lambda bi, h, i, j: (0, 0)),
                pl.BlockSpec((LANES, 1), lambda bi, h, i, j: (0, 0))]
    args = [safe, qt, kk, vt, lp, subg]
    body, aliases = _da_body, {}
    if into is not None:
        in_specs.append(pl.BlockSpec(memory_space=pl.ANY))
        args.append(into)
        body, aliases = _da_body_into, {len(args) - 1: 0}
    return pl.pallas_call(
        functools.partial(body, lam_init=lam_init, nk=nk),
        grid=(b, DA_HEADS, nq, nk),
        in_specs=in_specs,
        out_specs=pl.BlockSpec((1, tq, LANES), lambda bi, h, i, j: (bi, i + q_off, h)),
        out_shape=jax.ShapeDtypeStruct((b, tall, DA_HEADS * LANES), BF16),
        input_output_aliases=aliases,
        scratch_shapes=[pltpu.VMEM((8, tq), F32), pltpu.VMEM((8, tq), F32), pltpu.VMEM((2, DA_V, tq), F32)],
        compiler_params=_cparams(("parallel", "parallel", "parallel", "arbitrary")),
        name="diff_attn",
    )(*args)


def _gla_body(qf_ref, kf_ref, vf_ref, laf_ref, qb_ref, kb_ref, vb_ref, lab_ref, trif_ref, trib_ref,
              of_ref, ob_ref, s_ref, *, tb):
    i = pl.program_id(1)

    @pl.when(i == 0)
    def _():
        s_ref[...] = jnp.zeros(s_ref.shape, F32)

    c = GLA_CHUNK
    w = GLA_HEADS * GLA_DK
    nch = tb // c
    lane = lax.broadcasted_iota(jnp.int32, (1, w), 1)
    heads = range(GLA_HEADS)
    dirs = ((qf_ref, kf_ref, vf_ref, laf_ref, trif_ref, of_ref, False),
            (qb_ref, kb_ref, vb_ref, lab_ref, trib_ref, ob_ref, True))
    cums = [_split_dot_left(tri_ref[...], la_ref[0]) for _, _, _, la_ref, tri_ref, _, _ in dirs]
    prep = []
    for (q_ref, k_ref, v_ref, _, tri_ref, _, reverse), cum in zip(dirs, cums):
        tot = [cum[ch * c:ch * c + 1, :] if reverse else cum[(ch + 1) * c - 1:(ch + 1) * c, :]
               for ch in range(nch)]
        tot_rows = jnp.concatenate([jnp.broadcast_to(t, (c, w)) for t in tot], axis=0)
        k = k_ref[0]
        qe = q_ref[0] * jnp.exp(cum)
        prep.append(dict(
            keep=tri_ref[...] > 0,
            ke=(k * jnp.exp(-cum)).astype(BF16),
            kd=(k * jnp.exp(tot_rows - cum)).astype(BF16),
            dec=[jnp.exp(t) for t in tot],
            order=list(reversed(range(nch))) if reverse else list(range(nch)),
            qh=[jnp.where((lane >= hd * GLA_DK) & (lane < (hd + 1) * GLA_DK), qe, 0.0).astype(BF16)
                for hd in heads],
            vh=[v_ref[0, :, hd * GLA_DV:(hd + 1) * GLA_DV] for hd in heads]))
    inc = [[[_dot_tn(p["vh"][hd][ch * c:(ch + 1) * c], p["kd"][ch * c:(ch + 1) * c]) for ch in range(nch)]
            for hd in heads] for p in prep]
    a = [[jnp.where(p["keep"], _dot_nt(p["qh"][hd], p["ke"]), 0.0).astype(BF16) for hd in heads] for p in prep]
    entry = []
    for d, p in enumerate(prep):
        per_head = []
        for hd in heads:
            st = s_ref[d, hd]
            seen = {}
            for ch in p["order"]:
                seen[ch] = st.astype(BF16)
                st = st * p["dec"][ch] + inc[d][hd][ch]
            s_ref[d, hd] = st
            per_head.append(seen)
        entry.append(per_head)
    o_intra = [[_dot(a[d][hd], p["vh"][hd]) for hd in heads] for d, p in enumerate(prep)]
    for d, p in enumerate(prep):
        o_ref = dirs[d][5]
        for hd in heads:
            for ch in range(nch):
                rows = slice(ch * c, (ch + 1) * c)
                o_ref[0, rows, hd * GLA_DV:(hd + 1) * GLA_DV] = (
                    o_intra[d][hd][rows] + _dot_nt(p["qh"][hd][rows], entry[d][hd][ch]))


def _gla_call(gq, gk, gv, la, tri_f, tri_b, *, t_lat):
    b, tall, w = gq.shape
    tb = GLA_BLOCK
    n_lat = t_lat // tb
    nblk = tall // tb
    blk_f = lambda i: jnp.where(i == 0, n_lat, i - 1)
    blk_b = lambda i: jnp.where(i == 0, n_lat, n_lat - i)
    wv_ = GLA_HEADS * GLA_DV

    def specs(blk, la_col):
        return [pl.BlockSpec((1, tb, w), lambda bi, i: (bi, blk(i), 0)),
                pl.BlockSpec((1, tb, w), lambda bi, i: (bi, blk(i), 0)),
                pl.BlockSpec((1, tb, wv_), lambda bi, i: (bi, blk(i), 0)),
                pl.BlockSpec((1, tb, w), lambda bi, i: (bi, blk(i), la_col))]

    return pl.pallas_call(
        functools.partial(_gla_body, tb=tb),
        grid=(b, nblk),
        in_specs=specs(blk_f, 0) + specs(blk_b, 1) + [pl.BlockSpec((tb, tb), lambda bi, i: (0, 0))] * 2,
        out_specs=[pl.BlockSpec((1, tb, wv_), lambda bi, i: (bi, blk_f(i), 0)),
                   pl.BlockSpec((1, tb, wv_), lambda bi, i: (bi, blk_b(i), 0))],
        out_shape=[jax.ShapeDtypeStruct((b, tall, wv_), F32)] * 2,
        scratch_shapes=[pltpu.VMEM((2, GLA_HEADS, GLA_DV, w), F32)],
        compiler_params=_cparams(("parallel", "arbitrary")),
        name="gla",
    )(gq, gk, gv, la, gq, gk, gv, la, tri_f, tri_b)


def _na_body(safe_ref, q_ref, k_ref, v_ref, kc_ref, vc_ref, bias_ref, o_ref, *, rows):
    i = pl.program_id(2)
    kb0 = jnp.clip(i * NA_QROWS - NA_KR // 2, 0, rows - NA_BAND)
    start = pl.multiple_of(kb0 * GRID_W, GRID_W)
    nband = NA_BAND * GRID_W

    def attend(fixed_ref):
        q = q_ref[0]
        kb = k_ref[0, pl.ds(start, nband), :]
        kc = kc_ref[0]
        vall = jnp.concatenate([v_ref[0, pl.ds(start, nband), :], vc_ref[0]], axis=0)
        klane = lax.broadcasted_iota(jnp.int32, vall.shape, 1)
        vaug = jnp.concatenate([vall, (klane == 0).astype(BF16)], axis=1)
        lane = lax.broadcasted_iota(jnp.int32, q.shape, 1)
        qh = [jnp.where((lane < HEAD_DIM) if hh == 0 else (lane >= HEAD_DIM), q, jnp.zeros_like(q))
              for hh in range(2)]
        s_loc = [_dot_nt(qh[hh], kb) for hh in range(2)]
        s_ctx = [_dot_nt(qh[hh], kc) for hh in range(2)]
        p = []
        for hh in range(2):
            sl = s_loc[hh] + bias_ref[0, hh].astype(F32)
            if fixed_ref:
                m = safe_ref[0]
            else:
                m = jnp.maximum(jnp.max(sl, axis=-1, keepdims=True), jnp.max(s_ctx[hh], axis=-1, keepdims=True))
            p.append(jnp.concatenate([jnp.exp2(sl - m), jnp.exp2(s_ctx[hh] - m)], axis=1).astype(BF16))
        acc = [_dot(p[hh], vaug) for hh in range(2)]
        outs = [acc[hh][:, 0:LANES] / acc[hh][:, LANES:LANES + 1] for hh in range(2)]
        o_ref[0] = jnp.where(lane < HEAD_DIM, outs[0], outs[1]).astype(BF16)

    fixed = safe_ref[0] > 0.0
    pl.when(fixed)(functools.partial(attend, True))
    pl.when(jnp.logical_not(fixed))(functools.partial(attend, False))


def _na_call(safe, nq, nk, nv, bias, *, t_lat):
    b, tall, w = nq.shape
    tc = tall - t_lat
    rows = t_lat // GRID_W
    tq = NA_QROWS * GRID_W
    nsteps = rows // NA_QROWS
    npair = w // LANES
    ctx_blk = t_lat // tc

    step_types, _ = _na_block_types(rows)
    common = max(set(step_types), key=step_types.count)

    def btype(i):
        t = common
        for step, kind in enumerate(step_types):
            if kind != common:
                t = jnp.where(i == step, kind, t)
        return t

    return pl.pallas_call(
        functools.partial(_na_body, rows=rows),
        grid=(b, npair, nsteps),
        in_specs=[pl.BlockSpec(memory_space=pltpu.SMEM),
                  pl.BlockSpec((1, tq, LANES), lambda bi, hp, i: (bi, i, hp)),
                  pl.BlockSpec((1, t_lat, LANES), lambda bi, hp, i: (bi, 0, hp)),
                  pl.BlockSpec((1, t_lat, LANES), lambda bi, hp, i: (bi, 0, hp)),
                  pl.BlockSpec((1, tc, LANES), lambda bi, hp, i: (bi, ctx_blk, hp)),
                  pl.BlockSpec((1, tc, LANES), lambda bi, hp, i: (bi, ctx_blk, hp)),
                  pl.BlockSpec((1, 2, tq, NA_BAND * GRID_W), lambda bi, hp, i: (btype(i), hp, 0, 0))],
        out_specs=pl.BlockSpec((1, tq, LANES), lambda bi, hp, i: (bi, i, hp)),
        out_shape=jax.ShapeDtypeStruct((b, tall, w), BF16),
        compiler_params=_cparams(("parallel", "parallel", "arbitrary")),
        name="nbr_attn",
    )(safe, nq, nk, nv, nk, nv, bias)


def _na_block_types(rows):
    kinds, reps, step_types = {}, [], []
    for r0 in range(0, rows, NA_QROWS):
        kb0 = min(max(r0 - NA_KR // 2, 0), rows - NA_BAND)
        r = r0 + np.arange(NA_QROWS)
        sig = (kb0 - r0, tuple(np.clip(r - NA_KR // 2, 0, rows - NA_KR) - r))
        if sig not in kinds:
            kinds[sig] = len(reps)
            reps.append(r0)
        step_types.append(kinds[sig])
    return step_types, reps


def _na_bias_tiles(rpb, rows):
    assert rows >= NA_BAND and rows % NA_QROWS == 0 and NA_QROWS + NA_KR - 1 <= NA_BAND
    nl, nh, na, nb = rpb.shape
    cidx = np.arange(GRID_W)
    rel_c = cidx[None, :] - cidx[:, None] + NA_KC - 1
    sel = jnp.asarray(rel_c[None] == np.arange(nb)[:, None, None], F32)
    toep = jnp.einsum('lhab,bqk->lhqak', rpb.astype(F32) * LOG2E, sel, precision=lax.Precision.HIGHEST)
    toep = jnp.pad(toep.astype(BF16), ((0, 0), (0, 0), (0, 0), (NA_BAND, NA_BAND), (0, 0)))
    tiles = []
    for r0 in _na_block_types(rows)[1]:
        kb0 = min(max(r0 - NA_KR // 2, 0), rows - NA_BAND)
        qi = np.arange(NA_QROWS * GRID_W)
        qr, qc = r0 + qi // GRID_W, qi % GRID_W
        kj = np.arange(NA_BAND * GRID_W)
        kr, kc = kb0 + kj // GRID_W, kj % GRID_W
        rs = np.clip(qr - NA_KR // 2, 0, rows - NA_KR)
        cs = np.clip(qc - NA_KC // 2, 0, GRID_W - NA_KC)
        valid = ((kr[None, :] >= rs[:, None]) & (kr[None, :] < rs[:, None] + NA_KR)
                 & (kc[None, :] >= cs[:, None]) & (kc[None, :] < cs[:, None] + NA_KC))
        parts = []
        for q_row in range(NA_QROWS):
            a0 = kb0 - (r0 + q_row) + NA_KR - 1 + NA_BAND
            blk = toep[:, :, :, a0:a0 + NA_BAND, :]
            parts.append(blk.reshape(nl, nh, GRID_W, NA_BAND * GRID_W))
        tile = jnp.concatenate(parts, axis=2)
        tiles.append(jnp.where(jnp.asarray(valid)[None, None], tile, NEG_BIG))
    return jnp.stack(tiles, axis=1)


def _ctx_attn_body(q_ref, k_ref, v_ref, y_ref, o_ref):
    del y_ref
    q = q_ref[0]
    k = k_ref[0]
    v = v_ref[0]
    lane = lax.broadcasted_iota(jnp.int32, q.shape, 1)
    outs = []
    for hh in range(2):
        qh = jnp.where((lane < HEAD_DIM) if hh == 0 else (lane >= HEAD_DIM), q, jnp.zeros_like(q))
        s = _dot_nt(qh, k)
        m = jnp.max(s, axis=-1, keepdims=True)
        p = jnp.exp2(s - m)
        outs.append(_dot(p.astype(BF16), v) / jnp.sum(p, axis=-1, keepdims=True))
    o_ref[0] = jnp.where(lane < HEAD_DIM, outs[0], outs[1]).astype(BF16)


def _ctx_attn_call(nq, nk, nv, into, *, t_lat):
    b, tall, w = nq.shape
    tc = tall - t_lat
    ctx_blk = t_lat // tc
    spec = pl.BlockSpec((1, tc, LANES), lambda bi, hp: (bi, ctx_blk, hp))
    return pl.pallas_call(
        _ctx_attn_body,
        grid=(b, w // LANES),
        in_specs=[spec, spec, spec, pl.BlockSpec(memory_space=pl.ANY)],
        out_specs=spec,
        out_shape=jax.ShapeDtypeStruct((b, tall, w), BF16),
        input_output_aliases={3: 0},
        compiler_params=_cparams(("parallel", "parallel")),
        name="ctx_attn",
    )(nq, nk, nv, into)


def _merge_mlp_body(x_ref, yd_ref, of_ref, ob_ref, sg_ref, yn_ref, gt_ref, wd_ref, wg_ref, wn_ref, wo_ref,
                    gn_ref, n2_ref, w1_ref, w2_ref, mb_ref, mc_ref, o_ref, *, t_lat, tm, d):
    r = pl.program_id(1)
    ctx = _is_ctx_rows(r, tm, t_lat)
    g1 = jnp.where(ctx, mc_ref[0, :, 2 * d:3 * d], mb_ref[0, :, 2 * d:3 * d])
    og = of_ref[0] + ob_ref[0]
    parts = []
    for hd in range(GLA_HEADS):
        oh = og[:, hd * GLA_DV:(hd + 1) * GLA_DV]
        ms = jnp.mean(oh * oh, axis=-1, keepdims=True)
        parts.append(oh * lax.rsqrt(ms + EPS) * gn_ref[...])
    yg = (jnp.concatenate(parts, axis=-1) * sg_ref[0].astype(F32)).astype(BF16)
    m = (gt_ref[0, :, 0:d].astype(F32) * _dot(yd_ref[0], wd_ref[...])
         + gt_ref[0, :, d:2 * d].astype(F32) * _dot(yg, wg_ref[...])
         + gt_ref[0, :, 2 * d:3 * d].astype(F32) * _dot(yn_ref[0], wn_ref[...]))
    x1 = x_ref[0] + g1 * _dot(m.astype(BF16), wo_ref[...])
    sh = jnp.where(ctx, mc_ref[0, :, 3 * d:4 * d], mb_ref[0, :, 3 * d:4 * d])
    sc = jnp.where(ctx, mc_ref[0, :, 4 * d:5 * d], mb_ref[0, :, 4 * d:5 * d])
    g2 = jnp.where(ctx, mc_ref[0, :, 5 * d:6 * d], mb_ref[0, :, 5 * d:6 * d])
    h = _norm_mod(x1, n2_ref[...], sc, sh).astype(BF16)
    acc = jnp.zeros((tm, d), F32)
    step = 1024
    for j in range(w1_ref.shape[1] // step):
        a = jnp.maximum(_dot(h, w1_ref[:, j * step:(j + 1) * step]), 0.0)
        acc = acc + _dot((a * a).astype(BF16), w2_ref[j * step:(j + 1) * step, :])
    o_ref[0] = x1 + g2 * acc


def _merge_mlp_call(xall, yd, of, ob, sg, yn, gates, wd, wg, wn, wo, gn, n2, w1, w2, mod3, t_lat, *, lat_only):
    b, tall, d = xall.shape
    tm = next(t for t in (512, 256) if t_lat % t == 0) if lat_only else _row_tile(tall)
    n_rows = t_lat if lat_only else tall
    nb = b
    dff = w1.shape[1]
    const = lambda bi, r: (0, 0)
    row = lambda n: pl.BlockSpec((1, tm, n), lambda bi, r: (bi, r, 0))
    once = lambda shape: pl.BlockSpec(shape, const, pipeline_mode=pl.Buffered(1))
    return pl.pallas_call(
        functools.partial(_merge_mlp_body, t_lat=t_lat, tm=tm, d=d),
        grid=(b, n_rows // tm),
        in_specs=[row(d), row(BR_W), row(BR_W), row(BR_W), row(BR_W), row(BR_W), row(3 * d),
                  once((BR_W, d)), once((BR_W, d)), once((BR_W, d)), once((d, d)),
                  pl.BlockSpec((1, GLA_DV), const), pl.BlockSpec((1, d), const),
                  once((d, dff)), once((dff, d)),
                  pl.BlockSpec((1, 1, 6 * d), lambda bi, r: (bi, 0, 0)),
                  pl.BlockSpec((1, 1, 6 * d), lambda bi, r: (nb, 0, 0))],
        out_specs=row(d),
        out_shape=jax.ShapeDtypeStruct((b, n_rows, d), F32),
        compiler_params=pltpu.CompilerParams(dimension_semantics=("parallel", "parallel"),
                                             vmem_limit_bytes=VMEM_LIMIT_MERGE_MLP),
        name="merge_mlp",
    )(xall, yd, of, ob, sg, yn, gates, wd, wg, wn, wo, gn, n2.reshape(1, d), w1, w2, mod3, mod3)


def kernel(x, c, ctx, c_ctx, w_mod, b_mod, norm1_g, norm2_g, w_in, da_qn_g, da_kn_g, da_lambda, da_subln_g,
           gla_a2, gla_a_b, gla_gn_g, na_qn_g, na_kn_g, na_rpb, w_br_da, w_br_gla, w_br_na, w_out, w_ff1, w_ff2):
    b, t_lat, d = x.shape
    tc = ctx.shape[1]
    tall = t_lat + tc
    depth = w_mod.shape[0]
    rows = t_lat // GRID_W
    assert d == D_MODEL and t_lat % (2 * tc) == 0 and tc == 256 and b < 8

    xall = jnp.concatenate([x, ctx], axis=1)
    cvec = jnp.zeros((8, d), F32).at[0:b].set(c).at[b].set(c_ctx)
    mod = _mod_call(cvec, w_mod, b_mod)
    cos_t, sin_t = _rope_tables(t_lat, tall)

    seg = np.arange(NORM_LANES) // HEAD_DIM
    bd = jnp.asarray(seg[:, None] == seg[None, :], BF16)
    ci = np.arange(GLA_BLOCK)
    same_chunk = (ci[None, :] // GLA_CHUNK) == (ci[:, None] // GLA_CHUNK)
    tri_f = jnp.asarray(same_chunk & (ci[None, :] <= ci[:, None]), BF16)
    tri_b = jnp.asarray(same_chunk & (ci[None, :] >= ci[:, None]), BF16)

    o_dq, o_dk, o_dv = 0, 512, 1024
    o_gq, o_gk, o_gv, o_gg, o_ga = 1536, 1792, 2048, 2560, 3072
    o_nq, o_nk, o_nv = 3104, 3616, 4128
    o_gate = 4640
    tq_da = next(t for t in (2048, 1024, 512, 256) if t_lat % t == 0)
    tk_da = next(t for t in (3328, 1280, 256) if tall % t == 0)
    na_bias = _na_bias_tiles(na_rpb, rows)

    for l in range(depth):
        need_ctx = l < depth - 1
        lam_init = 0.8 - 0.6 * math.exp(-0.3 * l)
        mod3 = mod[l].reshape(8, 1, 6 * d)
        wl = w_in[l].astype(BF16)
        cut = lambda a, n: wl[:, a:a + n]
        wa = jnp.concatenate([cut(o_ga, 2 * GLA_RANK), jnp.zeros((d, LANES - 2 * GLA_RANK), BF16)], axis=1)
        a2p = jnp.zeros((LANES, 2 * GLA_HEADS * GLA_DK), F32)
        a2p = a2p.at[0:GLA_RANK, 0:256].set(gla_a2[l, 0]).at[GLA_RANK:2 * GLA_RANK, 256:512].set(gla_a2[l, 1])
        ab = gla_a_b[l].reshape(1, 2 * GLA_HEADS * GLA_DK)
        tile2 = lambda g: jnp.tile(g, NORM_LANES // HEAD_DIM).reshape(1, NORM_LANES)

        h = _hnorm_call(xall, norm1_g[l], mod3, t_lat)
        qt, kk, vt = _proj_da_call(h, cut(o_dq, 512), cut(o_dk, 512), cut(o_dv, 512), bd,
                                   tile2(da_qn_g[l]), tile2(da_kn_g[l]), cos_t, sin_t)
        gq, gk, gv, sgg, la = _proj_gla_call(h, cut(o_gq, 256), cut(o_gk, 256), cut(o_gv, 512), cut(o_gg, 512),
                                             wa, a2p.astype(BF16), ab)
        nq, nk, nv = _proj_na_call(h, cut(o_nq, 512), cut(o_nk, 512), cut(o_nv, 512), bd,
                                   tile2(na_qn_g[l]), tile2(na_kn_g[l]))
        def score_bound(gq_, gk_, extra):
            bnd = (math.sqrt(HEAD_DIM) * LOG2E * 1.01) * jnp.max(jnp.abs(gq_)) * jnp.max(jnp.abs(gk_)) + extra + 1e-3
            return jnp.where(bnd <= DA_SAFE_LOG2, bnd, -1.0).astype(F32).reshape(1)
        safe = score_bound(da_qn_g[l], da_kn_g[l], 0.0)
        safe_na = score_bound(na_qn_g[l], na_kn_g[l], jnp.max(jnp.abs(na_rpb[l])) * LOG2E)
        gates = _proj_gate_call(h, cut(o_gate, 3 * d))

        subg = da_subln_g[l].reshape(LANES, 1)
        y_da = _da_call(safe, qt, kk, vt, da_lambda[l], subg, lam_init,
                        q_off=0, nq=t_lat // tq_da, k_off=0, nk=tall // tk_da, tq=tq_da, tk=tk_da)
        o_f, o_b = _gla_call(gq, gk, gv, la, tri_f, tri_b, t_lat=t_lat)
        y_na = _na_call(safe_na, nq, nk, nv, na_bias[l], t_lat=t_lat)
        if need_ctx:
            y_da = _da_call(safe, qt, kk, vt, da_lambda[l], subg, lam_init,
                            q_off=t_lat // tc, nq=1, k_off=t_lat // tc, nk=1, tq=tc, tk=tc, into=y_da)
            y_na = _ctx_attn_call(nq, nk, nv, y_na, t_lat=t_lat)

        xall = _merge_mlp_call(xall, y_da, o_f, o_b, sgg, y_na, gates,
                               w_br_da[l].astype(BF16), w_br_gla[l].astype(BF16), w_br_na[l].astype(BF16),
                               w_out[l].astype(BF16), gla_gn_g[l].reshape(1, GLA_DV), norm2_g[l],
                               w_ff1[l].astype(BF16), w_ff2[l].astype(BF16), mod3, t_lat, lat_only=not need_ctx)
    return xall
```

```python
import functools
import math

import numpy as np
import jax
import jax.numpy as jnp
from jax import lax
from jax.experimental import pallas as pl
from jax.experimental.pallas import tpu as pltpu

F32 = jnp.float32
BF16 = jnp.bfloat16
F8 = jnp.float8_e4m3fn

D_MODEL = 1024
GRID_W = 64
HEAD_DIM = 64
EPS = 1e-6
ROPE_BASE = 10000.0
ROPE_PAIRS = HEAD_DIM // 4
DA_HEADS = 4
GLA_HEADS = 4
GLA_DK = 64
GLA_DV = 128
GLA_RANK = 16
GLA_TAU = 16.0
GLA_CHUNK = 64
NA_HEADS = 8
NA_KR = 8
NA_KC = 16
BR_W = 512

DA_V = 2 * HEAD_DIM
LOG2E = math.log2(math.e)
DA_SAFE_LOG2 = 45.0

GLA_BLOCK = 256
NORM_LANES = 256
LANES = 128
NA_QROWS = 8
NA_BAND = 16
NEG_BIG = -1e30
VMEM_LIMIT = 48 * 1024 * 1024
VMEM_LIMIT_MERGE_MLP = 56 * 1024 * 1024


def _cparams(sem):
    return pltpu.CompilerParams(dimension_semantics=sem, vmem_limit_bytes=VMEM_LIMIT)


def _dot(a, b):
    return jnp.dot(a, b, preferred_element_type=F32)


def _dot_nt(a, b):
    return lax.dot_general(a, b, (((1,), (1,)), ((), ())), preferred_element_type=F32)


def _dot_tn(a, b):
    return lax.dot_general(a, b, (((0,), (0,)), ((), ())), preferred_element_type=F32)


def _split_dot(x, w_exact):
    hi = x.astype(BF16)
    lo = (x - hi.astype(F32)).astype(BF16)
    return _dot(hi, w_exact) + _dot(lo, w_exact)


def _split_dot_left(w_exact, x):
    hi = x.astype(BF16)
    lo = (x - hi.astype(F32)).astype(BF16)
    return _dot(w_exact, hi) + _dot(w_exact, lo)


def _sigmoid(x):
    return 1.0 / (1.0 + jnp.exp(-x))


def _row_tile(tall):
    for tm in (640, 256):
        if tall % tm == 0:
            return tm
    raise ValueError(f"unsupported token count {tall}")


def _mod_body(c_ref, w_ref, b_ref, o_ref):
    cv = c_ref[...]
    s = cv * _sigmoid(cv)
    o_ref[0] = _dot(s.astype(BF16), w_ref[0].astype(BF16)) + b_ref[0]


def _mod_call(cvec, w_mod, b_mod):
    depth, d, n = w_mod.shape
    tn = 1536
    return pl.pallas_call(
        _mod_body,
        grid=(depth, n // tn),
        in_specs=[pl.BlockSpec((8, d), lambda l, j: (0, 0)),
                  pl.BlockSpec((1, d, tn), lambda l, j: (l, 0, j)),
                  pl.BlockSpec((1, 1, tn), lambda l, j: (l, 0, j))],
        out_specs=pl.BlockSpec((1, 8, tn), lambda l, j: (l, 0, j)),
        out_shape=jax.ShapeDtypeStruct((depth, 8, n), F32),
        compiler_params=_cparams(("parallel", "parallel")),
        name="mod",
    )(cvec, w_mod, b_mod.reshape(depth, 1, n))


def _rope_body(freq_ref, cos_ref, sin_ref, *, t_lat, tm):
    i = pl.program_id(0)
    t = i * tm + lax.broadcasted_iota(jnp.int32, (tm, LANES), 0)
    lane = lax.broadcasted_iota(jnp.int32, (tm, LANES), 1)
    shift = int(math.log2(GRID_W))
    row = lax.shift_right_logical(t, shift).astype(F32)
    col = jnp.bitwise_and(t, GRID_W - 1).astype(F32)
    use_row = jnp.bitwise_and(lane, 2 * ROPE_PAIRS) == 0
    first_half = jnp.bitwise_and(lane, ROPE_PAIRS) == 0
    ang = jnp.where(use_row, row, col) * freq_ref[...]
    is_lat = t < t_lat
    cos_ref[...] = jnp.where(is_lat, jnp.cos(ang), 1.0)
    sn = jnp.sin(ang)
    sin_ref[...] = jnp.where(is_lat, jnp.where(first_half, -sn, sn), 0.0)


def _rope_tables(t_lat, tall):
    tm = 256
    freqs = ROPE_BASE ** (-jnp.arange(ROPE_PAIRS, dtype=F32) / ROPE_PAIRS)
    freq_lane = jnp.tile(freqs, LANES // ROPE_PAIRS).reshape(1, LANES)
    return pl.pallas_call(
        functools.partial(_rope_body, t_lat=t_lat, tm=tm),
        grid=(tall // tm,),
        in_specs=[pl.BlockSpec((1, LANES), lambda i: (0, 0))],
        out_specs=[pl.BlockSpec((tm, LANES), lambda i: (i, 0))] * 2,
        out_shape=[jax.ShapeDtypeStruct((tall, LANES), F32)] * 2,
        compiler_params=_cparams(("parallel",)),
        name="rope_tables",
    )(freq_lane)


def _is_ctx_rows(r, tm, t_lat):
    t = r * tm + lax.broadcasted_iota(jnp.int32, (tm, 1), 0)
    return t >= t_lat


def _norm_mod(x, g, sc, sh):
    ms = jnp.mean(x * x, axis=-1, keepdims=True)
    return (x * lax.rsqrt(ms + EPS) * g) * (1.0 + sc) + sh


def _segnorm64(z, bd, gain):
    ss = _split_dot(z * z, bd)
    return z * lax.rsqrt(ss * (1.0 / HEAD_DIM) + EPS) * gain


def _hnorm_body(x_ref, g_ref, mb_ref, mc_ref, h_ref, *, t_lat, tm, d):
    r = pl.program_id(1)
    ctx = _is_ctx_rows(r, tm, t_lat)
    sh = jnp.where(ctx, mc_ref[0, :, 0:d], mb_ref[0, :, 0:d])
    sc = jnp.where(ctx, mc_ref[0, :, d:2 * d], mb_ref[0, :, d:2 * d])
    h_ref[0] = _norm_mod(x_ref[0], g_ref[...], sc, sh).astype(BF16)


def _hnorm_call(xall, g, mod3, t_lat):
    b, tall, d = xall.shape
    tm = _row_tile(tall)
    nb = b
    return pl.pallas_call(
        functools.partial(_hnorm_body, t_lat=t_lat, tm=tm, d=d),
        grid=(b, tall // tm),
        in_specs=[pl.BlockSpec((1, tm, d), lambda bi, r: (bi, r, 0)),
                  pl.BlockSpec((1, d), lambda bi, r: (0, 0)),
                  pl.BlockSpec((1, 1, 6 * d), lambda bi, r: (bi, 0, 0)),
                  pl.BlockSpec((1, 1, 6 * d), lambda bi, r: (nb, 0, 0))],
        out_specs=pl.BlockSpec((1, tm, d), lambda bi, r: (bi, r, 0)),
        out_shape=jax.ShapeDtypeStruct((b, tall, d), BF16),
        compiler_params=_cparams(("parallel", "parallel")),
        name="hnorm",
    )(xall, g.reshape(1, d), mod3, mod3)


def _rope128(x, cs, sn, first_half):
    partner = jnp.where(first_half, pltpu.roll(x, x.shape[1] - ROPE_PAIRS, 1), pltpu.roll(x, ROPE_PAIRS, 1))
    return x * cs + partner * sn


def _proj_da_body(h_ref, wq_ref, wk_ref, wv_ref, bd_ref, gq_ref, gk_ref, cos_ref, sin_ref,
                  qt_ref, k_ref, vt_ref):
    h = h_ref[0]
    bd = bd_ref[...]
    wide = bd.shape[0]
    cs = jnp.concatenate([cos_ref[...]] * (wide // LANES), axis=1)
    sn = jnp.concatenate([sin_ref[...]] * (wide // LANES), axis=1)
    lane = lax.broadcasted_iota(jnp.int32, cs.shape, 1)
    first_half = jnp.bitwise_and(lane, ROPE_PAIRS) == 0
    low_lanes = jnp.bitwise_and(lane, HEAD_DIM) == 0
    zq = _dot(h, wq_ref[...])
    zk = _dot(h, wk_ref[...])
    zv = _dot(h, wv_ref[...])
    scale = math.sqrt(HEAD_DIM ** -0.5 * LOG2E)
    for j in range(zq.shape[1] // wide):
        sl = slice(j * wide, (j + 1) * wide)
        q = _rope128(_segnorm64(zq[:, sl], bd, gq_ref[...]) * scale, cs, sn, first_half)
        k = _rope128(_segnorm64(zk[:, sl], bd, gk_ref[...]) * scale, cs, sn, first_half)
        q_t = q.T
        qh_t = q_t.astype(F8).astype(F32)
        ql_t = q_t - qh_t
        k_hi = k.astype(F8).astype(F32)
        k_lo = k - k_hi
        k_map1 = jnp.where(low_lanes, k_hi, pltpu.roll(k_lo, HEAD_DIM, 1)).astype(F8)
        k_map2 = jnp.where(low_lanes, pltpu.roll(k_hi, wide - HEAD_DIM, 1), k_lo).astype(F8)
        for hh in range(wide // LANES):
            hd = j * (wide // LANES) + hh
            r0 = hd * 2 * LANES
            for a in range(2):
                src = slice(hh * LANES + a * HEAD_DIM, hh * LANES + (a + 1) * HEAD_DIM)
                qt_ref[0, r0 + a * LANES:r0 + a * LANES + HEAD_DIM, :] = qh_t[src].astype(F8)
                qt_ref[0, r0 + a * LANES + HEAD_DIM:r0 + (a + 1) * LANES, :] = ql_t[src].astype(F8)
            k_ref[0, :, r0:r0 + LANES] = k_map1[:, hh * LANES:(hh + 1) * LANES]
            k_ref[0, :, r0 + LANES:r0 + 2 * LANES] = k_map2[:, hh * LANES:(hh + 1) * LANES]
            vt_ref[0, hd * DA_V:(hd + 1) * DA_V, :] = zv[:, hd * LANES:(hd + 1) * LANES].T.astype(BF16)


def _proj_da_call(h, wq, wk, wv, bd, gq, gk, cos_t, sin_t):
    b, tall, d = h.shape
    tm = _row_tile(tall)
    w = DA_HEADS * LANES
    nbd = bd.shape[0]
    const = lambda bi, r: (0, 0)
    return pl.pallas_call(
        _proj_da_body,
        grid=(b, tall // tm),
        in_specs=[pl.BlockSpec((1, tm, d), lambda bi, r: (bi, r, 0)),
                  pl.BlockSpec((d, w), const), pl.BlockSpec((d, w), const), pl.BlockSpec((d, w), const),
                  pl.BlockSpec((nbd, nbd), const),
                  pl.BlockSpec((1, nbd), const), pl.BlockSpec((1, nbd), const),
                  pl.BlockSpec((tm, LANES), lambda bi, r: (r, 0)),
                  pl.BlockSpec((tm, LANES), lambda bi, r: (r, 0))],
        out_specs=[pl.BlockSpec((1, 2 * w, tm), lambda bi, r: (bi, 0, r)),
                   pl.BlockSpec((1, tm, 2 * w), lambda bi, r: (bi, r, 0)),
                   pl.BlockSpec((1, DA_HEADS * DA_V, tm), lambda bi, r: (bi, 0, r))],
        out_shape=[jax.ShapeDtypeStruct((b, 2 * w, tall), F8),
                   jax.ShapeDtypeStruct((b, tall, 2 * w), F8),
                   jax.ShapeDtypeStruct((b, DA_HEADS * DA_V, tall), BF16)],
        compiler_params=_cparams(("parallel", "parallel")),
        name="proj_da",
    )(h, wq, wk, wv, bd, gq, gk, cos_t, sin_t)


def _proj_gla_body(h_ref, wq_ref, wk_ref, wv_ref, wg_ref, wa_ref, a2_ref, ab_ref,
                   q_ref, k_ref, v_ref, sg_ref, la_ref):
    h = h_ref[0]
    q_ref[0] = _dot(h, wq_ref[...]) * (GLA_DK ** -0.5)
    k_ref[0] = _dot(h, wk_ref[...])
    v_ref[0] = _dot(h, wv_ref[...]).astype(BF16)
    g = _dot(h, wg_ref[...])
    sg_ref[0] = (g * _sigmoid(g)).astype(BF16)
    ga = _dot(h, wa_ref[...])
    z = _dot(ga.astype(BF16), a2_ref[...]) + ab_ref[...]
    la_ref[0] = (jnp.minimum(z, 0.0) - jnp.log1p(jnp.exp(-jnp.abs(z)))) * (1.0 / GLA_TAU)


def _proj_gla_call(h, wq, wk, wv, wg, wa, a2p, ab):
    b, tall, d = h.shape
    tm = _row_tile(tall)
    wqk = GLA_HEADS * GLA_DK
    wv_ = GLA_HEADS * GLA_DV
    const = lambda bi, r: (0, 0)
    row = lambda n: pl.BlockSpec((1, tm, n), lambda bi, r: (bi, r, 0))
    return pl.pallas_call(
        _proj_gla_body,
        grid=(b, tall // tm),
        in_specs=[row(d),
                  pl.BlockSpec((d, wqk), const), pl.BlockSpec((d, wqk), const),
                  pl.BlockSpec((d, wv_), const), pl.BlockSpec((d, wv_), const),
                  pl.BlockSpec((d, LANES), const),
                  pl.BlockSpec((LANES, 2 * wqk), const), pl.BlockSpec((1, 2 * wqk), const)],
        out_specs=[row(wqk), row(wqk), row(wv_), row(wv_), row(2 * wqk)],
        out_shape=[jax.ShapeDtypeStruct((b, tall, wqk), F32),
                   jax.ShapeDtypeStruct((b, tall, wqk), F32),
                   jax.ShapeDtypeStruct((b, tall, wv_), BF16),
                   jax.ShapeDtypeStruct((b, tall, wv_), BF16),
                   jax.ShapeDtypeStruct((b, tall, 2 * wqk), F32)],
        compiler_params=_cparams(("parallel", "parallel")),
        name="proj_gla",
    )(h, wq, wk, wv, wg, wa, a2p, ab)


def _proj_na_body(h_ref, wq_ref, wk_ref, wv_ref, bd_ref, gq_ref, gk_ref, q_ref, k_ref, v_ref):
    h = h_ref[0]
    bd = bd_ref[...]
    wide = bd.shape[0]
    zq = _dot(h, wq_ref[...])
    zk = _dot(h, wk_ref[...])
    scale = HEAD_DIM ** -0.5 * LOG2E
    for j in range(NA_HEADS * HEAD_DIM // wide):
        sl = slice(j * wide, (j + 1) * wide)
        q_ref[0, :, sl] = (_segnorm64(zq[:, sl], bd, gq_ref[...]) * scale).astype(BF16)
        k_ref[0, :, sl] = _segnorm64(zk[:, sl], bd, gk_ref[...]).astype(BF16)
    v_ref[0] = _dot(h, wv_ref[...]).astype(BF16)


def _proj_na_call(h, wq, wk, wv, bd, gq, gk):
    b, tall, d = h.shape
    tm = _row_tile(tall)
    w = NA_HEADS * HEAD_DIM
    nbd = bd.shape[0]
    const = lambda bi, r: (0, 0)
    row = lambda n: pl.BlockSpec((1, tm, n), lambda bi, r: (bi, r, 0))
    return pl.pallas_call(
        _proj_na_body,
        grid=(b, tall // tm),
        in_specs=[row(d), pl.BlockSpec((d, w), const), pl.BlockSpec((d, w), const), pl.BlockSpec((d, w), const),
                  pl.BlockSpec((nbd, nbd), const),
                  pl.BlockSpec((1, nbd), const), pl.BlockSpec((1, nbd), const)],
        out_specs=[row(w), row(w), row(w)],
        out_shape=[jax.ShapeDtypeStruct((b, tall, w), BF16)] * 3,
        compiler_params=_cparams(("parallel", "parallel")),
        name="proj_na",
    )(h, wq, wk, wv, bd, gq, gk)


def _proj_gate_body(h_ref, w_ref, o_ref):
    h = h_ref[0]
    n = w_ref.shape[1]
    step = 512
    for j in range(n // step):
        sl = slice(j * step, (j + 1) * step)
        o_ref[0, :, sl] = _sigmoid(_dot(h, w_ref[:, sl])).astype(BF16)


def _proj_gate_call(h, w):
    b, tall, d = h.shape
    tm = _row_tile(tall)
    n = w.shape[1]
    return pl.pallas_call(
        _proj_gate_body,
        grid=(b, tall // tm),
        in_specs=[pl.BlockSpec((1, tm, d), lambda bi, r: (bi, r, 0)),
                  pl.BlockSpec((d, n), lambda bi, r: (0, 0))],
        out_specs=pl.BlockSpec((1, tm, n), lambda bi, r: (bi, r, 0)),
        out_shape=jax.ShapeDtypeStruct((b, tall, n), BF16),
        compiler_params=_cparams(("parallel", "parallel")),
        name="proj_gate",
    )(h, w)


def _da_body(safe_ref, qt_ref, k_ref, vt_ref, lp_ref, sg_ref, o_ref, m_ref, l_ref, acc_ref, *, lam_init, nk):
    j = pl.program_id(3)
    bounded = safe_ref[0] > 0.0

    @pl.when(j == 0)
    def _():
        m_ref[...] = jnp.full(m_ref.shape, -jnp.inf, F32)
        l_ref[...] = jnp.zeros(l_ref.shape, F32)
        acc_ref[...] = jnp.zeros(acc_ref.shape, F32)

    def scores(a):
        kb = k_ref[0, :, a * LANES:(a + 1) * LANES]
        q_hi = qt_ref[0, a * LANES:a * LANES + HEAD_DIM, :]
        q_lo = qt_ref[0, a * LANES + HEAD_DIM:(a + 1) * LANES, :]
        w = jnp.concatenate([q_hi, q_hi, q_lo, q_lo], axis=0)
        return _dot(jnp.concatenate([kb, kb], axis=1), w)

    def bounded_step():
        vt = vt_ref[0]
        for a in range(2):
            p = jnp.exp2(scores(a) - safe_ref[0])
            l_ref[a:a + 1, :] += jnp.sum(p, axis=0, keepdims=True)
            acc_ref[a] += _dot(vt, p.astype(BF16))

    def plain_step():
        vt = vt_ref[0]
        for a in range(2):
            s = scores(a)
            m_old = m_ref[a:a + 1, :]
            m_new = jnp.maximum(m_old, jnp.max(s, axis=0, keepdims=True))
            alpha = jnp.exp2(m_old - m_new)
            p = jnp.exp2(s - m_new)
            l_ref[a:a + 1, :] = alpha * l_ref[a:a + 1, :] + jnp.sum(p, axis=0, keepdims=True)
            acc_ref[a] = alpha * acc_ref[a] + _dot(vt, p.astype(BF16))
            m_ref[a:a + 1, :] = m_new

    pl.when(bounded)(bounded_step)
    pl.when(jnp.logical_not(bounded))(plain_step)

    @pl.when(j == nk - 1)
    def _():
        lp = lp_ref[...]
        e1 = jnp.exp(jnp.sum(lp[0:1] * lp[1:2], axis=-1, keepdims=True))
        e2 = jnp.exp(jnp.sum(lp[2:3] * lp[3:4], axis=-1, keepdims=True))
        lam = e1 - e2 + lam_init
        o1 = acc_ref[0] / l_ref[0:1, :]
        o2 = acc_ref[1] / l_ref[1:2, :]
        o = o1 - lam * o2
        ms = jnp.mean(o * o, axis=0, keepdims=True)
        y = (o * lax.rsqrt(ms + EPS) * sg_ref[...]) * (1.0 - lam_init)
        o_ref[0] = y.T.astype(BF16)


def _da_body_into(safe_ref, qt_ref, k_ref, vt_ref, lp_ref, sg_ref, y_ref, o_ref, m_ref, l_ref, acc_ref, **kw):
    del y_ref
    _da_body(safe_ref, qt_ref, k_ref, vt_ref, lp_ref, sg_ref, o_ref, m_ref, l_ref, acc_ref, **kw)


def _da_call(safe, qt, kk, vt, lp, subg, lam_init, *, q_off, nq, k_off, nk, tq, tk, into=None):
    b, _, tall = qt.shape
    in_specs = [pl.BlockSpec(memory_space=pltpu.SMEM),
                pl.BlockSpec((1, 2 * LANES, tq), lambda bi, h, i, j: (bi, h, i + q_off)),
                pl.BlockSpec((1, tk, 2 * LANES), lambda bi, h, i, j: (bi, j + k_off, h)),
                pl.BlockSpec((1, DA_V, tk), lambda bi, h, i, j: (bi, h, j + k_off)),
                pl.BlockSpec((4, HEAD_DIM), lambda bi, h, i, j: (0, 0)),
                pl.BlockSpec((LANES, 1), lambda bi, h, i, j: (0, 0))]
    args = [safe, qt, kk, vt, lp, subg]
    body, aliases = _da_body, {}
    if into is not None:
        in_specs.append(pl.BlockSpec(memory_space=pl.ANY))
        args.append(into)
        body, aliases = _da_body_into, {len(args) - 1: 0}
    return pl.pallas_call(
        functools.partial(body, lam_init=lam_init, nk=nk),
        grid=(b, DA_HEADS, nq, nk),
        in_specs=in_specs,
        out_specs=pl.BlockSpec((1, tq, LANES), lambda bi, h, i, j: (bi, i + q_off, h)),
        out_shape=jax.ShapeDtypeStruct((b, tall, DA_HEADS * LANES), BF16),
        input_output_aliases=aliases,
        scratch_shapes=[pltpu.VMEM((8, tq), F32), pltpu.VMEM((8, tq), F32), pltpu.VMEM((2, DA_V, tq), F32)],
        compiler_params=_cparams(("parallel", "parallel", "parallel", "arbitrary")),
        name="diff_attn",
    )(*args)


def _gla_body(qf_ref, kf_ref, vf_ref, laf_ref, qb_ref, kb_ref, vb_ref, lab_ref, trif_ref, trib_ref,
              of_ref, ob_ref, s_ref, *, tb):
    i = pl.program_id(1)

    @pl.when(i == 0)
    def _():
        s_ref[...] = jnp.zeros(s_ref.shape, F32)

    c = GLA_CHUNK
    w = GLA_HEADS * GLA_DK
    nch = tb // c
    lane = lax.broadcasted_iota(jnp.int32, (1, w), 1)
    heads = range(GLA_HEADS)
    dirs = ((qf_ref, kf_ref, vf_ref, laf_ref, trif_ref, of_ref, False),
            (qb_ref, kb_ref, vb_ref, lab_ref, trib_ref, ob_ref, True))
    cums = [_split_dot_left(tri_ref[...], la_ref[0]) for _, _, _, la_ref, tri_ref, _, _ in dirs]
    prep = []
    for (q_ref, k_ref, v_ref, _, tri_ref, _, reverse), cum in zip(dirs, cums):
        tot = [cum[ch * c:ch * c + 1, :] if reverse else cum[(ch + 1) * c - 1:(ch + 1) * c, :]
               for ch in range(nch)]
        tot_rows = jnp.concatenate([jnp.broadcast_to(t, (c, w)) for t in tot], axis=0)
        k = k_ref[0]
        qe = q_ref[0] * jnp.exp(cum)
        prep.append(dict(
            keep=tri_ref[...] > 0,
            ke=(k * jnp.exp(-cum)).astype(BF16),
            kd=(k * jnp.exp(tot_rows - cum)).astype(BF16),
            dec=[jnp.exp(t) for t in tot],
            order=list(reversed(range(nch))) if reverse else list(range(nch)),
            qh=[jnp.where((lane >= hd * GLA_DK) & (lane < (hd + 1) * GLA_DK), qe, 0.0).astype(BF16)
                for hd in heads],
            vh=[v_ref[0, :, hd * GLA_DV:(hd + 1) * GLA_DV] for hd in heads]))
    inc = [[[_dot_tn(p["vh"][hd][ch * c:(ch + 1) * c], p["kd"][ch * c:(ch + 1) * c]) for ch in range(nch)]
            for hd in heads] for p in prep]
    a = [[jnp.where(p["keep"], _dot_nt(p["qh"][hd], p["ke"]), 0.0).astype(BF16) for hd in heads] for p in prep]
    entry = []
    for d, p in enumerate(prep):
        per_head = []
        for hd in heads:
            st = s_ref[d, hd]
            seen = {}
            for ch in p["order"]:
                seen[ch] = st.astype(BF16)
                st = st * p["dec"][ch] + inc[d][hd][ch]
            s_ref[d, hd] = st
            per_head.append(seen)
        entry.append(per_head)
    o_intra = [[_dot(a[d][hd], p["vh"][hd]) for hd in heads] for d, p in enumerate(prep)]
    for d, p in enumerate(prep):
        o_ref = dirs[d][5]
        for hd in heads:
            for ch in range(nch):
                rows = slice(ch * c, (ch + 1) * c)
                o_ref[0, rows, hd * GLA_DV:(hd + 1) * GLA_DV] = (
                    o_intra[d][hd][rows] + _dot_nt(p["qh"][hd][rows], entry[d][hd][ch]))


def _gla_call(gq, gk, gv, la, tri_f, tri_b, *, t_lat):
    b, tall, w = gq.shape
    tb = GLA_BLOCK
    n_lat = t_lat // tb
    nblk = tall // tb
    blk_f = lambda i: jnp.where(i == 0, n_lat, i - 1)
    blk_b = lambda i: jnp.where(i == 0, n_lat, n_lat - i)
    wv_ = GLA_HEADS * GLA_DV

    def specs(blk, la_col):
        return [pl.BlockSpec((1, tb, w), lambda bi, i: (bi, blk(i), 0)),
                pl.BlockSpec((1, tb, w), lambda bi, i: (bi, blk(i), 0)),
                pl.BlockSpec((1, tb, wv_), lambda bi, i: (bi, blk(i), 0)),
                pl.BlockSpec((1, tb, w), lambda bi, i: (bi, blk(i), la_col))]

    return pl.pallas_call(
        functools.partial(_gla_body, tb=tb),
        grid=(b, nblk),
        in_specs=specs(blk_f, 0) + specs(blk_b, 1) + [pl.BlockSpec((tb, tb), lambda bi, i: (0, 0))] * 2,
        out_specs=[pl.BlockSpec((1, tb, wv_), lambda bi, i: (bi, blk_f(i), 0)),
                   pl.BlockSpec((1, tb, wv_), lambda bi, i: (bi, blk_b(i), 0))],
        out_shape=[jax.ShapeDtypeStruct((b, tall, wv_), F32)] * 2,
        scratch_shapes=[pltpu.VMEM((2, GLA_HEADS, GLA_DV, w), F32)],
        compiler_params=_cparams(("parallel", "arbitrary")),
        name="gla",
    )(gq, gk, gv, la, gq, gk, gv, la, tri_f, tri_b)


def _na_body(safe_ref, q_ref, k_ref, v_ref, kc_ref, vc_ref, bias_ref, o_ref, *, rows):
    i = pl.program_id(2)
    kb0 = jnp.clip(i * NA_QROWS - NA_KR // 2, 0, rows - NA_BAND)
    start = pl.multiple_of(kb0 * GRID_W, GRID_W)
    nband = NA_BAND * GRID_W

    def attend(fixed_ref):
        q = q_ref[0]
        kb = k_ref[0, pl.ds(start, nband), :]
        kc = kc_ref[0]
        vall = jnp.concatenate([v_ref[0, pl.ds(start, nband), :], vc_ref[0]], axis=0)
        klane = lax.broadcasted_iota(jnp.int32, vall.shape, 1)
        vaug = jnp.concatenate([vall, (klane == 0).astype(BF16)], axis=1)
        lane = lax.broadcasted_iota(jnp.int32, q.shape, 1)
        qh = [jnp.where((lane < HEAD_DIM) if hh == 0 else (lane >= HEAD_DIM), q, jnp.zeros_like(q))
              for hh in range(2)]
        s_loc = [_dot_nt(qh[hh], kb) for hh in range(2)]
        s_ctx = [_dot_nt(qh[hh], kc) for hh in range(2)]
        p = []
        for hh in range(2):
            sl = s_loc[hh] + bias_ref[0, hh].astype(F32)
            if fixed_ref:
                m = safe_ref[0]
            else:
                m = jnp.maximum(jnp.max(sl, axis=-1, keepdims=True), jnp.max(s_ctx[hh], axis=-1, keepdims=True))
            p.append(jnp.concatenate([jnp.exp2(sl - m), jnp.exp2(s_ctx[hh] - m)], axis=1).astype(BF16))
        acc = [_dot(p[hh], vaug) for hh in range(2)]
        outs = [acc[hh][:, 0:LANES] / acc[hh][:, LANES:LANES + 1] for hh in range(2)]
        o_ref[0] = jnp.where(lane < HEAD_DIM, outs[0], outs[1]).astype(BF16)

    fixed = safe_ref[0] > 0.0
    pl.when(fixed)(functools.partial(attend, True))
    pl.when(jnp.logical_not(fixed))(functools.partial(attend, False))


def _na_call(safe, nq, nk, nv, bias, *, t_lat):
    b, tall, w = nq.shape
    tc = tall - t_lat
    rows = t_lat // GRID_W
    tq = NA_QROWS * GRID_W
    nsteps = rows // NA_QROWS
    npair = w // LANES
    ctx_blk = t_lat // tc

    step_types, _ = _na_block_types(rows)
    common = max(set(step_types), key=step_types.count)

    def btype(i):
        t = common
        for step, kind in enumerate(step_types):
            if kind != common:
                t = jnp.where(i == step, kind, t)
        return t

    return pl.pallas_call(
        functools.partial(_na_body, rows=rows),
        grid=(b, npair, nsteps),
        in_specs=[pl.BlockSpec(memory_space=pltpu.SMEM),
                  pl.BlockSpec((1, tq, LANES), lambda bi, hp, i: (bi, i, hp)),
                  pl.BlockSpec((1, t_lat, LANES), lambda bi, hp, i: (bi, 0, hp)),
                  pl.BlockSpec((1, t_lat, LANES), lambda bi, hp, i: (bi, 0, hp)),
                  pl.BlockSpec((1, tc, LANES), lambda bi, hp, i: (bi, ctx_blk, hp)),
                  pl.BlockSpec((1, tc, LANES), lambda bi, hp, i: (bi, ctx_blk, hp)),
                  pl.BlockSpec((1, 2, tq, NA_BAND * GRID_W), lambda bi, hp, i: (btype(i), hp, 0, 0))],
        out_specs=pl.BlockSpec((1, tq, LANES), lambda bi, hp, i: (bi, i, hp)),
        out_shape=jax.ShapeDtypeStruct((b, tall, w), BF16),
        compiler_params=_cparams(("parallel", "parallel", "arbitrary")),
        name="nbr_attn",
    )(safe, nq, nk, nv, nk, nv, bias)


def _na_block_types(rows):
    kinds, reps, step_types = {}, [], []
    for r0 in range(0, rows, NA_QROWS):
        kb0 = min(max(r0 - NA_KR // 2, 0), rows - NA_BAND)
        r = r0 + np.arange(NA_QROWS)
        sig = (kb0 - r0, tuple(np.clip(r - NA_KR // 2, 0, rows - NA_KR) - r))
        if sig not in kinds:
            kinds[sig] = len(reps)
            reps.append(r0)
        step_types.append(kinds[sig])
    return step_types, reps


def _na_bias_tiles(rpb, rows):
    assert rows >= NA_BAND and rows % NA_QROWS == 0 and NA_QROWS + NA_KR - 1 <= NA_BAND
    nl, nh, na, nb = rpb.shape
    cidx = np.arange(GRID_W)
    rel_c = cidx[None, :] - cidx[:, None] + NA_KC - 1
    sel = jnp.asarray(rel_c[None] == np.arange(nb)[:, None, None], F32)
    toep = jnp.einsum('lhab,bqk->lhqak', rpb.astype(F32) * LOG2E, sel, precision=lax.Precision.HIGHEST)
    toep = jnp.pad(toep.astype(BF16), ((0, 0), (0, 0), (0, 0), (NA_BAND, NA_BAND), (0, 0)))
    tiles = []
    for r0 in _na_block_types(rows)[1]:
        kb0 = min(max(r0 - NA_KR // 2, 0), rows - NA_BAND)
        qi = np.arange(NA_QROWS * GRID_W)
        qr, qc = r0 + qi // GRID_W, qi % GRID_W
        kj = np.arange(NA_BAND * GRID_W)
        kr, kc = kb0 + kj // GRID_W, kj % GRID_W
        rs = np.clip(qr - NA_KR // 2, 0, rows - NA_KR)
        cs = np.clip(qc - NA_KC // 2, 0, GRID_W - NA_KC)
        valid = ((kr[None, :] >= rs[:, None]) & (kr[None, :] < rs[:, None] + NA_KR)
                 & (kc[None, :] >= cs[:, None]) & (kc[None, :] < cs[:, None] + NA_KC))
        parts = []
        for q_row in range(NA_QROWS):
            a0 = kb0 - (r0 + q_row) + NA_KR - 1 + NA_BAND
            blk = toep[:, :, :, a0:a0 + NA_BAND, :]
            parts.append(blk.reshape(nl, nh, GRID_W, NA_BAND * GRID_W))
        tile = jnp.concatenate(parts, axis=2)
        tiles.append(jnp.where(jnp.asarray(valid)[None, None], tile, NEG_BIG))
    return jnp.stack(tiles, axis=1)


def _ctx_attn_body(q_ref, k_ref, v_ref, y_ref, o_ref):
    del y_ref
    q = q_ref[0]
    k = k_ref[0]
    v = v_ref[0]
    lane = lax.broadcasted_iota(jnp.int32, q.shape, 1)
    outs = []
    for hh in range(2):
        qh = jnp.where((lane < HEAD_DIM) if hh == 0 else (lane >= HEAD_DIM), q, jnp.zeros_like(q))
        s = _dot_nt(qh, k)
        m = jnp.max(s, axis=-1, keepdims=True)
        p = jnp.exp2(s - m)
        outs.append(_dot(p.astype(BF16), v) / jnp.sum(p, axis=-1, keepdims=True))
    o_ref[0] = jnp.where(lane < HEAD_DIM, outs[0], outs[1]).astype(BF16)


def _ctx_attn_call(nq, nk, nv, into, *, t_lat):
    b, tall, w = nq.shape
    tc = tall - t_lat
    ctx_blk = t_lat // tc
    spec = pl.BlockSpec((1, tc, LANES), lambda bi, hp: (bi, ctx_blk, hp))
    return pl.pallas_call(
        _ctx_attn_body,
        grid=(b, w // LANES),
        in_specs=[spec, spec, spec, pl.BlockSpec(memory_space=pl.ANY)],
        out_specs=spec,
        out_shape=jax.ShapeDtypeStruct((b, tall, w), BF16),
        input_output_aliases={3: 0},
        compiler_params=_cparams(("parallel", "parallel")),
        name="ctx_attn",
    )(nq, nk, nv, into)


def _merge_mlp_body(x_ref, yd_ref, of_ref, ob_ref, sg_ref, yn_ref, gt_ref, wd_ref, wg_ref, wn_ref, wo_ref,
                    gn_ref, n2_ref, w1_ref, w2_ref, mb_ref, mc_ref, o_ref, *, t_lat, tm, d):
    r = pl.program_id(1)
    ctx = _is_ctx_rows(r, tm, t_lat)
    g1 = jnp.where(ctx, mc_ref[0, :, 2 * d:3 * d], mb_ref[0, :, 2 * d:3 * d])
    og = of_ref[0] + ob_ref[0]
    parts = []
    for hd in range(GLA_HEADS):
        oh = og[:, hd * GLA_DV:(hd + 1) * GLA_DV]
        ms = jnp.mean(oh * oh, axis=-1, keepdims=True)
        parts.append(oh * lax.rsqrt(ms + EPS) * gn_ref[...])
    yg = (jnp.concatenate(parts, axis=-1) * sg_ref[0].astype(F32)).astype(BF16)
    m = (gt_ref[0, :, 0:d].astype(F32) * _dot(yd_ref[0], wd_ref[...])
         + gt_ref[0, :, d:2 * d].astype(F32) * _dot(yg, wg_ref[...])
         + gt_ref[0, :, 2 * d:3 * d].astype(F32) * _dot(yn_ref[0], wn_ref[...]))
    x1 = x_ref[0] + g1 * _dot(m.astype(BF16), wo_ref[...])
    sh = jnp.where(ctx, mc_ref[0, :, 3 * d:4 * d], mb_ref[0, :, 3 * d:4 * d])
    sc = jnp.where(ctx, mc_ref[0, :, 4 * d:5 * d], mb_ref[0, :, 4 * d:5 * d])
    g2 = jnp.where(ctx, mc_ref[0, :, 5 * d:6 * d], mb_ref[0, :, 5 * d:6 * d])
    h = _norm_mod(x1, n2_ref[...], sc, sh).astype(BF16)
    acc = jnp.zeros((tm, d), F32)
    step = 1024
    for j in range(w1_ref.shape[1] // step):
        a = jnp.maximum(_dot(h, w1_ref[:, j * step:(j + 1) * step]), 0.0)
        acc = acc + _dot((a * a).astype(BF16), w2_ref[j * step:(j + 1) * step, :])
    o_ref[0] = x1 + g2 * acc


def _merge_mlp_call(xall, yd, of, ob, sg, yn, gates, wd, wg, wn, wo, gn, n2, w1, w2, mod3, t_lat, *, lat_only):
    b, tall, d = xall.shape
    tm = next(t for t in (512, 256) if t_lat % t == 0) if lat_only else _row_tile(tall)
    n_rows = t_lat if lat_only else tall
    nb = b
    dff = w1.shape[1]
    const = lambda bi, r: (0, 0)
    row = lambda n: pl.BlockSpec((1, tm, n), lambda bi, r: (bi, r, 0))
    once = lambda shape: pl.BlockSpec(shape, const, pipeline_mode=pl.Buffered(1))
    return pl.pallas_call(
        functools.partial(_merge_mlp_body, t_lat=t_lat, tm=tm, d=d),
        grid=(b, n_rows // tm),
        in_specs=[row(d), row(BR_W), row(BR_W), row(BR_W), row(BR_W), row(BR_W), row(3 * d),
                  once((BR_W, d)), once((BR_W, d)), once((BR_W, d)), once((d, d)),
                  pl.BlockSpec((1, GLA_DV), const), pl.BlockSpec((1, d), const),
                  once((d, dff)), once((dff, d)),
                  pl.BlockSpec((1, 1, 6 * d), lambda bi, r: (bi, 0, 0)),
                  pl.BlockSpec((1, 1, 6 * d), lambda bi, r: (nb, 0, 0))],
        out_specs=row(d),
        out_shape=jax.ShapeDtypeStruct((b, n_rows, d), F32),
        compiler_params=pltpu.CompilerParams(dimension_semantics=("parallel", "parallel"),
                                             vmem_limit_bytes=VMEM_LIMIT_MERGE_MLP),
        name="merge_mlp",
    )(xall, yd, of, ob, sg, yn, gates, wd, wg, wn, wo, gn, n2.reshape(1, d), w1, w2, mod3, mod3)


def kernel(x, c, ctx, c_ctx, w_mod, b_mod, norm1_g, norm2_g, w_in, da_qn_g, da_kn_g, da_lambda, da_subln_g,
           gla_a2, gla_a_b, gla_gn_g, na_qn_g, na_kn_g, na_rpb, w_br_da, w_br_gla, w_br_na, w_out, w_ff1, w_ff2):
    b, t_lat, d = x.shape
    tc = ctx.shape[1]
    tall = t_lat + tc
    depth = w_mod.shape[0]
    rows = t_lat // GRID_W
    assert d == D_MODEL and t_lat % (2 * tc) == 0 and tc == 256 and b < 8

    xall = jnp.concatenate([x, ctx], axis=1)
    cvec = jnp.zeros((8, d), F32).at[0:b].set(c).at[b].set(c_ctx)
    mod = _mod_call(cvec, w_mod, b_mod)
    cos_t, sin_t = _rope_tables(t_lat, tall)

    seg = np.arange(NORM_LANES) // HEAD_DIM
    bd = jnp.asarray(seg[:, None] == seg[None, :], BF16)
    ci = np.arange(GLA_BLOCK)
    same_chunk = (ci[None, :] // GLA_CHUNK) == (ci[:, None] // GLA_CHUNK)
    tri_f = jnp.asarray(same_chunk & (ci[None, :] <= ci[:, None]), BF16)
    tri_b = jnp.asarray(same_chunk & (ci[None, :] >= ci[:, None]), BF16)

    o_dq, o_dk, o_dv = 0, 512, 1024
    o_gq, o_gk, o_gv, o_gg, o_ga = 1536, 1792, 2048, 2560, 3072
    o_nq, o_nk, o_nv = 3104, 3616, 4128
    o_gate = 4640
    tq_da = next(t for t in (2048, 1024, 512, 256) if t_lat % t == 0)
    tk_da = next(t for t in (3328, 1280, 256) if tall % t == 0)
    na_bias = _na_bias_tiles(na_rpb, rows)

    for l in range(depth):
        need_ctx = l < depth - 1
        lam_init = 0.8 - 0.6 * math.exp(-0.3 * l)
        mod3 = mod[l].reshape(8, 1, 6 * d)
        wl = w_in[l].astype(BF16)
        cut = lambda a, n: wl[:, a:a + n]
        wa = jnp.concatenate([cut(o_ga, 2 * GLA_RANK), jnp.zeros((d, LANES - 2 * GLA_RANK), BF16)], axis=1)
        a2p = jnp.zeros((LANES, 2 * GLA_HEADS * GLA_DK), F32)
        a2p = a2p.at[0:GLA_RANK, 0:256].set(gla_a2[l, 0]).at[GLA_RANK:2 * GLA_RANK, 256:512].set(gla_a2[l, 1])
        ab = gla_a_b[l].reshape(1, 2 * GLA_HEADS * GLA_DK)
        tile2 = lambda g: jnp.tile(g, NORM_LANES // HEAD_DIM).reshape(1, NORM_LANES)

        h = _hnorm_call(xall, norm1_g[l], mod3, t_lat)
        qt, kk, vt = _proj_da_call(h, cut(o_dq, 512), cut(o_dk, 512), cut(o_dv, 512), bd,
                                   tile2(da_qn_g[l]), tile2(da_kn_g[l]), cos_t, sin_t)
        gq, gk, gv, sgg, la = _proj_gla_call(h, cut(o_gq, 256), cut(o_gk, 256), cut(o_gv, 512), cut(o_gg, 512),
                                             wa, a2p.astype(BF16), ab)
        nq, nk, nv = _proj_na_call(h, cut(o_nq, 512), cut(o_nk, 512), cut(o_nv, 512), bd,
                                   tile2(na_qn_g[l]), tile2(na_kn_g[l]))
        def score_bound(gq_, gk_, extra):
            bnd = (math.sqrt(HEAD_DIM) * LOG2E * 1.01) * jnp.max(jnp.abs(gq_)) * jnp.max(jnp.abs(gk_)) + extra + 1e-3
            return jnp.where(bnd <= DA_SAFE_LOG2, bnd, -1.0).astype(F32).reshape(1)
        safe = score_bound(da_qn_g[l], da_kn_g[l], 0.0)
        safe_na = score_bound(na_qn_g[l], na_kn_g[l], jnp.max(jnp.abs(na_rpb[l])) * LOG2E)
        gates = _proj_gate_call(h, cut(o_gate, 3 * d))

        subg = da_subln_g[l].reshape(LANES, 1)
        y_da = _da_call(safe, qt, kk, vt, da_lambda[l], subg, lam_init,
                        q_off=0, nq=t_lat // tq_da, k_off=0, nk=tall // tk_da, tq=tq_da, tk=tk_da)
        o_f, o_b = _gla_call(gq, gk, gv, la, tri_f, tri_b, t_lat=t_lat)
        y_na = _na_call(safe_na, nq, nk, nv, na_bias[l], t_lat=t_lat)
        if need_ctx:
            y_da = _da_call(safe, qt, kk, vt, da_lambda[l], subg, lam_init,
                            q_off=t_lat // tc, nq=1, k_off=t_lat // tc, nk=1, tq=tc, tk=tc, into=y_da)
            y_na = _ctx_attn_call(nq, nk, nv, y_na, t_lat=t_lat)

        xall = _merge_mlp_call(xall, y_da, o_f, o_b, sgg, y_na, gates,
                               w_br_da[l].astype(BF16), w_br_gla[l].astype(BF16), w_br_na[l].astype(BF16),
                               w_out[l].astype(BF16), gla_gn_g[l].reshape(1, GLA_DV), norm2_g[l],
                               w_ff1[l].astype(BF16), w_ff2[l].astype(BF16), mod3, t_lat, lat_only=not need_ctx)
    return xall
```

```python
import functools
import math

import numpy as np
import jax
import jax.numpy as jnp
from jax import lax
from jax.experimental import pallas as pl
from jax.experimental.pallas import tpu as pltpu

F32 = jnp.float32
BF16 = jnp.bfloat16
F8 = jnp.float8_e4m3fn

D_MODEL = 1024
GRID_W = 64
HEAD_DIM = 64
EPS = 1e-6
ROPE_BASE = 10000.0
ROPE_PAIRS = HEAD_DIM // 4
DA_HEADS = 4
GLA_HEADS = 4
GLA_DK = 64
GLA_DV = 128
GLA_RANK = 16
GLA_TAU = 16.0
GLA_CHUNK = 64
NA_HEADS = 8
NA_KR = 8
NA_KC = 16
BR_W = 512

DA_V = 2 * HEAD_DIM
LOG2E = math.log2(math.e)
DA_SAFE_LOG2 = 45.0

GLA_BLOCK = 256
NORM_LANES = 256
LANES = 128
NA_QROWS = 8
NA_BAND = 16
NEG_BIG = -1e30
VMEM_LIMIT = 48 * 1024 * 1024
VMEM_LIMIT_MERGE_MLP = 56 * 1024 * 1024


def _cparams(sem):
    return pltpu.CompilerParams(dimension_semantics=sem, vmem_limit_bytes=VMEM_LIMIT)


def _dot(a, b):
    return jnp.dot(a, b, preferred_element_type=F32)


def _dot_nt(a, b):
    return lax.dot_general(a, b, (((1,), (1,)), ((), ())), preferred_element_type=F32)


def _dot_tn(a, b):
    return lax.dot_general(a, b, (((0,), (0,)), ((), ())), preferred_element_type=F32)


def _split_dot(x, w_exact):
    hi = x.astype(BF16)
    lo = (x - hi.astype(F32)).astype(BF16)
    return _dot(hi, w_exact) + _dot(lo, w_exact)


def _split_dot_left(w_exact, x):
    hi = x.astype(BF16)
    lo = (x - hi.astype(F32)).astype(BF16)
    return _dot(w_exact, hi) + _dot(w_exact, lo)


def _sigmoid(x):
    return 1.0 / (1.0 + jnp.exp(-x))


def _row_tile(tall):
    for tm in (640, 256):
        if tall % tm == 0:
            return tm
    raise ValueError(f"unsupported token count {tall}")


def _mod_body(c_ref, w_ref, b_ref, o_ref):
    cv = c_ref[...]
    s = cv * _sigmoid(cv)
    o_ref[0] = _dot(s.astype(BF16), w_ref[0].astype(BF16)) + b_ref[0]


def _mod_call(cvec, w_mod, b_mod):
    depth, d, n = w_mod.shape
    tn = 1536
    return pl.pallas_call(
        _mod_body,
        grid=(depth, n // tn),
        in_specs=[pl.BlockSpec((8, d), lambda l, j: (0, 0)),
                  pl.BlockSpec((1, d, tn), lambda l, j: (l, 0, j)),
                  pl.BlockSpec((1, 1, tn), lambda l, j: (l, 0, j))],
        out_specs=pl.BlockSpec((1, 8, tn), lambda l, j: (l, 0, j)),
        out_shape=jax.ShapeDtypeStruct((depth, 8, n), F32),
        compiler_params=_cparams(("parallel", "parallel")),
        name="mod",
    )(cvec, w_mod, b_mod.reshape(depth, 1, n))


def _rope_body(freq_ref, cos_ref, sin_ref, *, t_lat, tm):
    i = pl.program_id(0)
    t = i * tm + lax.broadcasted_iota(jnp.int32, (tm, LANES), 0)
    lane = lax.broadcasted_iota(jnp.int32, (tm, LANES), 1)
    shift = int(math.log2(GRID_W))
    row = lax.shift_right_logical(t, shift).astype(F32)
    col = jnp.bitwise_and(t, GRID_W - 1).astype(F32)
    use_row = jnp.bitwise_and(lane, 2 * ROPE_PAIRS) == 0
    first_half = jnp.bitwise_and(lane, ROPE_PAIRS) == 0
    ang = jnp.where(use_row, row, col) * freq_ref[...]
    is_lat = t < t_lat
    cos_ref[...] = jnp.where(is_lat, jnp.cos(ang), 1.0)
    sn = jnp.sin(ang)
    sin_ref[...] = jnp.where(is_lat, jnp.where(first_half, -sn, sn), 0.0)


def _rope_tables(t_lat, tall):
    tm = 256
    freqs = ROPE_BASE ** (-jnp.arange(ROPE_PAIRS, dtype=F32) / ROPE_PAIRS)
    freq_lane = jnp.tile(freqs, LANES // ROPE_PAIRS).reshape(1, LANES)
    return pl.pallas_call(
        functools.partial(_rope_body, t_lat=t_lat, tm=tm),
        grid=(tall // tm,),
        in_specs=[pl.BlockSpec((1, LANES), lambda i: (0, 0))],
        out_specs=[pl.BlockSpec((tm, LANES), lambda i: (i, 0))] * 2,
        out_shape=[jax.ShapeDtypeStruct((tall, LANES), F32)] * 2,
        compiler_params=_cparams(("parallel",)),
        name="rope_tables",
    )(freq_lane)


def _is_ctx_rows(r, tm, t_lat):
    t = r * tm + lax.broadcasted_iota(jnp.int32, (tm, 1), 0)
    return t >= t_lat


def _norm_mod(x, g, sc, sh):
    ms = jnp.mean(x * x, axis=-1, keepdims=True)
    return (x * lax.rsqrt(ms + EPS) * g) * (1.0 + sc) + sh


def _segnorm64(z, bd, gain):
    ss = _split_dot(z * z, bd)
    return z * lax.rsqrt(ss * (1.0 / HEAD_DIM) + EPS) * gain


def _hnorm_body(x_ref, g_ref, mb_ref, mc_ref, h_ref, *, t_lat, tm, d):
    r = pl.program_id(1)
    ctx = _is_ctx_rows(r, tm, t_lat)
    sh = jnp.where(ctx, mc_ref[0, :, 0:d], mb_ref[0, :, 0:d])
    sc = jnp.where(ctx, mc_ref[0, :, d:2 * d], mb_ref[0, :, d:2 * d])
    h_ref[0] = _norm_mod(x_ref[0], g_ref[...], sc, sh).astype(BF16)


def _hnorm_call(xall, g, mod3, t_lat):
    b, tall, d = xall.shape
    tm = _row_tile(tall)
    nb = b
    return pl.pallas_call(
        functools.partial(_hnorm_body, t_lat=t_lat, tm=tm, d=d),
        grid=(b, tall // tm),
        in_specs=[pl.BlockSpec((1, tm, d), lambda bi, r: (bi, r, 0)),
                  pl.BlockSpec((1, d), lambda bi, r: (0, 0)),
                  pl.BlockSpec((1, 1, 6 * d), lambda bi, r: (bi, 0, 0)),
                  pl.BlockSpec((1, 1, 6 * d), lambda bi, r: (nb, 0, 0))],
        out_specs=pl.BlockSpec((1, tm, d), lambda bi, r: (bi, r, 0)),
        out_shape=jax.ShapeDtypeStruct((b, tall, d), BF16),
        compiler_params=_cparams(("parallel", "parallel")),
        name="hnorm",
    )(xall, g.reshape(1, d), mod3, mod3)


def _rope128(x, cs, sn, first_half):
    partner = jnp.where(first_half, pltpu.roll(x, x.shape[1] - ROPE_PAIRS, 1), pltpu.roll(x, ROPE_PAIRS, 1))
    return x * cs + partner * sn


def _proj_da_body(h_ref, wq_ref, wk_ref, wv_ref, bd_ref, gq_ref, gk_ref, cos_ref, sin_ref,
                  qt_ref, k_ref, vt_ref):
    h = h_ref[0]
    bd = bd_ref[...]
    wide = bd.shape[0]
    cs = jnp.concatenate([cos_ref[...]] * (wide // LANES), axis=1)
    sn = jnp.concatenate([sin_ref[...]] * (wide // LANES), axis=1)
    lane = lax.broadcasted_iota(jnp.int32, cs.shape, 1)
    first_half = jnp.bitwise_and(lane, ROPE_PAIRS) == 0
    low_lanes = jnp.bitwise_and(lane, HEAD_DIM) == 0
    zq = _dot(h, wq_ref[...])
    zk = _dot(h, wk_ref[...])
    zv = _dot(h, wv_ref[...])
    scale = math.sqrt(HEAD_DIM ** -0.5 * LOG2E)
    for j in range(zq.shape[1] // wide):
        sl = slice(j * wide, (j + 1) * wide)
        q = _rope128(_segnorm64(zq[:, sl], bd, gq_ref[...]) * scale, cs, sn, first_half)
        k = _rope128(_segnorm64(zk[:, sl], bd, gk_ref[...]) * scale, cs, sn, first_half)
        q_t = q.T
        qh_t = q_t.astype(F8).astype(F32)
        ql_t = q_t - qh_t
        k_hi = k.astype(F8).astype(F32)
        k_lo = k - k_hi
        k_map1 = jnp.where(low_lanes, k_hi, pltpu.roll(k_lo, HEAD_DIM, 1)).astype(F8)
        k_map2 = jnp.where(low_lanes, pltpu.roll(k_hi, wide - HEAD_DIM, 1), k_lo).astype(F8)
        for hh in range(wide // LANES):
            hd = j * (wide // LANES) + hh
            r0 = hd * 2 * LANES
            for a in range(2):
                src = slice(hh * LANES + a * HEAD_DIM, hh * LANES + (a + 1) * HEAD_DIM)
                qt_ref[0, r0 + a * LANES:r0 + a * LANES + HEAD_DIM, :] = qh_t[src].astype(F8)
                qt_ref[0, r0 + a * LANES + HEAD_DIM:r0 + (a + 1) * LANES, :] = ql_t[src].astype(F8)
            k_ref[0, :, r0:r0 + LANES] = k_map1[:, hh * LANES:(hh + 1) * LANES]
            k_ref[0, :, r0 + LANES:r0 + 2 * LANES] = k_map2[:, hh * LANES:(hh + 1) * LANES]
            vt_ref[0, hd * DA_V:(hd + 1) * DA_V, :] = zv[:, hd * LANES:(hd + 1) * LANES].T.astype(BF16)


def _proj_da_call(h, wq, wk, wv, bd, gq, gk, cos_t, sin_t):
    b, tall, d = h.shape
    tm = _row_tile(tall)
    w = DA_HEADS * LANES
    nbd = bd.shape[0]
    const = lambda bi, r: (0, 0)
    return pl.pallas_call(
        _proj_da_body,
        grid=(b, tall // tm),
        in_specs=[pl.BlockSpec((1, tm, d), lambda bi, r: (bi, r, 0)),
                  pl.BlockSpec((d, w), const), pl.BlockSpec((d, w), const), pl.BlockSpec((d, w), const),
                  pl.BlockSpec((nbd, nbd), const),
                  pl.BlockSpec((1, nbd), const), pl.BlockSpec((1, nbd), const),
                  pl.BlockSpec((tm, LANES), lambda bi, r: (r, 0)),
                  pl.BlockSpec((tm, LANES), lambda bi, r: (r, 0))],
        out_specs=[pl.BlockSpec((1, 2 * w, tm), lambda bi, r: (bi, 0, r)),
                   pl.BlockSpec((1, tm, 2 * w), lambda bi, r: (bi, r, 0)),
                   pl.BlockSpec((1, DA_HEADS * DA_V, tm), lambda bi, r: (bi, 0, r))],
        out_shape=[jax.ShapeDtypeStruct((b, 2 * w, tall), F8),
                   jax.ShapeDtypeStruct((b, tall, 2 * w), F8),
                   jax.ShapeDtypeStruct((b, DA_HEADS * DA_V, tall), BF16)],
        compiler_params=_cparams(("parallel", "parallel")),
        name="proj_da",
    )(h, wq, wk, wv, bd, gq, gk, cos_t, sin_t)


def _proj_gla_body(h_ref, wq_ref, wk_ref, wv_ref, wg_ref, wa_ref, a2_ref, ab_ref,
                   q_ref, k_ref, v_ref, sg_ref, la_ref):
    h = h_ref[0]
    q_ref[0] = _dot(h, wq_ref[...]) * (GLA_DK ** -0.5)
    k_ref[0] = _dot(h, wk_ref[...])
    v_ref[0] = _dot(h, wv_ref[...]).astype(BF16)
    g = _dot(h, wg_ref[...])
    sg_ref[0] = (g * _sigmoid(g)).astype(BF16)
    ga = _dot(h, wa_ref[...])
    z = _dot(ga.astype(BF16), a2_ref[...]) + ab_ref[...]
    la_ref[0] = (jnp.minimum(z, 0.0) - jnp.log1p(jnp.exp(-jnp.abs(z)))) * (1.0 / GLA_TAU)


def _proj_gla_call(h, wq, wk, wv, wg, wa, a2p, ab):
    b, tall, d = h.shape
    tm = _row_tile(tall)
    wqk = GLA_HEADS * GLA_DK
    wv_ = GLA_HEADS * GLA_DV
    const = lambda bi, r: (0, 0)
    row = lambda n: pl.BlockSpec((1, tm, n), lambda bi, r: (bi, r, 0))
    return pl.pallas_call(
        _proj_gla_body,
        grid=(b, tall // tm),
        in_specs=[row(d),
                  pl.BlockSpec((d, wqk), const), pl.BlockSpec((d, wqk), const),
                  pl.BlockSpec((d, wv_), const), pl.BlockSpec((d, wv_), const),
                  pl.BlockSpec((d, LANES), const),
                  pl.BlockSpec((LANES, 2 * wqk), const), pl.BlockSpec((1, 2 * wqk), const)],
        out_specs=[row(wqk), row(wqk), row(wv_), row(wv_), row(2 * wqk)],
        out_shape=[jax.ShapeDtypeStruct((b, tall, wqk), F32),
                   jax.ShapeDtypeStruct((b, tall, wqk), F32),
                   jax.ShapeDtypeStruct((b, tall, wv_), BF16),
                   jax.ShapeDtypeStruct((b, tall, wv_), BF16),
                   jax.ShapeDtypeStruct((b, tall, 2 * wqk), F32)],
        compiler_params=_cparams(("parallel", "parallel")),
        name="proj_gla",
    )(h, wq, wk, wv, wg, wa, a2p, ab)


def _proj_na_body(h_ref, wq_ref, wk_ref, wv_ref, bd_ref, gq_ref, gk_ref, q_ref, k_ref, v_ref):
    h = h_ref[0]
    bd = bd_ref[...]
    wide = bd.shape[0]
    zq = _dot(h, wq_ref[...])
    zk = _dot(h, wk_ref[...])
    scale = HEAD_DIM ** -0.5 * LOG2E
    for j in range(NA_HEADS * HEAD_DIM // wide):
        sl = slice(j * wide, (j + 1) * wide)
        q_ref[0, :, sl] = (_segnorm64(zq[:, sl], bd, gq_ref[...]) * scale).astype(BF16)
        k_ref[0, :, sl] = _segnorm64(zk[:, sl], bd, gk_ref[...]).astype(BF16)
    v_ref[0] = _dot(h, wv_ref[...]).astype(BF16)


def _proj_na_call(h, wq, wk, wv, bd, gq, gk):
    b, tall, d = h.shape
    tm = _row_tile(tall)
    w = NA_HEADS * HEAD_DIM
    nbd = bd.shape[0]
    const = lambda bi, r: (0, 0)
    row = lambda n: pl.BlockSpec((1, tm, n), lambda bi, r: (bi, r, 0))
    return pl.pallas_call(
        _proj_na_body,
        grid=(b, tall // tm),
        in_specs=[row(d), pl.BlockSpec((d, w), const), pl.BlockSpec((d, w), const), pl.BlockSpec((d, w), const),
                  pl.BlockSpec((nbd, nbd), const),
                  pl.BlockSpec((1, nbd), const), pl.BlockSpec((1, nbd), const)],
        out_specs=[row(w), row(w), row(w)],
        out_shape=[jax.ShapeDtypeStruct((b, tall, w), BF16)] * 3,
        compiler_params=_cparams(("parallel", "parallel")),
        name="proj_na",
    )(h, wq, wk, wv, bd, gq, gk)


def _proj_gate_body(h_ref, w_ref, o_ref):
    h = h_ref[0]
    n = w_ref.shape[1]
    step = 512
    for j in range(n // step):
        sl = slice(j * step, (j + 1) * step)
        o_ref[0, :, sl] = _sigmoid(_dot(h, w_ref[:, sl])).astype(BF16)


def _proj_gate_call(h, w):
    b, tall, d = h.shape
    tm = _row_tile(tall)
    n = w.shape[1]
    return pl.pallas_call(
        _proj_gate_body,
        grid=(b, tall // tm),
        in_specs=[pl.BlockSpec((1, tm, d), lambda bi, r: (bi, r, 0)),
                  pl.BlockSpec((d, n), lambda bi, r: (0, 0))],
        out_specs=pl.BlockSpec((1, tm, n), lambda bi, r: (bi, r, 0)),
        out_shape=jax.ShapeDtypeStruct((b, tall, n), BF16),
        compiler_params=_cparams(("parallel", "parallel")),
        name="proj_gate",
    )(h, w)


def _da_body(safe_ref, qt_ref, k_ref, vt_ref, lp_ref, sg_ref, o_ref, m_ref, l_ref, acc_ref, *, lam_init, nk):
    j = pl.program_id(3)
    bounded = safe_ref[0] > 0.0

    @pl.when(j == 0)
    def _():
        m_ref[...] = jnp.full(m_ref.shape, -jnp.inf, F32)
        l_ref[...] = jnp.zeros(l_ref.shape, F32)
        acc_ref[...] = jnp.zeros(acc_ref.shape, F32)

    def scores(a):
        kb = k_ref[0, :, a * LANES:(a + 1) * LANES]
        q_hi = qt_ref[0, a * LANES:a * LANES + HEAD_DIM, :]
        q_lo = qt_ref[0, a * LANES + HEAD_DIM:(a + 1) * LANES, :]
        w = jnp.concatenate([q_hi, q_hi, q_lo, q_lo], axis=0)
        return _dot(jnp.concatenate([kb, kb], axis=1), w)

    def bounded_step():
        vt = vt_ref[0]
        for a in range(2):
            p = jnp.exp2(scores(a) - safe_ref[0])
            l_ref[a:a + 1, :] += jnp.sum(p, axis=0, keepdims=True)
            acc_ref[a] += _dot(vt, p.astype(BF16))

    def plain_step():
        vt = vt_ref[0]
        for a in range(2):
            s = scores(a)
            m_old = m_ref[a:a + 1, :]
            m_new = jnp.maximum(m_old, jnp.max(s, axis=0, keepdims=True))
            alpha = jnp.exp2(m_old - m_new)
            p = jnp.exp2(s - m_new)
            l_ref[a:a + 1, :] = alpha * l_ref[a:a + 1, :] + jnp.sum(p, axis=0, keepdims=True)
            acc_ref[a] = alpha * acc_ref[a] + _dot(vt, p.astype(BF16))
            m_ref[a:a + 1, :] = m_new

    pl.when(bounded)(bounded_step)
    pl.when(jnp.logical_not(bounded))(plain_step)

    @pl.when(j == nk - 1)
    def _():
        lp = lp_ref[...]
        e1 = jnp.exp(jnp.sum(lp[0:1] * lp[1:2], axis=-1, keepdims=True))
        e2 = jnp.exp(jnp.sum(lp[2:3] * lp[3:4], axis=-1, keepdims=True))
        lam = e1 - e2 + lam_init
        o1 = acc_ref[0] / l_ref[0:1, :]
        o2 = acc_ref[1] / l_ref[1:2, :]
        o = o1 - lam * o2
        ms = jnp.mean(o * o, axis=0, keepdims=True)
        y = (o * lax.rsqrt(ms + EPS) * sg_ref[...]) * (1.0 - lam_init)
        o_ref[0] = y.T.astype(BF16)


def _da_body_into(safe_ref, qt_ref, k_ref, vt_ref, lp_ref, sg_ref, y_ref, o_ref, m_ref, l_ref, acc_ref, **kw):
    del y_ref
    _da_body(safe_ref, qt_ref, k_ref, vt_ref, lp_ref, sg_ref, o_ref, m_ref, l_ref, acc_ref, **kw)


def _da_call(safe, qt, kk, vt, lp, subg, lam_init, *, q_off, nq, k_off, nk, tq, tk, into=None):
    b, _, tall = qt.shape
    in_specs = [pl.BlockSpec(memory_space=pltpu.SMEM),
                pl.BlockSpec((1, 2 * LANES, tq), lambda bi, h, i, j: (bi, h, i + q_off)),
                pl.BlockSpec((1, tk, 2 * LANES), lambda bi, h, i, j: (bi, j + k_off, h)),
                pl.BlockSpec((1, DA_V, tk), lambda bi, h, i, j: (bi, h, j + k_off)),
                pl.BlockSpec((4, HEAD_DIM), lambda bi, h, i, j: (0, 0)),
                pl.BlockSpec((LANES, 1), lambda bi, h, i, j: (0, 0))]
    args = [safe, qt, kk, vt, lp, subg]
    body, aliases = _da_body, {}
    if into is not None:
        in_specs.append(pl.BlockSpec(memory_space=pl.ANY))
        args.append(into)
        body, aliases = _da_body_into, {len(args) - 1: 0}
    return pl.pallas_call(
        functools.partial(body, lam_init=lam_init, nk=nk),
        grid=(b, DA_HEADS, nq, nk),
        in_specs=in_specs,
        out_specs=pl.BlockSpec((1, tq, LANES), lambda bi, h, i, j: (bi, i + q_off, h)),
        out_shape=jax.ShapeDtypeStruct((b, tall, DA_HEADS * LANES), BF16),
        input_output_aliases=aliases,
        scratch_shapes=[pltpu.VMEM((8, tq), F32), pltpu.VMEM((8, tq), F32), pltpu.VMEM((2, DA_V, tq), F32)],
        compiler_params=_cparams(("parallel", "parallel", "parallel", "arbitrary")),
        name="diff_attn",
    )(*args)


def _gla_body(qf_ref, kf_ref, vf_ref, laf_ref, qb_ref, kb_ref, vb_ref, lab_ref, trif_ref, trib_ref,
              of_ref, ob_ref, s_ref, *, tb):
    i = pl.program_id(1)

    @pl.when(i == 0)
    def _():
        s_ref[...] = jnp.zeros(s_ref.shape, F32)

    c = GLA_CHUNK
    w = GLA_HEADS * GLA_DK
    nch = tb // c
    lane = lax.broadcasted_iota(jnp.int32, (1, w), 1)
    heads = range(GLA_HEADS)
    dirs = ((qf_ref, kf_ref, vf_ref, laf_ref, trif_ref, of_ref, False),
            (qb_ref, kb_ref, vb_ref, lab_ref, trib_ref, ob_ref, True))
    cums = [_split_dot_left(tri_ref[...], la_ref[0]) for _, _, _, la_ref, tri_ref, _, _ in dirs]
    prep = []
    for (q_ref, k_ref, v_ref, _, tri_ref, _, reverse), cum in zip(dirs, cums):
        tot = [cum[ch * c:ch * c + 1, :] if reverse else cum[(ch + 1) * c - 1:(ch + 1) * c, :]
               for ch in range(nch)]
        tot_rows = jnp.concatenate([jnp.broadcast_to(t, (c, w)) for t in tot], axis=0)
        k = k_ref[0]
        qe = q_ref[0] * jnp.exp(cum)
        prep.append(dict(
            keep=tri_ref[...] > 0,
            ke=(k * jnp.exp(-cum)).astype(BF16),
            kd=(k * jnp.exp(tot_rows - cum)).astype(BF16),
            dec=[jnp.exp(t) for t in tot],
            order=list(reversed(range(nch))) if reverse else list(range(nch)),
            qh=[jnp.where((lane >= hd * GLA_DK) & (lane < (hd + 1) * GLA_DK), qe, 0.0).astype(BF16)
                for hd in heads],
            vh=[v_ref[0, :, hd * GLA_DV:(hd + 1) * GLA_DV] for hd in heads]))
    inc = [[[_dot_tn(p["vh"][hd][ch * c:(ch + 1) * c], p["kd"][ch * c:(ch + 1) * c]) for ch in range(nch)]
            for hd in heads] for p in prep]
    a = [[jnp.where(p["keep"], _dot_nt(p["qh"][hd], p["ke"]), 0.0).astype(BF16) for hd in heads] for p in prep]
    entry = []
    for d, p in enumerate(prep):
        per_head = []
        for hd in heads:
            st = s_ref[d, hd]
            seen = {}
            for ch in p["order"]:
                seen[ch] = st.astype(BF16)
                st = st * p["dec"][ch] + inc[d][hd][ch]
            s_ref[d, hd] = st
            per_head.append(seen)
        entry.append(per_head)
    o_intra = [[_dot(a[d][hd], p["vh"][hd]) for hd in heads] for d, p in enumerate(prep)]
    for d, p in enumerate(prep):
        o_ref = dirs[d][5]
        for hd in heads:
            for ch in range(nch):
                rows = slice(ch * c, (ch + 1) * c)
                o_ref[0, rows, hd * GLA_DV:(hd + 1) * GLA_DV] = (
                    o_intra[d][hd][rows] + _dot_nt(p["qh"][hd][rows], entry[d][hd][ch]))


def _gla_call(gq, gk, gv, la, tri_f, tri_b, *, t_lat):
    b, tall, w = gq.shape
    tb = GLA_BLOCK
    n_lat = t_lat // tb
    nblk = tall // tb
    blk_f = lambda i: jnp.where(i == 0, n_lat, i - 1)
    blk_b = lambda i: jnp.where(i == 0, n_lat, n_lat - i)
    wv_ = GLA_HEADS * GLA_DV

    def specs(blk, la_col):
        return [pl.BlockSpec((1, tb, w), lambda bi, i: (bi, blk(i), 0)),
                pl.BlockSpec((1, tb, w), lambda bi, i: (bi, blk(i), 0)),
                pl.BlockSpec((1, tb, wv_), lambda bi, i: (bi, blk(i), 0)),
                pl.BlockSpec((1, tb, w), lambda bi, i: (bi, blk(i), la_col))]

    return pl.pallas_call(
        functools.partial(_gla_body, tb=tb),
        grid=(b, nblk),
        in_specs=specs(blk_f, 0) + specs(blk_b, 1) + [pl.BlockSpec((tb, tb), lambda bi, i: (0, 0))] * 2,
        out_specs=[pl.BlockSpec((1, tb, wv_), lambda bi, i: (bi, blk_f(i), 0)),
                   pl.BlockSpec((1, tb, wv_), lambda bi, i: (bi, blk_b(i), 0))],
        out_shape=[jax.ShapeDtypeStruct((b, tall, wv_), F32)] * 2,
        scratch_shapes=[pltpu.VMEM((2, GLA_HEADS, GLA_DV, w), F32)],
        compiler_params=_cparams(("parallel", "arbitrary")),
        name="gla",
    )(gq, gk, gv, la, gq, gk, gv, la, tri_f, tri_b)


def _na_body(safe_ref, q_ref, k_ref, v_ref, kc_ref, vc_ref, bias_ref, y_ref, o_ref, *, rows):
    del y_ref
    i = pl.program_id(2)
    kb0 = jnp.clip(i * NA_QROWS - NA_KR // 2, 0, rows - NA_BAND)
    start = pl.multiple_of(kb0 * GRID_W, GRID_W)
    nband = NA_BAND * GRID_W

    def attend(fixed_ref):
        q = q_ref[0]
        kb = k_ref[0, pl.ds(start, nband), :]
        kc = kc_ref[0]
        vall = jnp.concatenate([v_ref[0, pl.ds(start, nband), :], vc_ref[0]], axis=0)
        klane = lax.broadcasted_iota(jnp.int32, vall.shape, 1)
        vaug = jnp.concatenate([vall, (klane == 0).astype(BF16)], axis=1)
        lane = lax.broadcasted_iota(jnp.int32, q.shape, 1)
        qh = [jnp.where((lane < HEAD_DIM) if hh == 0 else (lane >= HEAD_DIM), q, jnp.zeros_like(q))
              for hh in range(2)]
        s_loc = [_dot_nt(qh[hh], kb) for hh in range(2)]
        s_ctx = [_dot_nt(qh[hh], kc) for hh in range(2)]
        p = []
        for hh in range(2):
            sl = s_loc[hh] + bias_ref[0, hh].astype(F32)
            if fixed_ref:
                m = safe_ref[0]
            else:
                m = jnp.maximum(jnp.max(sl, axis=-1, keepdims=True), jnp.max(s_ctx[hh], axis=-1, keepdims=True))
            p.append(jnp.concatenate([jnp.exp2(sl - m), jnp.exp2(s_ctx[hh] - m)], axis=1).astype(BF16))
        acc = [_dot(p[hh], vaug) for hh in range(2)]
        outs = [acc[hh][:, 0:LANES] / acc[hh][:, LANES:LANES + 1] for hh in range(2)]
        o_ref[0] = jnp.where(lane < HEAD_DIM, outs[0], outs[1]).astype(BF16)

    fixed = safe_ref[0] > 0.0
    pl.when(fixed)(functools.partial(attend, True))
    pl.when(jnp.logical_not(fixed))(functools.partial(attend, False))


def _na_call(safe, nq, nk, nv, bias, into, *, t_lat):
    b, tall, w = nq.shape
    tc = tall - t_lat
    rows = t_lat // GRID_W
    tq = NA_QROWS * GRID_W
    nsteps = rows // NA_QROWS
    npair = w // LANES
    ctx_blk = t_lat // tc

    step_types, _ = _na_block_types(rows)
    common = max(set(step_types), key=step_types.count)

    def btype(i):
        t = common
        for step, kind in enumerate(step_types):
            if kind != common:
                t = jnp.where(i == step, kind, t)
        return t

    return pl.pallas_call(
        functools.partial(_na_body, rows=rows),
        grid=(b, npair, nsteps),
        in_specs=[pl.BlockSpec(memory_space=pltpu.SMEM),
                  pl.BlockSpec((1, tq, LANES), lambda bi, hp, i: (bi, i, hp)),
                  pl.BlockSpec((1, t_lat, LANES), lambda bi, hp, i: (bi, 0, hp)),
                  pl.BlockSpec((1, t_lat, LANES), lambda bi, hp, i: (bi, 0, hp)),
                  pl.BlockSpec((1, tc, LANES), lambda bi, hp, i: (bi, ctx_blk, hp)),
                  pl.BlockSpec((1, tc, LANES), lambda bi, hp, i: (bi, ctx_blk, hp)),
                  pl.BlockSpec((1, 2, tq, NA_BAND * GRID_W), lambda bi, hp, i: (btype(i), hp, 0, 0)),
                  pl.BlockSpec(memory_space=pl.ANY)],
        out_specs=pl.BlockSpec((1, tq, LANES), lambda bi, hp, i: (bi, i, hp)),
        out_shape=jax.ShapeDtypeStruct((b, tall, w), BF16),
        input_output_aliases={7: 0},
        compiler_params=_cparams(("parallel", "parallel", "arbitrary")),
        name="nbr_attn",
    )(safe, nq, nk, nv, nk, nv, bias, into)


def _na_block_types(rows):
    kinds, reps, step_types = {}, [], []
    for r0 in range(0, rows, NA_QROWS):
        kb0 = min(max(r0 - NA_KR // 2, 0), rows - NA_BAND)
        r = r0 + np.arange(NA_QROWS)
        sig = (kb0 - r0, tuple(np.clip(r - NA_KR // 2, 0, rows - NA_KR) - r))
        if sig not in kinds:
            kinds[sig] = len(reps)
            reps.append(r0)
        step_types.append(kinds[sig])
    return step_types, reps


def _na_bias_tiles(rpb, rows):
    assert rows >= NA_BAND and rows % NA_QROWS == 0 and NA_QROWS + NA_KR - 1 <= NA_BAND
    nl, nh, na, nb = rpb.shape
    cidx = np.arange(GRID_W)
    rel_c = cidx[None, :] - cidx[:, None] + NA_KC - 1
    sel = jnp.asarray(rel_c[None] == np.arange(nb)[:, None, None], F32)
    toep = jnp.einsum('lhab,bqk->lhqak', rpb.astype(F32) * LOG2E, sel, precision=lax.Precision.HIGHEST)
    toep = jnp.pad(toep.astype(BF16), ((0, 0), (0, 0), (0, 0), (NA_BAND, NA_BAND), (0, 0)))
    tiles = []
    for r0 in _na_block_types(rows)[1]:
        kb0 = min(max(r0 - NA_KR // 2, 0), rows - NA_BAND)
        qi = np.arange(NA_QROWS * GRID_W)
        qr, qc = r0 + qi // GRID_W, qi % GRID_W
        kj = np.arange(NA_BAND * GRID_W)
        kr, kc = kb0 + kj // GRID_W, kj % GRID_W
        rs = np.clip(qr - NA_KR // 2, 0, rows - NA_KR)
        cs = np.clip(qc - NA_KC // 2, 0, GRID_W - NA_KC)
        valid = ((kr[None, :] >= rs[:, None]) & (kr[None, :] < rs[:, None] + NA_KR)
                 & (kc[None, :] >= cs[:, None]) & (kc[None, :] < cs[:, None] + NA_KC))
        parts = []
        for q_row in range(NA_QROWS):
            a0 = kb0 - (r0 + q_row) + NA_KR - 1 + NA_BAND
            blk = toep[:, :, :, a0:a0 + NA_BAND, :]
            parts.append(blk.reshape(nl, nh, GRID_W, NA_BAND * GRID_W))
        tile = jnp.concatenate(parts, axis=2)
        tiles.append(jnp.where(jnp.asarray(valid)[None, None], tile, NEG_BIG))
    return jnp.stack(tiles, axis=1)


def _ctx_attn_body(q_ref, k_ref, v_ref, y_ref, o_ref):
    del y_ref
    q = q_ref[0]
    k = k_ref[0]
    v = v_ref[0]
    lane = lax.broadcasted_iota(jnp.int32, q.shape, 1)
    outs = []
    for hh in range(2):
        qh = jnp.where((lane < HEAD_DIM) if hh == 0 else (lane >= HEAD_DIM), q, jnp.zeros_like(q))
        s = _dot_nt(qh, k)
        m = jnp.max(s, axis=-1, keepdims=True)
        p = jnp.exp2(s - m)
        outs.append(_dot(p.astype(BF16), v) / jnp.sum(p, axis=-1, keepdims=True))
    o_ref[0] = jnp.where(lane < HEAD_DIM, outs[0], outs[1]).astype(BF16)


def _ctx_attn_call(nq, nk, nv, into, *, t_lat):
    b, tall, w = nq.shape
    tc = tall - t_lat
    ctx_blk = t_lat // tc
    spec = pl.BlockSpec((1, tc, LANES), lambda bi, hp: (bi, ctx_blk, hp))
    return pl.pallas_call(
        _ctx_attn_body,
        grid=(b, w // LANES),
        in_specs=[spec, spec, spec, pl.BlockSpec(memory_space=pl.ANY)],
        out_specs=spec,
        out_shape=jax.ShapeDtypeStruct((b, tall, w), BF16),
        input_output_aliases={3: 0},
        compiler_params=_cparams(("parallel", "parallel")),
        name="ctx_attn",
    )(nq, nk, nv, into)


def _merge_mlp_body(x_ref, yd_ref, of_ref, ob_ref, sg_ref, yn_ref, gt_ref, wd_ref, wg_ref, wn_ref, wo_ref,
                    gn_ref, n2_ref, w1_ref, w2_ref, mb_ref, mc_ref, o_ref, *, t_lat, tm, d):
    r = pl.program_id(1)
    ctx = _is_ctx_rows(r, tm, t_lat)
    g1 = jnp.where(ctx, mc_ref[0, :, 2 * d:3 * d], mb_ref[0, :, 2 * d:3 * d])
    og = of_ref[0] + ob_ref[0]
    parts = []
    for hd in range(GLA_HEADS):
        oh = og[:, hd * GLA_DV:(hd + 1) * GLA_DV]
        ms = jnp.mean(oh * oh, axis=-1, keepdims=True)
        parts.append(oh * lax.rsqrt(ms + EPS) * gn_ref[...])
    yg = (jnp.concatenate(parts, axis=-1) * sg_ref[0].astype(F32)).astype(BF16)
    m = (gt_ref[0, :, 0:d].astype(F32) * _dot(yd_ref[0], wd_ref[...])
         + gt_ref[0, :, d:2 * d].astype(F32) * _dot(yg, wg_ref[...])
         + gt_ref[0, :, 2 * d:3 * d].astype(F32) * _dot(yn_ref[0], wn_ref[...]))
    x1 = x_ref[0] + g1 * _dot(m.astype(BF16), wo_ref[...])
    sh = jnp.where(ctx, mc_ref[0, :, 3 * d:4 * d], mb_ref[0, :, 3 * d:4 * d])
    sc = jnp.where(ctx, mc_ref[0, :, 4 * d:5 * d], mb_ref[0, :, 4 * d:5 * d])
    g2 = jnp.where(ctx, mc_ref[0, :, 5 * d:6 * d], mb_ref[0, :, 5 * d:6 * d])
    h = _norm_mod(x1, n2_ref[...], sc, sh).astype(BF16)
    acc = jnp.zeros((tm, d), F32)
    step = 1024
    for j in range(w1_ref.shape[1] // step):
        a = jnp.maximum(_dot(h, w1_ref[:, j * step:(j + 1) * step]), 0.0)
        acc = acc + _dot((a * a).astype(BF16), w2_ref[j * step:(j + 1) * step, :])
    o_ref[0] = x1 + g2 * acc


def _merge_mlp_call(xall, yd, of, ob, sg, yn, gates, wd, wg, wn, wo, gn, n2, w1, w2, mod3, t_lat, *, lat_only):
    b, tall, d = xall.shape
    tm = next(t for t in (512, 256) if t_lat % t == 0) if lat_only else _row_tile(tall)
    n_rows = t_lat if lat_only else tall
    nb = b
    dff = w1.shape[1]
    const = lambda bi, r: (0, 0)
    row = lambda n: pl.BlockSpec((1, tm, n), lambda bi, r: (bi, r, 0))
    once = lambda shape: pl.BlockSpec(shape, const, pipeline_mode=pl.Buffered(1))
    return pl.pallas_call(
        functools.partial(_merge_mlp_body, t_lat=t_lat, tm=tm, d=d),
        grid=(b, n_rows // tm),
        in_specs=[row(d), row(BR_W), row(BR_W), row(BR_W), row(BR_W), row(BR_W), row(3 * d),
                  once((BR_W, d)), once((BR_W, d)), once((BR_W, d)), once((d, d)),
                  pl.BlockSpec((1, GLA_DV), const), pl.BlockSpec((1, d), const),
                  once((d, dff)), once((dff, d)),
                  pl.BlockSpec((1, 1, 6 * d), lambda bi, r: (bi, 0, 0)),
                  pl.BlockSpec((1, 1, 6 * d), lambda bi, r: (nb, 0, 0))],
        out_specs=row(d),
        out_shape=jax.ShapeDtypeStruct((b, n_rows, d), F32),
        compiler_params=pltpu.CompilerParams(dimension_semantics=("parallel", "parallel"),
                                             vmem_limit_bytes=VMEM_LIMIT_MERGE_MLP),
        name="merge_mlp",
    )(xall, yd, of, ob, sg, yn, gates, wd, wg, wn, wo, gn, n2.reshape(1, d), w1, w2, mod3, mod3)


def kernel(x, c, ctx, c_ctx, w_mod, b_mod, norm1_g, norm2_g, w_in, da_qn_g, da_kn_g, da_lambda, da_subln_g,
           gla_a2, gla_a_b, gla_gn_g, na_qn_g, na_kn_g, na_rpb, w_br_da, w_br_gla, w_br_na, w_out, w_ff1, w_ff2):
    b, t_lat, d = x.shape
    tc = ctx.shape[1]
    tall = t_lat + tc
    depth = w_mod.shape[0]
    rows = t_lat // GRID_W
    assert d == D_MODEL and t_lat % (2 * tc) == 0 and tc == 256 and b < 8

    xall = jnp.concatenate([x, ctx], axis=1)
    cvec = jnp.zeros((8, d), F32).at[0:b].set(c).at[b].set(c_ctx)
    mod = _mod_call(cvec, w_mod, b_mod)
    cos_t, sin_t = _rope_tables(t_lat, tall)

    seg = np.arange(NORM_LANES) // HEAD_DIM
    bd = jnp.asarray(seg[:, None] == seg[None, :], BF16)
    ci = np.arange(GLA_BLOCK)
    same_chunk = (ci[None, :] // GLA_CHUNK) == (ci[:, None] // GLA_CHUNK)
    tri_f = jnp.asarray(same_chunk & (ci[None, :] <= ci[:, None]), BF16)
    tri_b = jnp.asarray(same_chunk & (ci[None, :] >= ci[:, None]), BF16)

    o_dq, o_dk, o_dv = 0, 512, 1024
    o_gq, o_gk, o_gv, o_gg, o_ga = 1536, 1792, 2048, 2560, 3072
    o_nq, o_nk, o_nv = 3104, 3616, 4128
    o_gate = 4640
    tq_da = next(t for t in (2048, 1024, 512, 256) if t_lat % t == 0)
    tk_da = next(t for t in (3328, 1280, 256) if tall % t == 0)
    na_bias = _na_bias_tiles(na_rpb, rows)

    for l in range(depth):
        need_ctx = l < depth - 1
        lam_init = 0.8 - 0.6 * math.exp(-0.3 * l)
        mod3 = mod[l].reshape(8, 1, 6 * d)
        wl = w_in[l].astype(BF16)
        cut = lambda a, n: wl[:, a:a + n]
        wa = jnp.concatenate([cut(o_ga, 2 * GLA_RANK), jnp.zeros((d, LANES - 2 * GLA_RANK), BF16)], axis=1)
        a2p = jnp.zeros((LANES, 2 * GLA_HEADS * GLA_DK), F32)
        a2p = a2p.at[0:GLA_RANK, 0:256].set(gla_a2[l, 0]).at[GLA_RANK:2 * GLA_RANK, 256:512].set(gla_a2[l, 1])
        ab = gla_a_b[l].reshape(1, 2 * GLA_HEADS * GLA_DK)
        tile2 = lambda g: jnp.tile(g, NORM_LANES // HEAD_DIM).reshape(1, NORM_LANES)

        h = _hnorm_call(xall, norm1_g[l], mod3, t_lat)
        qt, kk, vt = _proj_da_call(h, cut(o_dq, 512), cut(o_dk, 512), cut(o_dv, 512), bd,
                                   tile2(da_qn_g[l]), tile2(da_kn_g[l]), cos_t, sin_t)
        gq, gk, gv, sgg, la = _proj_gla_call(h, cut(o_gq, 256), cut(o_gk, 256), cut(o_gv, 512), cut(o_gg, 512),
                                             wa, a2p.astype(BF16), ab)
        nq, nk, nv = _proj_na_call(h, cut(o_nq, 512), cut(o_nk, 512), cut(o_nv, 512), bd,
                                   tile2(na_qn_g[l]), tile2(na_kn_g[l]))
        def score_bound(gq_, gk_, extra):
            bnd = (math.sqrt(HEAD_DIM) * LOG2E * 1.01) * jnp.max(jnp.abs(gq_)) * jnp.max(jnp.abs(gk_)) + extra + 1e-3
            return jnp.where(bnd <= DA_SAFE_LOG2, bnd, -1.0).astype(F32).reshape(1)
        safe = score_bound(da_qn_g[l], da_kn_g[l], 0.0)
        safe_na = score_bound(na_qn_g[l], na_kn_g[l], jnp.max(jnp.abs(na_rpb[l])) * LOG2E)
        gates = _proj_gate_call(h, cut(o_gate, 3 * d))

        subg = da_subln_g[l].reshape(LANES, 1)
        y_da = _da_call(safe, qt, kk, vt, da_lambda[l], subg, lam_init,
                        q_off=0, nq=t_lat // tq_da, k_off=0, nk=tall // tk_da, tq=tq_da, tk=tk_da,
                        into=jnp.zeros((b, tall, BR_W), BF16))
        o_f, o_b = _gla_call(gq, gk, gv, la, tri_f, tri_b, t_lat=t_lat)
        y_na = _na_call(safe_na, nq, nk, nv, na_bias[l], jnp.zeros((b, tall, BR_W), BF16), t_lat=t_lat)
        if need_ctx:
            y_da = _da_call(safe, qt, kk, vt, da_lambda[l], subg, lam_init,
                            q_off=t_lat // tc, nq=1, k_off=t_lat // tc, nk=1, tq=tc, tk=tc, into=y_da)
            y_na = _ctx_attn_call(nq, nk, nv, y_na, t_lat=t_lat)

        xall = _merge_mlp_call(xall, y_da, o_f, o_b, sgg, y_na, gates,
                               w_br_da[l].astype(BF16), w_br_gla[l].astype(BF16), w_br_na[l].astype(BF16),
                               w_out[l].astype(BF16), gla_gn_g[l].reshape(1, GLA_DV), norm2_g[l],
                               w_ff1[l].astype(BF16), w_ff2[l].astype(BF16), mod3, t_lat, lat_only=not need_ctx)
    return xall
```
